```python
import jax, jax.numpy as jnp
from jax import lax
import numpy as np

D_MODEL = 1024
BATCH = 8
SEQ = 2048
DEPTH = 2
DEC_BATCH = 32
DEC_SEQ = 4
PAST_LEN = 16384
PAGE_SIZE = 128

EPS = 1e-6
N_MEM = 256
D_FF = 2816
DN_HEADS = 4
DN_DK = 128
DN_DV = 128
DN_CONV = 4
DN_CHUNK = 64
DN_QK_W = DN_HEADS * DN_DK
DN_V_W = DN_HEADS * DN_DV
DN_CONV_W = 2 * DN_QK_W + DN_V_W
SC_WIDTH = 512
SC_CONV = 3
MLA_HEADS = 8
MLA_Q_RANK = 256
MLA_KV_RANK = 256
MLA_NOPE = 64
MLA_ROPE = 32
MLA_V = 64
MLA_QK_HD = MLA_NOPE + MLA_ROPE
ROPE_THETA = 10000.0
Q_BLOCK = 128
MEM_HEADS = 4
MEM_HD = 128
MEM_W = MEM_HEADS * MEM_HD
N_BRANCH = 4
IN_SIZES = (DN_CONV_W, DN_V_W, DN_HEADS, DN_HEADS, SC_WIDTH, SC_WIDTH, SC_WIDTH,
            MLA_Q_RANK, MLA_KV_RANK, MLA_ROPE, MEM_W, N_BRANCH * D_MODEL)
N_IN = (DN_CONV_W + DN_V_W + 2 * DN_HEADS + 3 * SC_WIDTH + MLA_Q_RANK + MLA_KV_RANK
        + MLA_ROPE + MEM_W + N_BRANCH * D_MODEL)

kernel_name = 'hybrid_deltanet_conv_mla_memory_macaron_step'

F32 = jnp.float32


def _rmsnorm(x, g):
    x32 = x.astype(F32)
    y = x32 * lax.rsqrt(jnp.mean(x32 * x32, -1, keepdims=True) + EPS)
    return (y * g.astype(F32)).astype(x.dtype)


def _l2norm(x):
    x32 = x.astype(F32)
    return x32 * lax.rsqrt(jnp.sum(x32 * x32, -1, keepdims=True) + EPS)


def _swiglu(x, w_gu, w_down):
    gate, up = jnp.split(x @ w_gu, 2, axis=-1)
    return (jax.nn.silu(gate) * up) @ w_down


def _split_in(p):
    idx, acc = [], 0
    for s in IN_SIZES[:-1]:
        acc += s
        idx.append(acc)
    return jnp.split(p, idx, axis=-1)


def _causal_dwconv(x, prev, w):
    W, T = w.shape[0], x.shape[1]
    xx = jnp.concatenate([prev.astype(x.dtype), x], axis=1)
    y = sum(xx[:, j:j + T] * w[j] for j in range(W))
    return y, xx[:, xx.shape[1] - (W - 1):]


def _rope(x, pos):
    half = x.shape[-1] // 2
    inv = ROPE_THETA ** (-jnp.arange(half, dtype=F32) / half)
    ang = pos.astype(F32)[:, None] * inv
    ang = ang.reshape((pos.shape[0],) + (1,) * (x.ndim - 3) + (half,))
    cos, sin = jnp.cos(ang), jnp.sin(ang)
    x32 = x.astype(F32)
    x1, x2 = x32[..., :half], x32[..., half:]
    return jnp.concatenate([x1 * cos - x2 * sin, x2 * cos + x1 * sin], -1).astype(x.dtype)


def _gated_delta_rule(q, k, v, g, beta, S0):
    Bn, T, H, DK = q.shape
    DV = v.shape[-1]
    C = DN_CHUNK
    pad = (-T) % C

    def prep(a):
        a = a.astype(F32)
        a = jnp.pad(a, [(0, 0), (0, pad)] + [(0, 0)] * (a.ndim - 2))
        a = a.reshape((Bn, a.shape[1] // C, C) + a.shape[2:])
        return jnp.moveaxis(a, 3, 2)

    q, k, v, g, beta = prep(q), prep(k), prep(v), prep(g), prep(beta)
    q = q * DK ** -0.5
    gc = jnp.cumsum(g, axis=-1)
    incl = jnp.tril(jnp.ones((C, C), bool))
    strict = jnp.tril(jnp.ones((C, C), bool), -1)
    gam = jnp.exp(jnp.where(incl, gc[..., :, None] - gc[..., None, :], -jnp.inf))
    kb = k * beta[..., None]
    A = jnp.where(strict, jnp.einsum('bnhid,bnhjd->bnhij', kb, k) * gam, 0.0)
    M = A + jnp.eye(C, dtype=F32)
    rhs = jnp.concatenate([v * beta[..., None], kb * jnp.exp(gc)[..., None]], -1)
    sol = lax.linalg.triangular_solve(M, rhs, left_side=True, lower=True, unit_diagonal=True)
    u, w = sol[..., :DV], sol[..., DV:]
    aqk = jnp.einsum('bnhid,bnhjd->bnhij', q, k) * gam
    qg = q * jnp.exp(gc)[..., None]
    kdec = k * jnp.exp(gc[..., -1:] - gc)[..., None]
    glast = jnp.exp(gc[..., -1])

    def step(S, xs):
        u_i, w_i, qg_i, aqk_i, kdec_i, gl_i = xs
        v_new = u_i - jnp.einsum('bhcd,bhde->bhce', w_i, S)
        o = jnp.einsum('bhcd,bhde->bhce', qg_i, S) + jnp.einsum('bhij,bhje->bhie', aqk_i, v_new)
        S = S * gl_i[..., None, None] + jnp.einsum('bhcd,bhce->bhde', kdec_i, v_new)
        return S, o

    xs = tuple(jnp.moveaxis(a, 1, 0) for a in (u, w, qg, aqk, kdec, glast))
    S, o = lax.scan(step, S0.astype(F32), xs)
    o = jnp.transpose(o, (1, 0, 3, 2, 4)).reshape(Bn, -1, H, DV)[:, :T]
    return o, S


def _deltanet_branch(qkv, z, a, b, conv_prev, S0, conv_w, A_log, dt_bias, norm_g, w_out):
    Bn, T, _ = qkv.shape
    c, conv_new = _causal_dwconv(qkv, conv_prev, conv_w)
    c = jax.nn.silu(c)
    q, k, v = jnp.split(c, [DN_QK_W, 2 * DN_QK_W], axis=-1)
    q = _l2norm(q.reshape(Bn, T, DN_HEADS, DN_DK))
    k = _l2norm(k.reshape(Bn, T, DN_HEADS, DN_DK))
    v = v.reshape(Bn, T, DN_HEADS, DN_DV)
    g = -jnp.exp(A_log.astype(F32)) * jax.nn.softplus(a.astype(F32) + dt_bias.astype(F32))
    beta = jax.nn.sigmoid(b.astype(F32))
    o, S = _gated_delta_rule(q, k, v, g, beta, S0)
    zg = jax.nn.silu(z.astype(F32)).reshape(Bn, T, DN_HEADS, DN_DV)
    o = (_rmsnorm(o, norm_g) * zg).reshape(Bn, T, DN_V_W).astype(qkv.dtype)
    return o @ w_out, conv_new, S.astype(qkv.dtype)


def _shortconv_branch(bg, cg, xin, prev, conv_w, w_out):
    y, new_prev = _causal_dwconv(cg * xin, prev, conv_w)
    return (bg * y) @ w_out, new_prev


def _mla_project(cq, ckv_raw, kr_raw, pos, q_norm_a, w_q_b, kv_norm_a, q_norm):
    Bn, T, _ = cq.shape
    q = (_rmsnorm(cq, q_norm_a) @ w_q_b).reshape(Bn, T, MLA_HEADS, MLA_QK_HD)
    q = jnp.concatenate([q[..., :MLA_NOPE], _rope(q[..., MLA_NOPE:], pos)], -1)
    q = _rmsnorm(q, q_norm)
    ckv = _rmsnorm(ckv_raw, kv_norm_a)
    kr = _rope(kr_raw, pos)
    return q, ckv, kr


def _mla_w(w_kv_b):
    w = w_kv_b.reshape(MLA_KV_RANK, MLA_HEADS, MLA_NOPE + MLA_V)
    return w[..., :MLA_NOPE], w[..., MLA_NOPE:]


def _mla_keys(ckv, kr, w_kv_b, k_norm):
    w_uk, _ = _mla_w(w_kv_b)
    k_nope = jnp.einsum('...r,rhd->...hd', ckv, w_uk)
    k_rope = jnp.broadcast_to(kr[..., None, :], k_nope.shape[:-1] + (MLA_ROPE,))
    return _rmsnorm(jnp.concatenate([k_nope, k_rope], -1), k_norm)


def _mla_prompt_attn(q, ckv, kr, w_kv_b, k_norm):
    Bn, S = q.shape[:2]
    k = _mla_keys(ckv, kr, w_kv_b, k_norm)
    v = jnp.einsum('bsr,rhe->bshe', ckv, _mla_w(w_kv_b)[1])
    kpos = jnp.arange(S)
    scale = MLA_QK_HD ** -0.5

    def blk(i):
        qi = lax.dynamic_slice_in_dim(q, i * Q_BLOCK, Q_BLOCK, axis=1)
        s = jnp.einsum('bqhd,bkhd->bhqk', qi, k).astype(F32) * scale
        qpos = i * Q_BLOCK + jnp.arange(Q_BLOCK)
        s = jnp.where(kpos[None, :] <= qpos[:, None], s, -jnp.inf)
        p = jax.nn.softmax(s, axis=-1).astype(v.dtype)
        return jnp.einsum('bhqk,bkhe->bqhe', p, v)

    o = lax.map(blk, jnp.arange(S // Q_BLOCK))
    return jnp.moveaxis(o, 0, 1).reshape(Bn, S, MLA_HEADS * MLA_V)


def _mla_sample_attn(q, ckv_new, kr_new, ckv_pool, kr_pool, layer, page_table, w_kv_b, k_norm):
    DB, T = q.shape[:2]
    _, w_uv = _mla_w(w_kv_b)
    scale = MLA_QK_HD ** -0.5

    def one(args):
        pages, q_b, c_new, r_new = args
        c_all = jnp.concatenate([ckv_pool[layer, pages].reshape(-1, MLA_KV_RANK).astype(c_new.dtype), c_new], 0)
        r_all = jnp.concatenate([kr_pool[layer, pages].reshape(-1, MLA_ROPE).astype(r_new.dtype), r_new], 0)
        L = c_all.shape[0] - T
        k = _mla_keys(c_all, r_all, w_kv_b, k_norm)
        s = jnp.einsum('qhd,khd->hqk', q_b, k).astype(F32) * scale
        mask = jnp.arange(L + T)[None, :] <= L + jnp.arange(T)[:, None]
        s = jnp.where(mask, s, -jnp.inf)
        p = jax.nn.softmax(s, axis=-1).astype(c_all.dtype)
        pc = jnp.einsum('hqk,kr->qhr', p, c_all)
        return jnp.einsum('qhr,rhe->qhe', pc, w_uv)

    o = lax.map(one, (page_table, q, ckv_new, kr_new))
    return o.reshape(DB, T, MLA_HEADS * MLA_V)


def _mem_kv(mem, mem_norm, w_kv, k_norm):
    Bn, M, _ = mem.shape
    k, v = jnp.split(_rmsnorm(mem, mem_norm) @ w_kv, 2, axis=-1)
    k = _rmsnorm(k.reshape(Bn, M, MEM_HEADS, MEM_HD), k_norm)
    return k, v.reshape(Bn, M, MEM_HEADS, MEM_HD)


def _mem_attn(q_raw, mk, mv, q_norm, w_out):
    Bn, T, _ = q_raw.shape
    q = _rmsnorm(q_raw.reshape(Bn, T, MEM_HEADS, MEM_HD), q_norm)
    s = jnp.einsum('bqhd,bmhd->bhqm', q, mk.astype(q.dtype)).astype(F32) * MEM_HD ** -0.5
    p = jax.nn.softmax(s, axis=-1).astype(q.dtype)
    o = jnp.einsum('bhqm,bmhe->bqhe', p, mv.astype(q.dtype)).reshape(Bn, T, MEM_W)
    return o @ w_out


def _layer(x, pos, dn_conv_prev, dn_S0, sc_prev, mk, mv, mla_attend, w):
    Bn, T, D = x.shape
    x = x + 0.5 * _swiglu(_rmsnorm(x, w['ffn1_norm']), w['ffn1_w_gu'], w['ffn1_w_down'])
    h = _rmsnorm(x, w['mix_norm'])
    (dn_qkv, dn_z, dn_a, dn_b, sc_b, sc_c, sc_x,
     mla_q, mla_kv, mla_kr, mem_q, gates) = _split_in(h @ w['w_in'])
    y_dn, dn_conv_new, dn_S = _deltanet_branch(dn_qkv, dn_z, dn_a, dn_b, dn_conv_prev, dn_S0,
                                               w['dn_conv_w'], w['dn_A_log'], w['dn_dt_bias'],
                                               w['dn_norm'], w['dn_w_out'])
    y_sc, sc_new = _shortconv_branch(sc_b, sc_c, sc_x, sc_prev, w['sc_conv_w'], w['sc_w_out'])
    q, ckv, kr = _mla_project(mla_q, mla_kv, mla_kr, pos, w['mla_q_norm_a'], w['mla_w_q_b'],
                              w['mla_kv_norm_a'], w['mla_q_norm'])
    y_mla = mla_attend(q, ckv, kr) @ w['mla_w_out']
    y_mem = _mem_attn(mem_q, mk, mv, w['mem_q_norm'], w['mem_w_out'])
    g = jax.nn.sigmoid(gates.astype(F32)).astype(x.dtype).reshape(Bn, T, N_BRANCH, D)
    merged = g[..., 0, :] * y_dn + g[..., 1, :] * y_sc + g[..., 2, :] * y_mla + g[..., 3, :] * y_mem
    x = x + merged @ w['w_o']
    x = x + 0.5 * _swiglu(_rmsnorm(x, w['ffn2_norm']), w['ffn2_w_gu'], w['ffn2_w_down'])
    return x, dn_S, dn_conv_new, sc_new, ckv, kr


def setup_inputs(seed: int = 0) -> dict:
    key = jax.random.key(seed)
    ks = list(jax.random.split(key, 64))

    def nrm(shape, scale=1.0):
        return scale * jax.random.normal(ks.pop(), shape, F32)

    def gain(n):
        return jnp.ones((DEPTH, n), F32) + nrm((DEPTH, n), 0.02)

    n_pages = PAST_LEN // PAGE_SIZE
    n_phys = (5 * DEC_BATCH * n_pages + 3) // 4
    page_table = jax.random.permutation(ks.pop(), n_phys)[:DEC_BATCH * n_pages]
    page_table = page_table.reshape(DEC_BATCH, n_pages).astype(jnp.int32)
    dt = jnp.exp(jax.random.uniform(ks.pop(), (DEPTH, DN_HEADS), F32, np.log(1e-3), np.log(1e-1)))
    dt_bias = dt + jnp.log(-jnp.expm1(-dt))
    A_log = jnp.log(jax.random.uniform(ks.pop(), (DEPTH, DN_HEADS), F32, 1.0, 16.0))
    D = D_MODEL
    return {
        'x_prompt': nrm((BATCH, SEQ, D)),
        'x_sample': nrm((DEC_BATCH, DEC_SEQ, D)),
        'state_dn_S': nrm((DEPTH, DEC_BATCH, DN_HEADS, DN_DK, DN_DV), 0.1),
        'state_dn_conv': nrm((DEPTH, DEC_BATCH, DN_CONV - 1, DN_CONV_W)),
        'state_sc_conv': nrm((DEPTH, DEC_BATCH, SC_CONV - 1, SC_WIDTH)),
        'cache_mla_ckv': nrm((DEPTH, n_phys, PAGE_SIZE, MLA_KV_RANK)),
        'cache_mla_krope': nrm((DEPTH, n_phys, PAGE_SIZE, MLA_ROPE)),
        'cache_mem_k': nrm((DEPTH, DEC_BATCH, N_MEM, MEM_HEADS, MEM_HD)),
        'cache_mem_v': nrm((DEPTH, DEC_BATCH, N_MEM, MEM_HEADS, MEM_HD)),
        'page_table': page_table,
        'mem_prompt': nrm((BATCH, N_MEM, D)),
        'ffn1_norm': gain(D),
        'ffn1_w_gu': nrm((DEPTH, D, 2 * D_FF), D ** -0.5),
        'ffn1_w_down': nrm((DEPTH, D_FF, D), D_FF ** -0.5),
        'mix_norm': gain(D),
        'w_in': nrm((DEPTH, D, N_IN), D ** -0.5),
        'dn_conv_w': nrm((DEPTH, DN_CONV, DN_CONV_W), DN_CONV ** -0.5),
        'dn_A_log': A_log,
        'dn_dt_bias': dt_bias,
        'dn_norm': gain(DN_DV),
        'dn_w_out': nrm((DEPTH, DN_V_W, D), DN_V_W ** -0.5),
        'sc_conv_w': nrm((DEPTH, SC_CONV, SC_WIDTH), SC_CONV ** -0.5),
        'sc_w_out': nrm((DEPTH, SC_WIDTH, D), SC_WIDTH ** -0.5),
        'mla_q_norm_a': gain(MLA_Q_RANK),
        'mla_w_q_b': nrm((DEPTH, MLA_Q_RANK, MLA_HEADS * MLA_QK_HD), MLA_Q_RANK ** -0.5),
        'mla_kv_norm_a': gain(MLA_KV_RANK),
        'mla_w_kv_b': nrm((DEPTH, MLA_KV_RANK, MLA_HEADS * (MLA_NOPE + MLA_V)), MLA_KV_RANK ** -0.5),
        'mla_q_norm': gain(MLA_QK_HD),
        'mla_k_norm': gain(MLA_QK_HD),
        'mla_w_out': nrm((DEPTH, MLA_HEADS * MLA_V, D), (MLA_HEADS * MLA_V) ** -0.5),
        'mem_norm': gain(D),
        'mem_w_kv': nrm((DEPTH, D, 2 * MEM_W), D ** -0.5),
        'mem_q_norm': gain(MEM_HD),
        'mem_k_norm': gain(MEM_HD),
        'mem_w_out': nrm((DEPTH, MEM_W, D), MEM_W ** -0.5),
        'w_o': nrm((DEPTH, D, D), D ** -0.5),
        'ffn2_norm': gain(D),
        'ffn2_w_gu': nrm((DEPTH, D, 2 * D_FF), D ** -0.5),
        'ffn2_w_down': nrm((DEPTH, D_FF, D), D_FF ** -0.5),
    }


def reference(x_prompt, x_sample, state_dn_S, state_dn_conv, state_sc_conv, cache_mla_ckv,
              cache_mla_krope, cache_mem_k, cache_mem_v, page_table, mem_prompt,
              ffn1_norm, ffn1_w_gu, ffn1_w_down, mix_norm, w_in, dn_conv_w, dn_A_log, dn_dt_bias,
              dn_norm, dn_w_out, sc_conv_w, sc_w_out, mla_q_norm_a, mla_w_q_b, mla_kv_norm_a,
              mla_w_kv_b, mla_q_norm, mla_k_norm, mla_w_out, mem_norm, mem_w_kv, mem_q_norm,
              mem_k_norm, mem_w_out, w_o, ffn2_norm, ffn2_w_gu, ffn2_w_down):
    Bp, S, _ = x_prompt.shape
    Td = x_sample.shape[1]
    past = page_table.shape[1] * PAGE_SIZE
    pos_p = jnp.arange(S)
    pos_s = past + jnp.arange(Td)
    dt = x_prompt.dtype
    zero_S = jnp.zeros((Bp, DN_HEADS, DN_DK, DN_DV), F32)
    zero_dc = jnp.zeros((Bp, DN_CONV - 1, DN_CONV_W), dt)
    zero_sc = jnp.zeros((Bp, SC_CONV - 1, SC_WIDTH), dt)
    xp, xs = x_prompt, x_sample
    pS, pdc, psc, pckv, pkr, pmk, pmv = [], [], [], [], [], [], []
    sS, sdc, ssc, sckv, skr = [], [], [], [], []
    for l in range(DEPTH):
        w = dict(ffn1_norm=ffn1_norm[l], ffn1_w_gu=ffn1_w_gu[l], ffn1_w_down=ffn1_w_down[l],
                 mix_norm=mix_norm[l], w_in=w_in[l], dn_conv_w=dn_conv_w[l], dn_A_log=dn_A_log[l],
                 dn_dt_bias=dn_dt_bias[l], dn_norm=dn_norm[l], dn_w_out=dn_w_out[l],
                 sc_conv_w=sc_conv_w[l], sc_w_out=sc_w_out[l], mla_q_norm_a=mla_q_norm_a[l],
                 mla_w_q_b=mla_w_q_b[l], mla_kv_norm_a=mla_kv_norm_a[l], mla_q_norm=mla_q_norm[l],
                 mla_w_out=mla_w_out[l], mem_q_norm=mem_q_norm[l], mem_w_out=mem_w_out[l],
                 w_o=w_o[l], ffn2_norm=ffn2_norm[l], ffn2_w_gu=ffn2_w_gu[l], ffn2_w_down=ffn2_w_down[l])
        wkv, kn = mla_w_kv_b[l], mla_k_norm[l]
        mk, mv = _mem_kv(mem_prompt, mem_norm[l], mem_w_kv[l], mem_k_norm[l])
        xp, S_p, dc_p, sc_p, ckv_p, kr_p = _layer(
            xp, pos_p, zero_dc, zero_S, zero_sc, mk, mv,
            lambda q, c, r: _mla_prompt_attn(q, c, r, wkv, kn), w)
        pS.append(S_p); pdc.append(dc_p); psc.append(sc_p); pckv.append(ckv_p); pkr.append(kr_p)
        pmk.append(mk); pmv.append(mv)
        xs, S_s, dc_s, sc_s, ckv_s, kr_s = _layer(
            xs, pos_s, state_dn_conv[l], state_dn_S[l], state_sc_conv[l], cache_mem_k[l], cache_mem_v[l],
            lambda q, c, r: _mla_sample_attn(q, c, r, cache_mla_ckv, cache_mla_krope, l, page_table, wkv, kn), w)
        sS.append(S_s); sdc.append(dc_s); ssc.append(sc_s); sckv.append(ckv_s); skr.append(kr_s)
    p_dn_S = jnp.stack(pS)
    p_dn_conv = jnp.stack(pdc)
    p_sc_conv = jnp.stack(psc)
    p_mla_ckv = jnp.stack(pckv)
    p_mla_krope = jnp.stack(pkr)
    p_mem_k = jnp.stack(pmk)
    p_mem_v = jnp.stack(pmv)
    s_dn_S = jnp.stack(sS)
    s_dn_conv = jnp.stack(sdc)
    s_sc_conv = jnp.stack(ssc)
    s_mla_ckv = jnp.stack(sckv)
    s_mla_krope = jnp.stack(skr)
    return (xp, xs, p_dn_S, p_dn_conv, p_sc_conv, p_mla_ckv, p_mla_krope, p_mem_k, p_mem_v,
            s_dn_S, s_dn_conv, s_sc_conv, s_mla_ckv, s_mla_krope)
```

```python
import functools

import numpy as np
import jax
import jax.numpy as jnp
from jax import lax
from jax.experimental import pallas as pl
from jax.experimental.pallas import tpu as pltpu

F32 = jnp.float32
BF16 = jnp.bfloat16

D_MODEL = 1024
D_FF = 2816
EPS = 1e-6
N_MEM = 256
PAGE = 128
DN_H = 4
DN_DK = 128
DN_QK_W = 512
DN_CONV_W = 1536
DN_CONV = 4
DN_CHUNK = 64
SC_W = 512
SC_CONV = 3
MLA_H = 8
MLA_RANK = 256
MLA_NOPE = 64
MLA_ROPE = 32
MLA_V = 64
MLA_QK = 96
MLA_LANES = 128
ROPE_THETA = 10000.0
MEM_H = 4
MEM_HD = 128
MEM_W = 512

VMEM_LIMIT = 56 * 1024 * 1024


def _cparams(sem):
    return pltpu.CompilerParams(dimension_semantics=sem, vmem_limit_bytes=VMEM_LIMIT)


def _rms(x, g):
    ms = jnp.mean(x * x, axis=-1, keepdims=True)
    return x * lax.rsqrt(ms + EPS) * g


def _silu(x):
    return x * jax.nn.sigmoid(x)


def _rnd(x):
    return x.astype(BF16).astype(F32)


def _dot(a, b):
    return jnp.dot(a, b, preferred_element_type=F32)


def _dot_nt(a, b):
    return lax.dot_general(a, b, (((1,), (1,)), ((), ())), preferred_element_type=F32)


def _mm(a, b, small):
    if small:
        return _dot(_rnd(a), _rnd(b))
    return _dot(a.astype(BF16), b.astype(BF16))


def _mm_nt(a, b, small):
    if small:
        return _dot_nt(_rnd(a), _rnd(b))
    return _dot_nt(a.astype(BF16), b.astype(BF16))


def _mm_tn(a, b, small):
    dn = (((0,), (0,)), ((), ()))
    if small:
        return lax.dot_general(_rnd(a), _rnd(b), dn, preferred_element_type=F32)
    return lax.dot_general(a.astype(BF16), b.astype(BF16), dn, preferred_element_type=F32)


def _split2(x):
    hi = x.astype(BF16)
    lo = (x - hi.astype(F32)).astype(BF16)
    return hi, lo


def _mm_hi(a, b, small):
    if small:
        return jnp.dot(a, b, preferred_element_type=F32, precision=lax.Precision.HIGHEST)
    ah, al = _split2(a)
    bh, bl = _split2(b)
    return _dot(ah, bh) + (_dot(ah, bl) + _dot(al, bh))


def _mm_exact_left(lmat, b, small):
    if small:
        return jnp.dot(lmat, b, preferred_element_type=F32, precision=lax.Precision.HIGHEST)
    lb = lmat.astype(BF16)
    b1 = b.astype(BF16)
    r1 = b - b1.astype(F32)
    b2 = r1.astype(BF16)
    b3 = (r1 - b2.astype(F32)).astype(BF16)
    return _dot(lb, b1) + (_dot(lb, b2) + _dot(lb, b3))


def _merge_kernel(x_ref, y0_ref, y1_ref, y2_ref, y3_ref, wo_ref, o_ref):
    m = ((y0_ref[...] + y1_ref[...]) + y2_ref[...]) + y3_ref[...]
    o_ref[...] = x_ref[...] + _dot(m.astype(BF16), wo_ref[...])


def _merge_wo(x, ys, w_o, *, tm):
    m = x.shape[0]
    row = pl.BlockSpec((tm, D_MODEL), lambda i: (i, 0))
    return pl.pallas_call(
        _merge_kernel,
        grid=(m // tm,),
        in_specs=[row] * 5 + [pl.BlockSpec((D_MODEL, D_MODEL), lambda i: (0, 0))],
        out_specs=row,
        out_shape=jax.ShapeDtypeStruct((m, D_MODEL), F32),
        compiler_params=_cparams(("parallel",)),
        name="merge_wo",
    )(x, *ys, w_o)


def _ffn_kernel(emit_h, nj, *refs):
    it = iter(refs)
    x_ref, g_ref, wg_ref, wu_ref, wd_ref = (next(it) for _ in range(5))
    if emit_h:
        g2_ref = next(it)
    o_ref = next(it)
    if emit_h:
        h_ref = next(it)
    hs_ref, acc_ref = next(it), next(it)
    j = pl.program_id(1)

    @pl.when(j == 0)
    def _():
        hs_ref[...] = _rms(x_ref[...], g_ref[...]).astype(BF16)
        acc_ref[...] = jnp.zeros_like(acc_ref)

    h = hs_ref[...]
    gate = _dot(h, wg_ref[...])
    up = _dot(h, wu_ref[...])
    a = (_silu(gate) * up).astype(BF16)
    acc_ref[...] += _dot(a, wd_ref[...])

    @pl.when(j == nj - 1)
    def _():
        out = x_ref[...] + 0.5 * acc_ref[...]
        o_ref[...] = out
        if emit_h:
            h_ref[...] = _rms(out, g2_ref[...]).astype(BF16)


def _ffn(x, norm_g, w_gu, w_down, *, tm, h_gain=None):
    m = x.shape[0]
    tf = D_FF // 2
    nj = D_FF // tf
    row = lambda i, j: (i, 0)
    in_specs = [pl.BlockSpec((tm, D_MODEL), row),
                pl.BlockSpec((1, D_MODEL), lambda i, j: (0, 0)),
                pl.BlockSpec((D_MODEL, tf), lambda i, j: (0, j)),
                pl.BlockSpec((D_MODEL, tf), lambda i, j: (0, j + nj)),
                pl.BlockSpec((tf, D_MODEL), lambda i, j: (j, 0))]
    args = [x, norm_g, w_gu, w_gu, w_down]
    out_shape = [jax.ShapeDtypeStruct((m, D_MODEL), F32)]
    out_specs = [pl.BlockSpec((tm, D_MODEL), row)]
    if h_gain is not None:
        in_specs.append(pl.BlockSpec((1, D_MODEL), lambda i, j: (0, 0)))
        args.append(h_gain)
        out_shape.append(jax.ShapeDtypeStruct((m, D_MODEL), BF16))
        out_specs.append(pl.BlockSpec((tm, D_MODEL), row))
    res = pl.pallas_call(
        functools.partial(_ffn_kernel, h_gain is not None, nj),
        grid=(m // tm, nj),
        in_specs=in_specs, out_specs=out_specs, out_shape=out_shape,
        scratch_shapes=[pltpu.VMEM((tm, D_MODEL), BF16), pltpu.VMEM((tm, D_MODEL), F32)],
        compiler_params=_cparams(("parallel", "arbitrary")),
        name="ffn",
    )(*args)
    return res if h_gain is not None else res[0]


def _dn_kernel(bb_n, tt, chunk, tv_last, n_t, small,
               h_ref, s0_ref, cprev_ref, wqkv_ref, wz_ref, wab_ref, convw_ref, alog_ref, dtb_ref,
               normg_ref, wout_ref, wgate_ref,
               y_ref, snew_ref, cnew_ref,
               xbuf, cs, gb, zb, ob):
    t = pl.program_id(1)
    rows = bb_n * tt

    @pl.when(t == 0)
    def _():
        xbuf[:, 5:8, :] = cprev_ref[...]
        snew_ref[...] = s0_ref[...]

    h = h_ref[...]
    qkv = _dot(h, wqkv_ref[...])
    xbuf[:, 8:8 + tt, :] = qkv.reshape(bb_n, tt, DN_CONV_W)
    c = 0.0
    for j in range(DN_CONV):
        c = c + xbuf[:, 5 + j:5 + j + tt, :] * convw_ref[j:j + 1, :].reshape(1, 1, DN_CONV_W)

    @pl.when(t == n_t - 1)
    def _():
        cnew_ref[...] = xbuf[:, 5 + tv_last:8 + tv_last, :]

    if n_t > 1:
        xbuf[:, 0:8, :] = xbuf[:, tt:tt + 8, :]

    c = _silu(c)
    for grp in range(8):
        sl = slice(grp * 128, (grp + 1) * 128)
        xg = c[:, :, sl]
        xn = xg * lax.rsqrt(jnp.sum(xg * xg, axis=-1, keepdims=True) + EPS)
        if grp < DN_H:
            xn = xn * (DN_DK ** -0.5)
        cs[:, :, sl] = xn
    cs[:, :, 2 * DN_QK_W:] = c[:, :, 2 * DN_QK_W:]

    ab = _dot(h, wab_ref[...])
    g = -jnp.exp(alog_ref[...]) * jax.nn.softplus(ab + dtb_ref[...])
    lane = lax.broadcasted_iota(jnp.int32, (rows, 128), 1)
    gbv = jnp.where(lane < DN_H, g, jax.nn.sigmoid(ab)).reshape(bb_n, tt, 128)
    if tv_last < tt:
        trow = lax.broadcasted_iota(jnp.int32, (bb_n, tt, 128), 1)
        gbv = jnp.where(trow < tv_last, gbv, 0.0)
    gb[...] = gbv
    zb[...] = _silu(_dot(h, wz_ref[...])).reshape(bb_n, tt, DN_QK_W)

    ri = lax.broadcasted_iota(jnp.int32, (chunk, chunk), 0)
    ci = lax.broadcasted_iota(jnp.int32, (chunk, chunk), 1)
    incl = ci <= ri
    strict = ci < ri
    lmat = incl.astype(F32)
    umat = (ri > ci).astype(F32)
    n_pow = int(np.log2(chunk))
    normg = normg_ref[...]

    def chunk_body(k, _):
        b = k // (tt // chunk)
        r0 = pl.multiple_of((k % (tt // chunk)) * chunk, chunk)
        gbc = gb[b, pl.ds(r0, chunk), :]
        for hd in range(DN_H):
            qs = slice(hd * 128, (hd + 1) * 128)
            ks = slice(DN_QK_W + hd * 128, DN_QK_W + (hd + 1) * 128)
            vs = slice(2 * DN_QK_W + hd * 128, 2 * DN_QK_W + (hd + 1) * 128)
            q = cs[b, pl.ds(r0, chunk), qs]
            kk = cs[b, pl.ds(r0, chunk), ks]
            v = cs[b, pl.ds(r0, chunk), vs]
            gcol = gbc[:, hd:hd + 1]
            bcol = gbc[:, DN_H + hd:DN_H + hd + 1]
            g_b = jnp.broadcast_to(gcol, (chunk, 128))
            beta_b = jnp.broadcast_to(bcol, (chunk, 128))
            gc = _mm_exact_left(lmat, g_b, small)
            dmat = _mm_exact_left(lmat, jnp.broadcast_to(gcol, (chunk, chunk)) * umat, small)
            gam = jnp.where(incl, jnp.exp(dmat), 0.0)
            eg = jnp.exp(gc)
            kb = kk * beta_b
            a_mat = jnp.where(strict, _mm_nt(kb, kk, small) * gam, 0.0)
            x = jnp.concatenate([v * beta_b, kb * eg], axis=1)
            p = -a_mat
            for i in range(n_pow):
                x = x + _mm_hi(p, x, small)
                if i < n_pow - 1:
                    p = _mm_hi(p, p, small)
            u = x[:, :128]
            w = x[:, 128:]
            aqk = jnp.where(incl, _mm_nt(q, kk, small) * gam, 0.0)
            qg = q * eg
            gc_last = gc[chunk - 1:chunk, :]
            kdec = kk * jnp.exp(gc_last - gc)
            s_old = snew_ref[b, hd]
            v_new = u - _mm(w, s_old, small)
            o = _mm(qg, s_old, small) + _mm(aqk, v_new, small)
            snew_ref[b, hd] = s_old * jnp.exp(gc_last) + _mm_tn(kdec, v_new, small)
            on = _rms(o, normg) * zb[b, pl.ds(r0, chunk), qs]
            ob[b, pl.ds(r0, chunk), qs] = on
        return 0

    lax.fori_loop(0, bb_n * (tt // chunk), chunk_body, 0)

    y = _dot(ob[...].reshape(rows, DN_QK_W).astype(BF16), wout_ref[...])
    y_ref[...] = y * jax.nn.sigmoid(_dot(h, wgate_ref[...]))


def _deltanet(h, s0, cprev, wts, *, n_seq, t_pad, t_valid, bb_n, tt, chunk):
    n_t = t_pad // tt
    assert n_t == 1 or t_valid == t_pad
    tv_last = t_valid - (n_t - 1) * tt
    rows = bb_n * tt
    small = chunk < 16
    const = lambda *shape: pl.BlockSpec(shape, lambda b, t: (0,) * len(shape))
    in_specs = [
        pl.BlockSpec((rows, D_MODEL), lambda b, t: (b * n_t + t, 0)),
        pl.BlockSpec((bb_n, DN_H, DN_DK, DN_DK), lambda b, t: (b, 0, 0, 0)),
        pl.BlockSpec((bb_n, DN_CONV - 1, DN_CONV_W), lambda b, t: (b, 0, 0)),
        const(D_MODEL, DN_CONV_W), const(D_MODEL, DN_QK_W), const(D_MODEL, 128),
        const(DN_CONV, DN_CONV_W), const(1, 128), const(1, 128), const(1, 128),
        const(DN_QK_W, D_MODEL), const(D_MODEL, D_MODEL),
    ]
    out_specs = [
        pl.BlockSpec((rows, D_MODEL), lambda b, t: (b * n_t + t, 0)),
        pl.BlockSpec((bb_n, DN_H, DN_DK, DN_DK), lambda b, t: (b, 0, 0, 0)),
        pl.BlockSpec((bb_n, DN_CONV - 1, DN_CONV_W), lambda b, t: (b, 0, 0)),
    ]
    out_shape = [
        jax.ShapeDtypeStruct((n_seq * t_pad, D_MODEL), F32),
        jax.ShapeDtypeStruct((n_seq, DN_H, DN_DK, DN_DK), F32),
        jax.ShapeDtypeStruct((n_seq, DN_CONV - 1, DN_CONV_W), F32),
    ]
    return pl.pallas_call(
        functools.partial(_dn_kernel, bb_n, tt, chunk, tv_last, n_t, small),
        grid=(n_seq // bb_n, n_t),
        in_specs=in_specs, out_specs=out_specs, out_shape=out_shape,
        scratch_shapes=[pltpu.VMEM((bb_n, tt + 8, DN_CONV_W), F32), pltpu.VMEM((bb_n, tt, DN_CONV_W), F32),
                        pltpu.VMEM((bb_n, tt, 128), F32), pltpu.VMEM((bb_n, tt, DN_QK_W), F32),
                        pltpu.VMEM((bb_n, tt, DN_QK_W), F32)],
        compiler_params=_cparams(("parallel", "arbitrary")),
        name="deltanet",
    )(h, s0, cprev, *wts)


def _sc_kernel(bb_n, tt, tv_last, n_t,
               h_ref, prev_ref, win_ref, convw_ref, wout_ref, wgate_ref,
               y_ref, new_ref, ubuf):
    t = pl.program_id(1)
    rows = bb_n * tt

    @pl.when(t == 0)
    def _():
        ubuf[:, 6:8, :] = prev_ref[...]

    h = h_ref[...]
    p = _dot(h, win_ref[...])
    bgate = p[:, :SC_W]
    u = p[:, SC_W:2 * SC_W] * p[:, 2 * SC_W:]
    ubuf[:, 8:8 + tt, :] = u.reshape(bb_n, tt, SC_W)
    y = 0.0
    for j in range(SC_CONV):
        y = y + ubuf[:, 6 + j:6 + j + tt, :] * convw_ref[j:j + 1, :].reshape(1, 1, SC_W)

    @pl.when(t == n_t - 1)
    def _():
        new_ref[...] = ubuf[:, 6 + tv_last:8 + tv_last, :]

    if n_t > 1:
        ubuf[:, 0:8, :] = ubuf[:, tt:tt + 8, :]

    z = (bgate * y.reshape(rows, SC_W)).astype(BF16)
    y_ref[...] = _dot(z, wout_ref[...]) * jax.nn.sigmoid(_dot(h, wgate_ref[...]))


def _shortconv(h, prev, wts, *, n_seq, t_pad, t_valid, bb_n, tt):
    n_t = t_pad // tt
    assert n_t == 1 or t_valid == t_pad
    tv_last = t_valid - (n_t - 1) * tt
    rows = bb_n * tt
    const = lambda *shape: pl.BlockSpec(shape, lambda b, t: (0,) * len(shape))
    return pl.pallas_call(
        functools.partial(_sc_kernel, bb_n, tt, tv_last, n_t),
        grid=(n_seq // bb_n, n_t),
        in_specs=[pl.BlockSpec((rows, D_MODEL), lambda b, t: (b * n_t + t, 0)),
                  pl.BlockSpec((bb_n, SC_CONV - 1, SC_W), lambda b, t: (b, 0, 0)),
                  const(D_MODEL, 3 * SC_W), const(SC_CONV, SC_W), const(SC_W, D_MODEL),
                  const(D_MODEL, D_MODEL)],
        out_specs=[pl.BlockSpec((rows, D_MODEL), lambda b, t: (b * n_t + t, 0)),
                   pl.BlockSpec((bb_n, SC_CONV - 1, SC_W), lambda b, t: (b, 0, 0))],
        out_shape=[jax.ShapeDtypeStruct((n_seq * t_pad, D_MODEL), F32),
                   jax.ShapeDtypeStruct((n_seq, SC_CONV - 1, SC_W), F32)],
        scratch_shapes=[pltpu.VMEM((bb_n, tt + 8, SC_W), F32)],
        compiler_params=_cparams(("parallel", "arbitrary")),
        name="shortconv",
    )(h, prev, *wts)


def _memkv_kernel(m_ref, g_ref, wkv_ref, kg_ref, k_ref, v_ref):
    n = _rms(m_ref[...], g_ref[...]).astype(BF16)
    kv = _dot(n, wkv_ref[...])
    kg = kg_ref[...]
    for hd in range(MEM_H):
        sl = slice(hd * MEM_HD, (hd + 1) * MEM_HD)
        k_ref[:, sl] = _rms(kv[:, sl], kg)
    v_ref[...] = kv[:, MEM_W:]


def _mem_kv(mem2d, norm_g, w_kv, k_gain, *, tm):
    m = mem2d.shape[0]
    return pl.pallas_call(
        _memkv_kernel,
        grid=(m // tm,),
        in_specs=[pl.BlockSpec((tm, D_MODEL), lambda i: (i, 0)),
                  pl.BlockSpec((1, D_MODEL), lambda i: (0, 0)),
                  pl.BlockSpec((D_MODEL, 2 * MEM_W), lambda i: (0, 0)),
                  pl.BlockSpec((1, MEM_HD), lambda i: (0, 0))],
        out_specs=[pl.BlockSpec((tm, MEM_W), lambda i: (i, 0))] * 2,
        out_shape=[jax.ShapeDtypeStruct((m, MEM_W), F32)] * 2,
        compiler_params=_cparams(("parallel",)),
        name="mem_kv",
    )(mem2d, norm_g, w_kv, k_gain)


def _memattn_kernel(bb_n, tt, small,
                    h_ref, mk_ref, mv_ref, wq_ref, qg_ref, wout_ref, wgate_ref,
                    y_ref, qs, ob):
    rows = bb_n * tt
    h = h_ref[...]
    q = _dot(h, wq_ref[...])
    qg = qg_ref[...]
    for hd in range(MEM_H):
        sl = slice(hd * MEM_HD, (hd + 1) * MEM_HD)
        qs[:, :, sl] = _rms(q[:, sl], qg).reshape(bb_n, tt, MEM_HD)

    def seq_body(b, _):
        for hd in range(MEM_H):
            sl = slice(hd * MEM_HD, (hd + 1) * MEM_HD)
            s = _mm_nt(qs[b, :, sl], mk_ref[b, :, sl], small) * (MEM_HD ** -0.5)
            s = s - jnp.max(s, axis=-1, keepdims=True)
            e = jnp.exp(s)
            p = e / jnp.sum(e, axis=-1, keepdims=True)
            ob[b, :, sl] = _mm(p, mv_ref[b, :, sl], small)
        return 0

    lax.fori_loop(0, bb_n, seq_body, 0)
    y = _dot(ob[...].reshape(rows, MEM_W).astype(BF16), wout_ref[...])
    y_ref[...] = y * jax.nn.sigmoid(_dot(h, wgate_ref[...]))


def _mem_attn(h, mk, mv, wts, *, n_seq, t_pad, bb_n, tt):
    n_t = t_pad // tt
    rows = bb_n * tt
    const = lambda *shape: pl.BlockSpec(shape, lambda b, t: (0,) * len(shape))
    return pl.pallas_call(
        functools.partial(_memattn_kernel, bb_n, tt, tt < 16),
        grid=(n_seq // bb_n, n_t),
        in_specs=[pl.BlockSpec((rows, D_MODEL), lambda b, t: (b * n_t + t, 0)),
                  pl.BlockSpec((bb_n, N_MEM, MEM_W), lambda b, t: (b, 0, 0)),
                  pl.BlockSpec((bb_n, N_MEM, MEM_W), lambda b, t: (b, 0, 0)),
                  const(D_MODEL, MEM_W), const(1, MEM_HD), const(MEM_W, D_MODEL), const(D_MODEL, D_MODEL)],
        out_specs=pl.BlockSpec((rows, D_MODEL), lambda b, t: (b * n_t + t, 0)),
        out_shape=jax.ShapeDtypeStruct((n_seq * t_pad, D_MODEL), F32),
        scratch_shapes=[pltpu.VMEM((bb_n, tt, MEM_W), F32), pltpu.VMEM((bb_n, tt, MEM_W), F32)],
        compiler_params=_cparams(("parallel", "arbitrary")),
        name="mem_attn",
    )(h, mk, mv, *wts)


def _mlaproj_kernel(with_kv, *refs):
    it = iter(refs)
    h_ref, cos_ref, sin_ref = next(it), next(it), next(it)
    wq_ref, qna_ref, wqp_ref, wqs_ref, qg_ref = next(it), next(it), next(it), next(it), next(it)
    wkv_ref, kvna_ref, wkr_ref, wkrs_ref = next(it), next(it), next(it), next(it)
    if with_kv:
        wuk_ref, wuv_ref, kg_ref = next(it), next(it), next(it)
    q_ref, ckv_ref, kr_ref, krp_ref = next(it), next(it), next(it), next(it)
    if with_kv:
        k_ref, v_ref = next(it), next(it)

    h = h_ref[...]
    cos = cos_ref[...]
    sin = sin_ref[...]
    cqn = _rms(_dot(h, wq_ref[...]), qna_ref[...]).astype(BF16)
    q_raw = _dot(cqn, wqp_ref[...])
    q_swp = _dot(cqn, wqs_ref[...])
    qg = qg_ref[...]
    inv_n = 1.0 / MLA_QK
    for hd in range(MLA_H):
        sl = slice(hd * MLA_LANES, (hd + 1) * MLA_LANES)
        qh = q_raw[:, sl] * cos + q_swp[:, sl] * sin
        ms = jnp.sum(qh * qh, axis=-1, keepdims=True) * inv_n
        q_ref[:, sl] = (qh * lax.rsqrt(ms + EPS) * qg).astype(q_ref.dtype)

    ckv = _rms(_dot(h, wkv_ref[...]), kvna_ref[...])
    ckv_ref[...] = ckv
    krp = _dot(h, wkr_ref[...]) * cos + _dot(h, wkrs_ref[...]) * sin
    kr_ref[...] = krp[:, :MLA_ROPE]
    krp_ref[...] = krp

    if with_kv:
        cb = ckv.astype(BF16)
        k_raw = _dot(cb, wuk_ref[...])
        v_ref[...] = _dot(cb, wuv_ref[...]).astype(BF16)
        kg = kg_ref[...]
        for hd in range(MLA_H):
            sl = slice(hd * MLA_LANES, (hd + 1) * MLA_LANES)
            kh = k_raw[:, sl] + krp
            ms = jnp.sum(kh * kh, axis=-1, keepdims=True) * inv_n
            k_ref[:, sl] = (kh * lax.rsqrt(ms + EPS) * kg).astype(BF16)


def _mla_proj(h, cos, sin, wts, kv_wts, *, tm, n_tab, q_dtype):
    m = h.shape[0]
    with_kv = kv_wts is not None
    hw = MLA_H * MLA_LANES
    const = lambda *shape: pl.BlockSpec(shape, lambda i: (0,) * len(shape))
    in_specs = [pl.BlockSpec((tm, D_MODEL), lambda i: (i, 0)),
                pl.BlockSpec((tm, MLA_LANES), lambda i: (i % n_tab, 0)),
                pl.BlockSpec((tm, MLA_LANES), lambda i: (i % n_tab, 0)),
                const(D_MODEL, MLA_RANK), const(1, MLA_RANK), const(MLA_RANK, hw), const(MLA_RANK, hw),
                const(1, MLA_LANES),
                const(D_MODEL, MLA_RANK), const(1, MLA_RANK), const(D_MODEL, MLA_LANES), const(D_MODEL, MLA_LANES)]
    args = [h, cos, sin, *wts]
    out_specs = [pl.BlockSpec((tm, hw), lambda i: (i, 0)),
                 pl.BlockSpec((tm, MLA_RANK), lambda i: (i, 0)),
                 pl.BlockSpec((tm, MLA_ROPE), lambda i: (i, 0)),
                 pl.BlockSpec((tm, MLA_LANES), lambda i: (i, 0))]
    out_shape = [jax.ShapeDtypeStruct((m, hw), q_dtype),
                 jax.ShapeDtypeStruct((m, MLA_RANK), F32),
                 jax.ShapeDtypeStruct((m, MLA_ROPE), F32),
                 jax.ShapeDtypeStruct((m, MLA_LANES), F32)]
    if with_kv:
        in_specs += [const(MLA_RANK, hw), const(MLA_RANK, MLA_H * MLA_V), const(1, MLA_LANES)]
        args += list(kv_wts)
        out_specs += [pl.BlockSpec((tm, hw), lambda i: (i, 0)), pl.BlockSpec((tm, MLA_H * MLA_V), lambda i: (i, 0))]
        out_shape += [jax.ShapeDtypeStruct((m, hw), BF16), jax.ShapeDtypeStruct((m, MLA_H * MLA_V), BF16)]
    return pl.pallas_call(
        functools.partial(_mlaproj_kernel, with_kv),
        grid=(m // tm,),
        in_specs=in_specs, out_specs=out_specs, out_shape=out_shape,
        compiler_params=_cparams(("parallel",)),
        name="mla_proj",
    )(*args)


def _flash_kernel(tq, q_ref, k_ref, v_ref, o_ref, m_scr, l_scr, acc_scr):
    qi = pl.program_id(1)
    ki = pl.program_id(2)
    scale = MLA_QK ** -0.5

    @pl.when(ki == 0)
    def _():
        m_scr[...] = jnp.full(m_scr.shape, -jnp.inf, F32)
        l_scr[...] = jnp.zeros_like(l_scr)
        acc_scr[...] = jnp.zeros_like(acc_scr)

    def compute(diag):
        if diag:
            row = lax.broadcasted_iota(jnp.int32, (tq, tq), 0)
            col = lax.broadcasted_iota(jnp.int32, (tq, tq), 1)
            keep = col <= row
        for hd in range(MLA_H):
            sl = slice(hd * MLA_LANES, (hd + 1) * MLA_LANES)
            s = _dot_nt(q_ref[:, sl], k_ref[:, sl]) * scale
            if diag:
                s = jnp.where(keep, s, -jnp.inf)
            m_old = m_scr[hd]
            m_new = jnp.maximum(m_old, jnp.max(s, axis=-1, keepdims=True))
            alpha = jnp.exp(m_old - m_new)
            p = jnp.exp(s - m_new)
            l_scr[hd] = alpha * l_scr[hd] + jnp.sum(p, axis=-1, keepdims=True)
            pair = hd // 2
            pv = _dot(p.astype(BF16), v_ref[:, pair * 128:(pair + 1) * 128])
            acc_scr[hd] = alpha * acc_scr[hd] + pv
            m_scr[hd] = m_new

    @pl.when(ki < qi)
    def _():
        compute(False)

    @pl.when(ki == qi)
    def _():
        compute(True)
        lane = lax.broadcasted_iota(jnp.int32, (tq, 128), 1)
        for pair in range(MLA_H // 2):
            even = acc_scr[2 * pair] / l_scr[2 * pair]
            odd = acc_scr[2 * pair + 1] / l_scr[2 * pair + 1]
            o_ref[:, pair * 128:(pair + 1) * 128] = jnp.where(lane < MLA_V, even, odd).astype(o_ref.dtype)


def _mla_prompt_attn(q, k, v, *, n_seq, seq, tq):
    nq = seq // tq
    hw = MLA_H * MLA_LANES
    vw = MLA_H * MLA_V
    return pl.pallas_call(
        functools.partial(_flash_kernel, tq),
        grid=(n_seq, nq, nq),
        in_specs=[pl.BlockSpec((tq, hw), lambda b, i, j: (b * nq + i, 0)),
                  pl.BlockSpec((tq, hw), lambda b, i, j: (b * nq + jnp.minimum(i, j), 0)),
                  pl.BlockSpec((tq, vw), lambda b, i, j: (b * nq + jnp.minimum(i, j), 0))],
        out_specs=pl.BlockSpec((tq, vw), lambda b, i, j: (b * nq + i, 0)),
        out_shape=jax.ShapeDtypeStruct((n_seq * seq, vw), BF16),
        scratch_shapes=[pltpu.VMEM((MLA_H, tq, 1), F32), pltpu.VMEM((MLA_H, tq, 1), F32),
                        pltpu.VMEM((MLA_H, tq, 128), F32)],
        compiler_params=_cparams(("parallel", "parallel", "arbitrary")),
        name="mla_flash",
    )(q, k, v)


def _mla_sample_kernel(n_pg, n_steps, t_valid, *refs):
    it = iter(refs)
    _pt_ref = next(it)
    q_ref, cnew_ref, krnew_ref = next(it), next(it), next(it)
    wukp_ref, wukt_ref, wuvs_ref, kg_ref = next(it), next(it), next(it), next(it)
    c_refs = [next(it) for _ in range(n_pg)]
    r_refs = [next(it) for _ in range(n_pg)]
    o_ref = next(it)
    qabs, qrope, cbuf, rbuf, m_scr, l_scr, acc_scr = (next(it) for _ in range(7))
    st = pl.program_id(1)
    scale = MLA_QK ** -0.5
    inv_n = 1.0 / MLA_QK
    tk = n_pg * PAGE
    nq = 8

    @pl.when(st == 0)
    def _():
        m_scr[...] = jnp.full(m_scr.shape, -jnp.inf, F32)
        l_scr[...] = jnp.zeros_like(l_scr)
        acc_scr[...] = jnp.zeros_like(acc_scr)
        kg = kg_ref[...]
        for hd in range(MLA_H):
            sl = slice(hd * MLA_LANES, (hd + 1) * MLA_LANES)
            qk = q_ref[:, sl] * kg
            qabs[hd * nq:(hd + 1) * nq, :] = lax.dot_general(
                qk, wukp_ref[:, sl].astype(F32), (((1,), (1,)), ((), ())), preferred_element_type=F32,
                precision=lax.Precision.HIGHEST)
            qrope[hd * nq:(hd + 1) * nq, :] = qk[:, :MLA_ROPE]

    for i in range(n_pg):
        cbuf[i * PAGE:(i + 1) * PAGE, :] = c_refs[i][...].astype(BF16)
        rbuf[i * PAGE:(i + 1) * PAGE, :] = r_refs[i][...]

    cb = cbuf[...]
    kr = rbuf[...]
    knt = _dot_nt(wukt_ref[...], cb)
    ssq = jnp.sum((knt * knt).reshape(MLA_H, MLA_NOPE, tk), axis=1)
    ones = jnp.ones((16, MLA_ROPE), BF16)
    kr2 = kr * kr
    kr2a = kr2.astype(BF16)
    rem = kr2 - kr2a.astype(F32)
    kr2b = rem.astype(BF16)
    kr2c = (rem - kr2b.astype(F32)).astype(BF16)
    ssq_r = (_dot_nt(ones, kr2a) + (_dot_nt(ones, kr2b) + _dot_nt(ones, kr2c)))[:MLA_H]
    rs = lax.rsqrt((ssq + ssq_r) * inv_n + EPS)
    s = _dot_nt(qabs[...].astype(BF16), cb) + _dot_nt(qrope[...].astype(BF16), kr.astype(BF16))
    s = jnp.concatenate([s[hd * nq:(hd + 1) * nq, :] * rs[hd:hd + 1, :] for hd in range(MLA_H)], axis=0) * scale
    m_old = m_scr[...]
    m_new = jnp.maximum(m_old, jnp.max(s, axis=-1, keepdims=True))
    alpha = jnp.exp(m_old - m_new)
    p = jnp.exp(s - m_new)
    l_scr[...] = alpha * l_scr[...] + jnp.sum(p, axis=-1, keepdims=True)
    acc_scr[...] = alpha * acc_scr[...] + _dot(p.astype(BF16), cb)
    m_scr[...] = m_new

    @pl.when(st == n_steps - 1)
    def _():
        cn = cnew_ref[...]
        krp = krnew_ref[...]
        kn = _dot(_rnd(cn), wukp_ref[...].astype(F32))
        row = lax.broadcasted_iota(jnp.int32, (nq, nq), 0)
        col = lax.broadcasted_iota(jnp.int32, (nq, nq), 1)
        keep = (col <= row) & (col < t_valid)
        o = jnp.zeros((nq, MLA_H * MLA_V), F32)
        for hd in range(MLA_H):
            sl = slice(hd * MLA_LANES, (hd + 1) * MLA_LANES)
            rows = slice(hd * nq, (hd + 1) * nq)
            kh = kn[:, sl] + krp
            khn = kh * lax.rsqrt(jnp.sum(kh * kh, axis=-1, keepdims=True) * inv_n + EPS)
            sn = _dot_nt(_rnd(q_ref[:, sl] * kg_ref[...]), _rnd(khn)) * scale
            sn = jnp.where(keep, sn, -jnp.inf)
            m_o = m_scr[rows, :]
            m_n = jnp.maximum(m_o, jnp.max(sn, axis=-1, keepdims=True))
            al = jnp.exp(m_o - m_n)
            pn = jnp.exp(sn - m_n)
            l_f = al * l_scr[rows, :] + jnp.sum(pn, axis=-1, keepdims=True)
            pc = (al * acc_scr[rows, :] + _dot(_rnd(pn), _rnd(cn))) / l_f
            o = o + _dot(_rnd(pc), wuvs_ref[hd].astype(F32))
        o_ref[...] = o.astype(o_ref.dtype)


def _mla_sample_attn(page_table, q, c_new, kr_new, wts, ckv_pool, kr_pool, layer, *, n_seq, t_valid, n_pg):
    n_pages = page_table.shape[1]
    n_steps = n_pages // n_pg
    hw = MLA_H * MLA_LANES
    vw = MLA_H * MLA_V
    tk = n_pg * PAGE
    nq = 8

    def page_spec(width, i):
        return pl.BlockSpec((None, None, PAGE, width),
                            lambda b, s, pt: (layer, pt[b, s * n_pg + i], 0, 0))

    const = lambda *shape: pl.BlockSpec(shape, lambda b, s, pt: (0,) * len(shape))
    in_specs = [pl.BlockSpec((nq, hw), lambda b, s, pt: (b, 0)),
                pl.BlockSpec((nq, MLA_RANK), lambda b, s, pt: (b, 0)),
                pl.BlockSpec((nq, MLA_LANES), lambda b, s, pt: (b, 0)),
                const(MLA_RANK, hw), const(MLA_H * MLA_NOPE, MLA_RANK), const(MLA_H, MLA_RANK, vw),
                const(1, MLA_LANES)]
    in_specs += [page_spec(MLA_RANK, i) for i in range(n_pg)]
    in_specs += [page_spec(MLA_ROPE, i) for i in range(n_pg)]
    grid_spec = pltpu.PrefetchScalarGridSpec(
        num_scalar_prefetch=1,
        grid=(n_seq, n_steps),
        in_specs=in_specs,
        out_specs=pl.BlockSpec((nq, vw), lambda b, s, pt: (b, 0)),
        scratch_shapes=[pltpu.VMEM((MLA_H * nq, MLA_RANK), F32), pltpu.VMEM((MLA_H * nq, MLA_ROPE), F32),
                        pltpu.VMEM((tk, MLA_RANK), BF16), pltpu.VMEM((tk, MLA_ROPE), F32),
                        pltpu.VMEM((MLA_H * nq, 1), F32), pltpu.VMEM((MLA_H * nq, 1), F32),
                        pltpu.VMEM((MLA_H * nq, MLA_RANK), F32)],
    )
    return pl.pallas_call(
        functools.partial(_mla_sample_kernel, n_pg, n_steps, t_valid),
        grid_spec=grid_spec,
        out_shape=jax.ShapeDtypeStruct((n_seq * nq, vw), F32),
        compiler_params=_cparams(("parallel", "arbitrary")),
        name="mla_paged",
    )(page_table, q, c_new, kr_new, *wts, *([ckv_pool] * n_pg), *([kr_pool] * n_pg))


def _projgate_kernel(o_ref, h_ref, wout_ref, wgate_ref, y_ref):
    y_ref[...] = _dot(o_ref[...].astype(BF16), wout_ref[...]) * jax.nn.sigmoid(_dot(h_ref[...], wgate_ref[...]))


def _proj_gate(o, h, w_out, w_gate, *, tm):
    m, kdim = o.shape
    return pl.pallas_call(
        _projgate_kernel,
        grid=(m // tm,),
        in_specs=[pl.BlockSpec((tm, kdim), lambda i: (i, 0)),
                  pl.BlockSpec((tm, D_MODEL), lambda i: (i, 0)),
                  pl.BlockSpec((kdim, D_MODEL), lambda i: (0, 0)),
                  pl.BlockSpec((D_MODEL, D_MODEL), lambda i: (0, 0))],
        out_specs=pl.BlockSpec((tm, D_MODEL), lambda i: (i, 0)),
        out_shape=jax.ShapeDtypeStruct((m, D_MODEL), F32),
        compiler_params=_cparams(("parallel",)),
        name="proj_gate",
    )(o, h, w_out, w_gate)


def _pad_lanes(x, width):
    return jnp.pad(x, [(0, 0)] * (x.ndim - 1) + [(0, width - x.shape[-1])])


def _mla_head_layout(nope, r1, r2):
    z = jnp.zeros(nope.shape[:-1] + (MLA_LANES - MLA_QK,), nope.dtype)
    x = jnp.concatenate([r1, r2, nope, z], axis=-1)
    return x.reshape(x.shape[:-2] + (MLA_H * MLA_LANES,))


def _gain_layout(g):
    half = MLA_ROPE // 2
    return jnp.concatenate([g[MLA_NOPE:MLA_NOPE + half], g[MLA_NOPE + half:], g[:MLA_NOPE],
                            jnp.zeros((MLA_LANES - MLA_QK,), g.dtype)]).reshape(1, MLA_LANES)


def _rope_tables(pos):
    half = MLA_ROPE // 2
    inv = ROPE_THETA ** (-jnp.arange(half, dtype=F32) / half)
    ang = pos.astype(F32)[:, None] * inv
    cos, sin = jnp.cos(ang), jnp.sin(ang)
    n = pos.shape[0]
    cos_t = jnp.concatenate([cos, cos, jnp.ones((n, MLA_NOPE), F32), jnp.zeros((n, MLA_LANES - MLA_QK), F32)], -1)
    sin_t = jnp.concatenate([sin, sin, jnp.zeros((n, MLA_LANES - MLA_ROPE), F32)], -1)
    return cos_t, sin_t


def _layer_weights(l, p):
    w_in = p['w_in'][l]
    sizes = (DN_CONV_W, DN_QK_W, DN_H, DN_H, SC_W, SC_W, SC_W, MLA_RANK, MLA_RANK, MLA_ROPE, MEM_W, 4 * D_MODEL)
    offs = np.concatenate([[0], np.cumsum(sizes)])
    seg = [w_in[:, offs[i]:offs[i + 1]] for i in range(len(sizes))]
    bf = lambda x: x.astype(BF16)
    row = lambda x: x.reshape(1, -1)
    gates = [bf(seg[11][:, i * D_MODEL:(i + 1) * D_MODEL]) for i in range(4)]
    half = MLA_ROPE // 2

    w = {}
    w['ffn1'] = (row(p['ffn1_norm'][l]), bf(p['ffn1_w_gu'][l]), bf(p['ffn1_w_down'][l]))
    w['ffn2'] = (row(p['ffn2_norm'][l]), bf(p['ffn2_w_gu'][l]), bf(p['ffn2_w_down'][l]))
    w['mix_norm'] = row(p['mix_norm'][l])
    w['w_o'] = bf(p['w_o'][l])
    w['dn'] = (bf(seg[0]), bf(seg[1]), bf(_pad_lanes(jnp.concatenate([seg[2], seg[3]], 1), 128)),
               p['dn_conv_w'][l], _pad_lanes(row(p['dn_A_log'][l]), 128), _pad_lanes(row(p['dn_dt_bias'][l]), 128),
               row(p['dn_norm'][l]), bf(p['dn_w_out'][l]), gates[0])
    w['sc'] = (bf(jnp.concatenate([seg[4], seg[5], seg[6]], 1)), p['sc_conv_w'][l], bf(p['sc_w_out'][l]), gates[1])

    wq = p['mla_w_q_b'][l].reshape(MLA_RANK, MLA_H, MLA_QK)
    q_nope, q_r1, q_r2 = wq[..., :MLA_NOPE], wq[..., MLA_NOPE:MLA_NOPE + half], wq[..., MLA_NOPE + half:]
    wq_perm = _mla_head_layout(q_nope, q_r1, q_r2)
    wq_swap = _mla_head_layout(jnp.zeros_like(q_nope), -q_r2, q_r1)
    wkr = seg[9]
    wkr_pad = _pad_lanes(wkr, MLA_LANES)
    wkr_swap = _pad_lanes(jnp.concatenate([-wkr[:, half:], wkr[:, :half]], 1), MLA_LANES)
    w['mla_proj'] = (bf(seg[7]), row(p['mla_q_norm_a'][l]), bf(wq_perm), bf(wq_swap), _gain_layout(p['mla_q_norm'][l]),
                     bf(seg[8]), row(p['mla_kv_norm_a'][l]), bf(wkr_pad), bf(wkr_swap))
    wkv = p['mla_w_kv_b'][l].reshape(MLA_RANK, MLA_H, MLA_NOPE + MLA_V)
    w_uk, w_uv = wkv[..., :MLA_NOPE], wkv[..., MLA_NOPE:]
    zr = jnp.zeros((MLA_RANK, MLA_H, half), F32)
    wuk_perm = bf(_mla_head_layout(w_uk, zr, zr))
    k_gain = _gain_layout(p['mla_k_norm'][l])
    w['mla_kv'] = (wuk_perm, bf(w_uv.reshape(MLA_RANK, MLA_H * MLA_V)), k_gain)
    eye = jnp.eye(MLA_H, dtype=F32)
    wuv_sel = (w_uv[None] * eye[:, None, :, None]).reshape(MLA_H, MLA_RANK, MLA_H * MLA_V)
    w['mla_sample'] = (wuk_perm, bf(w_uk.reshape(MLA_RANK, MLA_H * MLA_NOPE).T), bf(wuv_sel), k_gain)
    w['mla_out'] = (bf(p['mla_w_out'][l]), gates[2])
    w['mem_kv'] = (row(p['mem_norm'][l]), bf(p['mem_w_kv'][l]), row(p['mem_k_norm'][l]))
    w['mem'] = (bf(seg[10]), row(p['mem_q_norm'][l]), bf(p['mem_w_out'][l]), gates[3])
    return w


def _group_layer(x, w, *, n_seq, t_pad, t_valid, tm, bb_n, tt, chunk, dn_state, sc_state, mem_kv, cos, sin,
                 n_tab, mla_attend, q_dtype):
    cfg = dict(n_seq=n_seq, t_pad=t_pad, bb_n=bb_n, tt=tt)
    x1, h = _ffn(x, *w['ffn1'], tm=tm, h_gain=w['mix_norm'])
    y_dn, dn_s, dn_c = _deltanet(h, dn_state[0], dn_state[1], w['dn'], t_valid=t_valid, chunk=chunk, **cfg)
    y_sc, sc_c = _shortconv(h, sc_state, w['sc'], t_valid=t_valid, **cfg)
    proj = _mla_proj(h, cos, sin, w['mla_proj'], w['mla_kv'] if mla_attend is None else None,
                     tm=tm, n_tab=n_tab, q_dtype=q_dtype)
    q, ckv, kr, krp = proj[:4]
    if mla_attend is None:
        o = _mla_prompt_attn(q, proj[4], proj[5], n_seq=n_seq, seq=t_pad, tq=tm)
    else:
        o = mla_attend(q, ckv, krp)
    y_mla = _proj_gate(o, h, *w['mla_out'], tm=tm)
    y_mem = _mem_attn(h, mem_kv[0], mem_kv[1], w['mem'], **cfg)
    x2 = _merge_wo(x1, (y_dn, y_sc, y_mla, y_mem), w['w_o'], tm=tm)
    x3 = _ffn(x2, *w['ffn2'], tm=tm)
    return x3, dn_s, dn_c, sc_c, ckv, kr


def kernel(x_prompt, x_sample, state_dn_S, state_dn_conv, state_sc_conv, cache_mla_ckv, cache_mla_krope, cache_mem_k, cache_mem_v, page_table, mem_prompt, ffn1_norm, ffn1_w_gu, ffn1_w_down, mix_norm, w_in, dn_conv_w, dn_A_log, dn_dt_bias, dn_norm, dn_w_out, sc_conv_w, sc_w_out, mla_q_norm_a, mla_w_q_b, mla_kv_norm_a, mla_w_kv_b, mla_q_norm, mla_k_norm, mla_w_out, mem_norm, mem_w_kv, mem_q_norm, mem_k_norm, mem_w_out, w_o, ffn2_norm, ffn2_w_gu, ffn2_w_down):
    params = dict(ffn1_norm=ffn1_norm, ffn1_w_gu=ffn1_w_gu, ffn1_w_down=ffn1_w_down, mix_norm=mix_norm, w_in=w_in,
                  dn_conv_w=dn_conv_w, dn_A_log=dn_A_log, dn_dt_bias=dn_dt_bias, dn_norm=dn_norm, dn_w_out=dn_w_out,
                  sc_conv_w=sc_conv_w, sc_w_out=sc_w_out, mla_q_norm_a=mla_q_norm_a, mla_w_q_b=mla_w_q_b,
                  mla_kv_norm_a=mla_kv_norm_a, mla_w_kv_b=mla_w_kv_b, mla_q_norm=mla_q_norm, mla_k_norm=mla_k_norm,
                  mla_w_out=mla_w_out, mem_norm=mem_norm, mem_w_kv=mem_w_kv, mem_q_norm=mem_q_norm,
                  mem_k_norm=mem_k_norm, mem_w_out=mem_w_out, w_o=w_o, ffn2_norm=ffn2_norm, ffn2_w_gu=ffn2_w_gu,
                  ffn2_w_down=ffn2_w_down)
    depth = w_in.shape[0]
    bp, seq, _ = x_prompt.shape
    bs, td, _ = x_sample.shape
    tds = 8
    n_pages = page_table.shape[1]
    past = n_pages * PAGE

    cos_p, sin_p = _rope_tables(jnp.arange(seq))
    cos_s, sin_s = _rope_tables(past + jnp.arange(tds))
    cos_s, sin_s = jnp.tile(cos_s, (bs, 1)), jnp.tile(sin_s, (bs, 1))

    xp = x_prompt.reshape(bp * seq, D_MODEL)
    xs = jnp.pad(x_sample, ((0, 0), (0, tds - td), (0, 0))).reshape(bs * tds, D_MODEL)
    zero_s = jnp.zeros((bp, DN_H, DN_DK, DN_DK), F32)
    zero_dc = jnp.zeros((bp, DN_CONV - 1, DN_CONV_W), F32)
    zero_sc = jnp.zeros((bp, SC_CONV - 1, SC_W), F32)
    mem2d = mem_prompt.reshape(bp * N_MEM, D_MODEL)

    outs = {k: [] for k in ('pS', 'pdc', 'psc', 'pckv', 'pkr', 'pmk', 'pmv', 'sS', 'sdc', 'ssc', 'sckv', 'skr')}
    tm_p = 512
    for l in range(depth):
        w = _layer_weights(l, params)
        mk, mv = _mem_kv(mem2d, *w['mem_kv'], tm=tm_p)
        mk3, mv3 = mk.reshape(bp, N_MEM, MEM_W), mv.reshape(bp, N_MEM, MEM_W)
        xp, s_p, dc_p, sc_p, ckv_p, kr_p = _group_layer(
            xp, w, n_seq=bp, t_pad=seq, t_valid=seq, tm=tm_p, bb_n=1, tt=tm_p, chunk=DN_CHUNK,
            dn_state=(zero_s, zero_dc), sc_state=zero_sc, mem_kv=(mk3, mv3), cos=cos_p, sin=sin_p,
            n_tab=seq // tm_p, mla_attend=None, q_dtype=BF16)
        outs['pS'].append(s_p); outs['pdc'].append(dc_p); outs['psc'].append(sc_p)
        outs['pckv'].append(ckv_p.reshape(bp, seq, MLA_RANK)); outs['pkr'].append(kr_p.reshape(bp, seq, MLA_ROPE))
        outs['pmk'].append(mk.reshape(bp, N_MEM, MEM_H, MEM_HD)); outs['pmv'].append(mv.reshape(bp, N_MEM, MEM_H, MEM_HD))

        def attend(q, ckv, krp, l=l, w=w):
            return _mla_sample_attn(page_table, q, ckv, krp, w['mla_sample'], cache_mla_ckv, cache_mla_krope, l,
                                    n_seq=bs, t_valid=td, n_pg=8)

        xs, s_s, dc_s, sc_s, ckv_s, kr_s = _group_layer(
            xs, w, n_seq=bs, t_pad=tds, t_valid=td, tm=bs * tds, bb_n=8, tt=tds, chunk=tds,
            dn_state=(state_dn_S[l], state_dn_conv[l]), sc_state=state_sc_conv[l],
            mem_kv=(cache_mem_k[l].reshape(bs, N_MEM, MEM_W), cache_mem_v[l].reshape(bs, N_MEM, MEM_W)),
            cos=cos_s, sin=sin_s, n_tab=1, mla_attend=attend, q_dtype=F32)
        outs['sS'].append(s_s); outs['sdc'].append(dc_s); outs['ssc'].append(sc_s)
        outs['sckv'].append(ckv_s.reshape(bs, tds, MLA_RANK)[:, :td])
        outs['skr'].append(kr_s.reshape(bs, tds, MLA_ROPE)[:, :td])

    st = lambda k: jnp.stack(outs[k])
    y_prompt = xp.reshape(bp, seq, D_MODEL)
    y_sample = xs.reshape(bs, tds, D_MODEL)[:, :td]
    return (y_prompt, y_sample, st('pS'), st('pdc'), st('psc'), st('pckv'), st('pkr'), st('pmk'), st('pmv'),
            st('sS'), st('sdc'), st('ssc'), st('sckv'), st('skr'))
```

```python
import functools

import numpy as np
import jax
import jax.numpy as jnp
from jax import lax
from jax.experimental import pallas as pl
from jax.experimental.pallas import tpu as pltpu

F32 = jnp.float32
BF16 = jnp.bfloat16

D_MODEL = 1024
D_FF = 2816
EPS = 1e-6
N_MEM = 256
PAGE = 128
DN_H = 4
DN_DK = 128
DN_QK_W = 512
DN_CONV_W = 1536
DN_CONV = 4
DN_CHUNK = 64
SC_W = 512
SC_CONV = 3
MLA_H = 8
MLA_RANK = 256
MLA_NOPE = 64
MLA_ROPE = 32
MLA_V = 64
MLA_QK = 96
MLA_LANES = 128
ROPE_THETA = 10000.0
LOG2E = 1.4426950408889634
MEM_H = 4
MEM_HD = 128
MEM_W = 512

VMEM_LIMIT = 56 * 1024 * 1024


def _cparams(sem):
    return pltpu.CompilerParams(dimension_semantics=sem, vmem_limit_bytes=VMEM_LIMIT)


def _rms(x, g):
    ms = jnp.mean(x * x, axis=-1, keepdims=True)
    return x * lax.rsqrt(ms + EPS) * g


def _silu(x):
    return x * jax.nn.sigmoid(x)


def _rnd(x):
    return x.astype(BF16).astype(F32)


def _dot(a, b):
    return jnp.dot(a, b, preferred_element_type=F32)


def _dot_nt(a, b):
    return lax.dot_general(a, b, (((1,), (1,)), ((), ())), preferred_element_type=F32)


def _mm(a, b, small):
    if small:
        return _dot(_rnd(a), _rnd(b))
    return _dot(a.astype(BF16), b.astype(BF16))


def _mm_nt(a, b, small):
    if small:
        return _dot_nt(_rnd(a), _rnd(b))
    return _dot_nt(a.astype(BF16), b.astype(BF16))


def _mm_tn(a, b, small):
    dn = (((0,), (0,)), ((), ()))
    if small:
        return lax.dot_general(_rnd(a), _rnd(b), dn, preferred_element_type=F32)
    return lax.dot_general(a.astype(BF16), b.astype(BF16), dn, preferred_element_type=F32)


def _split2(x):
    hi = x.astype(BF16)
    lo = (x - hi.astype(F32)).astype(BF16)
    return hi, lo


def _mm_hi(a, b, small):
    if small:
        return jnp.dot(a, b, preferred_element_type=F32, precision=lax.Precision.HIGHEST)
    ah, al = _split2(a)
    bh, bl = _split2(b)
    return _dot(ah, bh) + (_dot(ah, bl) + _dot(al, bh))


def _mm_exact_left(lmat, b, small):
    if small:
        return jnp.dot(lmat, b, preferred_element_type=F32, precision=lax.Precision.HIGHEST)
    lb = lmat.astype(BF16)
    b1 = b.astype(BF16)
    r1 = b - b1.astype(F32)
    b2 = r1.astype(BF16)
    b3 = (r1 - b2.astype(F32)).astype(BF16)
    return _dot(lb, b1) + (_dot(lb, b2) + _dot(lb, b3))


def _merge_kernel(x_ref, y0_ref, y1_ref, y2_ref, y3_ref, wo_ref, o_ref):
    m = ((y0_ref[...] + y1_ref[...]) + y2_ref[...]) + y3_ref[...]
    o_ref[...] = x_ref[...] + _dot(m.astype(BF16), wo_ref[...])


def _merge_wo(x, ys, w_o, *, tm):
    m = x.shape[0]
    row = pl.BlockSpec((tm, D_MODEL), lambda i: (i, 0))
    return pl.pallas_call(
        _merge_kernel,
        grid=(m // tm,),
        in_specs=[row] * 5 + [pl.BlockSpec((D_MODEL, D_MODEL), lambda i: (0, 0))],
        out_specs=row,
        out_shape=jax.ShapeDtypeStruct((m, D_MODEL), F32),
        compiler_params=_cparams(("parallel",)),
        name="merge_wo",
    )(x, *ys, w_o)


def _ffn_kernel(emit_h, nj, *refs):
    it = iter(refs)
    x_ref, g_ref, wg_ref, wu_ref, wd_ref = (next(it) for _ in range(5))
    if emit_h:
        g2_ref = next(it)
    o_ref = next(it)
    if emit_h:
        h_ref = next(it)
    hs_ref, acc_ref = next(it), next(it)
    j = pl.program_id(1)

    @pl.when(j == 0)
    def _():
        hs_ref[...] = _rms(x_ref[...], g_ref[...]).astype(BF16)
        acc_ref[...] = jnp.zeros_like(acc_ref)

    h = hs_ref[...]
    gate = _dot(h, wg_ref[...])
    up = _dot(h, wu_ref[...])
    a = (_silu(gate) * up).astype(BF16)
    acc_ref[...] += _dot(a, wd_ref[...])

    @pl.when(j == nj - 1)
    def _():
        out = x_ref[...] + 0.5 * acc_ref[...]
        o_ref[...] = out
        if emit_h:
            h_ref[...] = _rms(out, g2_ref[...]).astype(BF16)


def _ffn(x, norm_g, w_gu, w_down, *, tm, h_gain=None):
    m = x.shape[0]
    tf = D_FF // 2
    nj = D_FF // tf
    row = lambda i, j: (i, 0)
    in_specs = [pl.BlockSpec((tm, D_MODEL), row),
                pl.BlockSpec((1, D_MODEL), lambda i, j: (0, 0)),
                pl.BlockSpec((D_MODEL, tf), lambda i, j: (0, j)),
                pl.BlockSpec((D_MODEL, tf), lambda i, j: (0, j + nj)),
                pl.BlockSpec((tf, D_MODEL), lambda i, j: (j, 0))]
    args = [x, norm_g, w_gu, w_gu, w_down]
    out_shape = [jax.ShapeDtypeStruct((m, D_MODEL), F32)]
    out_specs = [pl.BlockSpec((tm, D_MODEL), row)]
    if h_gain is not None:
        in_specs.append(pl.BlockSpec((1, D_MODEL), lambda i, j: (0, 0)))
        args.append(h_gain)
        out_shape.append(jax.ShapeDtypeStruct((m, D_MODEL), BF16))
        out_specs.append(pl.BlockSpec((tm, D_MODEL), row))
    res = pl.pallas_call(
        functools.partial(_ffn_kernel, h_gain is not None, nj),
        grid=(m // tm, nj),
        in_specs=in_specs, out_specs=out_specs, out_shape=out_shape,
        scratch_shapes=[pltpu.VMEM((tm, D_MODEL), BF16), pltpu.VMEM((tm, D_MODEL), F32)],
        compiler_params=_cparams(("parallel", "arbitrary")),
        name="ffn",
    )(*args)
    return res if h_gain is not None else res[0]


def _dn_kernel(bb_n, tt, chunk, tv_last, n_t, small,
               h_ref, s0_ref, cprev_ref, wqkv_ref, wz_ref, wab_ref, convw_ref, alog_ref, dtb_ref,
               normg_ref, wout_ref, wgate_ref,
               y_ref, snew_ref, cnew_ref,
               xbuf, cs, gb, zb, ob, s_all, u_s, w_s, qg_s, kdec_s, aqk_s, gl_s):
    t = pl.program_id(1)
    rows = bb_n * tt

    @pl.when(t == 0)
    def _():
        xbuf[:, 5:8, :] = cprev_ref[...]
        snew_ref[...] = s0_ref[...]

    h = h_ref[...]
    qkv = _dot(h, wqkv_ref[...])
    xbuf[:, 8:8 + tt, :] = qkv.reshape(bb_n, tt, DN_CONV_W)
    c = 0.0
    for j in range(DN_CONV):
        c = c + xbuf[:, 5 + j:5 + j + tt, :] * convw_ref[j:j + 1, :].reshape(1, 1, DN_CONV_W)

    @pl.when(t == n_t - 1)
    def _():
        cnew_ref[...] = xbuf[:, 5 + tv_last:8 + tv_last, :]

    if n_t > 1:
        xbuf[:, 0:8, :] = xbuf[:, tt:tt + 8, :]

    c = _silu(c)
    for grp in range(8):
        sl = slice(grp * 128, (grp + 1) * 128)
        xg = c[:, :, sl]
        xn = xg * lax.rsqrt(jnp.sum(xg * xg, axis=-1, keepdims=True) + EPS)
        if grp < DN_H:
            xn = xn * (DN_DK ** -0.5)
        cs[:, :, sl] = xn
    cs[:, :, 2 * DN_QK_W:] = c[:, :, 2 * DN_QK_W:]

    ab = _dot(h, wab_ref[...])
    g = -jnp.exp(alog_ref[...]) * jax.nn.softplus(ab + dtb_ref[...])
    lane = lax.broadcasted_iota(jnp.int32, (rows, 128), 1)
    gbv = jnp.where(lane < DN_H, g, jax.nn.sigmoid(ab)).reshape(bb_n, tt, 128)
    if tv_last < tt:
        trow = lax.broadcasted_iota(jnp.int32, (bb_n, tt, 128), 1)
        gbv = jnp.where(trow < tv_last, gbv, 0.0)
    gb[...] = gbv
    zb[...] = _silu(_dot(h, wz_ref[...])).reshape(bb_n, tt, DN_QK_W)

    n4 = DN_H * chunk
    ri = lax.broadcasted_iota(jnp.int32, (n4, n4), 0)
    ci = lax.broadcasted_iota(jnp.int32, (n4, n4), 1)
    same = (ri // chunk) == (ci // chunk)
    incl = same & (ci <= ri)
    strict = same & (ci < ri)
    lmat = incl.astype(F32)
    umat = strict.astype(F32)
    vmask = (lax.broadcasted_iota(jnp.int32, (n4, DN_H * 128), 0) // chunk
             == lax.broadcasted_iota(jnp.int32, (n4, DN_H * 128), 1) // 128)
    n_pow = int(np.log2(chunk))
    normg = normg_ref[...]
    n_ch = tt // chunk
    total = bb_n * n_ch
    solve_mm = _mm_hi if small else _mm

    for hd in range(DN_H):
        s_all[:, :, hd * 128:(hd + 1) * 128] = snew_ref[:, hd]

    def stack_rows(ref, b, r0, off):
        return jnp.concatenate([ref[b, pl.ds(r0, chunk), off + hd * 128:off + (hd + 1) * 128]
                                for hd in range(DN_H)], axis=0)

    def level1(k, slot):
        b = k // n_ch
        r0 = pl.multiple_of((k % n_ch) * chunk, chunk)
        q = stack_rows(cs, b, r0, 0)
        kk = stack_rows(cs, b, r0, DN_QK_W)
        v = stack_rows(cs, b, r0, 2 * DN_QK_W)
        gbc = gb[b, pl.ds(r0, chunk), :]
        g_st = jnp.concatenate([jnp.broadcast_to(gbc[:, hd:hd + 1], (chunk, 128)) for hd in range(DN_H)], axis=0)
        beta_st = jnp.concatenate([jnp.broadcast_to(gbc[:, DN_H + hd:DN_H + hd + 1], (chunk, 128))
                                   for hd in range(DN_H)], axis=0)
        g_sq = jnp.concatenate([g_st] * (n4 // 128), axis=1) if n4 >= 128 else g_st[:, :n4]
        gc = _mm_exact_left(lmat, g_st, small)
        dmat = _mm_exact_left(lmat, g_sq * umat, small)
        gam = jnp.where(incl, jnp.exp(dmat), 0.0)
        eg = jnp.exp(gc)
        kb = kk * beta_st
        a_mat = jnp.where(strict, _mm_nt(kb, kk, small) * gam, 0.0)
        x = jnp.concatenate([v * beta_st, kb * eg], axis=1)
        p = -a_mat
        for i in range(n_pow):
            x = x + (_mm_hi if i < 2 else solve_mm)(p, x, small)
            if i < n_pow - 1:
                p = (_mm_hi if i < 1 else solve_mm)(p, p, small)
        gc_last = [gc[(hd + 1) * chunk - 1:(hd + 1) * chunk, :] for hd in range(DN_H)]
        gl_st = jnp.concatenate([jnp.broadcast_to(r, (chunk, 128)) for r in gc_last], axis=0)
        u_s[slot] = x[:, :128]
        w_s[slot] = x[:, 128:]
        qg_s[slot] = q * eg
        kdec_s[slot] = kk * jnp.exp(gl_st - gc)
        aqk_s[slot] = jnp.where(incl, _mm_nt(q, kk, small) * gam, 0.0)
        gl_s[slot] = jnp.exp(jnp.concatenate(gc_last, axis=1))

    def level2(k, slot):
        b = k // n_ch
        r0 = pl.multiple_of((k % n_ch) * chunk, chunk)
        s_old = s_all[b]
        w = w_s[slot]
        qg = qg_s[slot]
        ws, qs = [], []
        for hd in range(DN_H):
            rs = slice(hd * chunk, (hd + 1) * chunk)
            r = _mm(jnp.concatenate([w[rs], qg[rs]], axis=0), s_old[:, hd * 128:(hd + 1) * 128], small)
            ws.append(r[:chunk])
            qs.append(r[chunk:])
        v_new = u_s[slot] - jnp.concatenate(ws, axis=0)
        o = jnp.concatenate(qs, axis=0) + _mm(aqk_s[slot], v_new, small)
        vbd = jnp.where(vmask, jnp.concatenate([v_new] * DN_H, axis=1), 0.0)
        s_all[b] = s_old * gl_s[slot] + _mm_tn(kdec_s[slot], vbd, small)
        on = _rms(o, normg)
        for hd in range(DN_H):
            sl = slice(hd * 128, (hd + 1) * 128)
            ob[b, pl.ds(r0, chunk), sl] = on[hd * chunk:(hd + 1) * chunk] * zb[b, pl.ds(r0, chunk), sl]

    level1(0, 0)

    def chunk_body(k, _):
        slot = k % 2
        level2(k, slot)
        level1(jnp.minimum(k + 1, total - 1), 1 - slot)
        return 0

    lax.fori_loop(0, total, chunk_body, 0)

    for hd in range(DN_H):
        snew_ref[:, hd] = s_all[:, :, hd * 128:(hd + 1) * 128]

    y = _dot(ob[...].reshape(rows, DN_QK_W).astype(BF16), wout_ref[...])
    y_ref[...] = y * jax.nn.sigmoid(_dot(h, wgate_ref[...]))


def _deltanet(h, s0, cprev, wts, *, n_seq, t_pad, t_valid, bb_n, tt, chunk):
    n_t = t_pad // tt
    assert n_t == 1 or t_valid == t_pad
    tv_last = t_valid - (n_t - 1) * tt
    rows = bb_n * tt
    small = chunk < 16
    n4 = DN_H * chunk
    const = lambda *shape: pl.BlockSpec(shape, lambda b, t: (0,) * len(shape))
    in_specs = [
        pl.BlockSpec((rows, D_MODEL), lambda b, t: (b * n_t + t, 0)),
        pl.BlockSpec((bb_n, DN_H, DN_DK, DN_DK), lambda b, t: (b, 0, 0, 0)),
        pl.BlockSpec((bb_n, DN_CONV - 1, DN_CONV_W), lambda b, t: (b, 0, 0)),
        const(D_MODEL, DN_CONV_W), const(D_MODEL, DN_QK_W), const(D_MODEL, 128),
        const(DN_CONV, DN_CONV_W), const(1, 128), const(1, 128), const(1, 128),
        const(DN_QK_W, D_MODEL), const(D_MODEL, D_MODEL),
    ]
    out_specs = [
        pl.BlockSpec((rows, D_MODEL), lambda b, t: (b * n_t + t, 0)),
        pl.BlockSpec((bb_n, DN_H, DN_DK, DN_DK), lambda b, t: (b, 0, 0, 0)),
        pl.BlockSpec((bb_n, DN_CONV - 1, DN_CONV_W), lambda b, t: (b, 0, 0)),
    ]
    out_shape = [
        jax.ShapeDtypeStruct((n_seq * t_pad, D_MODEL), F32),
        jax.ShapeDtypeStruct((n_seq, DN_H, DN_DK, DN_DK), F32),
        jax.ShapeDtypeStruct((n_seq, DN_CONV - 1, DN_CONV_W), F32),
    ]
    return pl.pallas_call(
        functools.partial(_dn_kernel, bb_n, tt, chunk, tv_last, n_t, small),
        grid=(n_seq // bb_n, n_t),
        in_specs=in_specs, out_specs=out_specs, out_shape=out_shape,
        scratch_shapes=[pltpu.VMEM((bb_n, tt + 8, DN_CONV_W), F32), pltpu.VMEM((bb_n, tt, DN_CONV_W), F32),
                        pltpu.VMEM((bb_n, tt, 128), F32), pltpu.VMEM((bb_n, tt, DN_QK_W), F32),
                        pltpu.VMEM((bb_n, tt, DN_QK_W), F32),
                        pltpu.VMEM((bb_n, DN_DK, DN_H * 128), F32)]
                       + [pltpu.VMEM((2, n4, 128), F32)] * 4
                       + [pltpu.VMEM((2, n4, n4), F32), pltpu.VMEM((2, 1, DN_H * 128), F32)],
        compiler_params=_cparams(("parallel", "arbitrary")),
        name="deltanet",
    )(h, s0, cprev, *wts)


def _sc_kernel(bb_n, tt, tv_last, n_t,
               h_ref, prev_ref, win_ref, convw_ref, wout_ref, wgate_ref,
               y_ref, new_ref, ubuf):
    t = pl.program_id(1)
    rows = bb_n * tt

    @pl.when(t == 0)
    def _():
        ubuf[:, 6:8, :] = prev_ref[...]

    h = h_ref[...]
    p = _dot(h, win_ref[...])
    bgate = p[:, :SC_W]
    u = p[:, SC_W:2 * SC_W] * p[:, 2 * SC_W:]
    ubuf[:, 8:8 + tt, :] = u.reshape(bb_n, tt, SC_W)
    y = 0.0
    for j in range(SC_CONV):
        y = y + ubuf[:, 6 + j:6 + j + tt, :] * convw_ref[j:j + 1, :].reshape(1, 1, SC_W)

    @pl.when(t == n_t - 1)
    def _():
        new_ref[...] = ubuf[:, 6 + tv_last:8 + tv_last, :]

    if n_t > 1:
        ubuf[:, 0:8, :] = ubuf[:, tt:tt + 8, :]

    z = (bgate * y.reshape(rows, SC_W)).astype(BF16)
    y_ref[...] = _dot(z, wout_ref[...]) * jax.nn.sigmoid(_dot(h, wgate_ref[...]))


def _shortconv(h, prev, wts, *, n_seq, t_pad, t_valid, bb_n, tt):
    n_t = t_pad // tt
    assert n_t == 1 or t_valid == t_pad
    tv_last = t_valid - (n_t - 1) * tt
    rows = bb_n * tt
    const = lambda *shape: pl.BlockSpec(shape, lambda b, t: (0,) * len(shape))
    return pl.pallas_call(
        functools.partial(_sc_kernel, bb_n, tt, tv_last, n_t),
        grid=(n_seq // bb_n, n_t),
        in_specs=[pl.BlockSpec((rows, D_MODEL), lambda b, t: (b * n_t + t, 0)),
                  pl.BlockSpec((bb_n, SC_CONV - 1, SC_W), lambda b, t: (b, 0, 0)),
                  const(D_MODEL, 3 * SC_W), const(SC_CONV, SC_W), const(SC_W, D_MODEL),
                  const(D_MODEL, D_MODEL)],
        out_specs=[pl.BlockSpec((rows, D_MODEL), lambda b, t: (b * n_t + t, 0)),
                   pl.BlockSpec((bb_n, SC_CONV - 1, SC_W), lambda b, t: (b, 0, 0))],
        out_shape=[jax.ShapeDtypeStruct((n_seq * t_pad, D_MODEL), F32),
                   jax.ShapeDtypeStruct((n_seq, SC_CONV - 1, SC_W), F32)],
        scratch_shapes=[pltpu.VMEM((bb_n, tt + 8, SC_W), F32)],
        compiler_params=_cparams(("parallel", "arbitrary")),
        name="shortconv",
    )(h, prev, *wts)


def _memkv_kernel(m_ref, g_ref, wkv_ref, kg_ref, k_ref, v_ref):
    n = _rms(m_ref[...], g_ref[...]).astype(BF16)
    kv = _dot(n, wkv_ref[...])
    kg = kg_ref[...]
    for hd in range(MEM_H):
        sl = slice(hd * MEM_HD, (hd + 1) * MEM_HD)
        k_ref[:, sl] = _rms(kv[:, sl], kg)
    v_ref[...] = kv[:, MEM_W:]


def _mem_kv(mem2d, norm_g, w_kv, k_gain, *, tm):
    m = mem2d.shape[0]
    return pl.pallas_call(
        _memkv_kernel,
        grid=(m // tm,),
        in_specs=[pl.BlockSpec((tm, D_MODEL), lambda i: (i, 0)),
                  pl.BlockSpec((1, D_MODEL), lambda i: (0, 0)),
                  pl.BlockSpec((D_MODEL, 2 * MEM_W), lambda i: (0, 0)),
                  pl.BlockSpec((1, MEM_HD), lambda i: (0, 0))],
        out_specs=[pl.BlockSpec((tm, MEM_W), lambda i: (i, 0))] * 2,
        out_shape=[jax.ShapeDtypeStruct((m, MEM_W), F32)] * 2,
        compiler_params=_cparams(("parallel",)),
        name="mem_kv",
    )(mem2d, norm_g, w_kv, k_gain)


def _memattn_kernel(bb_n, tt, small,
                    h_ref, mk_ref, mv_ref, wq_ref, qg_ref, wout_ref, wgate_ref,
                    y_ref, qs, ob):
    rows = bb_n * tt
    h = h_ref[...]
    q = _dot(h, wq_ref[...])
    qg = qg_ref[...]
    for hd in range(MEM_H):
        sl = slice(hd * MEM_HD, (hd + 1) * MEM_HD)
        qs[:, :, sl] = _rms(q[:, sl], qg).reshape(bb_n, tt, MEM_HD)

    def seq_body(b, _):
        for hd in range(MEM_H):
            sl = slice(hd * MEM_HD, (hd + 1) * MEM_HD)
            s = _mm_nt(qs[b, :, sl], mk_ref[b, :, sl], small) * (MEM_HD ** -0.5)
            s = s - jnp.max(s, axis=-1, keepdims=True)
            e = jnp.exp(s)
            p = e / jnp.sum(e, axis=-1, keepdims=True)
            ob[b, :, sl] = _mm(p, mv_ref[b, :, sl], small)
        return 0

    lax.fori_loop(0, bb_n, seq_body, 0)
    y = _dot(ob[...].reshape(rows, MEM_W).astype(BF16), wout_ref[...])
    y_ref[...] = y * jax.nn.sigmoid(_dot(h, wgate_ref[...]))


def _mem_attn(h, mk, mv, wts, *, n_seq, t_pad, bb_n, tt):
    n_t = t_pad // tt
    rows = bb_n * tt
    const = lambda *shape: pl.BlockSpec(shape, lambda b, t: (0,) * len(shape))
    return pl.pallas_call(
        functools.partial(_memattn_kernel, bb_n, tt, tt < 16),
        grid=(n_seq // bb_n, n_t),
        in_specs=[pl.BlockSpec((rows, D_MODEL), lambda b, t: (b * n_t + t, 0)),
                  pl.BlockSpec((bb_n, N_MEM, MEM_W), lambda b, t: (b, 0, 0)),
                  pl.BlockSpec((bb_n, N_MEM, MEM_W), lambda b, t: (b, 0, 0)),
                  const(D_MODEL, MEM_W), const(1, MEM_HD), const(MEM_W, D_MODEL), const(D_MODEL, D_MODEL)],
        out_specs=pl.BlockSpec((rows, D_MODEL), lambda b, t: (b * n_t + t, 0)),
        out_shape=jax.ShapeDtypeStruct((n_seq * t_pad, D_MODEL), F32),
        scratch_shapes=[pltpu.VMEM((bb_n, tt, MEM_W), F32), pltpu.VMEM((bb_n, tt, MEM_W), F32)],
        compiler_params=_cparams(("parallel", "arbitrary")),
        name="mem_attn",
    )(h, mk, mv, *wts)


def _mlaproj_kernel(h_ref, cos_ref, sin_ref, wq_ref, qna_ref, wqp_ref, wqs_ref, qg_ref,
                    wkv_ref, kvna_ref, wkr_ref, wkrs_ref, wuk_ref, wuv_ref, kg_ref,
                    q_ref, ckv_ref, kr_ref, k_ref, v_ref):
    h = h_ref[...]
    cos = cos_ref[...]
    sin = sin_ref[...]
    cqn = _rms(_dot(h, wq_ref[...]), qna_ref[...]).astype(BF16)
    q_raw = _dot(cqn, wqp_ref[...])
    q_swp = _dot(cqn, wqs_ref[...])
    qg = qg_ref[...]
    inv_n = 1.0 / MLA_QK
    for hd in range(MLA_H):
        sl = slice(hd * MLA_LANES, (hd + 1) * MLA_LANES)
        qh = q_raw[:, sl] * cos + q_swp[:, sl] * sin
        ms = jnp.sum(qh * qh, axis=-1, keepdims=True) * inv_n
        q_ref[:, sl] = (qh * lax.rsqrt(ms + EPS) * qg).astype(q_ref.dtype)

    ckv = _rms(_dot(h, wkv_ref[...]), kvna_ref[...])
    ckv_ref[...] = ckv
    krp = _dot(h, wkr_ref[...]) * cos + _dot(h, wkrs_ref[...]) * sin
    kr_ref[...] = krp[:, :MLA_ROPE]

    cb = ckv.astype(BF16)
    k_raw = _dot(cb, wuk_ref[...])
    v_ref[...] = _dot(cb, wuv_ref[...]).astype(BF16)
    kg = kg_ref[...]
    for hd in range(MLA_H):
        sl = slice(hd * MLA_LANES, (hd + 1) * MLA_LANES)
        kh = k_raw[:, sl] + krp
        ms = jnp.sum(kh * kh, axis=-1, keepdims=True) * inv_n
        k_ref[:, sl] = (kh * lax.rsqrt(ms + EPS) * kg).astype(k_ref.dtype)


def _mla_proj(h, cos, sin, wts, *, tm, n_tab, qk_dtype):
    m = h.shape[0]
    hw = MLA_H * MLA_LANES
    vw = MLA_H * MLA_V
    const = lambda *shape: pl.BlockSpec(shape, lambda i: (0,) * len(shape))
    row = lambda width: pl.BlockSpec((tm, width), lambda i: (i, 0))
    in_specs = [row(D_MODEL),
                pl.BlockSpec((tm, MLA_LANES), lambda i: (i % n_tab, 0)),
                pl.BlockSpec((tm, MLA_LANES), lambda i: (i % n_tab, 0)),
                const(D_MODEL, MLA_RANK), const(1, MLA_RANK), const(MLA_RANK, hw), const(MLA_RANK, hw),
                const(1, MLA_LANES),
                const(D_MODEL, MLA_RANK), const(1, MLA_RANK), const(D_MODEL, MLA_LANES), const(D_MODEL, MLA_LANES),
                const(MLA_RANK, hw), const(MLA_RANK, vw), const(1, MLA_LANES)]
    return pl.pallas_call(
        _mlaproj_kernel,
        grid=(m // tm,),
        in_specs=in_specs,
        out_specs=[row(hw), row(MLA_RANK), row(MLA_ROPE), row(hw), row(vw)],
        out_shape=[jax.ShapeDtypeStruct((m, hw), qk_dtype),
                   jax.ShapeDtypeStruct((m, MLA_RANK), F32),
                   jax.ShapeDtypeStruct((m, MLA_ROPE), F32),
                   jax.ShapeDtypeStruct((m, hw), qk_dtype),
                   jax.ShapeDtypeStruct((m, vw), BF16)],
        compiler_params=_cparams(("parallel",)),
        name="mla_proj",
    )(h, cos, sin, *wts)


def _flash_kernel(tq, q_ref, k_ref, v_ref, o_ref, m_scr, l_scr, acc_scr):
    qi = pl.program_id(1)
    ki = pl.program_id(2)
    c2 = (MLA_QK ** -0.5) * LOG2E
    n_rep = tq // 128

    @pl.when(ki == 0)
    def _():
        m_scr[...] = jnp.full(m_scr.shape, -jnp.inf, F32)
        l_scr[...] = jnp.zeros_like(l_scr)
        acc_scr[...] = jnp.zeros_like(acc_scr)

    def compute(diag):
        if diag:
            row = lax.broadcasted_iota(jnp.int32, (tq, tq), 0)
            col = lax.broadcasted_iota(jnp.int32, (tq, tq), 1)
            keep = col <= row
        ones = jnp.ones((tq, 128), BF16)
        for hd in range(MLA_H):
            sl = slice(hd * MLA_LANES, (hd + 1) * MLA_LANES)
            s = _dot_nt(q_ref[:, sl], k_ref[:, sl]) * c2
            if diag:
                s = jnp.where(keep, s, -jnp.inf)
            m_old = m_scr[hd]
            m_new = jnp.maximum(m_old, jnp.max(s, axis=-1, keepdims=True))
            alpha = jnp.exp2(m_old - m_new)
            p = jnp.exp2(s - jnp.concatenate([m_new] * n_rep, axis=1)).astype(BF16)
            pair = hd // 2
            vext = jnp.concatenate([v_ref[:, pair * 128:(pair + 1) * 128], ones], axis=1)
            r = _dot(p, vext)
            acc_scr[hd] = alpha * acc_scr[hd] + r[:, :128]
            l_scr[hd] = alpha * l_scr[hd] + r[:, 128:]
            m_scr[hd] = m_new

    @pl.when(ki < qi)
    def _():
        compute(False)

    @pl.when(ki == qi)
    def _():
        compute(True)
        lane = lax.broadcasted_iota(jnp.int32, (tq, 128), 1)
        for pair in range(MLA_H // 2):
            even = acc_scr[2 * pair] / l_scr[2 * pair]
            odd = acc_scr[2 * pair + 1] / l_scr[2 * pair + 1]
            o_ref[:, pair * 128:(pair + 1) * 128] = jnp.where(lane < MLA_V, even, odd).astype(o_ref.dtype)


def _mla_prompt_attn(q, k, v, *, n_seq, seq, tq):
    nq = seq // tq
    hw = MLA_H * MLA_LANES
    vw = MLA_H * MLA_V
    return pl.pallas_call(
        functools.partial(_flash_kernel, tq),
        grid=(n_seq, nq, nq),
        in_specs=[pl.BlockSpec((tq, hw), lambda b, i, j: (b * nq + i, 0)),
                  pl.BlockSpec((tq, hw), lambda b, i, j: (b * nq + jnp.minimum(i, j), 0)),
                  pl.BlockSpec((tq, vw), lambda b, i, j: (b * nq + jnp.minimum(i, j), 0))],
        out_specs=pl.BlockSpec((tq, vw), lambda b, i, j: (b * nq + i, 0)),
        out_shape=jax.ShapeDtypeStruct((n_seq * seq, vw), BF16),
        scratch_shapes=[pltpu.VMEM((MLA_H, tq, 128), F32)] * 3,
        compiler_params=_cparams(("parallel", "parallel", "arbitrary")),
        name="mla_flash",
    )(q, k, v)


SAMPLE_ROWS = 8
SUB_KEYS = 1024


def _mla_sample_kernel(n_pg, n_steps, t_valid, *refs):
    it = iter(refs)
    _pt_ref = next(it)
    q_ref, knew_ref, cnew_ref = next(it), next(it), next(it)
    wukp_ref, wukt_ref, kg_ref = next(it), next(it), next(it)
    c_refs = [next(it) for _ in range(n_pg)]
    r_refs = [next(it) for _ in range(n_pg)]
    o_ref = next(it)
    lhs, qabs, qrope, qblk, cbuf, rbuf, sbuf, m_scr, l_scr, acc_scr = (next(it) for _ in range(10))
    st = pl.program_id(1)
    c2 = (MLA_QK ** -0.5) * LOG2E
    inv_n = 1.0 / MLA_QK
    tk = n_pg * PAGE
    nq = SAMPLE_ROWS
    n_up = MLA_H * MLA_NOPE

    @pl.when(st == 0)
    def _():
        m_scr[...] = jnp.full(m_scr.shape, -jnp.inf, F32)
        l_scr[...] = jnp.zeros_like(l_scr)
        acc_scr[...] = jnp.zeros_like(acc_scr)
        kg = kg_ref[...]
        q = q_ref[...]
        lane_head = lax.broadcasted_iota(jnp.int32, q.shape, 1) // MLA_LANES
        for hd in range(MLA_H):
            sl = slice(hd * MLA_LANES, (hd + 1) * MLA_LANES)
            qk = q[:, sl] * kg
            qabs[hd * nq:(hd + 1) * nq, :] = lax.dot_general(
                qk, wukp_ref[:, sl].astype(F32), (((1,), (1,)), ((), ())), preferred_element_type=F32,
                precision=lax.Precision.HIGHEST)
            qrope[hd * nq:(hd + 1) * nq, :] = qk[:, :MLA_ROPE]
            qblk[hd * nq:(hd + 1) * nq, :] = jnp.where(lane_head == hd, q, 0.0)
        lhs[:n_up, :] = wukt_ref[...]
        lhs[n_up:, :] = qabs[...].astype(BF16)

    for i in range(n_pg):
        cbuf[i * PAGE:(i + 1) * PAGE, :] = c_refs[i][...].astype(BF16)
        rbuf[:, i * PAGE:(i + 1) * PAGE] = r_refs[i][...]

    lhs_v = lhs[...]
    qr = qrope[...].astype(BF16)
    for j in range(tk // SUB_KEYS):
        ks = slice(j * SUB_KEYS, (j + 1) * SUB_KEYS)
        big = _dot_nt(lhs_v, cbuf[ks, :])
        knt = big[:n_up]
        ssq = jnp.sum((knt * knt).reshape(MLA_H, MLA_NOPE, SUB_KEYS), axis=1)
        krt = rbuf[:, ks]
        ssq_r = jnp.sum(krt * krt, axis=0, keepdims=True)
        rs = lax.rsqrt((ssq + ssq_r) * inv_n + EPS) * c2
        s = big[n_up:] + _dot(qr, krt.astype(BF16))
        sbuf[:, ks] = jnp.concatenate([s[hd * nq:(hd + 1) * nq, :] * rs[hd:hd + 1, :] for hd in range(MLA_H)],
                                      axis=0)
    s = sbuf[...]
    m_old = m_scr[...]
    m_new = jnp.maximum(m_old, jnp.max(s, axis=-1, keepdims=True))
    alpha = jnp.exp2(m_old - m_new)
    p = jnp.exp2(s - m_new)
    l_scr[...] = alpha * l_scr[...] + jnp.sum(p, axis=-1, keepdims=True)
    acc_scr[...] = alpha * acc_scr[...] + _dot(p.astype(BF16), cbuf[...])
    m_scr[...] = m_new

    @pl.when(st == n_steps - 1)
    def _():
        cn = cnew_ref[...]
        sn = _dot_nt(_rnd(qblk[...]), _rnd(knew_ref[...])) * c2
        row = lax.broadcasted_iota(jnp.int32, sn.shape, 0) % nq
        col = lax.broadcasted_iota(jnp.int32, sn.shape, 1)
        sn = jnp.where((col <= row) & (col < t_valid), sn, -jnp.inf)
        m_o = m_scr[...]
        m_n = jnp.maximum(m_o, jnp.max(sn, axis=-1, keepdims=True))
        al = jnp.exp2(m_o - m_n)
        pn = jnp.exp2(sn - m_n)
        l_f = al * l_scr[...] + jnp.sum(pn, axis=-1, keepdims=True)
        o_ref[...] = (al * acc_scr[...] + _dot(_rnd(pn), _rnd(cn))) / l_f


def _mla_sample_attn(page_table, q, k_new, c_new, wts, ckv_pool, kr_pool_t, layer, *, n_seq, t_valid, n_pg):
    n_pages = page_table.shape[1]
    n_steps = n_pages // n_pg
    hw = MLA_H * MLA_LANES
    tk = n_pg * PAGE
    nq = SAMPLE_ROWS
    nr = MLA_H * nq

    def page_spec(shape, i):
        return pl.BlockSpec((None, None) + shape, lambda b, s, pt: (layer, pt[b, s * n_pg + i], 0, 0))

    const = lambda *shape: pl.BlockSpec(shape, lambda b, s, pt: (0,) * len(shape))
    in_specs = [pl.BlockSpec((nq, hw), lambda b, s, pt: (b, 0)),
                pl.BlockSpec((nq, hw), lambda b, s, pt: (b, 0)),
                pl.BlockSpec((nq, MLA_RANK), lambda b, s, pt: (b, 0)),
                const(MLA_RANK, hw), const(MLA_H * MLA_NOPE, MLA_RANK), const(1, MLA_LANES)]
    in_specs += [page_spec((PAGE, MLA_RANK), i) for i in range(n_pg)]
    in_specs += [page_spec((MLA_ROPE, PAGE), i) for i in range(n_pg)]
    grid_spec = pltpu.PrefetchScalarGridSpec(
        num_scalar_prefetch=1,
        grid=(n_seq, n_steps),
        in_specs=in_specs,
        out_specs=pl.BlockSpec((nr, MLA_RANK), lambda b, s, pt: (b, 0)),
        scratch_shapes=[pltpu.VMEM((MLA_H * MLA_NOPE + nr, MLA_RANK), BF16),
                        pltpu.VMEM((nr, MLA_RANK), F32), pltpu.VMEM((nr, MLA_ROPE), F32),
                        pltpu.VMEM((nr, hw), F32),
                        pltpu.VMEM((tk, MLA_RANK), BF16), pltpu.VMEM((MLA_ROPE, tk), F32),
                        pltpu.VMEM((nr, tk), F32),
                        pltpu.VMEM((nr, 1), F32), pltpu.VMEM((nr, 1), F32), pltpu.VMEM((nr, MLA_RANK), F32)],
    )
    return pl.pallas_call(
        functools.partial(_mla_sample_kernel, n_pg, n_steps, t_valid),
        grid_spec=grid_spec,
        out_shape=jax.ShapeDtypeStruct((n_seq * nr, MLA_RANK), F32),
        compiler_params=_cparams(("parallel", "arbitrary")),
        name="mla_paged",
    )(page_table, q, k_new, c_new, *wts, *([ckv_pool] * n_pg), *([kr_pool_t] * n_pg))


def _mla_up_kernel(n_seq, pc_ref, h_ref, wuvs_ref, wout_ref, wgate_ref, y_ref):
    nq = SAMPLE_ROWS
    o = jnp.zeros((n_seq * nq, MLA_H * MLA_V), F32)
    for hd in range(MLA_H):
        pch = pc_ref[:, hd * nq:(hd + 1) * nq, :].reshape(n_seq * nq, MLA_RANK)
        o = o + _dot(pch.astype(BF16), wuvs_ref[hd])
    y_ref[...] = _dot(o.astype(BF16), wout_ref[...]) * jax.nn.sigmoid(_dot(h_ref[...], wgate_ref[...]))


def _mla_up_proj_gate(pc, h, wuv_sel, w_out, w_gate, *, n_seq):
    rows = n_seq * SAMPLE_ROWS
    return pl.pallas_call(
        functools.partial(_mla_up_kernel, n_seq),
        out_shape=jax.ShapeDtypeStruct((rows, D_MODEL), F32),
        compiler_params=pltpu.CompilerParams(vmem_limit_bytes=VMEM_LIMIT),
        name="mla_up_proj_gate",
    )(pc.reshape(n_seq, MLA_H * SAMPLE_ROWS, MLA_RANK), h, wuv_sel, w_out, w_gate)


def _projgate_kernel(o_ref, h_ref, wout_ref, wgate_ref, y_ref):
    y_ref[...] = _dot(o_ref[...].astype(BF16), wout_ref[...]) * jax.nn.sigmoid(_dot(h_ref[...], wgate_ref[...]))


def _proj_gate(o, h, w_out, w_gate, *, tm):
    m, kdim = o.shape
    return pl.pallas_call(
        _projgate_kernel,
        grid=(m // tm,),
        in_specs=[pl.BlockSpec((tm, kdim), lambda i: (i, 0)),
                  pl.BlockSpec((tm, D_MODEL), lambda i: (i, 0)),
                  pl.BlockSpec((kdim, D_MODEL), lambda i: (0, 0)),
                  pl.BlockSpec((D_MODEL, D_MODEL), lambda i: (0, 0))],
        out_specs=pl.BlockSpec((tm, D_MODEL), lambda i: (i, 0)),
        out_shape=jax.ShapeDtypeStruct((m, D_MODEL), F32),
        compiler_params=_cparams(("parallel",)),
        name="proj_gate",
    )(o, h, w_out, w_gate)


def _pad_lanes(x, width):
    return jnp.pad(x, [(0, 0)] * (x.ndim - 1) + [(0, width - x.shape[-1])])


def _mla_head_layout(nope, r1, r2):
    z = jnp.zeros(nope.shape[:-1] + (MLA_LANES - MLA_QK,), nope.dtype)
    x = jnp.concatenate([r1, r2, nope, z], axis=-1)
    return x.reshape(x.shape[:-2] + (MLA_H * MLA_LANES,))


def _gain_layout(g):
    half = MLA_ROPE // 2
    return jnp.concatenate([g[MLA_NOPE:MLA_NOPE + half], g[MLA_NOPE + half:], g[:MLA_NOPE],
                            jnp.zeros((MLA_LANES - MLA_QK,), g.dtype)]).reshape(1, MLA_LANES)


def _rope_tables(pos):
    half = MLA_ROPE // 2
    inv = ROPE_THETA ** (-jnp.arange(half, dtype=F32) / half)
    ang = pos.astype(F32)[:, None] * inv
    cos, sin = jnp.cos(ang), jnp.sin(ang)
    n = pos.shape[0]
    cos_t = jnp.concatenate([cos, cos, jnp.ones((n, MLA_NOPE), F32), jnp.zeros((n, MLA_LANES - MLA_QK), F32)], -1)
    sin_t = jnp.concatenate([sin, sin, jnp.zeros((n, MLA_LANES - MLA_ROPE), F32)], -1)
    return cos_t, sin_t


def _layer_weights(l, p):
    w_in = p['w_in'][l]
    sizes = (DN_CONV_W, DN_QK_W, DN_H, DN_H, SC_W, SC_W, SC_W, MLA_RANK, MLA_RANK, MLA_ROPE, MEM_W, 4 * D_MODEL)
    offs = np.concatenate([[0], np.cumsum(sizes)])
    seg = [w_in[:, offs[i]:offs[i + 1]] for i in range(len(sizes))]
    bf = lambda x: x.astype(BF16)
    row = lambda x: x.reshape(1, -1)
    gates = [bf(seg[11][:, i * D_MODEL:(i + 1) * D_MODEL]) for i in range(4)]
    half = MLA_ROPE // 2

    w = {}
    w['ffn1'] = (row(p['ffn1_norm'][l]), bf(p['ffn1_w_gu'][l]), bf(p['ffn1_w_down'][l]))
    w['ffn2'] = (row(p['ffn2_norm'][l]), bf(p['ffn2_w_gu'][l]), bf(p['ffn2_w_down'][l]))
    w['mix_norm'] = row(p['mix_norm'][l])
    w['w_o'] = bf(p['w_o'][l])
    w['dn'] = (bf(seg[0]), bf(seg[1]), bf(_pad_lanes(jnp.concatenate([seg[2], seg[3]], 1), 128)),
               p['dn_conv_w'][l], _pad_lanes(row(p['dn_A_log'][l]), 128), _pad_lanes(row(p['dn_dt_bias'][l]), 128),
               row(p['dn_norm'][l]), bf(p['dn_w_out'][l]), gates[0])
    w['sc'] = (bf(jnp.concatenate([seg[4], seg[5], seg[6]], 1)), p['sc_conv_w'][l], bf(p['sc_w_out'][l]), gates[1])

    wq = p['mla_w_q_b'][l].reshape(MLA_RANK, MLA_H, MLA_QK)
    q_nope, q_r1, q_r2 = wq[..., :MLA_NOPE], wq[..., MLA_NOPE:MLA_NOPE + half], wq[..., MLA_NOPE + half:]
    wq_perm = _mla_head_layout(q_nope, q_r1, q_r2)
    wq_swap = _mla_head_layout(jnp.zeros_like(q_nope), -q_r2, q_r1)
    wkr = seg[9]
    wkr_pad = _pad_lanes(wkr, MLA_LANES)
    wkr_swap = _pad_lanes(jnp.concatenate([-wkr[:, half:], wkr[:, :half]], 1), MLA_LANES)
    wkv = p['mla_w_kv_b'][l].reshape(MLA_RANK, MLA_H, MLA_NOPE + MLA_V)
    w_uk, w_uv = wkv[..., :MLA_NOPE], wkv[..., MLA_NOPE:]
    zr = jnp.zeros((MLA_RANK, MLA_H, half), F32)
    wuk_perm = bf(_mla_head_layout(w_uk, zr, zr))
    k_gain = _gain_layout(p['mla_k_norm'][l])
    w['mla_proj'] = (bf(seg[7]), row(p['mla_q_norm_a'][l]), bf(wq_perm), bf(wq_swap), _gain_layout(p['mla_q_norm'][l]),
                     bf(seg[8]), row(p['mla_kv_norm_a'][l]), bf(wkr_pad), bf(wkr_swap),
                     wuk_perm, bf(w_uv.reshape(MLA_RANK, MLA_H * MLA_V)), k_gain)
    eye = jnp.eye(MLA_H, dtype=F32)
    w['mla_wuv_sel'] = bf((w_uv[None] * eye[:, None, :, None]).reshape(MLA_H, MLA_RANK, MLA_H * MLA_V))
    w['mla_sample'] = (wuk_perm, bf(w_uk.reshape(MLA_RANK, MLA_H * MLA_NOPE).T), k_gain)
    w['mla_out'] = (bf(p['mla_w_out'][l]), gates[2])
    w['mem_kv'] = (row(p['mem_norm'][l]), bf(p['mem_w_kv'][l]), row(p['mem_k_norm'][l]))
    w['mem'] = (bf(seg[10]), row(p['mem_q_norm'][l]), bf(p['mem_w_out'][l]), gates[3])
    return w


def _group_layer(x, w, *, n_seq, t_pad, t_valid, tm, bb_n, tt, chunk, dn_state, sc_state, mem_kv, cos, sin,
                 n_tab, mla_attend, q_dtype):
    cfg = dict(n_seq=n_seq, t_pad=t_pad, bb_n=bb_n, tt=tt)
    x1, h = _ffn(x, *w['ffn1'], tm=tm, h_gain=w['mix_norm'])
    y_dn, dn_s, dn_c = _deltanet(h, dn_state[0], dn_state[1], w['dn'], t_valid=t_valid, chunk=chunk, **cfg)
    y_sc, sc_c = _shortconv(h, sc_state, w['sc'], t_valid=t_valid, **cfg)
    q, ckv, kr, k, v = _mla_proj(h, cos, sin, w['mla_proj'], tm=tm, n_tab=n_tab, qk_dtype=q_dtype)
    if mla_attend is None:
        o = _mla_prompt_attn(q, k, v, n_seq=n_seq, seq=t_pad, tq=tm)
        y_mla = _proj_gate(o, h, *w['mla_out'], tm=tm)
    else:
        y_mla = mla_attend(q, k, ckv, h)
    y_mem = _mem_attn(h, mem_kv[0], mem_kv[1], w['mem'], **cfg)
    x2 = _merge_wo(x1, (y_dn, y_sc, y_mla, y_mem), w['w_o'], tm=tm)
    x3 = _ffn(x2, *w['ffn2'], tm=tm)
    return x3, dn_s, dn_c, sc_c, ckv, kr


def kernel(x_prompt, x_sample, state_dn_S, state_dn_conv, state_sc_conv, cache_mla_ckv, cache_mla_krope, cache_mem_k, cache_mem_v, page_table, mem_prompt, ffn1_norm, ffn1_w_gu, ffn1_w_down, mix_norm, w_in, dn_conv_w, dn_A_log, dn_dt_bias, dn_norm, dn_w_out, sc_conv_w, sc_w_out, mla_q_norm_a, mla_w_q_b, mla_kv_norm_a, mla_w_kv_b, mla_q_norm, mla_k_norm, mla_w_out, mem_norm, mem_w_kv, mem_q_norm, mem_k_norm, mem_w_out, w_o, ffn2_norm, ffn2_w_gu, ffn2_w_down):
    params = dict(ffn1_norm=ffn1_norm, ffn1_w_gu=ffn1_w_gu, ffn1_w_down=ffn1_w_down, mix_norm=mix_norm, w_in=w_in,
                  dn_conv_w=dn_conv_w, dn_A_log=dn_A_log, dn_dt_bias=dn_dt_bias, dn_norm=dn_norm, dn_w_out=dn_w_out,
                  sc_conv_w=sc_conv_w, sc_w_out=sc_w_out, mla_q_norm_a=mla_q_norm_a, mla_w_q_b=mla_w_q_b,
                  mla_kv_norm_a=mla_kv_norm_a, mla_w_kv_b=mla_w_kv_b, mla_q_norm=mla_q_norm, mla_k_norm=mla_k_norm,
                  mla_w_out=mla_w_out, mem_norm=mem_norm, mem_w_kv=mem_w_kv, mem_q_norm=mem_q_norm,
                  mem_k_norm=mem_k_norm, mem_w_out=mem_w_out, w_o=w_o, ffn2_norm=ffn2_norm, ffn2_w_gu=ffn2_w_gu,
                  ffn2_w_down=ffn2_w_down)
    depth = w_in.shape[0]
    bp, seq, _ = x_prompt.shape
    bs, td, _ = x_sample.shape
    tds = SAMPLE_ROWS
    n_pages = page_table.shape[1]
    past = n_pages * PAGE
    krope_t = jnp.transpose(cache_mla_krope, (0, 1, 3, 2))

    cos_p, sin_p = _rope_tables(jnp.arange(seq))
    cos_s, sin_s = _rope_tables(past + jnp.arange(tds))
    cos_s, sin_s = jnp.tile(cos_s, (bs, 1)), jnp.tile(sin_s, (bs, 1))

    xp = x_prompt.reshape(bp * seq, D_MODEL)
    xs = jnp.pad(x_sample, ((0, 0), (0, tds - td), (0, 0))).reshape(bs * tds, D_MODEL)
    zero_s = jnp.zeros((bp, DN_H, DN_DK, DN_DK), F32)
    zero_dc = jnp.zeros((bp, DN_CONV - 1, DN_CONV_W), F32)
    zero_sc = jnp.zeros((bp, SC_CONV - 1, SC_W), F32)
    mem2d = mem_prompt.reshape(bp * N_MEM, D_MODEL)

    outs = {k: [] for k in ('pS', 'pdc', 'psc', 'pckv', 'pkr', 'pmk', 'pmv', 'sS', 'sdc', 'ssc', 'sckv', 'skr')}
    tm_p = 512
    for l in range(depth):
        w = _layer_weights(l, params)
        mk, mv = _mem_kv(mem2d, *w['mem_kv'], tm=tm_p)
        mk3, mv3 = mk.reshape(bp, N_MEM, MEM_W), mv.reshape(bp, N_MEM, MEM_W)
        xp, s_p, dc_p, sc_p, ckv_p, kr_p = _group_layer(
            xp, w, n_seq=bp, t_pad=seq, t_valid=seq, tm=tm_p, bb_n=1, tt=tm_p, chunk=DN_CHUNK,
            dn_state=(zero_s, zero_dc), sc_state=zero_sc, mem_kv=(mk3, mv3), cos=cos_p, sin=sin_p,
            n_tab=seq // tm_p, mla_attend=None, q_dtype=BF16)
        outs['pS'].append(s_p); outs['pdc'].append(dc_p); outs['psc'].append(sc_p)
        outs['pckv'].append(ckv_p.reshape(bp, seq, MLA_RANK)); outs['pkr'].append(kr_p.reshape(bp, seq, MLA_ROPE))
        outs['pmk'].append(mk.reshape(bp, N_MEM, MEM_H, MEM_HD)); outs['pmv'].append(mv.reshape(bp, N_MEM, MEM_H, MEM_HD))

        def attend(q, k, ckv, h, l=l, w=w):
            pc = _mla_sample_attn(page_table, q, k, ckv, w['mla_sample'], cache_mla_ckv, krope_t, l,
                                  n_seq=bs, t_valid=td, n_pg=32)
            return _mla_up_proj_gate(pc, h, w['mla_wuv_sel'], *w['mla_out'], n_seq=bs)

        xs, s_s, dc_s, sc_s, ckv_s, kr_s = _group_layer(
            xs, w, n_seq=bs, t_pad=tds, t_valid=td, tm=bs * tds, bb_n=8, tt=tds, chunk=tds,
            dn_state=(state_dn_S[l], state_dn_conv[l]), sc_state=state_sc_conv[l],
            mem_kv=(cache_mem_k[l].reshape(bs, N_MEM, MEM_W), cache_mem_v[l].reshape(bs, N_MEM, MEM_W)),
            cos=cos_s, sin=sin_s, n_tab=1, mla_attend=attend, q_dtype=F32)
        outs['sS'].append(s_s); outs['sdc'].append(dc_s); outs['ssc'].append(sc_s)
        outs['sckv'].append(ckv_s.reshape(bs, tds, MLA_RANK)[:, :td])
        outs['skr'].append(kr_s.reshape(bs, tds, MLA_ROPE)[:, :td])

    st = lambda k: jnp.stack(outs[k])
    y_prompt = xp.reshape(bp, seq, D_MODEL)
    y_sample = xs.reshape(bs, tds, D_MODEL)[:, :td]
    return (y_prompt, y_sample, st('pS'), st('pdc'), st('psc'), st('pckv'), st('pkr'), st('pmk'), st('pmv'),
            st('sS'), st('sdc'), st('ssc'), st('sckv'), st('skr'))
```

```python
import functools

import numpy as np
import jax
import jax.numpy as jnp
from jax import lax
from jax.experimental import pallas as pl
from jax.experimental.pallas import tpu as pltpu

F32 = jnp.float32
BF16 = jnp.bfloat16

D_MODEL = 1024
D_FF = 2816
EPS = 1e-6
N_MEM = 256
PAGE = 128
DN_H = 4
DN_DK = 128
DN_QK_W = 512
DN_CONV_W = 1536
DN_CONV = 4
DN_CHUNK = 64
SC_W = 512
SC_CONV = 3
MLA_H = 8
MLA_RANK = 256
MLA_NOPE = 64
MLA_ROPE = 32
MLA_V = 64
MLA_QK = 96
MLA_LANES = 128
ROPE_THETA = 10000.0
LOG2E = 1.4426950408889634
MEM_H = 4
MEM_HD = 128
MEM_W = 512

VMEM_LIMIT = 56 * 1024 * 1024


def _cparams(sem):
    return pltpu.CompilerParams(dimension_semantics=sem, vmem_limit_bytes=VMEM_LIMIT)


def _rms(x, g):
    ms = jnp.mean(x * x, axis=-1, keepdims=True)
    return x * lax.rsqrt(ms + EPS) * g


def _silu(x):
    return x * jax.nn.sigmoid(x)


def _rnd(x):
    return x.astype(BF16).astype(F32)


def _dot(a, b):
    return jnp.dot(a, b, preferred_element_type=F32)


def _dot_nt(a, b):
    return lax.dot_general(a, b, (((1,), (1,)), ((), ())), preferred_element_type=F32)


def _mm(a, b, small):
    if small:
        return _dot(_rnd(a), _rnd(b))
    return _dot(a.astype(BF16), b.astype(BF16))


def _mm_nt(a, b, small):
    if small:
        return _dot_nt(_rnd(a), _rnd(b))
    return _dot_nt(a.astype(BF16), b.astype(BF16))


def _mm_tn(a, b, small):
    dn = (((0,), (0,)), ((), ()))
    if small:
        return lax.dot_general(_rnd(a), _rnd(b), dn, preferred_element_type=F32)
    return lax.dot_general(a.astype(BF16), b.astype(BF16), dn, preferred_element_type=F32)


def _split2(x):
    hi = x.astype(BF16)
    lo = (x - hi.astype(F32)).astype(BF16)
    return hi, lo


def _mm_hi(a, b, small):
    if small:
        return jnp.dot(a, b, preferred_element_type=F32, precision=lax.Precision.HIGHEST)
    ah, al = _split2(a)
    bh, bl = _split2(b)
    return _dot(ah, bh) + (_dot(ah, bl) + _dot(al, bh))


def _mm_exact_left(lmat, b, small):
    if small:
        return jnp.dot(lmat, b, preferred_element_type=F32, precision=lax.Precision.HIGHEST)
    lb = lmat.astype(BF16)
    b1 = b.astype(BF16)
    r1 = b - b1.astype(F32)
    b2 = r1.astype(BF16)
    b3 = (r1 - b2.astype(F32)).astype(BF16)
    return _dot(lb, b1) + (_dot(lb, b2) + _dot(lb, b3))


FFN_SPLIT = 2


def _resident(*shape):
    return pl.BlockSpec(shape, lambda i: (0,) * len(shape), pipeline_mode=pl.Buffered(1))


def _ffn_kernel(merge, emit_h, *refs):
    it = iter(refs)
    x_ref = next(it)
    if merge:
        y_refs = [next(it) for _ in range(4)]
        wo_ref = next(it)
    g_ref, wgu_ref, wd_ref = next(it), next(it), next(it)
    if emit_h:
        g2_ref = next(it)
    o_ref = next(it)
    if emit_h:
        h_ref = next(it)

    x = x_ref[...]
    if merge:
        m = ((y_refs[0][...] + y_refs[1][...]) + y_refs[2][...]) + y_refs[3][...]
        x = x + _dot(m.astype(BF16), wo_ref[...])
    h = _rms(x, g_ref[...]).astype(BF16)
    tf = D_FF // FFN_SPLIT
    acc = None
    for j in range(FFN_SPLIT):
        gate = _dot(h, wgu_ref[:, j * tf:(j + 1) * tf])
        up = _dot(h, wgu_ref[:, D_FF + j * tf:D_FF + (j + 1) * tf])
        a = (_silu(gate) * up).astype(BF16)
        d = _dot(a, wd_ref[j * tf:(j + 1) * tf, :])
        acc = d if acc is None else acc + d
    out = x + 0.5 * acc
    o_ref[...] = out
    if emit_h:
        h_ref[...] = _rms(out, g2_ref[...]).astype(BF16)


def _ffn(x, norm_g, w_gu, w_down, *, tm, merge=None, h_gain=None):
    m = x.shape[0]
    row = pl.BlockSpec((tm, D_MODEL), lambda i: (i, 0))
    in_specs, args = [row], [x]
    if merge is not None:
        ys, w_o = merge
        in_specs += [row] * 4 + [_resident(D_MODEL, D_MODEL)]
        args += list(ys) + [w_o]
    in_specs += [_resident(1, D_MODEL), _resident(D_MODEL, 2 * D_FF), _resident(D_FF, D_MODEL)]
    args += [norm_g, w_gu, w_down]
    out_shape = [jax.ShapeDtypeStruct((m, D_MODEL), F32)]
    out_specs = [row]
    if h_gain is not None:
        in_specs.append(_resident(1, D_MODEL))
        args.append(h_gain)
        out_shape.append(jax.ShapeDtypeStruct((m, D_MODEL), BF16))
        out_specs.append(row)
    res = pl.pallas_call(
        functools.partial(_ffn_kernel, merge is not None, h_gain is not None),
        grid=(m // tm,),
        in_specs=in_specs, out_specs=out_specs, out_shape=out_shape,
        compiler_params=_cparams(("parallel",)),
        name="ffn_merge" if merge is not None else "ffn",
    )(*args)
    return res if h_gain is not None else res[0]


def _dn_kernel(bb_n, tt, chunk, tv_last, n_t, small,
               h_ref, s0_ref, cprev_ref, wqkv_ref, wz_ref, wab_ref, convw_ref, alog_ref, dtb_ref,
               normg_ref, wout_ref, wgate_ref,
               y_ref, snew_ref, cnew_ref,
               xbuf, cs, gb, zb, ob, s_all, u_s, w_s, qg_s, kdec_s, aqk_s, gl_s):
    t = pl.program_id(1)
    rows = bb_n * tt

    @pl.when(t == 0)
    def _():
        xbuf[:, 5:8, :] = cprev_ref[...]
        snew_ref[...] = s0_ref[...]

    h = h_ref[...]
    qkv = _dot(h, wqkv_ref[...])
    xbuf[:, 8:8 + tt, :] = qkv.reshape(bb_n, tt, DN_CONV_W)
    c = 0.0
    for j in range(DN_CONV):
        c = c + xbuf[:, 5 + j:5 + j + tt, :] * convw_ref[j:j + 1, :].reshape(1, 1, DN_CONV_W)

    @pl.when(t == n_t - 1)
    def _():
        cnew_ref[...] = xbuf[:, 5 + tv_last:8 + tv_last, :]

    if n_t > 1:
        xbuf[:, 0:8, :] = xbuf[:, tt:tt + 8, :]

    c = _silu(c)
    for grp in range(8):
        sl = slice(grp * 128, (grp + 1) * 128)
        xg = c[:, :, sl]
        xn = xg * lax.rsqrt(jnp.sum(xg * xg, axis=-1, keepdims=True) + EPS)
        if grp < DN_H:
            xn = xn * (DN_DK ** -0.5)
        cs[:, :, sl] = xn
    cs[:, :, 2 * DN_QK_W:] = c[:, :, 2 * DN_QK_W:]

    ab = _dot(h, wab_ref[...])
    g = -jnp.exp(alog_ref[...]) * jax.nn.softplus(ab + dtb_ref[...])
    lane = lax.broadcasted_iota(jnp.int32, (rows, 128), 1)
    gbv = jnp.where(lane < DN_H, g, jax.nn.sigmoid(ab)).reshape(bb_n, tt, 128)
    if tv_last < tt:
        trow = lax.broadcasted_iota(jnp.int32, (bb_n, tt, 128), 1)
        gbv = jnp.where(trow < tv_last, gbv, 0.0)
    gb[...] = gbv
    zb[...] = _silu(_dot(h, wz_ref[...])).reshape(bb_n, tt, DN_QK_W)

    n4 = DN_H * chunk
    ri = lax.broadcasted_iota(jnp.int32, (n4, n4), 0)
    ci = lax.broadcasted_iota(jnp.int32, (n4, n4), 1)
    same = (ri // chunk) == (ci // chunk)
    incl = same & (ci <= ri)
    strict = same & (ci < ri)
    lmat = incl.astype(F32)
    umat = strict.astype(F32)
    vmask = (lax.broadcasted_iota(jnp.int32, (n4, DN_H * 128), 0) // chunk
             == lax.broadcasted_iota(jnp.int32, (n4, DN_H * 128), 1) // 128)
    n_pow = int(np.log2(chunk))
    normg = normg_ref[...]
    n_ch = tt // chunk
    total = bb_n * n_ch
    solve_mm = _mm_hi if small else _mm

    for hd in range(DN_H):
        s_all[:, :, hd * 128:(hd + 1) * 128] = snew_ref[:, hd]

    def stack_rows(ref, b, r0, off):
        return jnp.concatenate([ref[b, pl.ds(r0, chunk), off + hd * 128:off + (hd + 1) * 128]
                                for hd in range(DN_H)], axis=0)

    def level1(k, slot):
        b = k // n_ch
        r0 = pl.multiple_of((k % n_ch) * chunk, chunk)
        q = stack_rows(cs, b, r0, 0)
        kk = stack_rows(cs, b, r0, DN_QK_W)
        v = stack_rows(cs, b, r0, 2 * DN_QK_W)
        gbc = gb[b, pl.ds(r0, chunk), :]
        g_st = jnp.concatenate([jnp.broadcast_to(gbc[:, hd:hd + 1], (chunk, 128)) for hd in range(DN_H)], axis=0)
        beta_st = jnp.concatenate([jnp.broadcast_to(gbc[:, DN_H + hd:DN_H + hd + 1], (chunk, 128))
                                   for hd in range(DN_H)], axis=0)
        g_sq = jnp.concatenate([g_st] * (n4 // 128), axis=1) if n4 >= 128 else g_st[:, :n4]
        gc = _mm_exact_left(lmat, g_st, small)
        if small:
            dmat = _mm_exact_left(lmat, g_sq * umat, small)
        else:
            gct = gc.T
            dmat = (jnp.concatenate([gc] * (n4 // 128), axis=1)
                    - jnp.concatenate([gct] * (n4 // 128), axis=0))
        gam = jnp.where(incl, jnp.exp(dmat), 0.0)
        eg = jnp.exp(gc)
        kb = kk * beta_st
        kq = _mm_nt(jnp.concatenate([kb, q], axis=0), kk, small)
        a_mat = jnp.where(strict, kq[:n4] * gam, 0.0)
        x = jnp.concatenate([v * beta_st, kb * eg], axis=1)
        p = -a_mat
        for i in range(n_pow):
            x = x + (_mm_hi if i < 2 else solve_mm)(p, x, small)
            if i < n_pow - 1:
                p = (_mm_hi if i < 1 else solve_mm)(p, p, small)
        gc_last = [gc[(hd + 1) * chunk - 1:(hd + 1) * chunk, :] for hd in range(DN_H)]
        gl_st = jnp.concatenate([jnp.broadcast_to(r, (chunk, 128)) for r in gc_last], axis=0)
        u_s[slot] = x[:, :128]
        w_s[slot] = x[:, 128:]
        qg_s[slot] = q * eg
        kdec_s[slot] = kk * jnp.exp(gl_st - gc)
        aqk_s[slot] = jnp.where(incl, kq[n4:] * gam, 0.0)
        gl_s[slot] = jnp.exp(jnp.concatenate(gc_last, axis=1))

    def level2(k, slot):
        b = k // n_ch
        r0 = pl.multiple_of((k % n_ch) * chunk, chunk)
        s_old = s_all[b]
        w = w_s[slot]
        qg = qg_s[slot]
        ws, qs = [], []
        for hd in range(DN_H):
            rs = slice(hd * chunk, (hd + 1) * chunk)
            r = _mm(jnp.concatenate([w[rs], qg[rs]], axis=0), s_old[:, hd * 128:(hd + 1) * 128], small)
            ws.append(r[:chunk])
            qs.append(r[chunk:])
        v_new = u_s[slot] - jnp.concatenate(ws, axis=0)
        o = jnp.concatenate(qs, axis=0) + _mm(aqk_s[slot], v_new, small)
        vbd = jnp.where(vmask, jnp.concatenate([v_new] * DN_H, axis=1), 0.0)
        s_all[b] = s_old * gl_s[slot] + _mm_tn(kdec_s[slot], vbd, small)
        on = _rms(o, normg)
        for hd in range(DN_H):
            sl = slice(hd * 128, (hd + 1) * 128)
            ob[b, pl.ds(r0, chunk), sl] = on[hd * chunk:(hd + 1) * chunk] * zb[b, pl.ds(r0, chunk), sl]

    level1(0, 0)

    def chunk_body(k, _):
        slot = k % 2
        level2(k, slot)
        level1(jnp.minimum(k + 1, total - 1), 1 - slot)
        return 0

    lax.fori_loop(0, total, chunk_body, 0)

    for hd in range(DN_H):
        snew_ref[:, hd] = s_all[:, :, hd * 128:(hd + 1) * 128]

    y = _dot(ob[...].reshape(rows, DN_QK_W).astype(BF16), wout_ref[...])
    y_ref[...] = y * jax.nn.sigmoid(_dot(h, wgate_ref[...]))


def _deltanet(h, s0, cprev, wts, *, n_seq, t_pad, t_valid, bb_n, tt, chunk):
    n_t = t_pad // tt
    assert n_t == 1 or t_valid == t_pad
    tv_last = t_valid - (n_t - 1) * tt
    rows = bb_n * tt
    small = chunk < 16
    n4 = DN_H * chunk
    const = lambda *shape: pl.BlockSpec(shape, lambda b, t: (0,) * len(shape))
    in_specs = [
        pl.BlockSpec((rows, D_MODEL), lambda b, t: (b * n_t + t, 0)),
        pl.BlockSpec((bb_n, DN_H, DN_DK, DN_DK), lambda b, t: (b, 0, 0, 0)),
        pl.BlockSpec((bb_n, DN_CONV - 1, DN_CONV_W), lambda b, t: (b, 0, 0)),
        const(D_MODEL, DN_CONV_W), const(D_MODEL, DN_QK_W), const(D_MODEL, 128),
        const(DN_CONV, DN_CONV_W), const(1, 128), const(1, 128), const(1, 128),
        const(DN_QK_W, D_MODEL), const(D_MODEL, D_MODEL),
    ]
    out_specs = [
        pl.BlockSpec((rows, D_MODEL), lambda b, t: (b * n_t + t, 0)),
        pl.BlockSpec((bb_n, DN_H, DN_DK, DN_DK), lambda b, t: (b, 0, 0, 0)),
        pl.BlockSpec((bb_n, DN_CONV - 1, DN_CONV_W), lambda b, t: (b, 0, 0)),
    ]
    out_shape = [
        jax.ShapeDtypeStruct((n_seq * t_pad, D_MODEL), F32),
        jax.ShapeDtypeStruct((n_seq, DN_H, DN_DK, DN_DK), F32),
        jax.ShapeDtypeStruct((n_seq, DN_CONV - 1, DN_CONV_W), F32),
    ]
    return pl.pallas_call(
        functools.partial(_dn_kernel, bb_n, tt, chunk, tv_last, n_t, small),
        grid=(n_seq // bb_n, n_t),
        in_specs=in_specs, out_specs=out_specs, out_shape=out_shape,
        scratch_shapes=[pltpu.VMEM((bb_n, tt + 8, DN_CONV_W), F32), pltpu.VMEM((bb_n, tt, DN_CONV_W), F32),
                        pltpu.VMEM((bb_n, tt, 128), F32), pltpu.VMEM((bb_n, tt, DN_QK_W), F32),
                        pltpu.VMEM((bb_n, tt, DN_QK_W), F32),
                        pltpu.VMEM((bb_n, DN_DK, DN_H * 128), F32)]
                       + [pltpu.VMEM((2, n4, 128), F32)] * 4
                       + [pltpu.VMEM((2, n4, n4), F32), pltpu.VMEM((2, 1, DN_H * 128), F32)],
        compiler_params=_cparams(("parallel", "arbitrary")),
        name="deltanet",
    )(h, s0, cprev, *wts)


def _sc_kernel(bb_n, tt, tv_last, n_t,
               h_ref, prev_ref, win_ref, convw_ref, wout_ref, wgate_ref,
               y_ref, new_ref, ubuf):
    t = pl.program_id(1)
    rows = bb_n * tt

    @pl.when(t == 0)
    def _():
        ubuf[:, 6:8, :] = prev_ref[...]

    h = h_ref[...]
    p = _dot(h, win_ref[...])
    bgate = p[:, :SC_W]
    u = p[:, SC_W:2 * SC_W] * p[:, 2 * SC_W:]
    ubuf[:, 8:8 + tt, :] = u.reshape(bb_n, tt, SC_W)
    y = 0.0
    for j in range(SC_CONV):
        y = y + ubuf[:, 6 + j:6 + j + tt, :] * convw_ref[j:j + 1, :].reshape(1, 1, SC_W)

    @pl.when(t == n_t - 1)
    def _():
        new_ref[...] = ubuf[:, 6 + tv_last:8 + tv_last, :]

    if n_t > 1:
        ubuf[:, 0:8, :] = ubuf[:, tt:tt + 8, :]

    z = (bgate * y.reshape(rows, SC_W)).astype(BF16)
    y_ref[...] = _dot(z, wout_ref[...]) * jax.nn.sigmoid(_dot(h, wgate_ref[...]))


def _shortconv(h, prev, wts, *, n_seq, t_pad, t_valid, bb_n, tt):
    n_t = t_pad // tt
    assert n_t == 1 or t_valid == t_pad
    tv_last = t_valid - (n_t - 1) * tt
    rows = bb_n * tt
    const = lambda *shape: pl.BlockSpec(shape, lambda b, t: (0,) * len(shape))
    return pl.pallas_call(
        functools.partial(_sc_kernel, bb_n, tt, tv_last, n_t),
        grid=(n_seq // bb_n, n_t),
        in_specs=[pl.BlockSpec((rows, D_MODEL), lambda b, t: (b * n_t + t, 0)),
                  pl.BlockSpec((bb_n, SC_CONV - 1, SC_W), lambda b, t: (b, 0, 0)),
                  const(D_MODEL, 3 * SC_W), const(SC_CONV, SC_W), const(SC_W, D_MODEL),
                  const(D_MODEL, D_MODEL)],
        out_specs=[pl.BlockSpec((rows, D_MODEL), lambda b, t: (b * n_t + t, 0)),
                   pl.BlockSpec((bb_n, SC_CONV - 1, SC_W), lambda b, t: (b, 0, 0))],
        out_shape=[jax.ShapeDtypeStruct((n_seq * t_pad, D_MODEL), F32),
                   jax.ShapeDtypeStruct((n_seq, SC_CONV - 1, SC_W), F32)],
        scratch_shapes=[pltpu.VMEM((bb_n, tt + 8, SC_W), F32)],
        compiler_params=_cparams(("parallel", "arbitrary")),
        name="shortconv",
    )(h, prev, *wts)


def _memkv_kernel(m_ref, g_ref, wkv_ref, kg_ref, k_ref, v_ref):
    n = _rms(m_ref[...], g_ref[...]).astype(BF16)
    kv = _dot(n, wkv_ref[...])
    kg = kg_ref[...]
    for hd in range(MEM_H):
        sl = slice(hd * MEM_HD, (hd + 1) * MEM_HD)
        k_ref[:, sl] = _rms(kv[:, sl], kg)
    v_ref[...] = kv[:, MEM_W:]


def _mem_kv(mem2d, norm_g, w_kv, k_gain, *, tm):
    m = mem2d.shape[0]
    return pl.pallas_call(
        _memkv_kernel,
        grid=(m // tm,),
        in_specs=[pl.BlockSpec((tm, D_MODEL), lambda i: (i, 0)),
                  pl.BlockSpec((1, D_MODEL), lambda i: (0, 0)),
                  pl.BlockSpec((D_MODEL, 2 * MEM_W), lambda i: (0, 0)),
                  pl.BlockSpec((1, MEM_HD), lambda i: (0, 0))],
        out_specs=[pl.BlockSpec((tm, MEM_W), lambda i: (i, 0))] * 2,
        out_shape=[jax.ShapeDtypeStruct((m, MEM_W), F32)] * 2,
        compiler_params=_cparams(("parallel",)),
        name="mem_kv",
    )(mem2d, norm_g, w_kv, k_gain)


def _memattn_kernel(bb_n, tt, small,
                    h_ref, mk_ref, mv_ref, wq_ref, qg_ref, wout_ref, wgate_ref,
                    y_ref, qs, ob):
    rows = bb_n * tt
    h = h_ref[...]
    q = _dot(h, wq_ref[...])
    qg = qg_ref[...]
    for hd in range(MEM_H):
        sl = slice(hd * MEM_HD, (hd + 1) * MEM_HD)
        qs[:, :, sl] = _rms(q[:, sl], qg).reshape(bb_n, tt, MEM_HD)

    def seq_body(b, _):
        for hd in range(MEM_H):
            sl = slice(hd * MEM_HD, (hd + 1) * MEM_HD)
            s = _mm_nt(qs[b, :, sl], mk_ref[b, :, sl], small) * (MEM_HD ** -0.5)
            s = s - jnp.max(s, axis=-1, keepdims=True)
            e = jnp.exp(s)
            p = e / jnp.sum(e, axis=-1, keepdims=True)
            ob[b, :, sl] = _mm(p, mv_ref[b, :, sl], small)
        return 0

    lax.fori_loop(0, bb_n, seq_body, 0)
    y = _dot(ob[...].reshape(rows, MEM_W).astype(BF16), wout_ref[...])
    y_ref[...] = y * jax.nn.sigmoid(_dot(h, wgate_ref[...]))


def _mem_attn(h, mk, mv, wts, *, n_seq, t_pad, bb_n, tt):
    n_t = t_pad // tt
    rows = bb_n * tt
    const = lambda *shape: pl.BlockSpec(shape, lambda b, t: (0,) * len(shape))
    return pl.pallas_call(
        functools.partial(_memattn_kernel, bb_n, tt, tt < 16),
        grid=(n_seq // bb_n, n_t),
        in_specs=[pl.BlockSpec((rows, D_MODEL), lambda b, t: (b * n_t + t, 0)),
                  pl.BlockSpec((bb_n, N_MEM, MEM_W), lambda b, t: (b, 0, 0)),
                  pl.BlockSpec((bb_n, N_MEM, MEM_W), lambda b, t: (b, 0, 0)),
                  const(D_MODEL, MEM_W), const(1, MEM_HD), const(MEM_W, D_MODEL), const(D_MODEL, D_MODEL)],
        out_specs=pl.BlockSpec((rows, D_MODEL), lambda b, t: (b * n_t + t, 0)),
        out_shape=jax.ShapeDtypeStruct((n_seq * t_pad, D_MODEL), F32),
        scratch_shapes=[pltpu.VMEM((bb_n, tt, MEM_W), F32), pltpu.VMEM((bb_n, tt, MEM_W), F32)],
        compiler_params=_cparams(("parallel", "arbitrary")),
        name="mem_attn",
    )(h, mk, mv, *wts)


def _mlaproj_kernel(h_ref, cos_ref, sin_ref, wq_ref, qna_ref, wqp_ref, wqs_ref, qg_ref,
                    wkv_ref, kvna_ref, wkr_ref, wkrs_ref, wuk_ref, wuv_ref, kg_ref,
                    q_ref, ckv_ref, kr_ref, k_ref, v_ref):
    h = h_ref[...]
    cos = cos_ref[...]
    sin = sin_ref[...]
    cqn = _rms(_dot(h, wq_ref[...]), qna_ref[...]).astype(BF16)
    q_raw = _dot(cqn, wqp_ref[...])
    q_swp = _dot(cqn, wqs_ref[...])
    qg = qg_ref[...]
    inv_n = 1.0 / MLA_QK
    for hd in range(MLA_H):
        sl = slice(hd * MLA_LANES, (hd + 1) * MLA_LANES)
        qh = q_raw[:, sl] * cos + q_swp[:, sl] * sin
        ms = jnp.sum(qh * qh, axis=-1, keepdims=True) * inv_n
        q_ref[:, sl] = (qh * lax.rsqrt(ms + EPS) * qg).astype(q_ref.dtype)

    ckv = _rms(_dot(h, wkv_ref[...]), kvna_ref[...])
    ckv_ref[...] = ckv
    krp = _dot(h, wkr_ref[...]) * cos + _dot(h, wkrs_ref[...]) * sin
    kr_ref[...] = krp[:, :MLA_ROPE]

    cb = ckv.astype(BF16)
    k_raw = _dot(cb, wuk_ref[...])
    v_ref[...] = _dot(cb, wuv_ref[...]).astype(BF16)
    kg = kg_ref[...]
    for hd in range(MLA_H):
        sl = slice(hd * MLA_LANES, (hd + 1) * MLA_LANES)
        kh = k_raw[:, sl] + krp
        ms = jnp.sum(kh * kh, axis=-1, keepdims=True) * inv_n
        k_ref[:, sl] = (kh * lax.rsqrt(ms + EPS) * kg).astype(k_ref.dtype)


def _mla_proj(h, cos, sin, wts, *, tm, n_tab, qk_dtype):
    m = h.shape[0]
    hw = MLA_H * MLA_LANES
    vw = MLA_H * MLA_V
    const = lambda *shape: pl.BlockSpec(shape, lambda i: (0,) * len(shape))
    row = lambda width: pl.BlockSpec((tm, width), lambda i: (i, 0))
    in_specs = [row(D_MODEL),
                pl.BlockSpec((tm, MLA_LANES), lambda i: (i % n_tab, 0)),
                pl.BlockSpec((tm, MLA_LANES), lambda i: (i % n_tab, 0)),
                const(D_MODEL, MLA_RANK), const(1, MLA_RANK), const(MLA_RANK, hw), const(MLA_RANK, hw),
                const(1, MLA_LANES),
                const(D_MODEL, MLA_RANK), const(1, MLA_RANK), const(D_MODEL, MLA_LANES), const(D_MODEL, MLA_LANES),
                const(MLA_RANK, hw), const(MLA_RANK, vw), const(1, MLA_LANES)]
    return pl.pallas_call(
        _mlaproj_kernel,
        grid=(m // tm,),
        in_specs=in_specs,
        out_specs=[row(hw), row(MLA_RANK), row(MLA_ROPE), row(hw), row(vw)],
        out_shape=[jax.ShapeDtypeStruct((m, hw), qk_dtype),
                   jax.ShapeDtypeStruct((m, MLA_RANK), F32),
                   jax.ShapeDtypeStruct((m, MLA_ROPE), F32),
                   jax.ShapeDtypeStruct((m, hw), qk_dtype),
                   jax.ShapeDtypeStruct((m, vw), BF16)],
        compiler_params=_cparams(("parallel",)),
        name="mla_proj",
    )(h, cos, sin, *wts)


def _flash_kernel(tq, q_ref, k_ref, v_ref, o_ref, m_scr, l_scr, acc_scr):
    qi = pl.program_id(1)
    ki = pl.program_id(2)
    c2 = (MLA_QK ** -0.5) * LOG2E
    n_rep = tq // 128

    @pl.when(ki == 0)
    def _():
        m_scr[...] = jnp.full(m_scr.shape, -jnp.inf, F32)
        l_scr[...] = jnp.zeros_like(l_scr)
        acc_scr[...] = jnp.zeros_like(acc_scr)

    def compute(diag):
        if diag:
            row = lax.broadcasted_iota(jnp.int32, (tq, tq), 0)
            col = lax.broadcasted_iota(jnp.int32, (tq, tq), 1)
            keep = col <= row
        ones = jnp.ones((tq, 128), BF16)
        for hd in range(MLA_H):
            sl = slice(hd * MLA_LANES, (hd + 1) * MLA_LANES)
            s = _dot_nt(q_ref[:, sl], k_ref[:, sl]) * c2
            if diag:
                s = jnp.where(keep, s, -jnp.inf)
            m_old = m_scr[hd]
            m_new = jnp.maximum(m_old, jnp.max(s, axis=-1, keepdims=True))
            alpha = jnp.exp2(m_old - m_new)
            p = jnp.exp2(s - jnp.concatenate([m_new] * n_rep, axis=1)).astype(BF16)
            pair = hd // 2
            vext = jnp.concatenate([v_ref[:, pair * 128:(pair + 1) * 128], ones], axis=1)
            r = _dot(p, vext)
            acc_scr[hd] = alpha * acc_scr[hd] + r[:, :128]
            l_scr[hd] = alpha * l_scr[hd] + r[:, 128:]
            m_scr[hd] = m_new

    @pl.when(ki < qi)
    def _():
        compute(False)

    @pl.when(ki == qi)
    def _():
        compute(True)
        lane = lax.broadcasted_iota(jnp.int32, (tq, 128), 1)
        for pair in range(MLA_H // 2):
            even = acc_scr[2 * pair] / l_scr[2 * pair]
            odd = acc_scr[2 * pair + 1] / l_scr[2 * pair + 1]
            o_ref[:, pair * 128:(pair + 1) * 128] = jnp.where(lane < MLA_V, even, odd).astype(o_ref.dtype)


def _mla_prompt_attn(q, k, v, *, n_seq, seq, tq):
    nq = seq // tq
    hw = MLA_H * MLA_LANES
    vw = MLA_H * MLA_V
    return pl.pallas_call(
        functools.partial(_flash_kernel, tq),
        grid=(n_seq, nq, nq),
        in_specs=[pl.BlockSpec((tq, hw), lambda b, i, j: (b * nq + i, 0)),
                  pl.BlockSpec((tq, hw), lambda b, i, j: (b * nq + jnp.minimum(i, j), 0)),
                  pl.BlockSpec((tq, vw), lambda b, i, j: (b * nq + jnp.minimum(i, j), 0))],
        out_specs=pl.BlockSpec((tq, vw), lambda b, i, j: (b * nq + i, 0)),
        out_shape=jax.ShapeDtypeStruct((n_seq * seq, vw), BF16),
        scratch_shapes=[pltpu.VMEM((MLA_H, tq, 128), F32)] * 3,
        compiler_params=_cparams(("parallel", "parallel", "arbitrary")),
        name="mla_flash",
    )(q, k, v)


SAMPLE_ROWS = 8
SUB_KEYS = 1024


def _mla_sample_kernel(layer, n_seq, n_pg, n_steps, t_valid,
                       pt_ref, q_ref, knew_ref, cnew_ref, wukp_ref, wukt_ref, kg_ref, ckv_hbm, kr_hbm,
                       o_ref,
                       lhs, qabs, qrope, qblk, m_scr, l_scr, acc_scr, cbuf, rbuf, sem):
    b = pl.program_id(0)
    st = pl.program_id(1)
    c2 = (MLA_QK ** -0.5) * LOG2E
    inv_n = 1.0 / MLA_QK
    tk = n_pg * PAGE
    nq = SAMPLE_ROWS
    n_up = MLA_H * MLA_NOPE
    g = b * n_steps + st
    slot = g % 2

    def page_copies(bb, ss, sl):
        cps = []
        for i in range(n_pg):
            page = pt_ref[bb, ss * n_pg + i]
            cps.append(pltpu.make_async_copy(ckv_hbm.at[layer, page], cbuf.at[sl, pl.ds(i * PAGE, PAGE), :],
                                             sem.at[sl, 0]))
            cps.append(pltpu.make_async_copy(kr_hbm.at[layer, page], rbuf.at[sl, :, pl.ds(i * PAGE, PAGE)],
                                             sem.at[sl, 1]))
        return cps

    @pl.when(g == 0)
    def _():
        for cp in page_copies(0, 0, 0):
            cp.start()

    for cp in page_copies(b, st, slot):
        cp.wait()

    @pl.when(g + 1 < n_seq * n_steps)
    def _():
        last = st == n_steps - 1
        for cp in page_copies(jnp.where(last, b + 1, b), jnp.where(last, 0, st + 1), 1 - slot):
            cp.start()

    @pl.when(st == 0)
    def _():
        m_scr[...] = jnp.full(m_scr.shape, -jnp.inf, F32)
        l_scr[...] = jnp.zeros_like(l_scr)
        acc_scr[...] = jnp.zeros_like(acc_scr)
        kg = kg_ref[...]
        q = q_ref[...]
        lane_head = lax.broadcasted_iota(jnp.int32, q.shape, 1) // MLA_LANES
        for hd in range(MLA_H):
            sl = slice(hd * MLA_LANES, (hd + 1) * MLA_LANES)
            qk = q[:, sl] * kg
            qabs[hd * nq:(hd + 1) * nq, :] = lax.dot_general(
                qk, wukp_ref[:, sl].astype(F32), (((1,), (1,)), ((), ())), preferred_element_type=F32,
                precision=lax.Precision.HIGHEST)
            qrope[hd * nq:(hd + 1) * nq, :] = qk[:, :MLA_ROPE]
            qblk[hd * nq:(hd + 1) * nq, :] = jnp.where(lane_head == hd, q, 0.0)
        lhs[:n_up, :] = wukt_ref[...]
        lhs[n_up:, :] = qabs[...].astype(BF16)

    lhs_v = lhs[...]
    qr = qrope[...].astype(BF16)
    cbs, scores = [], []
    for j in range(tk // SUB_KEYS):
        cb = cbuf[slot, j * SUB_KEYS:(j + 1) * SUB_KEYS, :].astype(BF16)
        krt = rbuf[slot, :, j * SUB_KEYS:(j + 1) * SUB_KEYS]
        big = _dot_nt(lhs_v, cb)
        knt = big[:n_up]
        ssq = jnp.sum((knt * knt).reshape(MLA_H, MLA_NOPE, SUB_KEYS), axis=1)
        ssq_r = jnp.sum(krt * krt, axis=0, keepdims=True)
        rs = lax.rsqrt((ssq + ssq_r) * inv_n + EPS) * c2
        s = big[n_up:] + _dot(qr, krt.astype(BF16))
        scores.append(jnp.concatenate([s[hd * nq:(hd + 1) * nq, :] * rs[hd:hd + 1, :] for hd in range(MLA_H)],
                                      axis=0))
        cbs.append(cb)
    m_old = m_scr[...]
    m_new = m_old
    for s in scores:
        m_new = jnp.maximum(m_new, jnp.max(s, axis=-1, keepdims=True))
    alpha = jnp.exp2(m_old - m_new)
    l_new = alpha * l_scr[...]
    acc = alpha * acc_scr[...]
    for s, cb in zip(scores, cbs):
        p = jnp.exp2(s - m_new)
        l_new = l_new + jnp.sum(p, axis=-1, keepdims=True)
        acc = acc + _dot(p.astype(BF16), cb)
    l_scr[...] = l_new
    acc_scr[...] = acc
    m_scr[...] = m_new

    @pl.when(st == n_steps - 1)
    def _():
        cn = cnew_ref[...]
        sn = _dot_nt(_rnd(qblk[...]), _rnd(knew_ref[...])) * c2
        row = lax.broadcasted_iota(jnp.int32, sn.shape, 0) % nq
        col = lax.broadcasted_iota(jnp.int32, sn.shape, 1)
        sn = jnp.where((col <= row) & (col < t_valid), sn, -jnp.inf)
        m_o = m_scr[...]
        m_n = jnp.maximum(m_o, jnp.max(sn, axis=-1, keepdims=True))
        al = jnp.exp2(m_o - m_n)
        pn = jnp.exp2(sn - m_n)
        l_f = al * l_scr[...] + jnp.sum(pn, axis=-1, keepdims=True)
        o_ref[...] = (al * acc_scr[...] + _dot(_rnd(pn), _rnd(cn))) / l_f


def _mla_sample_attn(page_table, q, k_new, c_new, wts, ckv_pool, kr_pool_t, layer, *, n_seq, t_valid, n_pg):
    n_pages = page_table.shape[1]
    n_steps = n_pages // n_pg
    hw = MLA_H * MLA_LANES
    tk = n_pg * PAGE
    nq = SAMPLE_ROWS
    nr = MLA_H * nq

    const = lambda *shape: pl.BlockSpec(shape, lambda b, s, pt: (0,) * len(shape))
    hbm = pl.BlockSpec(memory_space=pl.ANY)
    in_specs = [pl.BlockSpec((nq, hw), lambda b, s, pt: (b, 0)),
                pl.BlockSpec((nq, hw), lambda b, s, pt: (b, 0)),
                pl.BlockSpec((nq, MLA_RANK), lambda b, s, pt: (b, 0)),
                const(MLA_RANK, hw), const(MLA_H * MLA_NOPE, MLA_RANK), const(1, MLA_LANES),
                hbm, hbm]
    grid_spec = pltpu.PrefetchScalarGridSpec(
        num_scalar_prefetch=1,
        grid=(n_seq, n_steps),
        in_specs=in_specs,
        out_specs=pl.BlockSpec((nr, MLA_RANK), lambda b, s, pt: (b, 0)),
        scratch_shapes=[pltpu.VMEM((MLA_H * MLA_NOPE + nr, MLA_RANK), BF16),
                        pltpu.VMEM((nr, MLA_RANK), F32), pltpu.VMEM((nr, MLA_ROPE), F32),
                        pltpu.VMEM((nr, hw), F32),
                        pltpu.VMEM((nr, 1), F32), pltpu.VMEM((nr, 1), F32), pltpu.VMEM((nr, MLA_RANK), F32),
                        pltpu.VMEM((2, tk, MLA_RANK), F32), pltpu.VMEM((2, MLA_ROPE, tk), F32),
                        pltpu.SemaphoreType.DMA((2, 2))],
    )
    return pl.pallas_call(
        functools.partial(_mla_sample_kernel, layer, n_seq, n_pg, n_steps, t_valid),
        grid_spec=grid_spec,
        out_shape=jax.ShapeDtypeStruct((n_seq * nr, MLA_RANK), F32),
        compiler_params=_cparams(("arbitrary", "arbitrary")),
        name="mla_paged",
    )(page_table, q, k_new, c_new, *wts, ckv_pool, kr_pool_t)


def _mla_up_kernel(n_seq, pc_ref, h_ref, wuvs_ref, wout_ref, wgate_ref, y_ref):
    nq = SAMPLE_ROWS
    o = jnp.zeros((n_seq * nq, MLA_H * MLA_V), F32)
    for hd in range(MLA_H):
        pch = pc_ref[:, hd * nq:(hd + 1) * nq, :].reshape(n_seq * nq, MLA_RANK)
        o = o + _dot(pch.astype(BF16), wuvs_ref[hd])
    y_ref[...] = _dot(o.astype(BF16), wout_ref[...]) * jax.nn.sigmoid(_dot(h_ref[...], wgate_ref[...]))


def _mla_up_proj_gate(pc, h, wuv_sel, w_out, w_gate, *, n_seq):
    rows = n_seq * SAMPLE_ROWS
    return pl.pallas_call(
        functools.partial(_mla_up_kernel, n_seq),
        out_shape=jax.ShapeDtypeStruct((rows, D_MODEL), F32),
        compiler_params=pltpu.CompilerParams(vmem_limit_bytes=VMEM_LIMIT),
        name="mla_up_proj_gate",
    )(pc.reshape(n_seq, MLA_H * SAMPLE_ROWS, MLA_RANK), h, wuv_sel, w_out, w_gate)


def _projgate_kernel(o_ref, h_ref, wout_ref, wgate_ref, y_ref):
    y_ref[...] = _dot(o_ref[...].astype(BF16), wout_ref[...]) * jax.nn.sigmoid(_dot(h_ref[...], wgate_ref[...]))


def _proj_gate(o, h, w_out, w_gate, *, tm):
    m, kdim = o.shape
    return pl.pallas_call(
        _projgate_kernel,
        grid=(m // tm,),
        in_specs=[pl.BlockSpec((tm, kdim), lambda i: (i, 0)),
                  pl.BlockSpec((tm, D_MODEL), lambda i: (i, 0)),
                  pl.BlockSpec((kdim, D_MODEL), lambda i: (0, 0)),
                  pl.BlockSpec((D_MODEL, D_MODEL), lambda i: (0, 0))],
        out_specs=pl.BlockSpec((tm, D_MODEL), lambda i: (i, 0)),
        out_shape=jax.ShapeDtypeStruct((m, D_MODEL), F32),
        compiler_params=_cparams(("parallel",)),
        name="proj_gate",
    )(o, h, w_out, w_gate)


def _pad_lanes(x, width):
    return jnp.pad(x, [(0, 0)] * (x.ndim - 1) + [(0, width - x.shape[-1])])


def _mla_head_layout(nope, r1, r2):
    z = jnp.zeros(nope.shape[:-1] + (MLA_LANES - MLA_QK,), nope.dtype)
    x = jnp.concatenate([r1, r2, nope, z], axis=-1)
    return x.reshape(x.shape[:-2] + (MLA_H * MLA_LANES,))


def _gain_layout(g):
    half = MLA_ROPE // 2
    return jnp.concatenate([g[MLA_NOPE:MLA_NOPE + half], g[MLA_NOPE + half:], g[:MLA_NOPE],
                            jnp.zeros((MLA_LANES - MLA_QK,), g.dtype)]).reshape(1, MLA_LANES)


def _rope_tables(pos):
    half = MLA_ROPE // 2
    inv = ROPE_THETA ** (-jnp.arange(half, dtype=F32) / half)
    ang = pos.astype(F32)[:, None] * inv
    cos, sin = jnp.cos(ang), jnp.sin(ang)
    n = pos.shape[0]
    cos_t = jnp.concatenate([cos, cos, jnp.ones((n, MLA_NOPE), F32), jnp.zeros((n, MLA_LANES - MLA_QK), F32)], -1)
    sin_t = jnp.concatenate([sin, sin, jnp.zeros((n, MLA_LANES - MLA_ROPE), F32)], -1)
    return cos_t, sin_t


def _layer_weights(l, p):
    w_in = p['w_in'][l]
    sizes = (DN_CONV_W, DN_QK_W, DN_H, DN_H, SC_W, SC_W, SC_W, MLA_RANK, MLA_RANK, MLA_ROPE, MEM_W, 4 * D_MODEL)
    offs = np.concatenate([[0], np.cumsum(sizes)])
    seg = [w_in[:, offs[i]:offs[i + 1]] for i in range(len(sizes))]
    bf = lambda x: x.astype(BF16)
    row = lambda x: x.reshape(1, -1)
    gates = [bf(seg[11][:, i * D_MODEL:(i + 1) * D_MODEL]) for i in range(4)]
    half = MLA_ROPE // 2

    w = {}
    w['ffn1'] = (row(p['ffn1_norm'][l]), bf(p['ffn1_w_gu'][l]), bf(p['ffn1_w_down'][l]))
    w['ffn2'] = (row(p['ffn2_norm'][l]), bf(p['ffn2_w_gu'][l]), bf(p['ffn2_w_down'][l]))
    w['mix_norm'] = row(p['mix_norm'][l])
    w['w_o'] = bf(p['w_o'][l])
    w['dn'] = (bf(seg[0]), bf(seg[1]), bf(_pad_lanes(jnp.concatenate([seg[2], seg[3]], 1), 128)),
               p['dn_conv_w'][l], _pad_lanes(row(p['dn_A_log'][l]), 128), _pad_lanes(row(p['dn_dt_bias'][l]), 128),
               row(p['dn_norm'][l]), bf(p['dn_w_out'][l]), gates[0])
    w['sc'] = (bf(jnp.concatenate([seg[4], seg[5], seg[6]], 1)), p['sc_conv_w'][l], bf(p['sc_w_out'][l]), gates[1])

    wq = p['mla_w_q_b'][l].reshape(MLA_RANK, MLA_H, MLA_QK)
    q_nope, q_r1, q_r2 = wq[..., :MLA_NOPE], wq[..., MLA_NOPE:MLA_NOPE + half], wq[..., MLA_NOPE + half:]
    wq_perm = _mla_head_layout(q_nope, q_r1, q_r2)
    wq_swap = _mla_head_layout(jnp.zeros_like(q_nope), -q_r2, q_r1)
    wkr = seg[9]
    wkr_pad = _pad_lanes(wkr, MLA_LANES)
    wkr_swap = _pad_lanes(jnp.concatenate([-wkr[:, half:], wkr[:, :half]], 1), MLA_LANES)
    wkv = p['mla_w_kv_b'][l].reshape(MLA_RANK, MLA_H, MLA_NOPE + MLA_V)
    w_uk, w_uv = wkv[..., :MLA_NOPE], wkv[..., MLA_NOPE:]
    zr = jnp.zeros((MLA_RANK, MLA_H, half), F32)
    wuk_perm = bf(_mla_head_layout(w_uk, zr, zr))
    k_gain = _gain_layout(p['mla_k_norm'][l])
    w['mla_proj'] = (bf(seg[7]), row(p['mla_q_norm_a'][l]), bf(wq_perm), bf(wq_swap), _gain_layout(p['mla_q_norm'][l]),
                     bf(seg[8]), row(p['mla_kv_norm_a'][l]), bf(wkr_pad), bf(wkr_swap),
                     wuk_perm, bf(w_uv.reshape(MLA_RANK, MLA_H * MLA_V)), k_gain)
    eye = jnp.eye(MLA_H, dtype=F32)
    w['mla_wuv_sel'] = bf((w_uv[None] * eye[:, None, :, None]).reshape(MLA_H, MLA_RANK, MLA_H * MLA_V))
    w['mla_sample'] = (wuk_perm, bf(w_uk.reshape(MLA_RANK, MLA_H * MLA_NOPE).T), k_gain)
    w['mla_out'] = (bf(p['mla_w_out'][l]), gates[2])
    w['mem_kv'] = (row(p['mem_norm'][l]), bf(p['mem_w_kv'][l]), row(p['mem_k_norm'][l]))
    w['mem'] = (bf(seg[10]), row(p['mem_q_norm'][l]), bf(p['mem_w_out'][l]), gates[3])
    return w


def _group_layer(x, w, *, n_seq, t_pad, t_valid, tm, bb_n, tt, chunk, dn_state, sc_state, mem_kv, cos, sin,
                 n_tab, mla_attend, q_dtype):
    cfg = dict(n_seq=n_seq, t_pad=t_pad, bb_n=bb_n, tt=tt)
    x1, h = _ffn(x, *w['ffn1'], tm=tm, h_gain=w['mix_norm'])
    y_dn, dn_s, dn_c = _deltanet(h, dn_state[0], dn_state[1], w['dn'], t_valid=t_valid, chunk=chunk, **cfg)
    y_sc, sc_c = _shortconv(h, sc_state, w['sc'], t_valid=t_valid, **cfg)
    q, ckv, kr, k, v = _mla_proj(h, cos, sin, w['mla_proj'], tm=tm, n_tab=n_tab, qk_dtype=q_dtype)
    if mla_attend is None:
        o = _mla_prompt_attn(q, k, v, n_seq=n_seq, seq=t_pad, tq=tm)
        y_mla = _proj_gate(o, h, *w['mla_out'], tm=tm)
    else:
        y_mla = mla_attend(q, k, ckv, h)
    y_mem = _mem_attn(h, mem_kv[0], mem_kv[1], w['mem'], **cfg)
    x3 = _ffn(x1, *w['ffn2'], tm=min(tm, 256), merge=((y_dn, y_sc, y_mla, y_mem), w['w_o']))
    return x3, dn_s, dn_c, sc_c, ckv, kr


def kernel(x_prompt, x_sample, state_dn_S, state_dn_conv, state_sc_conv, cache_mla_ckv, cache_mla_krope, cache_mem_k, cache_mem_v, page_table, mem_prompt, ffn1_norm, ffn1_w_gu, ffn1_w_down, mix_norm, w_in, dn_conv_w, dn_A_log, dn_dt_bias, dn_norm, dn_w_out, sc_conv_w, sc_w_out, mla_q_norm_a, mla_w_q_b, mla_kv_norm_a, mla_w_kv_b, mla_q_norm, mla_k_norm, mla_w_out, mem_norm, mem_w_kv, mem_q_norm, mem_k_norm, mem_w_out, w_o, ffn2_norm, ffn2_w_gu, ffn2_w_down):
    params = dict(ffn1_norm=ffn1_norm, ffn1_w_gu=ffn1_w_gu, ffn1_w_down=ffn1_w_down, mix_norm=mix_norm, w_in=w_in,
                  dn_conv_w=dn_conv_w, dn_A_log=dn_A_log, dn_dt_bias=dn_dt_bias, dn_norm=dn_norm, dn_w_out=dn_w_out,
                  sc_conv_w=sc_conv_w, sc_w_out=sc_w_out, mla_q_norm_a=mla_q_norm_a, mla_w_q_b=mla_w_q_b,
                  mla_kv_norm_a=mla_kv_norm_a, mla_w_kv_b=mla_w_kv_b, mla_q_norm=mla_q_norm, mla_k_norm=mla_k_norm,
                  mla_w_out=mla_w_out, mem_norm=mem_norm, mem_w_kv=mem_w_kv, mem_q_norm=mem_q_norm,
                  mem_k_norm=mem_k_norm, mem_w_out=mem_w_out, w_o=w_o, ffn2_norm=ffn2_norm, ffn2_w_gu=ffn2_w_gu,
                  ffn2_w_down=ffn2_w_down)
    depth = w_in.shape[0]
    bp, seq, _ = x_prompt.shape
    bs, td, _ = x_sample.shape
    tds = SAMPLE_ROWS
    n_pages = page_table.shape[1]
    past = n_pages * PAGE
    krope_t = jnp.transpose(cache_mla_krope, (0, 1, 3, 2))

    cos_p, sin_p = _rope_tables(jnp.arange(seq))
    cos_s, sin_s = _rope_tables(past + jnp.arange(tds))
    cos_s, sin_s = jnp.tile(cos_s, (bs, 1)), jnp.tile(sin_s, (bs, 1))

    xp = x_prompt.reshape(bp * seq, D_MODEL)
    xs = jnp.pad(x_sample, ((0, 0), (0, tds - td), (0, 0))).reshape(bs * tds, D_MODEL)
    zero_s = jnp.zeros((bp, DN_H, DN_DK, DN_DK), F32)
    zero_dc = jnp.zeros((bp, DN_CONV - 1, DN_CONV_W), F32)
    zero_sc = jnp.zeros((bp, SC_CONV - 1, SC_W), F32)
    mem2d = mem_prompt.reshape(bp * N_MEM, D_MODEL)

    outs = {k: [] for k in ('pS', 'pdc', 'psc', 'pckv', 'pkr', 'pmk', 'pmv', 'sS', 'sdc', 'ssc', 'sckv', 'skr')}
    tm_p = 512
    for l in range(depth):
        w = _layer_weights(l, params)
        mk, mv = _mem_kv(mem2d, *w['mem_kv'], tm=tm_p)
        mk3, mv3 = mk.reshape(bp, N_MEM, MEM_W), mv.reshape(bp, N_MEM, MEM_W)
        xp, s_p, dc_p, sc_p, ckv_p, kr_p = _group_layer(
            xp, w, n_seq=bp, t_pad=seq, t_valid=seq, tm=tm_p, bb_n=1, tt=tm_p, chunk=DN_CHUNK,
            dn_state=(zero_s, zero_dc), sc_state=zero_sc, mem_kv=(mk3, mv3), cos=cos_p, sin=sin_p,
            n_tab=seq // tm_p, mla_attend=None, q_dtype=BF16)
        outs['pS'].append(s_p); outs['pdc'].append(dc_p); outs['psc'].append(sc_p)
        outs['pckv'].append(ckv_p.reshape(bp, seq, MLA_RANK)); outs['pkr'].append(kr_p.reshape(bp, seq, MLA_ROPE))
        outs['pmk'].append(mk.reshape(bp, N_MEM, MEM_H, MEM_HD)); outs['pmv'].append(mv.reshape(bp, N_MEM, MEM_H, MEM_HD))

        def attend(q, k, ckv, h, l=l, w=w):
            pc = _mla_sample_attn(page_table, q, k, ckv, w['mla_sample'], cache_mla_ckv, krope_t, l,
                                  n_seq=bs, t_valid=td, n_pg=32)
            return _mla_up_proj_gate(pc, h, w['mla_wuv_sel'], *w['mla_out'], n_seq=bs)

        xs, s_s, dc_s, sc_s, ckv_s, kr_s = _group_layer(
            xs, w, n_seq=bs, t_pad=tds, t_valid=td, tm=bs * tds, bb_n=8, tt=tds, chunk=tds,
            dn_state=(state_dn_S[l], state_dn_conv[l]), sc_state=state_sc_conv[l],
            mem_kv=(cache_mem_k[l].reshape(bs, N_MEM, MEM_W), cache_mem_v[l].reshape(bs, N_MEM, MEM_W)),
            cos=cos_s, sin=sin_s, n_tab=1, mla_attend=attend, q_dtype=F32)
        outs['sS'].append(s_s); outs['sdc'].append(dc_s); outs['ssc'].append(sc_s)
        outs['sckv'].append(ckv_s.reshape(bs, tds, MLA_RANK)[:, :td])
        outs['skr'].append(kr_s.reshape(bs, tds, MLA_ROPE)[:, :td])

    st = lambda k: jnp.stack(outs[k])
    y_prompt = xp.reshape(bp, seq, D_MODEL)
    y_sample = xs.reshape(bs, tds, D_MODEL)[:, :td]
    return (y_prompt, y_sample, st('pS'), st('pdc'), st('psc'), st('pckv'), st('pkr'), st('pmk'), st('pmv'),
            st('sS'), st('sdc'), st('ssc'), st('sckv'), st('skr'))
```

```python
import functools

import numpy as np
import jax
import jax.numpy as jnp
from jax import lax
from jax.experimental import pallas as pl
from jax.experimental.pallas import tpu as pltpu

F32 = jnp.float32
BF16 = jnp.bfloat16

D_MODEL = 1024
D_FF = 2816
EPS = 1e-6
N_MEM = 256
PAGE = 128
DN_H = 4
DN_DK = 128
DN_QK_W = 512
DN_CONV_W = 1536
DN_CONV = 4
DN_CHUNK = 64
DN_GROUP = 2
SC_W = 512
SC_CONV = 3
MLA_H = 8
MLA_RANK = 256
MLA_NOPE = 64
MLA_ROPE = 32
MLA_V = 64
MLA_QK = 96
MLA_LANES = 128
ROPE_THETA = 10000.0
LOG2E = 1.4426950408889634
MEM_H = 4
MEM_HD = 128
MEM_W = 512

VMEM_LIMIT = 56 * 1024 * 1024


def _cparams(sem):
    return pltpu.CompilerParams(dimension_semantics=sem, vmem_limit_bytes=VMEM_LIMIT)


def _rms(x, g):
    ms = jnp.mean(x * x, axis=-1, keepdims=True)
    return x * lax.rsqrt(ms + EPS) * g


def _silu(x):
    return x * jax.nn.sigmoid(x)


def _rnd(x):
    return x.astype(BF16).astype(F32)


def _dot(a, b):
    return jnp.dot(a, b, preferred_element_type=F32)


def _dot_nt(a, b):
    return lax.dot_general(a, b, (((1,), (1,)), ((), ())), preferred_element_type=F32)


def _mm(a, b, small):
    if small:
        return _dot(_rnd(a), _rnd(b))
    return _dot(a.astype(BF16), b.astype(BF16))


def _mm_nt(a, b, small):
    if small:
        return _dot_nt(_rnd(a), _rnd(b))
    return _dot_nt(a.astype(BF16), b.astype(BF16))


def _mm_tn(a, b, small):
    dn = (((0,), (0,)), ((), ()))
    if small:
        return lax.dot_general(_rnd(a), _rnd(b), dn, preferred_element_type=F32)
    return lax.dot_general(a.astype(BF16), b.astype(BF16), dn, preferred_element_type=F32)


def _run_interleaved(gens):
    gens = list(gens)
    while gens:
        for g in list(gens):
            try:
                next(g)
            except StopIteration:
                gens.remove(g)


def _split2(x):
    hi = x.astype(BF16)
    lo = (x - hi.astype(F32)).astype(BF16)
    return hi, lo


def _mm_hi(a, b, small):
    if small:
        return jnp.dot(a, b, preferred_element_type=F32, precision=lax.Precision.HIGHEST)
    ah, al = _split2(a)
    bh, bl = _split2(b)
    return _dot(ah, bh) + (_dot(ah, bl) + _dot(al, bh))


def _mm_exact_left(lmat, b, small):
    if small:
        return jnp.dot(lmat, b, preferred_element_type=F32, precision=lax.Precision.HIGHEST)
    lb = lmat.astype(BF16)
    b1 = b.astype(BF16)
    r1 = b - b1.astype(F32)
    b2 = r1.astype(BF16)
    b3 = (r1 - b2.astype(F32)).astype(BF16)
    return _dot(lb, b1) + (_dot(lb, b2) + _dot(lb, b3))


FFN_SPLIT = 2


def _resident(*shape):
    return pl.BlockSpec(shape, lambda i: (0,) * len(shape), pipeline_mode=pl.Buffered(1))


def _ffn_kernel(merge, emit_h, *refs):
    it = iter(refs)
    x_ref = next(it)
    if merge:
        y_refs = [next(it) for _ in range(4)]
        wo_ref = next(it)
    g_ref, wgu_ref, wd_ref = next(it), next(it), next(it)
    if emit_h:
        g2_ref = next(it)
    o_ref = next(it)
    if emit_h:
        h_ref = next(it)

    x = x_ref[...]
    if merge:
        m = ((y_refs[0][...] + y_refs[1][...]) + y_refs[2][...]) + y_refs[3][...]
        x = x + _dot(m.astype(BF16), wo_ref[...])
    h = _rms(x, g_ref[...]).astype(BF16)
    tf = D_FF // FFN_SPLIT
    def gate_up(j):
        return (_dot(h, wgu_ref[:, j * tf:(j + 1) * tf]),
                _dot(h, wgu_ref[:, D_FF + j * tf:D_FF + (j + 1) * tf]))

    acc = None
    nxt = gate_up(0)
    for j in range(FFN_SPLIT):
        gate, up = nxt
        if j + 1 < FFN_SPLIT:
            nxt = gate_up(j + 1)
        a = (_silu(gate) * up).astype(BF16)
        d = _dot(a, wd_ref[j * tf:(j + 1) * tf, :])
        acc = d if acc is None else acc + d
    out = x + 0.5 * acc
    o_ref[...] = out
    if emit_h:
        h_ref[...] = _rms(out, g2_ref[...]).astype(BF16)


def _ffn(x, norm_g, w_gu, w_down, *, tm, merge=None, h_gain=None):
    m = x.shape[0]
    row = pl.BlockSpec((tm, D_MODEL), lambda i: (i, 0))
    in_specs, args = [row], [x]
    if merge is not None:
        ys, w_o = merge
        in_specs += [row] * 4 + [_resident(D_MODEL, D_MODEL)]
        args += list(ys) + [w_o]
    in_specs += [_resident(1, D_MODEL), _resident(D_MODEL, 2 * D_FF), _resident(D_FF, D_MODEL)]
    args += [norm_g, w_gu, w_down]
    out_shape = [jax.ShapeDtypeStruct((m, D_MODEL), F32)]
    out_specs = [row]
    if h_gain is not None:
        in_specs.append(_resident(1, D_MODEL))
        args.append(h_gain)
        out_shape.append(jax.ShapeDtypeStruct((m, D_MODEL), BF16))
        out_specs.append(row)
    res = pl.pallas_call(
        functools.partial(_ffn_kernel, merge is not None, h_gain is not None),
        grid=(m // tm,),
        in_specs=in_specs, out_specs=out_specs, out_shape=out_shape,
        compiler_params=_cparams(("parallel",)),
        name="ffn_merge" if merge is not None else "ffn",
    )(*args)
    return res if h_gain is not None else res[0]


def _dn_kernel(bb_n, tt, chunk, tv_last, n_t, small,
               h_ref, s0_ref, cprev_ref, wqkv_ref, wz_ref, wab_ref, convw_ref, alog_ref, dtb_ref,
               normg_ref, wout_ref, wgate_ref,
               y_ref, snew_ref, cnew_ref,
               xbuf, cs, gb, zb, ob, s_all, u_s, w_s, qg_s, kdec_s, aqk_s, gl_s, gate_s):
    t = pl.program_id(1)
    rows = bb_n * tt

    @pl.when(t == 0)
    def _():
        xbuf[:, 5:8, :] = cprev_ref[...]
        snew_ref[...] = s0_ref[...]

    h = h_ref[...]
    qkv = _dot(h, wqkv_ref[...])
    ab = _dot(h, wab_ref[...])
    zs = _dot(h, wz_ref[...])
    gate = _dot(h, wgate_ref[...])
    xbuf[:, 8:8 + tt, :] = qkv.reshape(bb_n, tt, DN_CONV_W)
    c = 0.0
    for j in range(DN_CONV):
        c = c + xbuf[:, 5 + j:5 + j + tt, :] * convw_ref[j:j + 1, :].reshape(1, 1, DN_CONV_W)

    @pl.when(t == n_t - 1)
    def _():
        cnew_ref[...] = xbuf[:, 5 + tv_last:8 + tv_last, :]

    if n_t > 1:
        xbuf[:, 0:8, :] = xbuf[:, tt:tt + 8, :]

    c = _silu(c)
    for grp in range(8):
        sl = slice(grp * 128, (grp + 1) * 128)
        xg = c[:, :, sl]
        xn = xg * lax.rsqrt(jnp.sum(xg * xg, axis=-1, keepdims=True) + EPS)
        if grp < DN_H:
            xn = xn * (DN_DK ** -0.5)
        cs[:, :, sl] = xn
    cs[:, :, 2 * DN_QK_W:] = c[:, :, 2 * DN_QK_W:]

    g = -jnp.exp(alog_ref[...]) * jax.nn.softplus(ab + dtb_ref[...])
    lane = lax.broadcasted_iota(jnp.int32, (rows, 128), 1)
    gbv = jnp.where(lane < DN_H, g, jax.nn.sigmoid(ab)).reshape(bb_n, tt, 128)
    if tv_last < tt:
        trow = lax.broadcasted_iota(jnp.int32, (bb_n, tt, 128), 1)
        gbv = jnp.where(trow < tv_last, gbv, 0.0)
    gb[...] = gbv
    zb[...] = _silu(zs).reshape(bb_n, tt, DN_QK_W)
    gate_s[...] = jax.nn.sigmoid(gate)

    n4 = DN_H * chunk
    ri = lax.broadcasted_iota(jnp.int32, (n4, n4), 0)
    ci = lax.broadcasted_iota(jnp.int32, (n4, n4), 1)
    same = (ri // chunk) == (ci // chunk)
    incl = same & (ci <= ri)
    strict = same & (ci < ri)
    lmat = incl.astype(F32)
    umat = strict.astype(F32)
    vmask = (lax.broadcasted_iota(jnp.int32, (n4, DN_H * 128), 0) // chunk
             == lax.broadcasted_iota(jnp.int32, (n4, DN_H * 128), 1) // 128)
    n_pow = int(np.log2(chunk))
    normg = normg_ref[...]
    n_ch = tt // chunk
    total = bb_n * n_ch
    solve_mm = _mm_hi if small else _mm

    for hd in range(DN_H):
        s_all[:, :, hd * 128:(hd + 1) * 128] = snew_ref[:, hd]

    def stack_rows(ref, b, r0, off):
        return jnp.concatenate([ref[b, pl.ds(r0, chunk), off + hd * 128:off + (hd + 1) * 128]
                                for hd in range(DN_H)], axis=0)

    def level1(k, slot):
        b = k // n_ch
        r0 = pl.multiple_of((k % n_ch) * chunk, chunk)
        q = stack_rows(cs, b, r0, 0)
        kk = stack_rows(cs, b, r0, DN_QK_W)
        v = stack_rows(cs, b, r0, 2 * DN_QK_W)
        gbc = gb[b, pl.ds(r0, chunk), :]
        g_st = jnp.concatenate([jnp.broadcast_to(gbc[:, hd:hd + 1], (chunk, 128)) for hd in range(DN_H)], axis=0)
        beta_st = jnp.concatenate([jnp.broadcast_to(gbc[:, DN_H + hd:DN_H + hd + 1], (chunk, 128))
                                   for hd in range(DN_H)], axis=0)
        g_sq = jnp.concatenate([g_st] * (n4 // 128), axis=1) if n4 >= 128 else g_st[:, :n4]
        gc = _mm_exact_left(lmat, g_st, small)
        yield
        if small:
            dmat = _mm_exact_left(lmat, g_sq * umat, small)
        else:
            gct = gc.T
            dmat = (jnp.concatenate([gc] * (n4 // 128), axis=1)
                    - jnp.concatenate([gct] * (n4 // 128), axis=0))
        gam = jnp.where(incl, jnp.exp(dmat), 0.0)
        eg = jnp.exp(gc)
        kb = kk * beta_st
        kq = _mm_nt(jnp.concatenate([kb, q], axis=0), kk, small)
        yield
        a_mat = jnp.where(strict, kq[:n4] * gam, 0.0)
        x = jnp.concatenate([v * beta_st, kb * eg], axis=1)
        p = -a_mat
        for i in range(n_pow):
            x = x + (_mm_hi if i < 2 else solve_mm)(p, x, small)
            if i < n_pow - 1:
                p = (_mm_hi if i < 1 else solve_mm)(p, p, small)
            yield
        gc_last = [gc[(hd + 1) * chunk - 1:(hd + 1) * chunk, :] for hd in range(DN_H)]
        gl_st = jnp.concatenate([jnp.broadcast_to(r, (chunk, 128)) for r in gc_last], axis=0)
        u_s[slot] = x[:, :128]
        w_s[slot] = x[:, 128:]
        qg_s[slot] = q * eg
        kdec_s[slot] = kk * jnp.exp(gl_st - gc)
        aqk_s[slot] = jnp.where(incl, kq[n4:] * gam, 0.0)
        gl_s[slot] = jnp.exp(jnp.concatenate(gc_last, axis=1))

    def level2(k, slot):
        b = k // n_ch
        r0 = pl.multiple_of((k % n_ch) * chunk, chunk)
        s_old = s_all[b]
        w = w_s[slot]
        qg = qg_s[slot]
        ws, qs = [], []
        for hd in range(DN_H):
            rs = slice(hd * chunk, (hd + 1) * chunk)
            r = _mm(jnp.concatenate([w[rs], qg[rs]], axis=0), s_old[:, hd * 128:(hd + 1) * 128], small)
            ws.append(r[:chunk])
            qs.append(r[chunk:])
        yield
        v_new = u_s[slot] - jnp.concatenate(ws, axis=0)
        o = jnp.concatenate(qs, axis=0) + _mm(aqk_s[slot], v_new, small)
        vbd = jnp.where(vmask, jnp.concatenate([v_new] * DN_H, axis=1), 0.0)
        s_all[b] = s_old * gl_s[slot] + _mm_tn(kdec_s[slot], vbd, small)
        yield
        on = _rms(o, normg)
        for hd in range(DN_H):
            sl = slice(hd * 128, (hd + 1) * 128)
            ob[b, pl.ds(r0, chunk), sl] = on[hd * chunk:(hd + 1) * chunk] * zb[b, pl.ds(r0, chunk), sl]

    assert total % DN_GROUP == 0

    def recurrence(m, base):
        for i in range(DN_GROUP):
            yield from level2(m * DN_GROUP + i, base + i)

    _run_interleaved([level1(i, i) for i in range(DN_GROUP)])

    def group_body(m, _):
        base = (m % 2) * DN_GROUP
        _run_interleaved([recurrence(m, base)]
                         + [level1(jnp.minimum((m + 1) * DN_GROUP + i, total - 1), DN_GROUP - base + i)
                            for i in range(DN_GROUP)])
        return 0

    lax.fori_loop(0, total // DN_GROUP, group_body, 0)

    for hd in range(DN_H):
        snew_ref[:, hd] = s_all[:, :, hd * 128:(hd + 1) * 128]

    y = _dot(ob[...].reshape(rows, DN_QK_W).astype(BF16), wout_ref[...])
    y_ref[...] = y * gate_s[...]


def _deltanet(h, s0, cprev, wts, *, n_seq, t_pad, t_valid, bb_n, tt, chunk):
    n_t = t_pad // tt
    assert n_t == 1 or t_valid == t_pad
    tv_last = t_valid - (n_t - 1) * tt
    rows = bb_n * tt
    small = chunk < 16
    n4 = DN_H * chunk
    const = lambda *shape: pl.BlockSpec(shape, lambda b, t: (0,) * len(shape))
    in_specs = [
        pl.BlockSpec((rows, D_MODEL), lambda b, t: (b * n_t + t, 0)),
        pl.BlockSpec((bb_n, DN_H, DN_DK, DN_DK), lambda b, t: (b, 0, 0, 0)),
        pl.BlockSpec((bb_n, DN_CONV - 1, DN_CONV_W), lambda b, t: (b, 0, 0)),
        const(D_MODEL, DN_CONV_W), const(D_MODEL, DN_QK_W), const(D_MODEL, 128),
        const(DN_CONV, DN_CONV_W), const(1, 128), const(1, 128), const(1, 128),
        const(DN_QK_W, D_MODEL), const(D_MODEL, D_MODEL),
    ]
    out_specs = [
        pl.BlockSpec((rows, D_MODEL), lambda b, t: (b * n_t + t, 0)),
        pl.BlockSpec((bb_n, DN_H, DN_DK, DN_DK), lambda b, t: (b, 0, 0, 0)),
        pl.BlockSpec((bb_n, DN_CONV - 1, DN_CONV_W), lambda b, t: (b, 0, 0)),
    ]
    out_shape = [
        jax.ShapeDtypeStruct((n_seq * t_pad, D_MODEL), F32),
        jax.ShapeDtypeStruct((n_seq, DN_H, DN_DK, DN_DK), F32),
        jax.ShapeDtypeStruct((n_seq, DN_CONV - 1, DN_CONV_W), F32),
    ]
    return pl.pallas_call(
        functools.partial(_dn_kernel, bb_n, tt, chunk, tv_last, n_t, small),
        grid=(n_seq // bb_n, n_t),
        in_specs=in_specs, out_specs=out_specs, out_shape=out_shape,
        scratch_shapes=[pltpu.VMEM((bb_n, tt + 8, DN_CONV_W), F32), pltpu.VMEM((bb_n, tt, DN_CONV_W), F32),
                        pltpu.VMEM((bb_n, tt, 128), F32), pltpu.VMEM((bb_n, tt, DN_QK_W), F32),
                        pltpu.VMEM((bb_n, tt, DN_QK_W), F32),
                        pltpu.VMEM((bb_n, DN_DK, DN_H * 128), F32)]
                       + [pltpu.VMEM((2 * DN_GROUP, n4, 128), F32)] * 4
                       + [pltpu.VMEM((2 * DN_GROUP, n4, n4), F32), pltpu.VMEM((2 * DN_GROUP, 1, DN_H * 128), F32),
                          pltpu.VMEM((rows, D_MODEL), F32)],
        compiler_params=_cparams(("parallel", "arbitrary")),
        name="deltanet",
    )(h, s0, cprev, *wts)


def _sc_kernel(bb_n, tt, tv_last, n_t,
               h_ref, prev_ref, win_ref, convw_ref, wout_ref, wgate_ref,
               y_ref, new_ref, ubuf):
    t = pl.program_id(1)
    rows = bb_n * tt

    @pl.when(t == 0)
    def _():
        ubuf[:, 6:8, :] = prev_ref[...]

    h = h_ref[...]
    p = _dot(h, win_ref[...])
    gate = jax.nn.sigmoid(_dot(h, wgate_ref[...]))
    bgate = p[:, :SC_W]
    u = p[:, SC_W:2 * SC_W] * p[:, 2 * SC_W:]
    ubuf[:, 8:8 + tt, :] = u.reshape(bb_n, tt, SC_W)
    y = 0.0
    for j in range(SC_CONV):
        y = y + ubuf[:, 6 + j:6 + j + tt, :] * convw_ref[j:j + 1, :].reshape(1, 1, SC_W)

    @pl.when(t == n_t - 1)
    def _():
        new_ref[...] = ubuf[:, 6 + tv_last:8 + tv_last, :]

    if n_t > 1:
        ubuf[:, 0:8, :] = ubuf[:, tt:tt + 8, :]

    z = (bgate * y.reshape(rows, SC_W)).astype(BF16)
    y_ref[...] = _dot(z, wout_ref[...]) * gate


def _shortconv(h, prev, wts, *, n_seq, t_pad, t_valid, bb_n, tt):
    n_t = t_pad // tt
    assert n_t == 1 or t_valid == t_pad
    tv_last = t_valid - (n_t - 1) * tt
    rows = bb_n * tt
    const = lambda *shape: pl.BlockSpec(shape, lambda b, t: (0,) * len(shape))
    return pl.pallas_call(
        functools.partial(_sc_kernel, bb_n, tt, tv_last, n_t),
        grid=(n_seq // bb_n, n_t),
        in_specs=[pl.BlockSpec((rows, D_MODEL), lambda b, t: (b * n_t + t, 0)),
                  pl.BlockSpec((bb_n, SC_CONV - 1, SC_W), lambda b, t: (b, 0, 0)),
                  const(D_MODEL, 3 * SC_W), const(SC_CONV, SC_W), const(SC_W, D_MODEL),
                  const(D_MODEL, D_MODEL)],
        out_specs=[pl.BlockSpec((rows, D_MODEL), lambda b, t: (b * n_t + t, 0)),
                   pl.BlockSpec((bb_n, SC_CONV - 1, SC_W), lambda b, t: (b, 0, 0))],
        out_shape=[jax.ShapeDtypeStruct((n_seq * t_pad, D_MODEL), F32),
                   jax.ShapeDtypeStruct((n_seq, SC_CONV - 1, SC_W), F32)],
        scratch_shapes=[pltpu.VMEM((bb_n, tt + 8, SC_W), F32)],
        compiler_params=_cparams(("parallel", "arbitrary")),
        name="shortconv",
    )(h, prev, *wts)


def _memkv_kernel(m_ref, g_ref, wkv_ref, kg_ref, k_ref, v_ref):
    n = _rms(m_ref[...], g_ref[...]).astype(BF16)
    kv = _dot(n, wkv_ref[...])
    kg = kg_ref[...]
    for hd in range(MEM_H):
        sl = slice(hd * MEM_HD, (hd + 1) * MEM_HD)
        k_ref[:, sl] = _rms(kv[:, sl], kg)
    v_ref[...] = kv[:, MEM_W:]


def _mem_kv(mem2d, norm_g, w_kv, k_gain, *, tm):
    m = mem2d.shape[0]
    return pl.pallas_call(
        _memkv_kernel,
        grid=(m // tm,),
        in_specs=[pl.BlockSpec((tm, D_MODEL), lambda i: (i, 0)),
                  pl.BlockSpec((1, D_MODEL), lambda i: (0, 0)),
                  pl.BlockSpec((D_MODEL, 2 * MEM_W), lambda i: (0, 0)),
                  pl.BlockSpec((1, MEM_HD), lambda i: (0, 0))],
        out_specs=[pl.BlockSpec((tm, MEM_W), lambda i: (i, 0))] * 2,
        out_shape=[jax.ShapeDtypeStruct((m, MEM_W), F32)] * 2,
        compiler_params=_cparams(("parallel",)),
        name="mem_kv",
    )(mem2d, norm_g, w_kv, k_gain)


def _memattn_kernel(bb_n, tt, small,
                    h_ref, mk_ref, mv_ref, wq_ref, qg_ref, wout_ref, wgate_ref,
                    y_ref, qs, ob):
    rows = bb_n * tt
    h = h_ref[...]
    q = _dot(h, wq_ref[...])
    gate = jax.nn.sigmoid(_dot(h, wgate_ref[...]))
    qg = qg_ref[...]
    for hd in range(MEM_H):
        sl = slice(hd * MEM_HD, (hd + 1) * MEM_HD)
        qs[:, :, sl] = _rms(q[:, sl], qg).reshape(bb_n, tt, MEM_HD)

    def seq_body(b, _):
        def scores(hd):
            sl = slice(hd * MEM_HD, (hd + 1) * MEM_HD)
            return _mm_nt(qs[b, :, sl], mk_ref[b, :, sl], small)

        s_next = scores(0)
        for hd in range(MEM_H):
            sl = slice(hd * MEM_HD, (hd + 1) * MEM_HD)
            s = s_next * (MEM_HD ** -0.5)
            if hd + 1 < MEM_H:
                s_next = scores(hd + 1)
            s = s - jnp.max(s, axis=-1, keepdims=True)
            e = jnp.exp(s)
            p = e / jnp.sum(e, axis=-1, keepdims=True)
            ob[b, :, sl] = _mm(p, mv_ref[b, :, sl], small)
        return 0

    lax.fori_loop(0, bb_n, seq_body, 0)
    y = _dot(ob[...].reshape(rows, MEM_W).astype(BF16), wout_ref[...])
    y_ref[...] = y * gate


def _mem_attn(h, mk, mv, wts, *, n_seq, t_pad, bb_n, tt):
    n_t = t_pad // tt
    rows = bb_n * tt
    const = lambda *shape: pl.BlockSpec(shape, lambda b, t: (0,) * len(shape))
    return pl.pallas_call(
        functools.partial(_memattn_kernel, bb_n, tt, tt < 16),
        grid=(n_seq // bb_n, n_t),
        in_specs=[pl.BlockSpec((rows, D_MODEL), lambda b, t: (b * n_t + t, 0)),
                  pl.BlockSpec((bb_n, N_MEM, MEM_W), lambda b, t: (b, 0, 0)),
                  pl.BlockSpec((bb_n, N_MEM, MEM_W), lambda b, t: (b, 0, 0)),
                  const(D_MODEL, MEM_W), const(1, MEM_HD), const(MEM_W, D_MODEL), const(D_MODEL, D_MODEL)],
        out_specs=pl.BlockSpec((rows, D_MODEL), lambda b, t: (b * n_t + t, 0)),
        out_shape=jax.ShapeDtypeStruct((n_seq * t_pad, D_MODEL), F32),
        scratch_shapes=[pltpu.VMEM((bb_n, tt, MEM_W), F32), pltpu.VMEM((bb_n, tt, MEM_W), F32)],
        compiler_params=_cparams(("parallel", "arbitrary")),
        name="mem_attn",
    )(h, mk, mv, *wts)


def _mlaproj_kernel(h_ref, cos_ref, sin_ref, wq_ref, qna_ref, wqp_ref, wqs_ref, qg_ref,
                    wkv_ref, kvna_ref, wkr_ref, wkrs_ref, wuk_ref, wuv_ref, kg_ref,
                    q_ref, ckv_ref, kr_ref, k_ref, v_ref):
    h = h_ref[...]
    cos = cos_ref[...]
    sin = sin_ref[...]
    cq = _dot(h, wq_ref[...])
    ckv_raw = _dot(h, wkv_ref[...])
    kr_a = _dot(h, wkr_ref[...])
    kr_b = _dot(h, wkrs_ref[...])
    cqn = _rms(cq, qna_ref[...]).astype(BF16)
    q_raw = _dot(cqn, wqp_ref[...])
    q_swp = _dot(cqn, wqs_ref[...])
    ckv = _rms(ckv_raw, kvna_ref[...])
    cb = ckv.astype(BF16)
    k_raw = _dot(cb, wuk_ref[...])
    v_ref[...] = _dot(cb, wuv_ref[...]).astype(BF16)
    ckv_ref[...] = ckv
    krp = kr_a * cos + kr_b * sin
    kr_ref[...] = krp[:, :MLA_ROPE]

    qg = qg_ref[...]
    inv_n = 1.0 / MLA_QK
    for hd in range(MLA_H):
        sl = slice(hd * MLA_LANES, (hd + 1) * MLA_LANES)
        qh = q_raw[:, sl] * cos + q_swp[:, sl] * sin
        ms = jnp.sum(qh * qh, axis=-1, keepdims=True) * inv_n
        q_ref[:, sl] = (qh * lax.rsqrt(ms + EPS) * qg).astype(q_ref.dtype)

    kg = kg_ref[...]
    for hd in range(MLA_H):
        sl = slice(hd * MLA_LANES, (hd + 1) * MLA_LANES)
        kh = k_raw[:, sl] + krp
        ms = jnp.sum(kh * kh, axis=-1, keepdims=True) * inv_n
        k_ref[:, sl] = (kh * lax.rsqrt(ms + EPS) * kg).astype(k_ref.dtype)


def _mla_proj(h, cos, sin, wts, *, tm, n_tab, qk_dtype):
    m = h.shape[0]
    hw = MLA_H * MLA_LANES
    vw = MLA_H * MLA_V
    const = lambda *shape: pl.BlockSpec(shape, lambda i: (0,) * len(shape))
    row = lambda width: pl.BlockSpec((tm, width), lambda i: (i, 0))
    in_specs = [row(D_MODEL),
                pl.BlockSpec((tm, MLA_LANES), lambda i: (i % n_tab, 0)),
                pl.BlockSpec((tm, MLA_LANES), lambda i: (i % n_tab, 0)),
                const(D_MODEL, MLA_RANK), const(1, MLA_RANK), const(MLA_RANK, hw), const(MLA_RANK, hw),
                const(1, MLA_LANES),
                const(D_MODEL, MLA_RANK), const(1, MLA_RANK), const(D_MODEL, MLA_LANES), const(D_MODEL, MLA_LANES),
                const(MLA_RANK, hw), const(MLA_RANK, vw), const(1, MLA_LANES)]
    return pl.pallas_call(
        _mlaproj_kernel,
        grid=(m // tm,),
        in_specs=in_specs,
        out_specs=[row(hw), row(MLA_RANK), row(MLA_ROPE), row(hw), row(vw)],
        out_shape=[jax.ShapeDtypeStruct((m, hw), qk_dtype),
                   jax.ShapeDtypeStruct((m, MLA_RANK), F32),
                   jax.ShapeDtypeStruct((m, MLA_ROPE), F32),
                   jax.ShapeDtypeStruct((m, hw), qk_dtype),
                   jax.ShapeDtypeStruct((m, vw), BF16)],
        compiler_params=_cparams(("parallel",)),
        name="mla_proj",
    )(h, cos, sin, *wts)


def _flash_kernel(tq, q_ref, k_ref, v_ref, o_ref, m_scr, l_scr, acc_scr):
    qi = pl.program_id(1)
    ki = pl.program_id(2)
    c2 = (MLA_QK ** -0.5) * LOG2E
    n_rep = tq // 128

    @pl.when(ki == 0)
    def _():
        m_scr[...] = jnp.full(m_scr.shape, -jnp.inf, F32)
        l_scr[...] = jnp.zeros_like(l_scr)
        acc_scr[...] = jnp.zeros_like(acc_scr)

    def compute(diag):
        if diag:
            row = lax.broadcasted_iota(jnp.int32, (tq, tq), 0)
            col = lax.broadcasted_iota(jnp.int32, (tq, tq), 1)
            keep = col <= row
        ones = jnp.ones((tq, 128), BF16)

        def qk(hd):
            sl = slice(hd * MLA_LANES, (hd + 1) * MLA_LANES)
            return _dot_nt(q_ref[:, sl], k_ref[:, sl])

        s_next = qk(0)
        for hd in range(MLA_H):
            s = s_next * c2
            if hd + 1 < MLA_H:
                s_next = qk(hd + 1)
            if diag:
                s = jnp.where(keep, s, -jnp.inf)
            m_old = m_scr[hd]
            m_new = jnp.maximum(m_old, jnp.max(s, axis=-1, keepdims=True))
            alpha = jnp.exp2(m_old - m_new)
            p = jnp.exp2(s - jnp.concatenate([m_new] * n_rep, axis=1)).astype(BF16)
            pair = hd // 2
            vext = jnp.concatenate([v_ref[:, pair * 128:(pair + 1) * 128], ones], axis=1)
            r = _dot(p, vext)
            acc_scr[hd] = alpha * acc_scr[hd] + r[:, :128]
            l_scr[hd] = alpha * l_scr[hd] + r[:, 128:]
            m_scr[hd] = m_new

    @pl.when(ki < qi)
    def _():
        compute(False)

    @pl.when(ki == qi)
    def _():
        compute(True)
        lane = lax.broadcasted_iota(jnp.int32, (tq, 128), 1)
        for pair in range(MLA_H // 2):
            even = acc_scr[2 * pair] / l_scr[2 * pair]
            odd = acc_scr[2 * pair + 1] / l_scr[2 * pair + 1]
            o_ref[:, pair * 128:(pair + 1) * 128] = jnp.where(lane < MLA_V, even, odd).astype(o_ref.dtype)


def _mla_prompt_attn(q, k, v, *, n_seq, seq, tq):
    nq = seq // tq
    hw = MLA_H * MLA_LANES
    vw = MLA_H * MLA_V
    return pl.pallas_call(
        functools.partial(_flash_kernel, tq),
        grid=(n_seq, nq, nq),
        in_specs=[pl.BlockSpec((tq, hw), lambda b, i, j: (b * nq + i, 0)),
                  pl.BlockSpec((tq, hw), lambda b, i, j: (b * nq + jnp.minimum(i, j), 0)),
                  pl.BlockSpec((tq, vw), lambda b, i, j: (b * nq + jnp.minimum(i, j), 0))],
        out_specs=pl.BlockSpec((tq, vw), lambda b, i, j: (b * nq + i, 0)),
        out_shape=jax.ShapeDtypeStruct((n_seq * seq, vw), BF16),
        scratch_shapes=[pltpu.VMEM((MLA_H, tq, 128), F32)] * 3,
        compiler_params=_cparams(("parallel", "parallel", "arbitrary")),
        name="mla_flash",
    )(q, k, v)


SAMPLE_ROWS = 8
SUB_KEYS = 1024


def _mla_sample_kernel(layer, n_seq, n_pg, n_steps, t_valid,
                       pt_ref, q_ref, knew_ref, cnew_ref, wukp_ref, wukt_ref, kg_ref, ckv_hbm, kr_hbm,
                       o_ref,
                       lhs, qabs, qrope, qblk, m_scr, l_scr, acc_scr, cbuf, rbuf, sem):
    b = pl.program_id(0)
    st = pl.program_id(1)
    c2 = (MLA_QK ** -0.5) * LOG2E
    inv_n = 1.0 / MLA_QK
    tk = n_pg * PAGE
    nq = SAMPLE_ROWS
    n_up = MLA_H * MLA_NOPE
    g = b * n_steps + st
    slot = g % 2

    def page_copies(bb, ss, sl):
        cps = []
        for i in range(n_pg):
            page = pt_ref[bb, ss * n_pg + i]
            cps.append(pltpu.make_async_copy(ckv_hbm.at[layer, page], cbuf.at[sl, pl.ds(i * PAGE, PAGE), :],
                                             sem.at[sl, 0]))
            cps.append(pltpu.make_async_copy(kr_hbm.at[layer, page], rbuf.at[sl, :, pl.ds(i * PAGE, PAGE)],
                                             sem.at[sl, 1]))
        return cps

    @pl.when(g == 0)
    def _():
        for cp in page_copies(0, 0, 0):
            cp.start()

    for cp in page_copies(b, st, slot):
        cp.wait()

    @pl.when(g + 1 < n_seq * n_steps)
    def _():
        last = st == n_steps - 1
        for cp in page_copies(jnp.where(last, b + 1, b), jnp.where(last, 0, st + 1), 1 - slot):
            cp.start()

    @pl.when(st == 0)
    def _():
        m_scr[...] = jnp.full(m_scr.shape, -jnp.inf, F32)
        l_scr[...] = jnp.zeros_like(l_scr)
        acc_scr[...] = jnp.zeros_like(acc_scr)
        kg = kg_ref[...]
        q = q_ref[...]
        lane_head = lax.broadcasted_iota(jnp.int32, q.shape, 1) // MLA_LANES
        for hd in range(MLA_H):
            sl = slice(hd * MLA_LANES, (hd + 1) * MLA_LANES)
            qk = q[:, sl] * kg
            qabs[hd * nq:(hd + 1) * nq, :] = lax.dot_general(
                qk, wukp_ref[:, sl].astype(F32), (((1,), (1,)), ((), ())), preferred_element_type=F32,
                precision=lax.Precision.HIGHEST)
            qrope[hd * nq:(hd + 1) * nq, :] = qk[:, :MLA_ROPE]
            qblk[hd * nq:(hd + 1) * nq, :] = jnp.where(lane_head == hd, q, 0.0)
        lhs[:n_up, :] = wukt_ref[...]
        lhs[n_up:, :] = qabs[...].astype(BF16)

    lhs_v = lhs[...]
    qr = qrope[...].astype(BF16)

    def score_block(j):
        cb = cbuf[slot, j * SUB_KEYS:(j + 1) * SUB_KEYS, :].astype(BF16)
        krt = rbuf[slot, :, j * SUB_KEYS:(j + 1) * SUB_KEYS]
        big = _dot_nt(lhs_v, cb)
        knt = big[:n_up]
        ssq = jnp.sum((knt * knt).reshape(MLA_H, MLA_NOPE, SUB_KEYS), axis=1)
        ssq_r = jnp.sum(krt * krt, axis=0, keepdims=True)
        rs = lax.rsqrt((ssq + ssq_r) * inv_n + EPS) * c2
        s = big[n_up:] + _dot(qr, krt.astype(BF16))
        s = jnp.concatenate([s[hd * nq:(hd + 1) * nq, :] * rs[hd:hd + 1, :] for hd in range(MLA_H)], axis=0)
        return s, cb

    m_run = m_scr[...]
    l_new = l_scr[...]
    acc = acc_scr[...]
    n_sub = tk // SUB_KEYS
    blk = score_block(0)
    for j in range(n_sub):
        s, cb = blk
        if j + 1 < n_sub:
            blk = score_block(j + 1)
        m_new = jnp.maximum(m_run, jnp.max(s, axis=-1, keepdims=True))
        alpha = jnp.exp2(m_run - m_new)
        p = jnp.exp2(s - m_new)
        l_new = alpha * l_new + jnp.sum(p, axis=-1, keepdims=True)
        acc = alpha * acc + _dot(p.astype(BF16), cb)
        m_run = m_new
    l_scr[...] = l_new
    acc_scr[...] = acc
    m_scr[...] = m_new

    @pl.when(st == n_steps - 1)
    def _():
        cn = cnew_ref[...]
        sn = _dot_nt(_rnd(qblk[...]), _rnd(knew_ref[...])) * c2
        row = lax.broadcasted_iota(jnp.int32, sn.shape, 0) % nq
        col = lax.broadcasted_iota(jnp.int32, sn.shape, 1)
        sn = jnp.where((col <= row) & (col < t_valid), sn, -jnp.inf)
        m_o = m_scr[...]
        m_n = jnp.maximum(m_o, jnp.max(sn, axis=-1, keepdims=True))
        al = jnp.exp2(m_o - m_n)
        pn = jnp.exp2(sn - m_n)
        l_f = al * l_scr[...] + jnp.sum(pn, axis=-1, keepdims=True)
        o_ref[...] = (al * acc_scr[...] + _dot(_rnd(pn), _rnd(cn))) / l_f


def _mla_sample_attn(page_table, q, k_new, c_new, wts, ckv_pool, kr_pool_t, layer, *, n_seq, t_valid, n_pg):
    n_pages = page_table.shape[1]
    n_steps = n_pages // n_pg
    hw = MLA_H * MLA_LANES
    tk = n_pg * PAGE
    nq = SAMPLE_ROWS
    nr = MLA_H * nq

    const = lambda *shape: pl.BlockSpec(shape, lambda b, s, pt: (0,) * len(shape))
    hbm = pl.BlockSpec(memory_space=pl.ANY)
    in_specs = [pl.BlockSpec((nq, hw), lambda b, s, pt: (b, 0)),
                pl.BlockSpec((nq, hw), lambda b, s, pt: (b, 0)),
                pl.BlockSpec((nq, MLA_RANK), lambda b, s, pt: (b, 0)),
                const(MLA_RANK, hw), const(MLA_H * MLA_NOPE, MLA_RANK), const(1, MLA_LANES),
                hbm, hbm]
    grid_spec = pltpu.PrefetchScalarGridSpec(
        num_scalar_prefetch=1,
        grid=(n_seq, n_steps),
        in_specs=in_specs,
        out_specs=pl.BlockSpec((nr, MLA_RANK), lambda b, s, pt: (b, 0)),
        scratch_shapes=[pltpu.VMEM((MLA_H * MLA_NOPE + nr, MLA_RANK), BF16),
                        pltpu.VMEM((nr, MLA_RANK), F32), pltpu.VMEM((nr, MLA_ROPE), F32),
                        pltpu.VMEM((nr, hw), F32),
                        pltpu.VMEM((nr, 1), F32), pltpu.VMEM((nr, 1), F32), pltpu.VMEM((nr, MLA_RANK), F32),
                        pltpu.VMEM((2, tk, MLA_RANK), F32), pltpu.VMEM((2, MLA_ROPE, tk), F32),
                        pltpu.SemaphoreType.DMA((2, 2))],
    )
    return pl.pallas_call(
        functools.partial(_mla_sample_kernel, layer, n_seq, n_pg, n_steps, t_valid),
        grid_spec=grid_spec,
        out_shape=jax.ShapeDtypeStruct((n_seq * nr, MLA_RANK), F32),
        compiler_params=_cparams(("arbitrary", "arbitrary")),
        name="mla_paged",
    )(page_table, q, k_new, c_new, *wts, ckv_pool, kr_pool_t)


def _mla_up_kernel(n_seq, pc_ref, h_ref, wuvs_ref, wout_ref, wgate_ref, y_ref):
    nq = SAMPLE_ROWS
    o = jnp.zeros((n_seq * nq, MLA_H * MLA_V), F32)
    for hd in range(MLA_H):
        pch = pc_ref[:, hd * nq:(hd + 1) * nq, :].reshape(n_seq * nq, MLA_RANK)
        o = o + _dot(pch.astype(BF16), wuvs_ref[hd])
    y_ref[...] = _dot(o.astype(BF16), wout_ref[...]) * jax.nn.sigmoid(_dot(h_ref[...], wgate_ref[...]))


def _mla_up_proj_gate(pc, h, wuv_sel, w_out, w_gate, *, n_seq):
    rows = n_seq * SAMPLE_ROWS
    return pl.pallas_call(
        functools.partial(_mla_up_kernel, n_seq),
        out_shape=jax.ShapeDtypeStruct((rows, D_MODEL), F32),
        compiler_params=pltpu.CompilerParams(vmem_limit_bytes=VMEM_LIMIT),
        name="mla_up_proj_gate",
    )(pc.reshape(n_seq, MLA_H * SAMPLE_ROWS, MLA_RANK), h, wuv_sel, w_out, w_gate)


def _projgate_kernel(o_ref, h_ref, wout_ref, wgate_ref, y_ref):
    y_ref[...] = _dot(o_ref[...].astype(BF16), wout_ref[...]) * jax.nn.sigmoid(_dot(h_ref[...], wgate_ref[...]))


def _proj_gate(o, h, w_out, w_gate, *, tm):
    m, kdim = o.shape
    return pl.pallas_call(
        _projgate_kernel,
        grid=(m // tm,),
        in_specs=[pl.BlockSpec((tm, kdim), lambda i: (i, 0)),
                  pl.BlockSpec((tm, D_MODEL), lambda i: (i, 0)),
                  pl.BlockSpec((kdim, D_MODEL), lambda i: (0, 0)),
                  pl.BlockSpec((D_MODEL, D_MODEL), lambda i: (0, 0))],
        out_specs=pl.BlockSpec((tm, D_MODEL), lambda i: (i, 0)),
        out_shape=jax.ShapeDtypeStruct((m, D_MODEL), F32),
        compiler_params=_cparams(("parallel",)),
        name="proj_gate",
    )(o, h, w_out, w_gate)


def _pad_lanes(x, width):
    return jnp.pad(x, [(0, 0)] * (x.ndim - 1) + [(0, width - x.shape[-1])])


def _mla_head_layout(nope, r1, r2):
    z = jnp.zeros(nope.shape[:-1] + (MLA_LANES - MLA_QK,), nope.dtype)
    x = jnp.concatenate([r1, r2, nope, z], axis=-1)
    return x.reshape(x.shape[:-2] + (MLA_H * MLA_LANES,))


def _gain_layout(g):
    half = MLA_ROPE // 2
    return jnp.concatenate([g[MLA_NOPE:MLA_NOPE + half], g[MLA_NOPE + half:], g[:MLA_NOPE],
                            jnp.zeros((MLA_LANES - MLA_QK,), g.dtype)]).reshape(1, MLA_LANES)


def _rope_tables(pos):
    half = MLA_ROPE // 2
    inv = ROPE_THETA ** (-jnp.arange(half, dtype=F32) / half)
    ang = pos.astype(F32)[:, None] * inv
    cos, sin = jnp.cos(ang), jnp.sin(ang)
    n = pos.shape[0]
    cos_t = jnp.concatenate([cos, cos, jnp.ones((n, MLA_NOPE), F32), jnp.zeros((n, MLA_LANES - MLA_QK), F32)], -1)
    sin_t = jnp.concatenate([sin, sin, jnp.zeros((n, MLA_LANES - MLA_ROPE), F32)], -1)
    return cos_t, sin_t


def _layer_weights(l, p):
    w_in = p['w_in'][l]
    sizes = (DN_CONV_W, DN_QK_W, DN_H, DN_H, SC_W, SC_W, SC_W, MLA_RANK, MLA_RANK, MLA_ROPE, MEM_W, 4 * D_MODEL)
    offs = np.concatenate([[0], np.cumsum(sizes)])
    seg = [w_in[:, offs[i]:offs[i + 1]] for i in range(len(sizes))]
    bf = lambda x: x.astype(BF16)
    row = lambda x: x.reshape(1, -1)
    gates = [bf(seg[11][:, i * D_MODEL:(i + 1) * D_MODEL]) for i in range(4)]
    half = MLA_ROPE // 2

    w = {}
    w['ffn1'] = (row(p['ffn1_norm'][l]), bf(p['ffn1_w_gu'][l]), bf(p['ffn1_w_down'][l]))
    w['ffn2'] = (row(p['ffn2_norm'][l]), bf(p['ffn2_w_gu'][l]), bf(p['ffn2_w_down'][l]))
    w['mix_norm'] = row(p['mix_norm'][l])
    w['w_o'] = bf(p['w_o'][l])
    w['dn'] = (bf(seg[0]), bf(seg[1]), bf(_pad_lanes(jnp.concatenate([seg[2], seg[3]], 1), 128)),
               p['dn_conv_w'][l], _pad_lanes(row(p['dn_A_log'][l]), 128), _pad_lanes(row(p['dn_dt_bias'][l]), 128),
               row(p['dn_norm'][l]), bf(p['dn_w_out'][l]), gates[0])
    w['sc'] = (bf(jnp.concatenate([seg[4], seg[5], seg[6]], 1)), p['sc_conv_w'][l], bf(p['sc_w_out'][l]), gates[1])

    wq = p['mla_w_q_b'][l].reshape(MLA_RANK, MLA_H, MLA_QK)
    q_nope, q_r1, q_r2 = wq[..., :MLA_NOPE], wq[..., MLA_NOPE:MLA_NOPE + half], wq[..., MLA_NOPE + half:]
    wq_perm = _mla_head_layout(q_nope, q_r1, q_r2)
    wq_swap = _mla_head_layout(jnp.zeros_like(q_nope), -q_r2, q_r1)
    wkr = seg[9]
    wkr_pad = _pad_lanes(wkr, MLA_LANES)
    wkr_swap = _pad_lanes(jnp.concatenate([-wkr[:, half:], wkr[:, :half]], 1), MLA_LANES)
    wkv = p['mla_w_kv_b'][l].reshape(MLA_RANK, MLA_H, MLA_NOPE + MLA_V)
    w_uk, w_uv = wkv[..., :MLA_NOPE], wkv[..., MLA_NOPE:]
    zr = jnp.zeros((MLA_RANK, MLA_H, half), F32)
    wuk_perm = bf(_mla_head_layout(w_uk, zr, zr))
    k_gain = _gain_layout(p['mla_k_norm'][l])
    w['mla_proj'] = (bf(seg[7]), row(p['mla_q_norm_a'][l]), bf(wq_perm), bf(wq_swap), _gain_layout(p['mla_q_norm'][l]),
                     bf(seg[8]), row(p['mla_kv_norm_a'][l]), bf(wkr_pad), bf(wkr_swap),
                     wuk_perm, bf(w_uv.reshape(MLA_RANK, MLA_H * MLA_V)), k_gain)
    eye = jnp.eye(MLA_H, dtype=F32)
    w['mla_wuv_sel'] = bf((w_uv[None] * eye[:, None, :, None]).reshape(MLA_H, MLA_RANK, MLA_H * MLA_V))
    w['mla_sample'] = (wuk_perm, bf(w_uk.reshape(MLA_RANK, MLA_H * MLA_NOPE).T), k_gain)
    w['mla_out'] = (bf(p['mla_w_out'][l]), gates[2])
    w['mem_kv'] = (row(p['mem_norm'][l]), bf(p['mem_w_kv'][l]), row(p['mem_k_norm'][l]))
    w['mem'] = (bf(seg[10]), row(p['mem_q_norm'][l]), bf(p['mem_w_out'][l]), gates[3])
    return w


def _group_layer(x, w, *, n_seq, t_pad, t_valid, tm, bb_n, tt, chunk, dn_state, sc_state, mem_kv, cos, sin,
                 n_tab, mla_attend, q_dtype):
    cfg = dict(n_seq=n_seq, t_pad=t_pad, bb_n=bb_n, tt=tt)
    x1, h = _ffn(x, *w['ffn1'], tm=tm, h_gain=w['mix_norm'])
    y_dn, dn_s, dn_c = _deltanet(h, dn_state[0], dn_state[1], w['dn'], t_valid=t_valid, chunk=chunk, **cfg)
    y_sc, sc_c = _shortconv(h, sc_state, w['sc'], t_valid=t_valid, **cfg)
    q, ckv, kr, k, v = _mla_proj(h, cos, sin, w['mla_proj'], tm=tm, n_tab=n_tab, qk_dtype=q_dtype)
    if mla_attend is None:
        o = _mla_prompt_attn(q, k, v, n_seq=n_seq, seq=t_pad, tq=tm)
        y_mla = _proj_gate(o, h, *w['mla_out'], tm=tm)
    else:
        y_mla = mla_attend(q, k, ckv, h)
    y_mem = _mem_attn(h, mem_kv[0], mem_kv[1], w['mem'], **cfg)
    x3 = _ffn(x1, *w['ffn2'], tm=min(tm, 256), merge=((y_dn, y_sc, y_mla, y_mem), w['w_o']))
    return x3, dn_s, dn_c, sc_c, ckv, kr


def kernel(x_prompt, x_sample, state_dn_S, state_dn_conv, state_sc_conv, cache_mla_ckv, cache_mla_krope, cache_mem_k, cache_mem_v, page_table, mem_prompt, ffn1_norm, ffn1_w_gu, ffn1_w_down, mix_norm, w_in, dn_conv_w, dn_A_log, dn_dt_bias, dn_norm, dn_w_out, sc_conv_w, sc_w_out, mla_q_norm_a, mla_w_q_b, mla_kv_norm_a, mla_w_kv_b, mla_q_norm, mla_k_norm, mla_w_out, mem_norm, mem_w_kv, mem_q_norm, mem_k_norm, mem_w_out, w_o, ffn2_norm, ffn2_w_gu, ffn2_w_down):
    params = dict(ffn1_norm=ffn1_norm, ffn1_w_gu=ffn1_w_gu, ffn1_w_down=ffn1_w_down, mix_norm=mix_norm, w_in=w_in,
                  dn_conv_w=dn_conv_w, dn_A_log=dn_A_log, dn_dt_bias=dn_dt_bias, dn_norm=dn_norm, dn_w_out=dn_w_out,
                  sc_conv_w=sc_conv_w, sc_w_out=sc_w_out, mla_q_norm_a=mla_q_norm_a, mla_w_q_b=mla_w_q_b,
                  mla_kv_norm_a=mla_kv_norm_a, mla_w_kv_b=mla_w_kv_b, mla_q_norm=mla_q_norm, mla_k_norm=mla_k_norm,
                  mla_w_out=mla_w_out, mem_norm=mem_norm, mem_w_kv=mem_w_kv, mem_q_norm=mem_q_norm,
                  mem_k_norm=mem_k_norm, mem_w_out=mem_w_out, w_o=w_o, ffn2_norm=ffn2_norm, ffn2_w_gu=ffn2_w_gu,
                  ffn2_w_down=ffn2_w_down)
    depth = w_in.shape[0]
    bp, seq, _ = x_prompt.shape
    bs, td, _ = x_sample.shape
    tds = SAMPLE_ROWS
    n_pages = page_table.shape[1]
    past = n_pages * PAGE
    krope_t = jnp.transpose(cache_mla_krope, (0, 1, 3, 2))

    cos_p, sin_p = _rope_tables(jnp.arange(seq))
    cos_s, sin_s = _rope_tables(past + jnp.arange(tds))
    cos_s, sin_s = jnp.tile(cos_s, (bs, 1)), jnp.tile(sin_s, (bs, 1))

    xp = x_prompt.reshape(bp * seq, D_MODEL)
    xs = jnp.pad(x_sample, ((0, 0), (0, tds - td), (0, 0))).reshape(bs * tds, D_MODEL)
    zero_s = jnp.zeros((bp, DN_H, DN_DK, DN_DK), F32)
    zero_dc = jnp.zeros((bp, DN_CONV - 1, DN_CONV_W), F32)
    zero_sc = jnp.zeros((bp, SC_CONV - 1, SC_W), F32)
    mem2d = mem_prompt.reshape(bp * N_MEM, D_MODEL)

    outs = {k: [] for k in ('pS', 'pdc', 'psc', 'pckv', 'pkr', 'pmk', 'pmv', 'sS', 'sdc', 'ssc', 'sckv', 'skr')}
    tm_p = 512
    for l in range(depth):
        w = _layer_weights(l, params)
        mk, mv = _mem_kv(mem2d, *w['mem_kv'], tm=tm_p)
        mk3, mv3 = mk.reshape(bp, N_MEM, MEM_W), mv.reshape(bp, N_MEM, MEM_W)
        xp, s_p, dc_p, sc_p, ckv_p, kr_p = _group_layer(
            xp, w, n_seq=bp, t_pad=seq, t_valid=seq, tm=tm_p, bb_n=1, tt=tm_p, chunk=DN_CHUNK,
            dn_state=(zero_s, zero_dc), sc_state=zero_sc, mem_kv=(mk3, mv3), cos=cos_p, sin=sin_p,
            n_tab=seq // tm_p, mla_attend=None, q_dtype=BF16)
        outs['pS'].append(s_p); outs['pdc'].append(dc_p); outs['psc'].append(sc_p)
        outs['pckv'].append(ckv_p.reshape(bp, seq, MLA_RANK)); outs['pkr'].append(kr_p.reshape(bp, seq, MLA_ROPE))
        outs['pmk'].append(mk.reshape(bp, N_MEM, MEM_H, MEM_HD)); outs['pmv'].append(mv.reshape(bp, N_MEM, MEM_H, MEM_HD))

        def attend(q, k, ckv, h, l=l, w=w):
            pc = _mla_sample_attn(page_table, q, k, ckv, w['mla_sample'], cache_mla_ckv, krope_t, l,
                                  n_seq=bs, t_valid=td, n_pg=32)
            return _mla_up_proj_gate(pc, h, w['mla_wuv_sel'], *w['mla_out'], n_seq=bs)

        xs, s_s, dc_s, sc_s, ckv_s, kr_s = _group_layer(
            xs, w, n_seq=bs, t_pad=tds, t_valid=td, tm=bs * tds, bb_n=8, tt=tds, chunk=tds,
            dn_state=(state_dn_S[l], state_dn_conv[l]), sc_state=state_sc_conv[l],
            mem_kv=(cache_mem_k[l].reshape(bs, N_MEM, MEM_W), cache_mem_v[l].reshape(bs, N_MEM, MEM_W)),
            cos=cos_s, sin=sin_s, n_tab=1, mla_attend=attend, q_dtype=F32)
        outs['sS'].append(s_s); outs['sdc'].append(dc_s); outs['ssc'].append(sc_s)
        outs['sckv'].append(ckv_s.reshape(bs, tds, MLA_RANK)[:, :td])
        outs['skr'].append(kr_s.reshape(bs, tds, MLA_ROPE)[:, :td])

    st = lambda k: jnp.stack(outs[k])
    y_prompt = xp.reshape(bp, seq, D_MODEL)
    y_sample = xs.reshape(bs, tds, D_MODEL)[:, :td]
    return (y_prompt, y_sample, st('pS'), st('pdc'), st('psc'), st('pckv'), st('pkr'), st('pmk'), st('pmv'),
            st('sS'), st('sdc'), st('ssc'), st('sckv'), st('skr'))
```

```python
import functools

import numpy as np
import jax
import jax.numpy as jnp
from jax import lax
from jax.experimental import pallas as pl
from jax.experimental.pallas import tpu as pltpu

F32 = jnp.float32
BF16 = jnp.bfloat16

D_MODEL = 1024
D_FF = 2816
EPS = 1e-6
N_MEM = 256
PAGE = 128
DN_H = 4
DN_DK = 128
DN_QK_W = 512
DN_CONV_W = 1536
DN_CONV = 4
DN_CHUNK = 64
DN_GROUP = 2
SC_W = 512
SC_CONV = 3
MLA_H = 8
MLA_RANK = 256
MLA_NOPE = 64
MLA_ROPE = 32
MLA_V = 64
MLA_QK = 96
MLA_LANES = 128
ROPE_THETA = 10000.0
LOG2E = 1.4426950408889634
MEM_H = 4
MEM_HD = 128
MEM_W = 512

VMEM_LIMIT = 56 * 1024 * 1024


def _cparams(sem):
    return pltpu.CompilerParams(dimension_semantics=sem, vmem_limit_bytes=VMEM_LIMIT)


def _rms(x, g):
    ms = jnp.mean(x * x, axis=-1, keepdims=True)
    return x * lax.rsqrt(ms + EPS) * g


def _silu(x):
    return x * jax.nn.sigmoid(x)


def _rnd(x):
    return x.astype(BF16).astype(F32)


def _dot(a, b):
    return jnp.dot(a, b, preferred_element_type=F32)


def _dot_nt(a, b):
    return lax.dot_general(a, b, (((1,), (1,)), ((), ())), preferred_element_type=F32)


def _mm(a, b, small):
    if small:
        return _dot(_rnd(a), _rnd(b))
    return _dot(a.astype(BF16), b.astype(BF16))


def _mm_nt(a, b, small):
    if small:
        return _dot_nt(_rnd(a), _rnd(b))
    return _dot_nt(a.astype(BF16), b.astype(BF16))


def _mm_tn(a, b, small):
    dn = (((0,), (0,)), ((), ()))
    if small:
        return lax.dot_general(_rnd(a), _rnd(b), dn, preferred_element_type=F32)
    return lax.dot_general(a.astype(BF16), b.astype(BF16), dn, preferred_element_type=F32)


def _run_interleaved(gens):
    gens = list(gens)
    while gens:
        for g in list(gens):
            try:
                next(g)
            except StopIteration:
                gens.remove(g)


def _split2(x):
    hi = x.astype(BF16)
    lo = (x - hi.astype(F32)).astype(BF16)
    return hi, lo


def _mm_hi(a, b, small):
    if small:
        return jnp.dot(a, b, preferred_element_type=F32, precision=lax.Precision.HIGHEST)
    ah, al = _split2(a)
    bh, bl = _split2(b)
    return _dot(ah, bh) + (_dot(ah, bl) + _dot(al, bh))


def _mm_exact_left(lmat, b, small):
    if small:
        return jnp.dot(lmat, b, preferred_element_type=F32, precision=lax.Precision.HIGHEST)
    lb = lmat.astype(BF16)
    b1 = b.astype(BF16)
    r1 = b - b1.astype(F32)
    b2 = r1.astype(BF16)
    b3 = (r1 - b2.astype(F32)).astype(BF16)
    return _dot(lb, b1) + (_dot(lb, b2) + _dot(lb, b3))


FFN_SPLIT = 2


def _resident(*shape):
    return pl.BlockSpec(shape, lambda i: (0,) * len(shape), pipeline_mode=pl.Buffered(1))


def _ffn_kernel(merge, emit_h, *refs):
    it = iter(refs)
    x_ref = next(it)
    if merge:
        y_refs = [next(it) for _ in range(4)]
        wo_ref = next(it)
    g_ref, wgu_ref, wd_ref = next(it), next(it), next(it)
    if emit_h:
        g2_ref = next(it)
    o_ref = next(it)
    if emit_h:
        h_ref = next(it)

    x = x_ref[...]
    if merge:
        m = ((y_refs[0][...] + y_refs[1][...]) + y_refs[2][...]) + y_refs[3][...]
        x = x + _dot(m.astype(BF16), wo_ref[...])
    h = _rms(x, g_ref[...]).astype(BF16)
    tf = D_FF // FFN_SPLIT
    def gate_up(j):
        return (_dot(h, wgu_ref[:, j * tf:(j + 1) * tf]),
                _dot(h, wgu_ref[:, D_FF + j * tf:D_FF + (j + 1) * tf]))

    acc = None
    nxt = gate_up(0)
    for j in range(FFN_SPLIT):
        gate, up = nxt
        if j + 1 < FFN_SPLIT:
            nxt = gate_up(j + 1)
        a = (_silu(gate) * up).astype(BF16)
        d = _dot(a, wd_ref[j * tf:(j + 1) * tf, :])
        acc = d if acc is None else acc + d
    out = x + 0.5 * acc
    o_ref[...] = out
    if emit_h:
        h_ref[...] = _rms(out, g2_ref[...]).astype(BF16)


def _ffn(x, norm_g, w_gu, w_down, *, tm, merge=None, h_gain=None):
    m = x.shape[0]
    row = pl.BlockSpec((tm, D_MODEL), lambda i: (i, 0))
    in_specs, args = [row], [x]
    if merge is not None:
        ys, w_o = merge
        in_specs += [row] * 4 + [_resident(D_MODEL, D_MODEL)]
        args += list(ys) + [w_o]
    in_specs += [_resident(1, D_MODEL), _resident(D_MODEL, 2 * D_FF), _resident(D_FF, D_MODEL)]
    args += [norm_g, w_gu, w_down]
    out_shape = [jax.ShapeDtypeStruct((m, D_MODEL), F32)]
    out_specs = [row]
    if h_gain is not None:
        in_specs.append(_resident(1, D_MODEL))
        args.append(h_gain)
        out_shape.append(jax.ShapeDtypeStruct((m, D_MODEL), BF16))
        out_specs.append(row)
    res = pl.pallas_call(
        functools.partial(_ffn_kernel, merge is not None, h_gain is not None),
        grid=(m // tm,),
        in_specs=in_specs, out_specs=out_specs, out_shape=out_shape,
        compiler_params=_cparams(("parallel",)),
        name="ffn_merge" if merge is not None else "ffn",
    )(*args)
    return res if h_gain is not None else res[0]


def _dn_kernel(bb_n, tt, chunk, tv_last, n_t, small,
               h_ref, s0_ref, cprev_ref, wqkv_ref, wz_ref, wab_ref, convw_ref, alog_ref, dtb_ref,
               normg_ref, wout_ref, wgate_ref,
               y_ref, snew_ref, cnew_ref,
               xbuf, cs, gb, zb, ob, s_all, u_s, w_s, qg_s, kdec_s, aqk_s, gl_s, gate_s):
    t = pl.program_id(1)
    rows = bb_n * tt

    @pl.when(t == 0)
    def _():
        xbuf[:, 5:8, :] = cprev_ref[...]
        snew_ref[...] = s0_ref[...]

    h = h_ref[...]
    qkv = _dot(h, wqkv_ref[...])
    ab = _dot(h, wab_ref[...])
    zs = _dot(h, wz_ref[...])
    gate = _dot(h, wgate_ref[...])
    xbuf[:, 8:8 + tt, :] = qkv.reshape(bb_n, tt, DN_CONV_W)

    @pl.when(t == n_t - 1)
    def _():
        cnew_ref[...] = xbuf[:, 5 + tv_last:8 + tv_last, :]

    g = -jnp.exp(alog_ref[...]) * jax.nn.softplus(ab + dtb_ref[...])
    lane = lax.broadcasted_iota(jnp.int32, (rows, 128), 1)
    gbv = jnp.where(lane < DN_H, g, jax.nn.sigmoid(ab)).reshape(bb_n, tt, 128)
    if tv_last < tt:
        trow = lax.broadcasted_iota(jnp.int32, (bb_n, tt, 128), 1)
        gbv = jnp.where(trow < tv_last, gbv, 0.0)
    gb[...] = gbv

    c = 0.0
    for j in range(DN_CONV):
        c = c + xbuf[:, 5 + j:5 + j + tt, :] * convw_ref[j:j + 1, :].reshape(1, 1, DN_CONV_W)
    if n_t > 1:
        xbuf[:, 0:8, :] = xbuf[:, tt:tt + 8, :]
    c = _silu(c)
    for grp in range(8):
        sl = slice(grp * 128, (grp + 1) * 128)
        xg = c[:, :, sl]
        xn = xg * lax.rsqrt(jnp.sum(xg * xg, axis=-1, keepdims=True) + EPS)
        if grp < DN_H:
            xn = xn * (DN_DK ** -0.5)
        cs[:, :, sl] = xn
    cs[:, :, 2 * DN_QK_W:] = c[:, :, 2 * DN_QK_W:]
    zb[...] = _silu(zs).reshape(bb_n, tt, DN_QK_W)
    gate_s[...] = jax.nn.sigmoid(gate)

    n4 = DN_H * chunk
    ri = lax.broadcasted_iota(jnp.int32, (n4, n4), 0)
    ci = lax.broadcasted_iota(jnp.int32, (n4, n4), 1)
    same = (ri // chunk) == (ci // chunk)
    incl = same & (ci <= ri)
    strict = same & (ci < ri)
    lmat = incl.astype(F32)
    umat = strict.astype(F32)
    vmask = (lax.broadcasted_iota(jnp.int32, (n4, DN_H * 128), 0) // chunk
             == lax.broadcasted_iota(jnp.int32, (n4, DN_H * 128), 1) // 128)
    n_pow = int(np.log2(chunk))
    normg = normg_ref[...]
    n_ch = tt // chunk
    total = bb_n * n_ch
    solve_mm = _mm_hi if small else _mm

    for hd in range(DN_H):
        s_all[:, :, hd * 128:(hd + 1) * 128] = snew_ref[:, hd]

    def stack_rows(ref, b, r0, off):
        return jnp.concatenate([ref[b, pl.ds(r0, chunk), off + hd * 128:off + (hd + 1) * 128]
                                for hd in range(DN_H)], axis=0)

    def level1(k, slot):
        b = k // n_ch
        r0 = (k % n_ch) * chunk
        if not isinstance(k, int):
            r0 = pl.multiple_of(r0, chunk)
        q = stack_rows(cs, b, r0, 0)
        kk = stack_rows(cs, b, r0, DN_QK_W)
        v = stack_rows(cs, b, r0, 2 * DN_QK_W)
        gbc = gb[b, pl.ds(r0, chunk), :]
        g_st = jnp.concatenate([jnp.broadcast_to(gbc[:, hd:hd + 1], (chunk, 128)) for hd in range(DN_H)], axis=0)
        beta_st = jnp.concatenate([jnp.broadcast_to(gbc[:, DN_H + hd:DN_H + hd + 1], (chunk, 128))
                                   for hd in range(DN_H)], axis=0)
        g_sq = jnp.concatenate([g_st] * (n4 // 128), axis=1) if n4 >= 128 else g_st[:, :n4]
        gc = _mm_exact_left(lmat, g_st, small)
        yield
        if small:
            dmat = _mm_exact_left(lmat, g_sq * umat, small)
        else:
            gct = gc.T
            dmat = (jnp.concatenate([gc] * (n4 // 128), axis=1)
                    - jnp.concatenate([gct] * (n4 // 128), axis=0))
        gam = jnp.where(incl, jnp.exp(dmat), 0.0)
        eg = jnp.exp(gc)
        kb = kk * beta_st
        kq = _mm_nt(jnp.concatenate([kb, q], axis=0), kk, small)
        yield
        a_mat = jnp.where(strict, kq[:n4] * gam, 0.0)
        x = jnp.concatenate([v * beta_st, kb * eg], axis=1)
        p = -a_mat
        for i in range(n_pow):
            x = x + (_mm_hi if i < 2 else solve_mm)(p, x, small)
            if i < n_pow - 1:
                p = (_mm_hi if i < 1 else solve_mm)(p, p, small)
            yield
        gc_last = [gc[(hd + 1) * chunk - 1:(hd + 1) * chunk, :] for hd in range(DN_H)]
        gl_st = jnp.concatenate([jnp.broadcast_to(r, (chunk, 128)) for r in gc_last], axis=0)
        u_s[slot] = x[:, :128]
        w_s[slot] = x[:, 128:]
        qg_s[slot] = q * eg
        kdec_s[slot] = kk * jnp.exp(gl_st - gc)
        aqk_s[slot] = jnp.where(incl, kq[n4:] * gam, 0.0)
        gl_s[slot] = jnp.exp(jnp.concatenate(gc_last, axis=1))

    def level2(k, slot):
        b = k // n_ch
        r0 = pl.multiple_of((k % n_ch) * chunk, chunk)
        s_old = s_all[b]
        w = w_s[slot]
        qg = qg_s[slot]
        ws, qs = [], []
        for hd in range(DN_H):
            rs = slice(hd * chunk, (hd + 1) * chunk)
            r = _mm(jnp.concatenate([w[rs], qg[rs]], axis=0), s_old[:, hd * 128:(hd + 1) * 128], small)
            ws.append(r[:chunk])
            qs.append(r[chunk:])
        yield
        v_new = u_s[slot] - jnp.concatenate(ws, axis=0)
        o = jnp.concatenate(qs, axis=0) + _mm(aqk_s[slot], v_new, small)
        vbd = jnp.where(vmask, jnp.concatenate([v_new] * DN_H, axis=1), 0.0)
        s_all[b] = s_old * gl_s[slot] + _mm_tn(kdec_s[slot], vbd, small)
        yield
        on = _rms(o, normg)
        for hd in range(DN_H):
            sl = slice(hd * 128, (hd + 1) * 128)
            ob[b, pl.ds(r0, chunk), sl] = on[hd * chunk:(hd + 1) * chunk] * zb[b, pl.ds(r0, chunk), sl]

    assert total % DN_GROUP == 0

    def recurrence(m, base):
        for i in range(DN_GROUP):
            yield from level2(m * DN_GROUP + i, base + i)

    _run_interleaved([level1(i, i) for i in range(DN_GROUP)])

    def group_body(m, _):
        base = (m % 2) * DN_GROUP
        _run_interleaved([recurrence(m, base)]
                         + [level1(jnp.minimum((m + 1) * DN_GROUP + i, total - 1), DN_GROUP - base + i)
                            for i in range(DN_GROUP)])
        return 0

    lax.fori_loop(0, total // DN_GROUP, group_body, 0)

    for hd in range(DN_H):
        snew_ref[:, hd] = s_all[:, :, hd * 128:(hd + 1) * 128]

    y = _dot(ob[...].reshape(rows, DN_QK_W).astype(BF16), wout_ref[...])
    y_ref[...] = y * gate_s[...]


def _deltanet(h, s0, cprev, wts, *, n_seq, t_pad, t_valid, bb_n, tt, chunk):
    n_t = t_pad // tt
    assert n_t == 1 or t_valid == t_pad
    tv_last = t_valid - (n_t - 1) * tt
    rows = bb_n * tt
    small = chunk < 16
    n4 = DN_H * chunk
    const = lambda *shape: pl.BlockSpec(shape, lambda b, t: (0,) * len(shape))
    in_specs = [
        pl.BlockSpec((rows, D_MODEL), lambda b, t: (b * n_t + t, 0)),
        pl.BlockSpec((bb_n, DN_H, DN_DK, DN_DK), lambda b, t: (b, 0, 0, 0)),
        pl.BlockSpec((bb_n, DN_CONV - 1, DN_CONV_W), lambda b, t: (b, 0, 0)),
        const(D_MODEL, DN_CONV_W), const(D_MODEL, DN_QK_W), const(D_MODEL, 128),
        const(DN_CONV, DN_CONV_W), const(1, 128), const(1, 128), const(1, 128),
        const(DN_QK_W, D_MODEL), const(D_MODEL, D_MODEL),
    ]
    out_specs = [
        pl.BlockSpec((rows, D_MODEL), lambda b, t: (b * n_t + t, 0)),
        pl.BlockSpec((bb_n, DN_H, DN_DK, DN_DK), lambda b, t: (b, 0, 0, 0)),
        pl.BlockSpec((bb_n, DN_CONV - 1, DN_CONV_W), lambda b, t: (b, 0, 0)),
    ]
    out_shape = [
        jax.ShapeDtypeStruct((n_seq * t_pad, D_MODEL), F32),
        jax.ShapeDtypeStruct((n_seq, DN_H, DN_DK, DN_DK), F32),
        jax.ShapeDtypeStruct((n_seq, DN_CONV - 1, DN_CONV_W), F32),
    ]
    return pl.pallas_call(
        functools.partial(_dn_kernel, bb_n, tt, chunk, tv_last, n_t, small),
        grid=(n_seq // bb_n, n_t),
        in_specs=in_specs, out_specs=out_specs, out_shape=out_shape,
        scratch_shapes=[pltpu.VMEM((bb_n, tt + 8, DN_CONV_W), F32), pltpu.VMEM((bb_n, tt, DN_CONV_W), F32),
                        pltpu.VMEM((bb_n, tt, 128), F32), pltpu.VMEM((bb_n, tt, DN_QK_W), F32),
                        pltpu.VMEM((bb_n, tt, DN_QK_W), F32),
                        pltpu.VMEM((bb_n, DN_DK, DN_H * 128), F32)]
                       + [pltpu.VMEM((2 * DN_GROUP, n4, 128), F32)] * 4
                       + [pltpu.VMEM((2 * DN_GROUP, n4, n4), F32), pltpu.VMEM((2 * DN_GROUP, 1, DN_H * 128), F32),
                          pltpu.VMEM((rows, D_MODEL), F32)],
        compiler_params=_cparams(("parallel", "arbitrary")),
        name="deltanet",
    )(h, s0, cprev, *wts)


def _sc_kernel(bb_n, tt, tv_last, n_t,
               h_ref, prev_ref, win_ref, convw_ref, wout_ref, wgate_ref,
               y_ref, new_ref, ubuf):
    t = pl.program_id(1)
    rows = bb_n * tt

    @pl.when(t == 0)
    def _():
        ubuf[:, 6:8, :] = prev_ref[...]

    h = h_ref[...]
    p = _dot(h, win_ref[...])
    gate = jax.nn.sigmoid(_dot(h, wgate_ref[...]))
    bgate = p[:, :SC_W]
    u = p[:, SC_W:2 * SC_W] * p[:, 2 * SC_W:]
    ubuf[:, 8:8 + tt, :] = u.reshape(bb_n, tt, SC_W)
    y = 0.0
    for j in range(SC_CONV):
        y = y + ubuf[:, 6 + j:6 + j + tt, :] * convw_ref[j:j + 1, :].reshape(1, 1, SC_W)

    @pl.when(t == n_t - 1)
    def _():
        new_ref[...] = ubuf[:, 6 + tv_last:8 + tv_last, :]

    if n_t > 1:
        ubuf[:, 0:8, :] = ubuf[:, tt:tt + 8, :]

    z = (bgate * y.reshape(rows, SC_W)).astype(BF16)
    y_ref[...] = _dot(z, wout_ref[...]) * gate


def _shortconv(h, prev, wts, *, n_seq, t_pad, t_valid, bb_n, tt):
    n_t = t_pad // tt
    assert n_t == 1 or t_valid == t_pad
    tv_last = t_valid - (n_t - 1) * tt
    rows = bb_n * tt
    const = lambda *shape: pl.BlockSpec(shape, lambda b, t: (0,) * len(shape))
    return pl.pallas_call(
        functools.partial(_sc_kernel, bb_n, tt, tv_last, n_t),
        grid=(n_seq // bb_n, n_t),
        in_specs=[pl.BlockSpec((rows, D_MODEL), lambda b, t: (b * n_t + t, 0)),
                  pl.BlockSpec((bb_n, SC_CONV - 1, SC_W), lambda b, t: (b, 0, 0)),
                  const(D_MODEL, 3 * SC_W), const(SC_CONV, SC_W), const(SC_W, D_MODEL),
                  const(D_MODEL, D_MODEL)],
        out_specs=[pl.BlockSpec((rows, D_MODEL), lambda b, t: (b * n_t + t, 0)),
                   pl.BlockSpec((bb_n, SC_CONV - 1, SC_W), lambda b, t: (b, 0, 0))],
        out_shape=[jax.ShapeDtypeStruct((n_seq * t_pad, D_MODEL), F32),
                   jax.ShapeDtypeStruct((n_seq, SC_CONV - 1, SC_W), F32)],
        scratch_shapes=[pltpu.VMEM((bb_n, tt + 8, SC_W), F32)],
        compiler_params=_cparams(("parallel", "arbitrary")),
        name="shortconv",
    )(h, prev, *wts)


def _memkv_kernel(m_ref, g_ref, wkv_ref, kg_ref, k_ref, v_ref):
    n = _rms(m_ref[...], g_ref[...]).astype(BF16)
    kv = _dot(n, wkv_ref[...])
    kg = kg_ref[...]
    for hd in range(MEM_H):
        sl = slice(hd * MEM_HD, (hd + 1) * MEM_HD)
        k_ref[:, sl] = _rms(kv[:, sl], kg)
    v_ref[...] = kv[:, MEM_W:]


def _mem_kv(mem2d, norm_g, w_kv, k_gain, *, tm):
    m = mem2d.shape[0]
    return pl.pallas_call(
        _memkv_kernel,
        grid=(m // tm,),
        in_specs=[pl.BlockSpec((tm, D_MODEL), lambda i: (i, 0)),
                  pl.BlockSpec((1, D_MODEL), lambda i: (0, 0)),
                  pl.BlockSpec((D_MODEL, 2 * MEM_W), lambda i: (0, 0)),
                  pl.BlockSpec((1, MEM_HD), lambda i: (0, 0))],
        out_specs=[pl.BlockSpec((tm, MEM_W), lambda i: (i, 0))] * 2,
        out_shape=[jax.ShapeDtypeStruct((m, MEM_W), F32)] * 2,
        compiler_params=_cparams(("parallel",)),
        name="mem_kv",
    )(mem2d, norm_g, w_kv, k_gain)


def _memattn_kernel(bb_n, tt, small,
                    h_ref, mk_ref, mv_ref, wq_ref, qg_ref, wout_ref, wgate_ref,
                    y_ref, qs, ob):
    rows = bb_n * tt
    h = h_ref[...]
    q = _dot(h, wq_ref[...])
    gate = jax.nn.sigmoid(_dot(h, wgate_ref[...]))
    qg = qg_ref[...]
    for hd in range(MEM_H):
        sl = slice(hd * MEM_HD, (hd + 1) * MEM_HD)
        qs[:, :, sl] = _rms(q[:, sl], qg).reshape(bb_n, tt, MEM_HD)

    def seq_body(b, _):
        def scores(hd):
            sl = slice(hd * MEM_HD, (hd + 1) * MEM_HD)
            return _mm_nt(qs[b, :, sl], mk_ref[b, :, sl], small)

        s_next = scores(0)
        for hd in range(MEM_H):
            sl = slice(hd * MEM_HD, (hd + 1) * MEM_HD)
            s = s_next * (MEM_HD ** -0.5)
            if hd + 1 < MEM_H:
                s_next = scores(hd + 1)
            s = s - jnp.max(s, axis=-1, keepdims=True)
            e = jnp.exp(s)
            p = e / jnp.sum(e, axis=-1, keepdims=True)
            ob[b, :, sl] = _mm(p, mv_ref[b, :, sl], small)
        return 0

    lax.fori_loop(0, bb_n, seq_body, 0)
    y = _dot(ob[...].reshape(rows, MEM_W).astype(BF16), wout_ref[...])
    y_ref[...] = y * gate


def _mem_attn(h, mk, mv, wts, *, n_seq, t_pad, bb_n, tt):
    n_t = t_pad // tt
    rows = bb_n * tt
    const = lambda *shape: pl.BlockSpec(shape, lambda b, t: (0,) * len(shape))
    return pl.pallas_call(
        functools.partial(_memattn_kernel, bb_n, tt, tt < 16),
        grid=(n_seq // bb_n, n_t),
        in_specs=[pl.BlockSpec((rows, D_MODEL), lambda b, t: (b * n_t + t, 0)),
                  pl.BlockSpec((bb_n, N_MEM, MEM_W), lambda b, t: (b, 0, 0)),
                  pl.BlockSpec((bb_n, N_MEM, MEM_W), lambda b, t: (b, 0, 0)),
                  const(D_MODEL, MEM_W), const(1, MEM_HD), const(MEM_W, D_MODEL), const(D_MODEL, D_MODEL)],
        out_specs=pl.BlockSpec((rows, D_MODEL), lambda b, t: (b * n_t + t, 0)),
        out_shape=jax.ShapeDtypeStruct((n_seq * t_pad, D_MODEL), F32),
        scratch_shapes=[pltpu.VMEM((bb_n, tt, MEM_W), F32), pltpu.VMEM((bb_n, tt, MEM_W), F32)],
        compiler_params=_cparams(("parallel", "arbitrary")),
        name="mem_attn",
    )(h, mk, mv, *wts)


def _mlaproj_kernel(h_ref, cos_ref, sin_ref, wq_ref, qna_ref, wqp_ref, wqs_ref, qg_ref,
                    wkv_ref, kvna_ref, wkr_ref, wkrs_ref, wuk_ref, wuv_ref, kg_ref,
                    q_ref, ckv_ref, kr_ref, k_ref, v_ref):
    h = h_ref[...]
    cos = cos_ref[...]
    sin = sin_ref[...]
    cq = _dot(h, wq_ref[...])
    ckv_raw = _dot(h, wkv_ref[...])
    kr_a = _dot(h, wkr_ref[...])
    kr_b = _dot(h, wkrs_ref[...])
    cqn = _rms(cq, qna_ref[...]).astype(BF16)
    q_raw = _dot(cqn, wqp_ref[...])
    q_swp = _dot(cqn, wqs_ref[...])
    ckv = _rms(ckv_raw, kvna_ref[...])
    cb = ckv.astype(BF16)
    k_raw = _dot(cb, wuk_ref[...])
    v_ref[...] = _dot(cb, wuv_ref[...]).astype(BF16)
    ckv_ref[...] = ckv
    krp = kr_a * cos + kr_b * sin
    kr_ref[...] = krp[:, :MLA_ROPE]

    qg = qg_ref[...]
    inv_n = 1.0 / MLA_QK
    for hd in range(MLA_H):
        sl = slice(hd * MLA_LANES, (hd + 1) * MLA_LANES)
        qh = q_raw[:, sl] * cos + q_swp[:, sl] * sin
        ms = jnp.sum(qh * qh, axis=-1, keepdims=True) * inv_n
        q_ref[:, sl] = (qh * lax.rsqrt(ms + EPS) * qg).astype(q_ref.dtype)

    kg = kg_ref[...]
    for hd in range(MLA_H):
        sl = slice(hd * MLA_LANES, (hd + 1) * MLA_LANES)
        kh = k_raw[:, sl] + krp
        ms = jnp.sum(kh * kh, axis=-1, keepdims=True) * inv_n
        k_ref[:, sl] = (kh * lax.rsqrt(ms + EPS) * kg).astype(k_ref.dtype)


def _mla_proj(h, cos, sin, wts, *, tm, n_tab, qk_dtype):
    m = h.shape[0]
    hw = MLA_H * MLA_LANES
    vw = MLA_H * MLA_V
    const = lambda *shape: pl.BlockSpec(shape, lambda i: (0,) * len(shape))
    row = lambda width: pl.BlockSpec((tm, width), lambda i: (i, 0))
    in_specs = [row(D_MODEL),
                pl.BlockSpec((tm, MLA_LANES), lambda i: (i % n_tab, 0)),
                pl.BlockSpec((tm, MLA_LANES), lambda i: (i % n_tab, 0)),
                const(D_MODEL, MLA_RANK), const(1, MLA_RANK), const(MLA_RANK, hw), const(MLA_RANK, hw),
                const(1, MLA_LANES),
                const(D_MODEL, MLA_RANK), const(1, MLA_RANK), const(D_MODEL, MLA_LANES), const(D_MODEL, MLA_LANES),
                const(MLA_RANK, hw), const(MLA_RANK, vw), const(1, MLA_LANES)]
    return pl.pallas_call(
        _mlaproj_kernel,
        grid=(m // tm,),
        in_specs=in_specs,
        out_specs=[row(hw), row(MLA_RANK), row(MLA_ROPE), row(hw), row(vw)],
        out_shape=[jax.ShapeDtypeStruct((m, hw), qk_dtype),
                   jax.ShapeDtypeStruct((m, MLA_RANK), F32),
                   jax.ShapeDtypeStruct((m, MLA_ROPE), F32),
                   jax.ShapeDtypeStruct((m, hw), qk_dtype),
                   jax.ShapeDtypeStruct((m, vw), BF16)],
        compiler_params=_cparams(("parallel",)),
        name="mla_proj",
    )(h, cos, sin, *wts)


def _flash_kernel(tq, q_ref, k_ref, v_ref, o_ref, m_scr, l_scr, acc_scr):
    qi = pl.program_id(1)
    ki = pl.program_id(2)
    c2 = (MLA_QK ** -0.5) * LOG2E
    n_rep = tq // 128

    @pl.when(ki == 0)
    def _():
        m_scr[...] = jnp.full(m_scr.shape, -jnp.inf, F32)
        l_scr[...] = jnp.zeros_like(l_scr)
        acc_scr[...] = jnp.zeros_like(acc_scr)

    def compute(diag):
        if diag:
            row = lax.broadcasted_iota(jnp.int32, (tq, tq), 0)
            col = lax.broadcasted_iota(jnp.int32, (tq, tq), 1)
            keep = col <= row
        ones = jnp.ones((tq, 128), BF16)

        def qk(hd):
            sl = slice(hd * MLA_LANES, (hd + 1) * MLA_LANES)
            return _dot_nt(q_ref[:, sl], k_ref[:, sl])

        s_next = qk(0)
        for hd in range(MLA_H):
            s = s_next * c2
            if hd + 1 < MLA_H:
                s_next = qk(hd + 1)
            if diag:
                s = jnp.where(keep, s, -jnp.inf)
            m_old = m_scr[hd]
            m_new = jnp.maximum(m_old, jnp.max(s, axis=-1, keepdims=True))
            alpha = jnp.exp2(m_old - m_new)
            p = jnp.exp2(s - jnp.concatenate([m_new] * n_rep, axis=1)).astype(BF16)
            pair = hd // 2
            vext = jnp.concatenate([v_ref[:, pair * 128:(pair + 1) * 128], ones], axis=1)
            r = _dot(p, vext)
            acc_scr[hd] = alpha * acc_scr[hd] + r[:, :128]
            l_scr[hd] = alpha * l_scr[hd] + r[:, 128:]
            m_scr[hd] = m_new

    @pl.when(ki < qi)
    def _():
        compute(False)

    @pl.when(ki == qi)
    def _():
        compute(True)
        lane = lax.broadcasted_iota(jnp.int32, (tq, 128), 1)
        for pair in range(MLA_H // 2):
            even = acc_scr[2 * pair] / l_scr[2 * pair]
            odd = acc_scr[2 * pair + 1] / l_scr[2 * pair + 1]
            o_ref[:, pair * 128:(pair + 1) * 128] = jnp.where(lane < MLA_V, even, odd).astype(o_ref.dtype)


def _mla_prompt_attn(q, k, v, *, n_seq, seq, tq):
    nq = seq // tq
    hw = MLA_H * MLA_LANES
    vw = MLA_H * MLA_V
    return pl.pallas_call(
        functools.partial(_flash_kernel, tq),
        grid=(n_seq, nq, nq),
        in_specs=[pl.BlockSpec((tq, hw), lambda b, i, j: (b * nq + i, 0)),
                  pl.BlockSpec((tq, hw), lambda b, i, j: (b * nq + jnp.minimum(i, j), 0)),
                  pl.BlockSpec((tq, vw), lambda b, i, j: (b * nq + jnp.minimum(i, j), 0))],
        out_specs=pl.BlockSpec((tq, vw), lambda b, i, j: (b * nq + i, 0)),
        out_shape=jax.ShapeDtypeStruct((n_seq * seq, vw), BF16),
        scratch_shapes=[pltpu.VMEM((MLA_H, tq, 128), F32)] * 3,
        compiler_params=_cparams(("parallel", "parallel", "arbitrary")),
        name="mla_flash",
    )(q, k, v)


SAMPLE_ROWS = 8
SUB_KEYS = 1024


def _mla_sample_kernel(layer, n_seq, n_pg, n_steps, t_valid,
                       pt_ref, qall_ref, q_ref, knew_ref, cnew_ref, wukp_ref, wukt_ref, kg_ref, ckv_hbm, kr_hbm,
                       o_ref,
                       lhs, qabs_all, qabs, qrope, qblk, m_scr, l_scr, acc_scr, cbuf, rbuf, sem):
    b = pl.program_id(0)
    st = pl.program_id(1)
    c2 = (MLA_QK ** -0.5) * LOG2E
    inv_n = 1.0 / MLA_QK
    tk = n_pg * PAGE
    nq = SAMPLE_ROWS
    n_up = MLA_H * MLA_NOPE
    g = b * n_steps + st
    slot = g % 2

    def page_copies(bb, ss, sl):
        cps = []
        for i in range(n_pg):
            page = pt_ref[bb, ss * n_pg + i]
            cps.append((pltpu.make_async_copy(ckv_hbm.at[layer, page], cbuf.at[sl, pl.ds(i * PAGE, PAGE), :],
                                              sem.at[sl, 0]), i % 2))
            cps.append((pltpu.make_async_copy(kr_hbm.at[layer, page], rbuf.at[sl, i], sem.at[sl, 1]), (i + 1) % 2))
        return cps

    n_total = n_seq * n_steps
    is_last = g == n_total - 1

    @pl.when(g == 0)
    def _():
        for cp, prio in page_copies(0, 0, 0):
            cp.start(priority=prio)
        kg = kg_ref[...]
        for hd in range(MLA_H):
            sl = slice(hd * MLA_LANES, (hd + 1) * MLA_LANES)
            qh, ql = _split2(qall_ref[:, sl] * kg)
            w = wukp_ref[:, sl]
            qabs_all[hd] = _dot_nt(qh, w) + _dot_nt(ql, w)

    @pl.when(st == 0)
    def _():
        m_scr[...] = jnp.full(m_scr.shape, -jnp.inf, F32)
        l_scr[...] = jnp.zeros_like(l_scr)
        acc_scr[...] = jnp.zeros_like(acc_scr)
        kg = kg_ref[...]
        q = q_ref[...]
        lane_head = lax.broadcasted_iota(jnp.int32, q.shape, 1) // MLA_LANES
        r0 = pl.multiple_of(b * nq, nq)
        for hd in range(MLA_H):
            sl = slice(hd * MLA_LANES, (hd + 1) * MLA_LANES)
            qabs[hd * nq:(hd + 1) * nq, :] = qabs_all[hd, pl.ds(r0, nq), :]
            qrope[hd * nq:(hd + 1) * nq, :] = q[:, hd * MLA_LANES:hd * MLA_LANES + MLA_ROPE] * kg[:, :MLA_ROPE]
            qblk[hd * nq:(hd + 1) * nq, :] = jnp.where(lane_head == hd, q, 0.0)
        lhs[:n_up, :] = wukt_ref[...]
        lhs[n_up:, :] = qabs[...].astype(BF16)

    for cp, _ in page_copies(b, st, slot):
        cp.wait()

    seq_end = st == n_steps - 1
    nb = jnp.where(is_last, b, jnp.where(seq_end, b + 1, b))
    ns = jnp.where(is_last, st, jnp.where(seq_end, 0, st + 1))
    for cp, prio in page_copies(nb, ns, 1 - slot):
        cp.start(priority=prio)

    lhs_v = lhs[...]
    qr = qrope[...].astype(BF16)
    pg_sub = SUB_KEYS // PAGE

    def score_block(j):
        cb = cbuf[slot, j * SUB_KEYS:(j + 1) * SUB_KEYS, :].astype(BF16)
        krt = jnp.concatenate([rbuf[slot, i] for i in range(j * pg_sub, (j + 1) * pg_sub)],
                              axis=1)
        big = _dot_nt(lhs_v, cb)
        knt = big[:n_up]
        ssq = jnp.sum((knt * knt).reshape(MLA_H, MLA_NOPE, SUB_KEYS), axis=1)
        ssq_r = jnp.sum(krt * krt, axis=0, keepdims=True)
        rs = lax.rsqrt((ssq + ssq_r) * inv_n + EPS) * c2
        s = big[n_up:] + _dot(qr, krt.astype(BF16))
        s = jnp.concatenate([s[hd * nq:(hd + 1) * nq, :] * rs[hd:hd + 1, :] for hd in range(MLA_H)], axis=0)
        return s, cb

    m_run = m_scr[...]
    l_new = l_scr[...]
    acc = acc_scr[...]
    n_sub = tk // SUB_KEYS
    blk = score_block(0)
    for j in range(n_sub):
        s, cb = blk
        if j + 1 < n_sub:
            blk = score_block(j + 1)
        m_new = jnp.maximum(m_run, jnp.max(s, axis=-1, keepdims=True))
        alpha = jnp.exp2(m_run - m_new)
        p = jnp.exp2(s - m_new)
        l_new = alpha * l_new + jnp.sum(p, axis=-1, keepdims=True)
        acc = alpha * acc + _dot(p.astype(BF16), cb)
        m_run = m_new
    l_scr[...] = l_new
    acc_scr[...] = acc
    m_scr[...] = m_new

    @pl.when(st == n_steps - 1)
    def _():
        cn = cnew_ref[...]
        sn = _dot_nt(_rnd(qblk[...]), _rnd(knew_ref[...])) * c2
        row = lax.broadcasted_iota(jnp.int32, sn.shape, 0) % nq
        col = lax.broadcasted_iota(jnp.int32, sn.shape, 1)
        sn = jnp.where((col <= row) & (col < t_valid), sn, -jnp.inf)
        m_o = m_scr[...]
        m_n = jnp.maximum(m_o, jnp.max(sn, axis=-1, keepdims=True))
        al = jnp.exp2(m_o - m_n)
        pn = jnp.exp2(sn - m_n)
        l_f = al * l_scr[...] + jnp.sum(pn, axis=-1, keepdims=True)
        o_ref[...] = (al * acc_scr[...] + _dot(_rnd(pn), _rnd(cn))) / l_f

    @pl.when(is_last)
    def _():
        for cp, _ in page_copies(b, st, 1 - slot):
            cp.wait()


def _mla_sample_attn(page_table, q, k_new, c_new, wts, ckv_pool, kr_pool_t, layer, *, n_seq, t_valid, n_pg):
    n_pages = page_table.shape[1]
    n_steps = n_pages // n_pg
    hw = MLA_H * MLA_LANES
    tk = n_pg * PAGE
    nq = SAMPLE_ROWS
    nr = MLA_H * nq

    const = lambda *shape: pl.BlockSpec(shape, lambda b, s, pt: (0,) * len(shape))
    hbm = pl.BlockSpec(memory_space=pl.ANY)
    in_specs = [const(n_seq * nq, hw),
                pl.BlockSpec((nq, hw), lambda b, s, pt: (b, 0)),
                pl.BlockSpec((nq, hw), lambda b, s, pt: (b, 0)),
                pl.BlockSpec((nq, MLA_RANK), lambda b, s, pt: (b, 0)),
                const(MLA_RANK, hw), const(MLA_H * MLA_NOPE, MLA_RANK), const(1, MLA_LANES),
                hbm, hbm]
    grid_spec = pltpu.PrefetchScalarGridSpec(
        num_scalar_prefetch=1,
        grid=(n_seq, n_steps),
        in_specs=in_specs,
        out_specs=pl.BlockSpec((nr, MLA_RANK), lambda b, s, pt: (b, 0)),
        scratch_shapes=[pltpu.VMEM((MLA_H * MLA_NOPE + nr, MLA_RANK), BF16),
                        pltpu.VMEM((MLA_H, n_seq * nq, MLA_RANK), F32),
                        pltpu.VMEM((nr, MLA_RANK), F32), pltpu.VMEM((nr, MLA_ROPE), F32),
                        pltpu.VMEM((nr, hw), F32),
                        pltpu.VMEM((nr, 1), F32), pltpu.VMEM((nr, 1), F32), pltpu.VMEM((nr, MLA_RANK), F32),
                        pltpu.VMEM((2, tk, MLA_RANK), F32), pltpu.VMEM((2, n_pg, MLA_ROPE, PAGE), F32),
                        pltpu.SemaphoreType.DMA((2, 2))],
    )
    return pl.pallas_call(
        functools.partial(_mla_sample_kernel, layer, n_seq, n_pg, n_steps, t_valid),
        grid_spec=grid_spec,
        out_shape=jax.ShapeDtypeStruct((n_seq * nr, MLA_RANK), F32),
        compiler_params=_cparams(("arbitrary", "arbitrary")),
        name="mla_paged",
    )(page_table, q, q, k_new, c_new, *wts, ckv_pool, kr_pool_t)


def _mla_up_kernel(n_seq, pc_ref, h_ref, wuvs_ref, wout_ref, wgate_ref, y_ref):
    nq = SAMPLE_ROWS
    o = jnp.zeros((n_seq * nq, MLA_H * MLA_V), F32)
    for hd in range(MLA_H):
        pch = pc_ref[:, hd * nq:(hd + 1) * nq, :].reshape(n_seq * nq, MLA_RANK)
        o = o + _dot(pch.astype(BF16), wuvs_ref[hd])
    y_ref[...] = _dot(o.astype(BF16), wout_ref[...]) * jax.nn.sigmoid(_dot(h_ref[...], wgate_ref[...]))


def _mla_up_proj_gate(pc, h, wuv_sel, w_out, w_gate, *, n_seq):
    rows = n_seq * SAMPLE_ROWS
    return pl.pallas_call(
        functools.partial(_mla_up_kernel, n_seq),
        out_shape=jax.ShapeDtypeStruct((rows, D_MODEL), F32),
        compiler_params=pltpu.CompilerParams(vmem_limit_bytes=VMEM_LIMIT),
        name="mla_up_proj_gate",
    )(pc.reshape(n_seq, MLA_H * SAMPLE_ROWS, MLA_RANK), h, wuv_sel, w_out, w_gate)


def _projgate_kernel(o_ref, h_ref, wout_ref, wgate_ref, y_ref):
    y_ref[...] = _dot(o_ref[...].astype(BF16), wout_ref[...]) * jax.nn.sigmoid(_dot(h_ref[...], wgate_ref[...]))


def _proj_gate(o, h, w_out, w_gate, *, tm):
    m, kdim = o.shape
    return pl.pallas_call(
        _projgate_kernel,
        grid=(m // tm,),
        in_specs=[pl.BlockSpec((tm, kdim), lambda i: (i, 0)),
                  pl.BlockSpec((tm, D_MODEL), lambda i: (i, 0)),
                  pl.BlockSpec((kdim, D_MODEL), lambda i: (0, 0)),
                  pl.BlockSpec((D_MODEL, D_MODEL), lambda i: (0, 0))],
        out_specs=pl.BlockSpec((tm, D_MODEL), lambda i: (i, 0)),
        out_shape=jax.ShapeDtypeStruct((m, D_MODEL), F32),
        compiler_params=_cparams(("parallel",)),
        name="proj_gate",
    )(o, h, w_out, w_gate)


def _pad_lanes(x, width):
    return jnp.pad(x, [(0, 0)] * (x.ndim - 1) + [(0, width - x.shape[-1])])


def _mla_head_layout(nope, r1, r2):
    z = jnp.zeros(nope.shape[:-1] + (MLA_LANES - MLA_QK,), nope.dtype)
    x = jnp.concatenate([r1, r2, nope, z], axis=-1)
    return x.reshape(x.shape[:-2] + (MLA_H * MLA_LANES,))


def _gain_layout(g):
    half = MLA_ROPE // 2
    return jnp.concatenate([g[MLA_NOPE:MLA_NOPE + half], g[MLA_NOPE + half:], g[:MLA_NOPE],
                            jnp.zeros((MLA_LANES - MLA_QK,), g.dtype)]).reshape(1, MLA_LANES)


def _rope_tables(pos):
    half = MLA_ROPE // 2
    inv = ROPE_THETA ** (-jnp.arange(half, dtype=F32) / half)
    ang = pos.astype(F32)[:, None] * inv
    cos, sin = jnp.cos(ang), jnp.sin(ang)
    n = pos.shape[0]
    cos_t = jnp.concatenate([cos, cos, jnp.ones((n, MLA_NOPE), F32), jnp.zeros((n, MLA_LANES - MLA_QK), F32)], -1)
    sin_t = jnp.concatenate([sin, sin, jnp.zeros((n, MLA_LANES - MLA_ROPE), F32)], -1)
    return cos_t, sin_t


def _layer_weights(l, p):
    w_in = p['w_in'][l]
    sizes = (DN_CONV_W, DN_QK_W, DN_H, DN_H, SC_W, SC_W, SC_W, MLA_RANK, MLA_RANK, MLA_ROPE, MEM_W, 4 * D_MODEL)
    offs = np.concatenate([[0], np.cumsum(sizes)])
    seg = [w_in[:, offs[i]:offs[i + 1]] for i in range(len(sizes))]
    bf = lambda x: x.astype(BF16)
    row = lambda x: x.reshape(1, -1)
    gates = [bf(seg[11][:, i * D_MODEL:(i + 1) * D_MODEL]) for i in range(4)]
    half = MLA_ROPE // 2

    w = {}
    w['ffn1'] = (row(p['ffn1_norm'][l]), bf(p['ffn1_w_gu'][l]), bf(p['ffn1_w_down'][l]))
    w['ffn2'] = (row(p['ffn2_norm'][l]), bf(p['ffn2_w_gu'][l]), bf(p['ffn2_w_down'][l]))
    w['mix_norm'] = row(p['mix_norm'][l])
    w['w_o'] = bf(p['w_o'][l])
    w['dn'] = (bf(seg[0]), bf(seg[1]), bf(_pad_lanes(jnp.concatenate([seg[2], seg[3]], 1), 128)),
               p['dn_conv_w'][l], _pad_lanes(row(p['dn_A_log'][l]), 128), _pad_lanes(row(p['dn_dt_bias'][l]), 128),
               row(p['dn_norm'][l]), bf(p['dn_w_out'][l]), gates[0])
    w['sc'] = (bf(jnp.concatenate([seg[4], seg[5], seg[6]], 1)), p['sc_conv_w'][l], bf(p['sc_w_out'][l]), gates[1])

    wq = p['mla_w_q_b'][l].reshape(MLA_RANK, MLA_H, MLA_QK)
    q_nope, q_r1, q_r2 = wq[..., :MLA_NOPE], wq[..., MLA_NOPE:MLA_NOPE + half], wq[..., MLA_NOPE + half:]
    wq_perm = _mla_head_layout(q_nope, q_r1, q_r2)
    wq_swap = _mla_head_layout(jnp.zeros_like(q_nope), -q_r2, q_r1)
    wkr = seg[9]
    wkr_pad = _pad_lanes(wkr, MLA_LANES)
    wkr_swap = _pad_lanes(jnp.concatenate([-wkr[:, half:], wkr[:, :half]], 1), MLA_LANES)
    wkv = p['mla_w_kv_b'][l].reshape(MLA_RANK, MLA_H, MLA_NOPE + MLA_V)
    w_uk, w_uv = wkv[..., :MLA_NOPE], wkv[..., MLA_NOPE:]
    zr = jnp.zeros((MLA_RANK, MLA_H, half), F32)
    wuk_perm = bf(_mla_head_layout(w_uk, zr, zr))
    k_gain = _gain_layout(p['mla_k_norm'][l])
    w['mla_proj'] = (bf(seg[7]), row(p['mla_q_norm_a'][l]), bf(wq_perm), bf(wq_swap), _gain_layout(p['mla_q_norm'][l]),
                     bf(seg[8]), row(p['mla_kv_norm_a'][l]), bf(wkr_pad), bf(wkr_swap),
                     wuk_perm, bf(w_uv.reshape(MLA_RANK, MLA_H * MLA_V)), k_gain)
    eye = jnp.eye(MLA_H, dtype=F32)
    w['mla_wuv_sel'] = bf((w_uv[None] * eye[:, None, :, None]).reshape(MLA_H, MLA_RANK, MLA_H * MLA_V))
    w['mla_sample'] = (wuk_perm, bf(w_uk.reshape(MLA_RANK, MLA_H * MLA_NOPE).T), k_gain)
    w['mla_out'] = (bf(p['mla_w_out'][l]), gates[2])
    w['mem_kv'] = (row(p['mem_norm'][l]), bf(p['mem_w_kv'][l]), row(p['mem_k_norm'][l]))
    w['mem'] = (bf(seg[10]), row(p['mem_q_norm'][l]), bf(p['mem_w_out'][l]), gates[3])
    return w


def _group_layer(x, w, *, n_seq, t_pad, t_valid, tm, bb_n, tt, chunk, dn_state, sc_state, mem_kv, cos, sin,
                 n_tab, mla_attend, q_dtype):
    cfg = dict(n_seq=n_seq, t_pad=t_pad, bb_n=bb_n, tt=tt)
    x1, h = _ffn(x, *w['ffn1'], tm=tm, h_gain=w['mix_norm'])
    y_dn, dn_s, dn_c = _deltanet(h, dn_state[0], dn_state[1], w['dn'], t_valid=t_valid, chunk=chunk, **cfg)
    y_sc, sc_c = _shortconv(h, sc_state, w['sc'], t_valid=t_valid, **cfg)
    q, ckv, kr, k, v = _mla_proj(h, cos, sin, w['mla_proj'], tm=tm, n_tab=n_tab, qk_dtype=q_dtype)
    if mla_attend is None:
        o = _mla_prompt_attn(q, k, v, n_seq=n_seq, seq=t_pad, tq=tm)
        y_mla = _proj_gate(o, h, *w['mla_out'], tm=tm)
    else:
        y_mla = mla_attend(q, k, ckv, h)
    y_mem = _mem_attn(h, mem_kv[0], mem_kv[1], w['mem'], **cfg)
    x3 = _ffn(x1, *w['ffn2'], tm=min(tm, 256), merge=((y_dn, y_sc, y_mla, y_mem), w['w_o']))
    return x3, dn_s, dn_c, sc_c, ckv, kr


def kernel(x_prompt, x_sample, state_dn_S, state_dn_conv, state_sc_conv, cache_mla_ckv, cache_mla_krope, cache_mem_k, cache_mem_v, page_table, mem_prompt, ffn1_norm, ffn1_w_gu, ffn1_w_down, mix_norm, w_in, dn_conv_w, dn_A_log, dn_dt_bias, dn_norm, dn_w_out, sc_conv_w, sc_w_out, mla_q_norm_a, mla_w_q_b, mla_kv_norm_a, mla_w_kv_b, mla_q_norm, mla_k_norm, mla_w_out, mem_norm, mem_w_kv, mem_q_norm, mem_k_norm, mem_w_out, w_o, ffn2_norm, ffn2_w_gu, ffn2_w_down):
    params = dict(ffn1_norm=ffn1_norm, ffn1_w_gu=ffn1_w_gu, ffn1_w_down=ffn1_w_down, mix_norm=mix_norm, w_in=w_in,
                  dn_conv_w=dn_conv_w, dn_A_log=dn_A_log, dn_dt_bias=dn_dt_bias, dn_norm=dn_norm, dn_w_out=dn_w_out,
                  sc_conv_w=sc_conv_w, sc_w_out=sc_w_out, mla_q_norm_a=mla_q_norm_a, mla_w_q_b=mla_w_q_b,
                  mla_kv_norm_a=mla_kv_norm_a, mla_w_kv_b=mla_w_kv_b, mla_q_norm=mla_q_norm, mla_k_norm=mla_k_norm,
                  mla_w_out=mla_w_out, mem_norm=mem_norm, mem_w_kv=mem_w_kv, mem_q_norm=mem_q_norm,
                  mem_k_norm=mem_k_norm, mem_w_out=mem_w_out, w_o=w_o, ffn2_norm=ffn2_norm, ffn2_w_gu=ffn2_w_gu,
                  ffn2_w_down=ffn2_w_down)
    depth = w_in.shape[0]
    bp, seq, _ = x_prompt.shape
    bs, td, _ = x_sample.shape
    tds = SAMPLE_ROWS
    n_pages = page_table.shape[1]
    past = n_pages * PAGE
    krope_t = jnp.transpose(cache_mla_krope, (0, 1, 3, 2))

    cos_p, sin_p = _rope_tables(jnp.arange(seq))
    cos_s, sin_s = _rope_tables(past + jnp.arange(tds))
    cos_s, sin_s = jnp.tile(cos_s, (bs, 1)), jnp.tile(sin_s, (bs, 1))

    xp = x_prompt.reshape(bp * seq, D_MODEL)
    xs = jnp.pad(x_sample, ((0, 0), (0, tds - td), (0, 0))).reshape(bs * tds, D_MODEL)
    zero_s = jnp.zeros((bp, DN_H, DN_DK, DN_DK), F32)
    zero_dc = jnp.zeros((bp, DN_CONV - 1, DN_CONV_W), F32)
    zero_sc = jnp.zeros((bp, SC_CONV - 1, SC_W), F32)
    mem2d = mem_prompt.reshape(bp * N_MEM, D_MODEL)

    outs = {k: [] for k in ('pS', 'pdc', 'psc', 'pckv', 'pkr', 'pmk', 'pmv', 'sS', 'sdc', 'ssc', 'sckv', 'skr')}
    tm_p = 512
    for l in range(depth):
        w = _layer_weights(l, params)
        mk, mv = _mem_kv(mem2d, *w['mem_kv'], tm=tm_p)
        mk3, mv3 = mk.reshape(bp, N_MEM, MEM_W), mv.reshape(bp, N_MEM, MEM_W)
        xp, s_p, dc_p, sc_p, ckv_p, kr_p = _group_layer(
            xp, w, n_seq=bp, t_pad=seq, t_valid=seq, tm=tm_p, bb_n=1, tt=tm_p, chunk=DN_CHUNK,
            dn_state=(zero_s, zero_dc), sc_state=zero_sc, mem_kv=(mk3, mv3), cos=cos_p, sin=sin_p,
            n_tab=seq // tm_p, mla_attend=None, q_dtype=BF16)
        outs['pS'].append(s_p); outs['pdc'].append(dc_p); outs['psc'].append(sc_p)
        outs['pckv'].append(ckv_p.reshape(bp, seq, MLA_RANK)); outs['pkr'].append(kr_p.reshape(bp, seq, MLA_ROPE))
        outs['pmk'].append(mk.reshape(bp, N_MEM, MEM_H, MEM_HD)); outs['pmv'].append(mv.reshape(bp, N_MEM, MEM_H, MEM_HD))

        def attend(q, k, ckv, h, l=l, w=w):
            pc = _mla_sample_attn(page_table, q, k, ckv, w['mla_sample'], cache_mla_ckv, krope_t, l,
                                  n_seq=bs, t_valid=td, n_pg=32)
            return _mla_up_proj_gate(pc, h, w['mla_wuv_sel'], *w['mla_out'], n_seq=bs)

        xs, s_s, dc_s, sc_s, ckv_s, kr_s = _group_layer(
            xs, w, n_seq=bs, t_pad=tds, t_valid=td, tm=bs * tds, bb_n=8, tt=tds, chunk=tds,
            dn_state=(state_dn_S[l], state_dn_conv[l]), sc_state=state_sc_conv[l],
            mem_kv=(cache_mem_k[l].reshape(bs, N_MEM, MEM_W), cache_mem_v[l].reshape(bs, N_MEM, MEM_W)),
            cos=cos_s, sin=sin_s, n_tab=1, mla_attend=attend, q_dtype=F32)
        outs['sS'].append(s_s); outs['sdc'].append(dc_s); outs['ssc'].append(sc_s)
        outs['sckv'].append(ckv_s.reshape(bs, tds, MLA_RANK)[:, :td])
        outs['skr'].append(kr_s.reshape(bs, tds, MLA_ROPE)[:, :td])

    st = lambda k: jnp.stack(outs[k])
    y_prompt = xp.reshape(bp, seq, D_MODEL)
    y_sample = xs.reshape(bs, tds, D_MODEL)[:, :td]
    return (y_prompt, y_sample, st('pS'), st('pdc'), st('psc'), st('pckv'), st('pkr'), st('pmk'), st('pmv'),
            st('sS'), st('sdc'), st('ssc'), st('sckv'), st('skr'))
```

```python
import functools

import numpy as np
import jax
import jax.numpy as jnp
from jax import lax
from jax.experimental import pallas as pl
from jax.experimental.pallas import tpu as pltpu

F32 = jnp.float32
BF16 = jnp.bfloat16

D_MODEL = 1024
D_FF = 2816
EPS = 1e-6
N_MEM = 256
PAGE = 128
DN_H = 4
DN_DK = 128
DN_QK_W = 512
DN_CONV_W = 1536
DN_CONV = 4
DN_CHUNK = 64
DN_GROUP = 2
SC_W = 512
SC_CONV = 3
MLA_H = 8
MLA_RANK = 256
MLA_NOPE = 64
MLA_ROPE = 32
MLA_V = 64
MLA_QK = 96
MLA_LANES = 128
ROPE_THETA = 10000.0
LOG2E = 1.4426950408889634
MEM_H = 4
MEM_HD = 128
MEM_W = 512

VMEM_LIMIT = 56 * 1024 * 1024


def _cparams(sem):
    return pltpu.CompilerParams(dimension_semantics=sem, vmem_limit_bytes=VMEM_LIMIT)


class _W:
    def __init__(self, arr, layer):
        self.arr, self.layer = arr, layer


def _wspec(w, resident=False):
    shape = w.arr.shape[1:]
    index = lambda *_: (w.layer,) + (0,) * len(shape)
    if resident:
        return pl.BlockSpec((None,) + shape, index, pipeline_mode=pl.Buffered(1))
    return pl.BlockSpec((None,) + shape, index)


def _rms(x, g):
    ms = jnp.mean(x * x, axis=-1, keepdims=True)
    return x * lax.rsqrt(ms + EPS) * g


def _silu(x):
    return x * jax.nn.sigmoid(x)


def _rnd(x):
    return x.astype(BF16).astype(F32)


def _dot(a, b):
    return jnp.dot(a, b, preferred_element_type=F32)


def _dot_nt(a, b):
    return lax.dot_general(a, b, (((1,), (1,)), ((), ())), preferred_element_type=F32)


def _mm(a, b, small):
    if small:
        return _dot(_rnd(a), _rnd(b))
    return _dot(a.astype(BF16), b.astype(BF16))


def _mm_nt(a, b, small):
    if small:
        return _dot_nt(_rnd(a), _rnd(b))
    return _dot_nt(a.astype(BF16), b.astype(BF16))


def _mm_tn(a, b, small):
    dn = (((0,), (0,)), ((), ()))
    if small:
        return lax.dot_general(_rnd(a), _rnd(b), dn, preferred_element_type=F32)
    return lax.dot_general(a.astype(BF16), b.astype(BF16), dn, preferred_element_type=F32)


def _run_interleaved(gens):
    gens = list(gens)
    while gens:
        for g in list(gens):
            try:
                next(g)
            except StopIteration:
                gens.remove(g)


def _split2(x):
    hi = x.astype(BF16)
    lo = (x - hi.astype(F32)).astype(BF16)
    return hi, lo


def _mm_hi(a, b, small):
    if small:
        return jnp.dot(a, b, preferred_element_type=F32, precision=lax.Precision.HIGHEST)
    ah, al = _split2(a)
    bh, bl = _split2(b)
    return _dot(ah, bh) + (_dot(ah, bl) + _dot(al, bh))


def _mm_exact_left(lmat, b, small):
    if small:
        return jnp.dot(lmat, b, preferred_element_type=F32, precision=lax.Precision.HIGHEST)
    lb = lmat.astype(BF16)
    b1 = b.astype(BF16)
    r1 = b - b1.astype(F32)
    b2 = r1.astype(BF16)
    b3 = (r1 - b2.astype(F32)).astype(BF16)
    return _dot(lb, b1) + (_dot(lb, b2) + _dot(lb, b3))


FFN_SPLIT = 2


def _ffn_kernel(merge, emit_h, *refs):
    it = iter(refs)
    x_ref = next(it)
    if merge:
        y_refs = [next(it) for _ in range(4)]
        wo_ref = next(it)
    g_ref, wgu_ref, wd_ref = next(it), next(it), next(it)
    if emit_h:
        g2_ref = next(it)
    o_ref = next(it)
    if emit_h:
        h_ref = next(it)

    x = x_ref[...]
    if merge:
        m = ((y_refs[0][...] + y_refs[1][...]) + y_refs[2][...]) + y_refs[3][...]
        x = x + _dot(m.astype(BF16), wo_ref[...])
    h = _rms(x, g_ref[...]).astype(BF16)
    tf = D_FF // FFN_SPLIT
    def gate_up(j):
        return (_dot(h, wgu_ref[:, j * tf:(j + 1) * tf]),
                _dot(h, wgu_ref[:, D_FF + j * tf:D_FF + (j + 1) * tf]))

    acc = None
    nxt = gate_up(0)
    for j in range(FFN_SPLIT):
        gate, up = nxt
        if j + 1 < FFN_SPLIT:
            nxt = gate_up(j + 1)
        a = (_silu(gate) * up).astype(BF16)
        d = _dot(a, wd_ref[j * tf:(j + 1) * tf, :])
        acc = d if acc is None else acc + d
    out = x + 0.5 * acc
    o_ref[...] = out
    if emit_h:
        h_ref[...] = _rms(out, g2_ref[...]).astype(BF16)


def _ffn(x, norm_g, w_gu, w_down, *, tm, merge=None, h_gain=None):
    m = x.shape[0]
    row = pl.BlockSpec((tm, D_MODEL), lambda i: (i, 0))
    in_specs, args = [row], [x]
    if merge is not None:
        ys, w_o = merge
        in_specs += [row] * 4 + [_wspec(w_o, resident=True)]
        args += list(ys) + [w_o.arr]
    in_specs += [_wspec(w, resident=True) for w in (norm_g, w_gu, w_down)]
    args += [norm_g.arr, w_gu.arr, w_down.arr]
    out_shape = [jax.ShapeDtypeStruct((m, D_MODEL), F32)]
    out_specs = [row]
    if h_gain is not None:
        in_specs.append(_wspec(h_gain, resident=True))
        args.append(h_gain.arr)
        out_shape.append(jax.ShapeDtypeStruct((m, D_MODEL), BF16))
        out_specs.append(row)
    res = pl.pallas_call(
        functools.partial(_ffn_kernel, merge is not None, h_gain is not None),
        grid=(m // tm,),
        in_specs=in_specs, out_specs=out_specs, out_shape=out_shape,
        compiler_params=_cparams(("parallel",)),
        name="ffn_merge" if merge is not None else "ffn",
    )(*args)
    return res if h_gain is not None else res[0]


def _dn_kernel(bb_n, tt, chunk, tv_last, n_t, small,
               h_ref, s0_ref, cprev_ref, wqkv_ref, wz_ref, wab_ref, convw_ref, alog_ref, dtb_ref,
               normg_ref, wout_ref, wgate_ref,
               y_ref, snew_ref, cnew_ref,
               xbuf, cs, gb, zb, ob, s_all, u_s, w_s, qg_s, kdec_s, aqk_s, gl_s, gate_s):
    t = pl.program_id(1)
    rows = bb_n * tt

    @pl.when(t == 0)
    def _():
        xbuf[:, 5:8, :] = cprev_ref[...]
        snew_ref[...] = s0_ref[...]

    h = h_ref[...]
    qkv = _dot(h, wqkv_ref[...])
    ab = _dot(h, wab_ref[...])
    zs = _dot(h, wz_ref[...])
    gate = _dot(h, wgate_ref[...])
    xbuf[:, 8:8 + tt, :] = qkv.reshape(bb_n, tt, DN_CONV_W)
    c = 0.0
    for j in range(DN_CONV):
        c = c + xbuf[:, 5 + j:5 + j + tt, :] * convw_ref[j:j + 1, :].reshape(1, 1, DN_CONV_W)
    cnew_ref[...] = xbuf[:, 5 + tv_last:8 + tv_last, :]
    if n_t > 1:
        xbuf[:, 0:8, :] = xbuf[:, tt:tt + 8, :]

    g = -jnp.exp(alog_ref[...]) * jax.nn.softplus(ab + dtb_ref[...])
    lane = lax.broadcasted_iota(jnp.int32, (rows, 128), 1)
    gbv = jnp.where(lane < DN_H, g, jax.nn.sigmoid(ab)).reshape(bb_n, tt, 128)
    if tv_last < tt:
        trow = lax.broadcasted_iota(jnp.int32, (bb_n, tt, 128), 1)
        gbv = jnp.where(trow < tv_last, gbv, 0.0)
    gb[...] = gbv
    c = _silu(c)
    for grp in range(8):
        sl = slice(grp * 128, (grp + 1) * 128)
        xg = c[:, :, sl]
        xn = xg * lax.rsqrt(jnp.sum(xg * xg, axis=-1, keepdims=True) + EPS)
        if grp < DN_H:
            xn = xn * (DN_DK ** -0.5)
        cs[:, :, sl] = xn
    cs[:, :, 2 * DN_QK_W:] = c[:, :, 2 * DN_QK_W:]
    zb[...] = _silu(zs).reshape(bb_n, tt, DN_QK_W)
    gate_s[...] = jax.nn.sigmoid(gate)

    n4 = DN_H * chunk
    ri = lax.broadcasted_iota(jnp.int32, (n4, n4), 0)
    ci = lax.broadcasted_iota(jnp.int32, (n4, n4), 1)
    same = (ri // chunk) == (ci // chunk)
    incl = same & (ci <= ri)
    strict = same & (ci < ri)
    lmat = incl.astype(F32)
    umat = strict.astype(F32)
    vmask = (lax.broadcasted_iota(jnp.int32, (n4, DN_H * 128), 0) // chunk
             == lax.broadcasted_iota(jnp.int32, (n4, DN_H * 128), 1) // 128)
    n_pow = int(np.log2(chunk))
    normg = normg_ref[...]
    n_ch = tt // chunk
    total = bb_n * n_ch
    solve_mm = _mm_hi if small else _mm

    for hd in range(DN_H):
        s_all[:, :, hd * 128:(hd + 1) * 128] = snew_ref[:, hd]

    def stack_rows(ref, b, r0, off):
        return jnp.concatenate([ref[b, pl.ds(r0, chunk), off + hd * 128:off + (hd + 1) * 128]
                                for hd in range(DN_H)], axis=0)

    def level1(k, slot):
        b = k // n_ch
        r0 = (k % n_ch) * chunk
        if not isinstance(k, int):
            r0 = pl.multiple_of(r0, chunk)
        q = stack_rows(cs, b, r0, 0)
        kk = stack_rows(cs, b, r0, DN_QK_W)
        v = stack_rows(cs, b, r0, 2 * DN_QK_W)
        gbc = gb[b, pl.ds(r0, chunk), :]
        g_st = jnp.concatenate([jnp.broadcast_to(gbc[:, hd:hd + 1], (chunk, 128)) for hd in range(DN_H)], axis=0)
        beta_st = jnp.concatenate([jnp.broadcast_to(gbc[:, DN_H + hd:DN_H + hd + 1], (chunk, 128))
                                   for hd in range(DN_H)], axis=0)
        g_sq = jnp.concatenate([g_st] * (n4 // 128), axis=1) if n4 >= 128 else g_st[:, :n4]
        gc = _mm_exact_left(lmat, g_st, small)
        yield
        if small:
            dmat = _mm_exact_left(lmat, g_sq * umat, small)
        else:
            gct = gc.T
            dmat = (jnp.concatenate([gc] * (n4 // 128), axis=1)
                    - jnp.concatenate([gct] * (n4 // 128), axis=0))
        gam = jnp.where(incl, jnp.exp(dmat), 0.0)
        eg = jnp.exp(gc)
        kb = kk * beta_st
        kq = _mm_nt(jnp.concatenate([kb, q], axis=0), kk, small)
        yield
        a_mat = jnp.where(strict, kq[:n4] * gam, 0.0)
        x = jnp.concatenate([v * beta_st, kb * eg], axis=1)
        p = -a_mat
        for i in range(n_pow):
            x = x + (_mm_hi if i < 2 else solve_mm)(p, x, small)
            if i < n_pow - 1:
                p = (_mm_hi if i < 1 else solve_mm)(p, p, small)
            yield
        gc_last = [gc[(hd + 1) * chunk - 1:(hd + 1) * chunk, :] for hd in range(DN_H)]
        gl_st = jnp.concatenate([jnp.broadcast_to(r, (chunk, 128)) for r in gc_last], axis=0)
        u_s[slot] = x[:, :128]
        w_s[slot] = x[:, 128:]
        qg_s[slot] = q * eg
        kdec_s[slot] = kk * jnp.exp(gl_st - gc)
        aqk_s[slot] = jnp.where(incl, kq[n4:] * gam, 0.0)
        gl_s[slot] = jnp.exp(jnp.concatenate(gc_last, axis=1))

    def level2(k, slot):
        b = k // n_ch
        r0 = pl.multiple_of((k % n_ch) * chunk, chunk)
        s_old = s_all[b]
        w = w_s[slot]
        qg = qg_s[slot]
        ws, qs = [], []
        for hd in range(DN_H):
            rs = slice(hd * chunk, (hd + 1) * chunk)
            r = _mm(jnp.concatenate([w[rs], qg[rs]], axis=0), s_old[:, hd * 128:(hd + 1) * 128], small)
            ws.append(r[:chunk])
            qs.append(r[chunk:])
        yield
        v_new = u_s[slot] - jnp.concatenate(ws, axis=0)
        o = jnp.concatenate(qs, axis=0) + _mm(aqk_s[slot], v_new, small)
        vbd = jnp.where(vmask, jnp.concatenate([v_new] * DN_H, axis=1), 0.0)
        s_all[b] = s_old * gl_s[slot] + _mm_tn(kdec_s[slot], vbd, small)
        yield
        on = _rms(o, normg)
        for hd in range(DN_H):
            sl = slice(hd * 128, (hd + 1) * 128)
            ob[b, pl.ds(r0, chunk), sl] = on[hd * chunk:(hd + 1) * chunk] * zb[b, pl.ds(r0, chunk), sl]

    assert total % DN_GROUP == 0

    def recurrence(m, base):
        for i in range(DN_GROUP):
            yield from level2(m * DN_GROUP + i, base + i)

    _run_interleaved([level1(i, i) for i in range(DN_GROUP)])

    def group_body(m, _):
        base = (m % 2) * DN_GROUP
        _run_interleaved([recurrence(m, base)]
                         + [level1(jnp.minimum((m + 1) * DN_GROUP + i, total - 1), DN_GROUP - base + i)
                            for i in range(DN_GROUP)])
        return 0

    lax.fori_loop(0, total // DN_GROUP, group_body, 0)

    for hd in range(DN_H):
        snew_ref[:, hd] = s_all[:, :, hd * 128:(hd + 1) * 128]

    y = _dot(ob[...].reshape(rows, DN_QK_W).astype(BF16), wout_ref[...])
    y_ref[...] = y * gate_s[...]


def _deltanet(h, s0, cprev, wts, *, n_seq, t_pad, t_valid, bb_n, tt, chunk):
    n_t = t_pad // tt
    assert n_t == 1 or t_valid == t_pad
    tv_last = t_valid - (n_t - 1) * tt
    rows = bb_n * tt
    small = chunk < 16
    n4 = DN_H * chunk
    in_specs = [
        pl.BlockSpec((rows, D_MODEL), lambda b, t: (b * n_t + t, 0)),
        pl.BlockSpec((bb_n, DN_H, DN_DK, DN_DK), lambda b, t: (b, 0, 0, 0)),
        pl.BlockSpec((bb_n, DN_CONV - 1, DN_CONV_W), lambda b, t: (b, 0, 0)),
    ] + [_wspec(w) for w in wts]
    out_specs = [
        pl.BlockSpec((rows, D_MODEL), lambda b, t: (b * n_t + t, 0)),
        pl.BlockSpec((bb_n, DN_H, DN_DK, DN_DK), lambda b, t: (b, 0, 0, 0)),
        pl.BlockSpec((bb_n, DN_CONV - 1, DN_CONV_W), lambda b, t: (b, 0, 0)),
    ]
    out_shape = [
        jax.ShapeDtypeStruct((n_seq * t_pad, D_MODEL), F32),
        jax.ShapeDtypeStruct((n_seq, DN_H, DN_DK, DN_DK), F32),
        jax.ShapeDtypeStruct((n_seq, DN_CONV - 1, DN_CONV_W), F32),
    ]
    return pl.pallas_call(
        functools.partial(_dn_kernel, bb_n, tt, chunk, tv_last, n_t, small),
        grid=(n_seq // bb_n, n_t),
        in_specs=in_specs, out_specs=out_specs, out_shape=out_shape,
        scratch_shapes=[pltpu.VMEM((bb_n, tt + 8, DN_CONV_W), F32), pltpu.VMEM((bb_n, tt, DN_CONV_W), F32),
                        pltpu.VMEM((bb_n, tt, 128), F32), pltpu.VMEM((bb_n, tt, DN_QK_W), F32),
                        pltpu.VMEM((bb_n, tt, DN_QK_W), F32),
                        pltpu.VMEM((bb_n, DN_DK, DN_H * 128), F32)]
                       + [pltpu.VMEM((2 * DN_GROUP, n4, 128), F32)] * 4
                       + [pltpu.VMEM((2 * DN_GROUP, n4, n4), F32), pltpu.VMEM((2 * DN_GROUP, 1, DN_H * 128), F32),
                          pltpu.VMEM((rows, D_MODEL), F32)],
        compiler_params=_cparams(("parallel", "arbitrary")),
        name="deltanet",
    )(h, s0, cprev, *[w.arr for w in wts])


def _sc_kernel(bb_n, tt, tv_last, n_t,
               h_ref, prev_ref, win_ref, convw_ref, wout_ref, wgate_ref,
               y_ref, new_ref, ubuf):
    t = pl.program_id(1)
    rows = bb_n * tt

    @pl.when(t == 0)
    def _():
        ubuf[:, 6:8, :] = prev_ref[...]

    h = h_ref[...]
    p = _dot(h, win_ref[...])
    gate = jax.nn.sigmoid(_dot(h, wgate_ref[...]))
    bgate = p[:, :SC_W]
    u = p[:, SC_W:2 * SC_W] * p[:, 2 * SC_W:]
    ubuf[:, 8:8 + tt, :] = u.reshape(bb_n, tt, SC_W)
    y = 0.0
    for j in range(SC_CONV):
        y = y + ubuf[:, 6 + j:6 + j + tt, :] * convw_ref[j:j + 1, :].reshape(1, 1, SC_W)

    new_ref[...] = ubuf[:, 6 + tv_last:8 + tv_last, :]
    if n_t > 1:
        ubuf[:, 0:8, :] = ubuf[:, tt:tt + 8, :]

    z = (bgate * y.reshape(rows, SC_W)).astype(BF16)
    y_ref[...] = _dot(z, wout_ref[...]) * gate


def _shortconv(h, prev, wts, *, n_seq, t_pad, t_valid, bb_n, tt):
    n_t = t_pad // tt
    assert n_t == 1 or t_valid == t_pad
    tv_last = t_valid - (n_t - 1) * tt
    rows = bb_n * tt
    return pl.pallas_call(
        functools.partial(_sc_kernel, bb_n, tt, tv_last, n_t),
        grid=(n_seq // bb_n, n_t),
        in_specs=[pl.BlockSpec((rows, D_MODEL), lambda b, t: (b * n_t + t, 0)),
                  pl.BlockSpec((bb_n, SC_CONV - 1, SC_W), lambda b, t: (b, 0, 0))]
                 + [_wspec(w) for w in wts],
        out_specs=[pl.BlockSpec((rows, D_MODEL), lambda b, t: (b * n_t + t, 0)),
                   pl.BlockSpec((bb_n, SC_CONV - 1, SC_W), lambda b, t: (b, 0, 0))],
        out_shape=[jax.ShapeDtypeStruct((n_seq * t_pad, D_MODEL), F32),
                   jax.ShapeDtypeStruct((n_seq, SC_CONV - 1, SC_W), F32)],
        scratch_shapes=[pltpu.VMEM((bb_n, tt + 8, SC_W), F32)],
        compiler_params=_cparams(("parallel", "arbitrary")),
        name="shortconv",
    )(h, prev, *[w.arr for w in wts])


def _memkv_kernel(m_ref, g_ref, wkv_ref, kg_ref, k_ref, v_ref):
    n = _rms(m_ref[...], g_ref[...]).astype(BF16)
    kv = _dot(n, wkv_ref[...])
    kg = kg_ref[...]
    for hd in range(MEM_H):
        sl = slice(hd * MEM_HD, (hd + 1) * MEM_HD)
        k_ref[:, sl] = _rms(kv[:, sl], kg)
    v_ref[...] = kv[:, MEM_W:]


def _mem_kv(mem2d, wts, *, tm):
    m = mem2d.shape[0]
    return pl.pallas_call(
        _memkv_kernel,
        grid=(m // tm,),
        in_specs=[pl.BlockSpec((tm, D_MODEL), lambda i: (i, 0))] + [_wspec(w) for w in wts],
        out_specs=[pl.BlockSpec((tm, MEM_W), lambda i: (i, 0))] * 2,
        out_shape=[jax.ShapeDtypeStruct((m, MEM_W), F32)] * 2,
        compiler_params=_cparams(("parallel",)),
        name="mem_kv",
    )(mem2d, *[w.arr for w in wts])


def _memattn_kernel(bb_n, tt, small,
                    h_ref, mk_ref, mv_ref, wq_ref, qg_ref, wout_ref, wgate_ref,
                    y_ref, qs, ob):
    rows = bb_n * tt
    h = h_ref[...]
    q = _dot(h, wq_ref[...])
    gate = jax.nn.sigmoid(_dot(h, wgate_ref[...]))
    qg = qg_ref[...]
    for hd in range(MEM_H):
        sl = slice(hd * MEM_HD, (hd + 1) * MEM_HD)
        qs[:, :, sl] = _rms(q[:, sl], qg).reshape(bb_n, tt, MEM_HD)

    def seq_body(b, _):
        def scores(hd):
            sl = slice(hd * MEM_HD, (hd + 1) * MEM_HD)
            return _mm_nt(qs[b, :, sl], mk_ref[b, :, sl], small)

        s_next = scores(0)
        for hd in range(MEM_H):
            sl = slice(hd * MEM_HD, (hd + 1) * MEM_HD)
            s = s_next * (MEM_HD ** -0.5)
            if hd + 1 < MEM_H:
                s_next = scores(hd + 1)
            s = s - jnp.max(s, axis=-1, keepdims=True)
            e = jnp.exp(s)
            p = e / jnp.sum(e, axis=-1, keepdims=True)
            ob[b, :, sl] = _mm(p, mv_ref[b, :, sl], small)
        return 0

    lax.fori_loop(0, bb_n, seq_body, 0)
    y = _dot(ob[...].reshape(rows, MEM_W).astype(BF16), wout_ref[...])
    y_ref[...] = y * gate


def _mem_attn(h, mk, mv, wts, *, n_seq, t_pad, bb_n, tt):
    n_t = t_pad // tt
    rows = bb_n * tt
    return pl.pallas_call(
        functools.partial(_memattn_kernel, bb_n, tt, tt < 16),
        grid=(n_seq // bb_n, n_t),
        in_specs=[pl.BlockSpec((rows, D_MODEL), lambda b, t: (b * n_t + t, 0)),
                  pl.BlockSpec((bb_n, N_MEM, MEM_W), lambda b, t: (b, 0, 0)),
                  pl.BlockSpec((bb_n, N_MEM, MEM_W), lambda b, t: (b, 0, 0))]
                 + [_wspec(w) for w in wts],
        out_specs=pl.BlockSpec((rows, D_MODEL), lambda b, t: (b * n_t + t, 0)),
        out_shape=jax.ShapeDtypeStruct((n_seq * t_pad, D_MODEL), F32),
        scratch_shapes=[pltpu.VMEM((bb_n, tt, MEM_W), F32), pltpu.VMEM((bb_n, tt, MEM_W), F32)],
        compiler_params=_cparams(("parallel", "arbitrary")),
        name="mem_attn",
    )(h, mk, mv, *[w.arr for w in wts])


def _mlaproj_kernel(h_ref, cos_ref, sin_ref, wq_ref, qna_ref, wqp_ref, wqs_ref, qg_ref,
                    wkv_ref, kvna_ref, wkr_ref, wkrs_ref, wuk_ref, wuv_ref, kg_ref,
                    q_ref, ckv_ref, kr_ref, k_ref, v_ref):
    h = h_ref[...]
    cos = cos_ref[...]
    sin = sin_ref[...]
    cq = _dot(h, wq_ref[...])
    ckv_raw = _dot(h, wkv_ref[...])
    kr_a = _dot(h, wkr_ref[...])
    kr_b = _dot(h, wkrs_ref[...])
    cqn = _rms(cq, qna_ref[...]).astype(BF16)
    q_raw = _dot(cqn, wqp_ref[...])
    q_swp = _dot(cqn, wqs_ref[...])
    ckv = _rms(ckv_raw, kvna_ref[...])
    cb = ckv.astype(BF16)
    k_raw = _dot(cb, wuk_ref[...])
    v_ref[...] = _dot(cb, wuv_ref[...]).astype(BF16)
    ckv_ref[...] = ckv
    krp = kr_a * cos + kr_b * sin
    kr_ref[...] = krp[:, :MLA_ROPE]

    qg = qg_ref[...]
    inv_n = 1.0 / MLA_QK
    for hd in range(MLA_H):
        sl = slice(hd * MLA_LANES, (hd + 1) * MLA_LANES)
        qh = q_raw[:, sl] * cos + q_swp[:, sl] * sin
        ms = jnp.sum(qh * qh, axis=-1, keepdims=True) * inv_n
        q_ref[:, sl] = (qh * lax.rsqrt(ms + EPS) * qg).astype(q_ref.dtype)

    kg = kg_ref[...]
    for hd in range(MLA_H):
        sl = slice(hd * MLA_LANES, (hd + 1) * MLA_LANES)
        kh = k_raw[:, sl] + krp
        ms = jnp.sum(kh * kh, axis=-1, keepdims=True) * inv_n
        k_ref[:, sl] = (kh * lax.rsqrt(ms + EPS) * kg).astype(k_ref.dtype)


def _mla_proj(h, cos, sin, wts, *, tm, n_tab, qk_dtype):
    m = h.shape[0]
    hw = MLA_H * MLA_LANES
    vw = MLA_H * MLA_V
    row = lambda width: pl.BlockSpec((tm, width), lambda i: (i, 0))
    in_specs = [row(D_MODEL),
                pl.BlockSpec((tm, MLA_LANES), lambda i: (i % n_tab, 0)),
                pl.BlockSpec((tm, MLA_LANES), lambda i: (i % n_tab, 0))] + [_wspec(w) for w in wts]
    return pl.pallas_call(
        _mlaproj_kernel,
        grid=(m // tm,),
        in_specs=in_specs,
        out_specs=[row(hw), row(MLA_RANK), row(MLA_ROPE), row(hw), row(vw)],
        out_shape=[jax.ShapeDtypeStruct((m, hw), qk_dtype),
                   jax.ShapeDtypeStruct((m, MLA_RANK), F32),
                   jax.ShapeDtypeStruct((m, MLA_ROPE), F32),
                   jax.ShapeDtypeStruct((m, hw), qk_dtype),
                   jax.ShapeDtypeStruct((m, vw), BF16)],
        compiler_params=_cparams(("parallel",)),
        name="mla_proj",
    )(h, cos, sin, *[w.arr for w in wts])


def _flash_kernel(tq, q_ref, k_ref, v_ref, o_ref, m_scr, l_scr, acc_scr):
    qi = pl.program_id(1)
    ki = pl.program_id(2)
    c2 = (MLA_QK ** -0.5) * LOG2E
    n_rep = tq // 128

    @pl.when(ki == 0)
    def _():
        m_scr[...] = jnp.full(m_scr.shape, -jnp.inf, F32)
        l_scr[...] = jnp.zeros_like(l_scr)
        acc_scr[...] = jnp.zeros_like(acc_scr)

    def compute(diag):
        if diag:
            row = lax.broadcasted_iota(jnp.int32, (tq, tq), 0)
            col = lax.broadcasted_iota(jnp.int32, (tq, tq), 1)
            keep = col <= row
        ones = jnp.ones((tq, 128), BF16)

        def qk(hd):
            sl = slice(hd * MLA_LANES, (hd + 1) * MLA_LANES)
            return _dot_nt(q_ref[:, sl], k_ref[:, sl])

        s_next = qk(0)
        for hd in range(MLA_H):
            s = s_next * c2
            if hd + 1 < MLA_H:
                s_next = qk(hd + 1)
            if diag:
                s = jnp.where(keep, s, -jnp.inf)
            m_old = m_scr[hd]
            m_new = jnp.maximum(m_old, jnp.max(s, axis=-1, keepdims=True))
            alpha = jnp.exp2(m_old - m_new)
            p = jnp.exp2(s - jnp.concatenate([m_new] * n_rep, axis=1)).astype(BF16)
            pair = hd // 2
            vext = jnp.concatenate([v_ref[:, pair * 128:(pair + 1) * 128], ones], axis=1)
            r = _dot(p, vext)
            acc_scr[hd] = alpha * acc_scr[hd] + r[:, :128]
            l_scr[hd] = alpha * l_scr[hd] + r[:, 128:]
            m_scr[hd] = m_new

    @pl.when(ki < qi)
    def _():
        compute(False)

    @pl.when(ki == qi)
    def _():
        compute(True)
        lane = lax.broadcasted_iota(jnp.int32, (tq, 128), 1)
        for pair in range(MLA_H // 2):
            even = acc_scr[2 * pair] / l_scr[2 * pair]
            odd = acc_scr[2 * pair + 1] / l_scr[2 * pair + 1]
            o_ref[:, pair * 128:(pair + 1) * 128] = jnp.where(lane < MLA_V, even, odd).astype(o_ref.dtype)


def _mla_prompt_attn(q, k, v, *, n_seq, seq, tq):
    nq = seq // tq
    hw = MLA_H * MLA_LANES
    vw = MLA_H * MLA_V
    return pl.pallas_call(
        functools.partial(_flash_kernel, tq),
        grid=(n_seq, nq, nq),
        in_specs=[pl.BlockSpec((tq, hw), lambda b, i, j: (b * nq + i, 0)),
                  pl.BlockSpec((tq, hw), lambda b, i, j: (b * nq + jnp.minimum(i, j), 0)),
                  pl.BlockSpec((tq, vw), lambda b, i, j: (b * nq + jnp.minimum(i, j), 0))],
        out_specs=pl.BlockSpec((tq, vw), lambda b, i, j: (b * nq + i, 0)),
        out_shape=jax.ShapeDtypeStruct((n_seq * seq, vw), BF16),
        scratch_shapes=[pltpu.VMEM((MLA_H, tq, 128), F32)] * 3,
        compiler_params=_cparams(("parallel", "parallel", "arbitrary")),
        name="mla_flash",
    )(q, k, v)


SAMPLE_ROWS = 8
SUB_KEYS = 1024


def _mla_sample_kernel(layer, n_seq, n_pg, n_steps, t_valid,
                       pt_ref, qall_ref, q_ref, knew_ref, cnew_ref, wukp_ref, wukt_ref, kg_ref, ckv_hbm, kr_hbm,
                       o_ref,
                       lhs, qabs_all, qabs, qrope, qblk, m_scr, l_scr, acc_scr, cbuf, rbuf, sem):
    b = pl.program_id(0)
    st = pl.program_id(1)
    c2 = (MLA_QK ** -0.5) * LOG2E
    inv_n = 1.0 / MLA_QK
    tk = n_pg * PAGE
    nq = SAMPLE_ROWS
    n_up = MLA_H * MLA_NOPE
    g = b * n_steps + st
    slot = g % 2

    def page_copies(bb, ss, sl):
        cps = []
        for i in range(n_pg):
            page = pt_ref[bb, ss * n_pg + i]
            cps.append((pltpu.make_async_copy(ckv_hbm.at[layer, page], cbuf.at[sl, pl.ds(i * PAGE, PAGE), :],
                                              sem.at[sl, 0]), i % 2))
            cps.append((pltpu.make_async_copy(kr_hbm.at[layer, page], rbuf.at[sl, i], sem.at[sl, 1]), (i + 1) % 2))
        return cps

    n_total = n_seq * n_steps
    is_last = g == n_total - 1

    @pl.when(g == 0)
    def _():
        for cp, prio in page_copies(0, 0, 0):
            cp.start(priority=prio)
        kg = kg_ref[...]
        for hd in range(MLA_H):
            sl = slice(hd * MLA_LANES, (hd + 1) * MLA_LANES)
            qh, ql = _split2(qall_ref[:, sl] * kg)
            w = wukp_ref[:, sl]
            qabs_all[hd] = _dot_nt(qh, w) + _dot_nt(ql, w)

    @pl.when(st == 0)
    def _():
        m_scr[...] = jnp.full(m_scr.shape, -jnp.inf, F32)
        l_scr[...] = jnp.zeros_like(l_scr)
        acc_scr[...] = jnp.zeros_like(acc_scr)
        kg = kg_ref[...]
        q = q_ref[...]
        lane_head = lax.broadcasted_iota(jnp.int32, q.shape, 1) // MLA_LANES
        r0 = pl.multiple_of(b * nq, nq)
        for hd in range(MLA_H):
            sl = slice(hd * MLA_LANES, (hd + 1) * MLA_LANES)
            qabs[hd * nq:(hd + 1) * nq, :] = qabs_all[hd, pl.ds(r0, nq), :]
            qrope[hd * nq:(hd + 1) * nq, :] = q[:, hd * MLA_LANES:hd * MLA_LANES + MLA_ROPE] * kg[:, :MLA_ROPE]
            qblk[hd * nq:(hd + 1) * nq, :] = jnp.where(lane_head == hd, q, 0.0)
        lhs[:n_up, :] = wukt_ref[...]
        lhs[n_up:, :] = qabs[...].astype(BF16)

    for cp, _ in page_copies(b, st, slot):
        cp.wait()

    seq_end = st == n_steps - 1
    nb = jnp.where(is_last, b, jnp.where(seq_end, b + 1, b))
    ns = jnp.where(is_last, st, jnp.where(seq_end, 0, st + 1))
    for cp, prio in page_copies(nb, ns, 1 - slot):
        cp.start(priority=prio)

    lhs_v = lhs[...]
    qr = qrope[...].astype(BF16)
    pg_sub = SUB_KEYS // PAGE

    def score_block(j):
        cb = cbuf[slot, j * SUB_KEYS:(j + 1) * SUB_KEYS, :].astype(BF16)
        krt = jnp.concatenate([rbuf[slot, i] for i in range(j * pg_sub, (j + 1) * pg_sub)],
                              axis=1)
        big = _dot_nt(lhs_v, cb)
        knt = big[:n_up]
        ssq = jnp.sum((knt * knt).reshape(MLA_H, MLA_NOPE, SUB_KEYS), axis=1)
        ssq_r = jnp.sum(krt * krt, axis=0, keepdims=True)
        rs = lax.rsqrt((ssq + ssq_r) * inv_n + EPS) * c2
        s = big[n_up:] + _dot(qr, krt.astype(BF16))
        s = jnp.concatenate([s[hd * nq:(hd + 1) * nq, :] * rs[hd:hd + 1, :] for hd in range(MLA_H)], axis=0)
        return s, cb

    m_run = m_scr[...]
    l_new = l_scr[...]
    acc = acc_scr[...]
    n_sub = tk // SUB_KEYS
    blk = score_block(0)
    for j in range(n_sub):
        s, cb = blk
        if j + 1 < n_sub:
            blk = score_block(j + 1)
        m_new = jnp.maximum(m_run, jnp.max(s, axis=-1, keepdims=True))
        alpha = jnp.exp2(m_run - m_new)
        p = jnp.exp2(s - m_new)
        l_new = alpha * l_new + jnp.sum(p, axis=-1, keepdims=True)
        acc = alpha * acc + _dot(p.astype(BF16), cb)
        m_run = m_new
    l_scr[...] = l_new
    acc_scr[...] = acc
    m_scr[...] = m_new

    @pl.when(st == n_steps - 1)
    def _():
        cn = cnew_ref[...]
        sn = _dot_nt(_rnd(qblk[...]), _rnd(knew_ref[...])) * c2
        row = lax.broadcasted_iota(jnp.int32, sn.shape, 0) % nq
        col = lax.broadcasted_iota(jnp.int32, sn.shape, 1)
        sn = jnp.where((col <= row) & (col < t_valid), sn, -jnp.inf)
        m_o = m_scr[...]
        m_n = jnp.maximum(m_o, jnp.max(sn, axis=-1, keepdims=True))
        al = jnp.exp2(m_o - m_n)
        pn = jnp.exp2(sn - m_n)
        l_f = al * l_scr[...] + jnp.sum(pn, axis=-1, keepdims=True)
        o_ref[...] = (al * acc_scr[...] + _dot(_rnd(pn), _rnd(cn))) / l_f

    @pl.when(is_last)
    def _():
        for cp, _ in page_copies(b, st, 1 - slot):
            cp.wait()


def _mla_sample_attn(page_table, q, k_new, c_new, wts, ckv_pool, kr_pool_t, layer, *, n_seq, t_valid, n_pg):
    n_pages = page_table.shape[1]
    n_steps = n_pages // n_pg
    hw = MLA_H * MLA_LANES
    tk = n_pg * PAGE
    nq = SAMPLE_ROWS
    nr = MLA_H * nq

    const = lambda *shape: pl.BlockSpec(shape, lambda b, s, pt: (0,) * len(shape))
    hbm = pl.BlockSpec(memory_space=pl.ANY)
    in_specs = [const(n_seq * nq, hw),
                pl.BlockSpec((nq, hw), lambda b, s, pt: (b, 0)),
                pl.BlockSpec((nq, hw), lambda b, s, pt: (b, 0)),
                pl.BlockSpec((nq, MLA_RANK), lambda b, s, pt: (b, 0))]
    in_specs += [_wspec(w) for w in wts] + [hbm, hbm]
    grid_spec = pltpu.PrefetchScalarGridSpec(
        num_scalar_prefetch=1,
        grid=(n_seq, n_steps),
        in_specs=in_specs,
        out_specs=pl.BlockSpec((nr, MLA_RANK), lambda b, s, pt: (b, 0)),
        scratch_shapes=[pltpu.VMEM((MLA_H * MLA_NOPE + nr, MLA_RANK), BF16),
                        pltpu.VMEM((MLA_H, n_seq * nq, MLA_RANK), F32),
                        pltpu.VMEM((nr, MLA_RANK), F32), pltpu.VMEM((nr, MLA_ROPE), F32),
                        pltpu.VMEM((nr, hw), F32),
                        pltpu.VMEM((nr, 1), F32), pltpu.VMEM((nr, 1), F32), pltpu.VMEM((nr, MLA_RANK), F32),
                        pltpu.VMEM((2, tk, MLA_RANK), F32), pltpu.VMEM((2, n_pg, MLA_ROPE, PAGE), F32),
                        pltpu.SemaphoreType.DMA((2, 2))],
    )
    return pl.pallas_call(
        functools.partial(_mla_sample_kernel, layer, n_seq, n_pg, n_steps, t_valid),
        grid_spec=grid_spec,
        out_shape=jax.ShapeDtypeStruct((n_seq * nr, MLA_RANK), F32),
        compiler_params=_cparams(("arbitrary", "arbitrary")),
        name="mla_paged",
    )(page_table, q, q, k_new, c_new, *[w.arr for w in wts], ckv_pool, kr_pool_t)


def _mla_up_kernel(n_seq, pc_ref, h_ref, wuvs_ref, wout_ref, wgate_ref, y_ref):
    nq = SAMPLE_ROWS
    o = jnp.zeros((n_seq * nq, MLA_H * MLA_V), F32)
    for hd in range(MLA_H):
        pch = pc_ref[:, hd * nq:(hd + 1) * nq, :].reshape(n_seq * nq, MLA_RANK)
        o = o + _dot(pch.astype(BF16), wuvs_ref[hd])
    y_ref[...] = _dot(o.astype(BF16), wout_ref[...]) * jax.nn.sigmoid(_dot(h_ref[...], wgate_ref[...]))


def _mla_up_proj_gate(pc, h, wuv_sel, w_out, w_gate, *, n_seq):
    rows = n_seq * SAMPLE_ROWS
    nr = MLA_H * SAMPLE_ROWS
    wts = (wuv_sel, w_out, w_gate)
    return pl.pallas_call(
        functools.partial(_mla_up_kernel, n_seq),
        grid=(1,),
        in_specs=[pl.BlockSpec((n_seq, nr, MLA_RANK), lambda i: (0, 0, 0)),
                  pl.BlockSpec((rows, D_MODEL), lambda i: (0, 0))] + [_wspec(w) for w in wts],
        out_specs=pl.BlockSpec((rows, D_MODEL), lambda i: (0, 0)),
        out_shape=jax.ShapeDtypeStruct((rows, D_MODEL), F32),
        compiler_params=_cparams(("arbitrary",)),
        name="mla_up_proj_gate",
    )(pc.reshape(n_seq, nr, MLA_RANK), h, *[w.arr for w in wts])


def _projgate_kernel(o_ref, h_ref, wout_ref, wgate_ref, y_ref):
    y_ref[...] = _dot(o_ref[...].astype(BF16), wout_ref[...]) * jax.nn.sigmoid(_dot(h_ref[...], wgate_ref[...]))


def _proj_gate(o, h, w_out, w_gate, *, tm):
    m, kdim = o.shape
    return pl.pallas_call(
        _projgate_kernel,
        grid=(m // tm,),
        in_specs=[pl.BlockSpec((tm, kdim), lambda i: (i, 0)),
                  pl.BlockSpec((tm, D_MODEL), lambda i: (i, 0)),
                  _wspec(w_out), _wspec(w_gate)],
        out_specs=pl.BlockSpec((tm, D_MODEL), lambda i: (i, 0)),
        out_shape=jax.ShapeDtypeStruct((m, D_MODEL), F32),
        compiler_params=_cparams(("parallel",)),
        name="proj_gate",
    )(o, h, w_out.arr, w_gate.arr)


def _pad_lanes(x, width):
    return jnp.pad(x, [(0, 0)] * (x.ndim - 1) + [(0, width - x.shape[-1])])


def _mla_head_layout(nope, r1, r2):
    z = jnp.zeros(nope.shape[:-1] + (MLA_LANES - MLA_QK,), nope.dtype)
    x = jnp.concatenate([r1, r2, nope, z], axis=-1)
    return x.reshape(x.shape[:-2] + (MLA_H * MLA_LANES,))


def _gain_layout(g):
    half = MLA_ROPE // 2
    return jnp.concatenate([g[MLA_NOPE:MLA_NOPE + half], g[MLA_NOPE + half:], g[:MLA_NOPE],
                            jnp.zeros((MLA_LANES - MLA_QK,), g.dtype)]).reshape(1, MLA_LANES)


def _rope_tables(pos):
    half = MLA_ROPE // 2
    inv = ROPE_THETA ** (-jnp.arange(half, dtype=F32) / half)
    ang = pos.astype(F32)[:, None] * inv
    cos, sin = jnp.cos(ang), jnp.sin(ang)
    n = pos.shape[0]
    cos_t = jnp.concatenate([cos, cos, jnp.ones((n, MLA_NOPE), F32), jnp.zeros((n, MLA_LANES - MLA_QK), F32)], -1)
    sin_t = jnp.concatenate([sin, sin, jnp.zeros((n, MLA_LANES - MLA_ROPE), F32)], -1)
    return cos_t, sin_t


def _layer_weights(p):
    w_in = p['w_in']
    sizes = (DN_CONV_W, DN_QK_W, DN_H, DN_H, SC_W, SC_W, SC_W, MLA_RANK, MLA_RANK, MLA_ROPE, MEM_W, 4 * D_MODEL)
    offs = np.concatenate([[0], np.cumsum(sizes)])
    seg = [w_in[:, offs[i]:offs[i + 1]] for i in range(len(sizes))]
    bf = lambda x: x.astype(BF16)
    row = lambda x: x.reshape(1, -1)
    gates = [bf(seg[11][:, i * D_MODEL:(i + 1) * D_MODEL]) for i in range(4)]
    half = MLA_ROPE // 2

    w = {}
    w['ffn1'] = (row(p['ffn1_norm']), bf(p['ffn1_w_gu']), bf(p['ffn1_w_down']))
    w['ffn2'] = (row(p['ffn2_norm']), bf(p['ffn2_w_gu']), bf(p['ffn2_w_down']))
    w['mix_norm'] = row(p['mix_norm'])
    w['w_o'] = bf(p['w_o'])
    w['dn'] = (bf(seg[0]), bf(seg[1]), bf(_pad_lanes(jnp.concatenate([seg[2], seg[3]], 1), 128)),
               p['dn_conv_w'], _pad_lanes(row(p['dn_A_log']), 128), _pad_lanes(row(p['dn_dt_bias']), 128),
               row(p['dn_norm']), bf(p['dn_w_out']), gates[0])
    w['sc'] = (bf(jnp.concatenate([seg[4], seg[5], seg[6]], 1)), p['sc_conv_w'], bf(p['sc_w_out']), gates[1])

    wq = p['mla_w_q_b'].reshape(MLA_RANK, MLA_H, MLA_QK)
    q_nope, q_r1, q_r2 = wq[..., :MLA_NOPE], wq[..., MLA_NOPE:MLA_NOPE + half], wq[..., MLA_NOPE + half:]
    wq_perm = _mla_head_layout(q_nope, q_r1, q_r2)
    wq_swap = _mla_head_layout(jnp.zeros_like(q_nope), -q_r2, q_r1)
    wkr = seg[9]
    wkr_pad = _pad_lanes(wkr, MLA_LANES)
    wkr_swap = _pad_lanes(jnp.concatenate([-wkr[:, half:], wkr[:, :half]], 1), MLA_LANES)
    wkv = p['mla_w_kv_b'].reshape(MLA_RANK, MLA_H, MLA_NOPE + MLA_V)
    w_uk, w_uv = wkv[..., :MLA_NOPE], wkv[..., MLA_NOPE:]
    zr = jnp.zeros((MLA_RANK, MLA_H, half), F32)
    wuk_perm = bf(_mla_head_layout(w_uk, zr, zr))
    k_gain = _gain_layout(p['mla_k_norm'])
    w['mla_proj'] = (bf(seg[7]), row(p['mla_q_norm_a']), bf(wq_perm), bf(wq_swap), _gain_layout(p['mla_q_norm']),
                     bf(seg[8]), row(p['mla_kv_norm_a']), bf(wkr_pad), bf(wkr_swap),
                     wuk_perm, bf(w_uv.reshape(MLA_RANK, MLA_H * MLA_V)), k_gain)
    eye = jnp.eye(MLA_H, dtype=F32)
    w['mla_wuv_sel'] = bf((w_uv[None] * eye[:, None, :, None]).reshape(MLA_H, MLA_RANK, MLA_H * MLA_V))
    w['mla_sample'] = (wuk_perm, bf(w_uk.reshape(MLA_RANK, MLA_H * MLA_NOPE).T), k_gain)
    w['mla_out'] = (bf(p['mla_w_out']), gates[2])
    w['mem_kv'] = (row(p['mem_norm']), bf(p['mem_w_kv']), row(p['mem_k_norm']))
    w['mem'] = (bf(seg[10]), row(p['mem_q_norm']), bf(p['mem_w_out']), gates[3])
    return w


def _group_layer(x, w, *, n_seq, t_pad, t_valid, tm, bb_n, tt, chunk, dn_state, sc_state, mem_kv, cos, sin,
                 n_tab, mla_attend, q_dtype):
    cfg = dict(n_seq=n_seq, t_pad=t_pad, bb_n=bb_n, tt=tt)
    x1, h = _ffn(x, *w['ffn1'], tm=tm, h_gain=w['mix_norm'])
    y_dn, dn_s, dn_c = _deltanet(h, dn_state[0], dn_state[1], w['dn'], t_valid=t_valid, chunk=chunk, **cfg)
    y_sc, sc_c = _shortconv(h, sc_state, w['sc'], t_valid=t_valid, **cfg)
    q, ckv, kr, k, v = _mla_proj(h, cos, sin, w['mla_proj'], tm=tm, n_tab=n_tab, qk_dtype=q_dtype)
    if mla_attend is None:
        o = _mla_prompt_attn(q, k, v, n_seq=n_seq, seq=t_pad, tq=tm)
        y_mla = _proj_gate(o, h, *w['mla_out'], tm=tm)
    else:
        y_mla = mla_attend(q, k, ckv, h)
    y_mem = _mem_attn(h, mem_kv[0], mem_kv[1], w['mem'], **cfg)
    x3 = _ffn(x1, *w['ffn2'], tm=min(tm, 256), merge=((y_dn, y_sc, y_mla, y_mem), w['w_o']))
    return x3, dn_s, dn_c, sc_c, ckv, kr


def kernel(x_prompt, x_sample, state_dn_S, state_dn_conv, state_sc_conv, cache_mla_ckv, cache_mla_krope, cache_mem_k, cache_mem_v, page_table, mem_prompt, ffn1_norm, ffn1_w_gu, ffn1_w_down, mix_norm, w_in, dn_conv_w, dn_A_log, dn_dt_bias, dn_norm, dn_w_out, sc_conv_w, sc_w_out, mla_q_norm_a, mla_w_q_b, mla_kv_norm_a, mla_w_kv_b, mla_q_norm, mla_k_norm, mla_w_out, mem_norm, mem_w_kv, mem_q_norm, mem_k_norm, mem_w_out, w_o, ffn2_norm, ffn2_w_gu, ffn2_w_down):
    params = dict(ffn1_norm=ffn1_norm, ffn1_w_gu=ffn1_w_gu, ffn1_w_down=ffn1_w_down, mix_norm=mix_norm, w_in=w_in,
                  dn_conv_w=dn_conv_w, dn_A_log=dn_A_log, dn_dt_bias=dn_dt_bias, dn_norm=dn_norm, dn_w_out=dn_w_out,
                  sc_conv_w=sc_conv_w, sc_w_out=sc_w_out, mla_q_norm_a=mla_q_norm_a, mla_w_q_b=mla_w_q_b,
                  mla_kv_norm_a=mla_kv_norm_a, mla_w_kv_b=mla_w_kv_b, mla_q_norm=mla_q_norm, mla_k_norm=mla_k_norm,
                  mla_w_out=mla_w_out, mem_norm=mem_norm, mem_w_kv=mem_w_kv, mem_q_norm=mem_q_norm,
                  mem_k_norm=mem_k_norm, mem_w_out=mem_w_out, w_o=w_o, ffn2_norm=ffn2_norm, ffn2_w_gu=ffn2_w_gu,
                  ffn2_w_down=ffn2_w_down)
    depth = w_in.shape[0]
    bp, seq, _ = x_prompt.shape
    bs, td, _ = x_sample.shape
    tds = SAMPLE_ROWS
    n_pages = page_table.shape[1]
    past = n_pages * PAGE
    krope_t = jnp.transpose(cache_mla_krope, (0, 1, 3, 2))

    cos_p, sin_p = _rope_tables(jnp.arange(seq))
    cos_s, sin_s = _rope_tables(past + jnp.arange(tds))
    cos_s, sin_s = jnp.tile(cos_s, (bs, 1)), jnp.tile(sin_s, (bs, 1))

    xp = x_prompt.reshape(bp * seq, D_MODEL)
    xs = jnp.pad(x_sample, ((0, 0), (0, tds - td), (0, 0))).reshape(bs * tds, D_MODEL)
    zero_s = jnp.zeros((bp, DN_H, DN_DK, DN_DK), F32)
    zero_dc = jnp.zeros((bp, DN_CONV - 1, DN_CONV_W), F32)
    zero_sc = jnp.zeros((bp, SC_CONV - 1, SC_W), F32)
    mem2d = mem_prompt.reshape(bp * N_MEM, D_MODEL)

    outs = {k: [] for k in ('pS', 'pdc', 'psc', 'pckv', 'pkr', 'pmk', 'pmv', 'sS', 'sdc', 'ssc', 'sckv', 'skr')}
    tm_p = 512
    w_all = jax.vmap(_layer_weights)(params)
    for l in range(depth):
        w = jax.tree.map(lambda a: _W(a, l), w_all)
        mk, mv = _mem_kv(mem2d, w['mem_kv'], tm=tm_p)
        mk3, mv3 = mk.reshape(bp, N_MEM, MEM_W), mv.reshape(bp, N_MEM, MEM_W)
        xp, s_p, dc_p, sc_p, ckv_p, kr_p = _group_layer(
            xp, w, n_seq=bp, t_pad=seq, t_valid=seq, tm=tm_p, bb_n=1, tt=tm_p, chunk=DN_CHUNK,
            dn_state=(zero_s, zero_dc), sc_state=zero_sc, mem_kv=(mk3, mv3), cos=cos_p, sin=sin_p,
            n_tab=seq // tm_p, mla_attend=None, q_dtype=BF16)
        outs['pS'].append(s_p); outs['pdc'].append(dc_p); outs['psc'].append(sc_p)
        outs['pckv'].append(ckv_p.reshape(bp, seq, MLA_RANK)); outs['pkr'].append(kr_p.reshape(bp, seq, MLA_ROPE))
        outs['pmk'].append(mk.reshape(bp, N_MEM, MEM_H, MEM_HD)); outs['pmv'].append(mv.reshape(bp, N_MEM, MEM_H, MEM_HD))

        def attend(q, k, ckv, h, l=l, w=w):
            pc = _mla_sample_attn(page_table, q, k, ckv, w['mla_sample'], cache_mla_ckv, krope_t, l,
                                  n_seq=bs, t_valid=td, n_pg=32)
            return _mla_up_proj_gate(pc, h, w['mla_wuv_sel'], *w['mla_out'], n_seq=bs)

        xs, s_s, dc_s, sc_s, ckv_s, kr_s = _group_layer(
            xs, w, n_seq=bs, t_pad=tds, t_valid=td, tm=bs * tds, bb_n=8, tt=tds, chunk=tds,
            dn_state=(state_dn_S[l], state_dn_conv[l]), sc_state=state_sc_conv[l],
            mem_kv=(cache_mem_k[l].reshape(bs, N_MEM, MEM_W), cache_mem_v[l].reshape(bs, N_MEM, MEM_W)),
            cos=cos_s, sin=sin_s, n_tab=1, mla_attend=attend, q_dtype=F32)
        outs['sS'].append(s_s); outs['sdc'].append(dc_s); outs['ssc'].append(sc_s)
        outs['sckv'].append(ckv_s.reshape(bs, tds, MLA_RANK)[:, :td])
        outs['skr'].append(kr_s.reshape(bs, tds, MLA_ROPE)[:, :td])

    st = lambda k: jnp.stack(outs[k])
    y_prompt = xp.reshape(bp, seq, D_MODEL)
    y_sample = xs.reshape(bs, tds, D_MODEL)[:, :td]
    return (y_prompt, y_sample, st('pS'), st('pdc'), st('psc'), st('pckv'), st('pkr'), st('pmk'), st('pmv'),
            st('sS'), st('sdc'), st('ssc'), st('sckv'), st('skr'))
```

```python
import functools

import numpy as np
import jax
import jax.numpy as jnp
from jax import lax
from jax.experimental import pallas as pl
from jax.experimental.pallas import tpu as pltpu

F32 = jnp.float32
BF16 = jnp.bfloat16

D_MODEL = 1024
D_FF = 2816
EPS = 1e-6
N_MEM = 256
PAGE = 128
DN_H = 4
DN_DK = 128
DN_QK_W = 512
DN_CONV_W = 1536
DN_CONV = 4
DN_CHUNK = 64
DN_UNROLL_GROUPS = 4
DN_GROUP = 2
SC_W = 512
SC_CONV = 3
MLA_H = 8
MLA_RANK = 256
MLA_NOPE = 64
MLA_ROPE = 32
MLA_V = 64
MLA_QK = 96
MLA_LANES = 128
ROPE_THETA = 10000.0
LOG2E = 1.4426950408889634
MEM_H = 4
MEM_HD = 128
MEM_W = 512

VMEM_LIMIT = 56 * 1024 * 1024


def _cparams(sem):
    return pltpu.CompilerParams(dimension_semantics=sem, vmem_limit_bytes=VMEM_LIMIT)


class _W:
    def __init__(self, arr, layer):
        self.arr, self.layer = arr, layer


def _wspec(w, resident=False):
    shape = w.arr.shape[1:]
    index = lambda *_: (w.layer,) + (0,) * len(shape)
    if resident:
        return pl.BlockSpec((None,) + shape, index, pipeline_mode=pl.Buffered(1))
    return pl.BlockSpec((None,) + shape, index)


def _rms(x, g):
    ms = jnp.mean(x * x, axis=-1, keepdims=True)
    return x * lax.rsqrt(ms + EPS) * g


def _silu(x):
    return x * jax.nn.sigmoid(x)


def _rnd(x):
    return x.astype(BF16).astype(F32)


def _dot(a, b):
    return jnp.dot(a, b, preferred_element_type=F32)


def _dot_nt(a, b):
    return lax.dot_general(a, b, (((1,), (1,)), ((), ())), preferred_element_type=F32)


def _mm(a, b, small):
    if small:
        return _dot(_rnd(a), _rnd(b))
    return _dot(a.astype(BF16), b.astype(BF16))


def _mm_nt(a, b, small):
    if small:
        return _dot_nt(_rnd(a), _rnd(b))
    return _dot_nt(a.astype(BF16), b.astype(BF16))


def _mm_tn(a, b, small):
    dn = (((0,), (0,)), ((), ()))
    if small:
        return lax.dot_general(_rnd(a), _rnd(b), dn, preferred_element_type=F32)
    return lax.dot_general(a.astype(BF16), b.astype(BF16), dn, preferred_element_type=F32)


def _run_interleaved(gens):
    gens = list(gens)
    while gens:
        for g in list(gens):
            try:
                next(g)
            except StopIteration:
                gens.remove(g)


def _split2(x):
    hi = x.astype(BF16)
    lo = (x - hi.astype(F32)).astype(BF16)
    return hi, lo


def _mm_hi(a, b, small):
    if small:
        return jnp.dot(a, b, preferred_element_type=F32, precision=lax.Precision.HIGHEST)
    ah, al = _split2(a)
    bh, bl = _split2(b)
    return _dot(ah, bh) + (_dot(ah, bl) + _dot(al, bh))


def _mm_exact_left(lmat, b, small):
    if small:
        return jnp.dot(lmat, b, preferred_element_type=F32, precision=lax.Precision.HIGHEST)
    lb = lmat.astype(BF16)
    b1 = b.astype(BF16)
    r1 = b - b1.astype(F32)
    b2 = r1.astype(BF16)
    b3 = (r1 - b2.astype(F32)).astype(BF16)
    return _dot(lb, b1) + (_dot(lb, b2) + _dot(lb, b3))


FFN_SPLIT = 2


def _ffn_kernel(merge, emit_h, *refs):
    it = iter(refs)
    x_ref = next(it)
    if merge:
        y_refs = [next(it) for _ in range(4)]
        wo_ref = next(it)
    g_ref, wgu_ref, wd_ref = next(it), next(it), next(it)
    if emit_h:
        g2_ref = next(it)
    o_ref = next(it)
    if emit_h:
        h_ref = next(it)

    x = x_ref[...]
    if merge:
        m = ((y_refs[0][...] + y_refs[1][...]) + y_refs[2][...]) + y_refs[3][...]
        x = x + _dot(m.astype(BF16), wo_ref[...])
    h = _rms(x, g_ref[...]).astype(BF16)
    tf = D_FF // FFN_SPLIT
    def gate_up(j):
        return (_dot(h, wgu_ref[:, j * tf:(j + 1) * tf]),
                _dot(h, wgu_ref[:, D_FF + j * tf:D_FF + (j + 1) * tf]))

    acc = None
    nxt = gate_up(0)
    for j in range(FFN_SPLIT):
        gate, up = nxt
        if j + 1 < FFN_SPLIT:
            nxt = gate_up(j + 1)
        a = (_silu(gate) * up).astype(BF16)
        d = _dot(a, wd_ref[j * tf:(j + 1) * tf, :])
        acc = d if acc is None else acc + d
    out = x + 0.5 * acc
    o_ref[...] = out
    if emit_h:
        h_ref[...] = _rms(out, g2_ref[...]).astype(BF16)


def _ffn(x, norm_g, w_gu, w_down, *, tm, merge=None, h_gain=None):
    m = x.shape[0]
    row = pl.BlockSpec((tm, D_MODEL), lambda i: (i, 0))
    in_specs, args = [row], [x]
    if merge is not None:
        ys, w_o = merge
        in_specs += [row] * 4 + [_wspec(w_o, resident=True)]
        args += list(ys) + [w_o.arr]
    in_specs += [_wspec(w, resident=True) for w in (norm_g, w_gu, w_down)]
    args += [norm_g.arr, w_gu.arr, w_down.arr]
    out_shape = [jax.ShapeDtypeStruct((m, D_MODEL), F32)]
    out_specs = [row]
    if h_gain is not None:
        in_specs.append(_wspec(h_gain, resident=True))
        args.append(h_gain.arr)
        out_shape.append(jax.ShapeDtypeStruct((m, D_MODEL), BF16))
        out_specs.append(row)
    res = pl.pallas_call(
        functools.partial(_ffn_kernel, merge is not None, h_gain is not None),
        grid=(m // tm,),
        in_specs=in_specs, out_specs=out_specs, out_shape=out_shape,
        compiler_params=_cparams(("parallel",)),
        name="ffn_merge" if merge is not None else "ffn",
    )(*args)
    return res if h_gain is not None else res[0]


def _dn_kernel(bb_n, tt, chunk, tv_last, n_t, small,
               h_ref, s0_ref, cprev_ref, wqkv_ref, wz_ref, wab_ref, convw_ref, alog_ref, dtb_ref,
               normg_ref, wout_ref, wgate_ref,
               y_ref, snew_ref, cnew_ref,
               xbuf, cs, gb, zb, ob, s_all, u_s, w_s, qg_s, kdec_s, aqk_s, gl_s, gate_s):
    t = pl.program_id(1)
    rows = bb_n * tt

    @pl.when(t == 0)
    def _():
        xbuf[:, 5:8, :] = cprev_ref[...]
        snew_ref[...] = s0_ref[...]

    h = h_ref[...]
    qkv = _dot(h, wqkv_ref[...])
    ab = _dot(h, wab_ref[...])
    zs = _dot(h, wz_ref[...])
    gate = _dot(h, wgate_ref[...])
    xbuf[:, 8:8 + tt, :] = qkv.reshape(bb_n, tt, DN_CONV_W)
    c = 0.0
    for j in range(DN_CONV):
        c = c + xbuf[:, 5 + j:5 + j + tt, :] * convw_ref[j:j + 1, :].reshape(1, 1, DN_CONV_W)
    cnew_ref[...] = xbuf[:, 5 + tv_last:8 + tv_last, :]
    if n_t > 1:
        xbuf[:, 0:8, :] = xbuf[:, tt:tt + 8, :]

    g = -jnp.exp(alog_ref[...]) * jax.nn.softplus(ab + dtb_ref[...])
    lane = lax.broadcasted_iota(jnp.int32, (rows, 128), 1)
    gbv = jnp.where(lane < DN_H, g, jax.nn.sigmoid(ab)).reshape(bb_n, tt, 128)
    if tv_last < tt:
        trow = lax.broadcasted_iota(jnp.int32, (bb_n, tt, 128), 1)
        gbv = jnp.where(trow < tv_last, gbv, 0.0)
    gb[...] = gbv
    c = _silu(c)
    for grp in range(8):
        sl = slice(grp * 128, (grp + 1) * 128)
        xg = c[:, :, sl]
        xn = xg * lax.rsqrt(jnp.sum(xg * xg, axis=-1, keepdims=True) + EPS)
        if grp < DN_H:
            xn = xn * (DN_DK ** -0.5)
        cs[:, :, sl] = xn
    cs[:, :, 2 * DN_QK_W:] = c[:, :, 2 * DN_QK_W:]
    zb[...] = _silu(zs).reshape(bb_n, tt, DN_QK_W)
    gate_s[...] = jax.nn.sigmoid(gate)

    n4 = DN_H * chunk
    ri = lax.broadcasted_iota(jnp.int32, (n4, n4), 0)
    ci = lax.broadcasted_iota(jnp.int32, (n4, n4), 1)
    same = (ri // chunk) == (ci // chunk)
    incl = same & (ci <= ri)
    strict = same & (ci < ri)
    lmat = incl.astype(F32)
    umat = strict.astype(F32)
    vmask = (lax.broadcasted_iota(jnp.int32, (n4, DN_H * 128), 0) // chunk
             == lax.broadcasted_iota(jnp.int32, (n4, DN_H * 128), 1) // 128)
    n_pow = int(np.log2(chunk))
    normg = normg_ref[...]
    n_ch = tt // chunk
    total = bb_n * n_ch
    solve_mm = _mm_hi if small else _mm

    for hd in range(DN_H):
        s_all[:, :, hd * 128:(hd + 1) * 128] = snew_ref[:, hd]

    def stack_rows(ref, b, r0, off):
        return jnp.concatenate([ref[b, pl.ds(r0, chunk), off + hd * 128:off + (hd + 1) * 128]
                                for hd in range(DN_H)], axis=0)

    def level1(k, slot):
        b = k // n_ch
        r0 = (k % n_ch) * chunk
        if not isinstance(k, int):
            r0 = pl.multiple_of(r0, chunk)
        q = stack_rows(cs, b, r0, 0)
        kk = stack_rows(cs, b, r0, DN_QK_W)
        v = stack_rows(cs, b, r0, 2 * DN_QK_W)
        gbc = gb[b, pl.ds(r0, chunk), :]
        g_st = jnp.concatenate([jnp.broadcast_to(gbc[:, hd:hd + 1], (chunk, 128)) for hd in range(DN_H)], axis=0)
        beta_st = jnp.concatenate([jnp.broadcast_to(gbc[:, DN_H + hd:DN_H + hd + 1], (chunk, 128))
                                   for hd in range(DN_H)], axis=0)
        g_sq = jnp.concatenate([g_st] * (n4 // 128), axis=1) if n4 >= 128 else g_st[:, :n4]
        gc = _mm_exact_left(lmat, g_st, small)
        yield
        if small:
            dmat = _mm_exact_left(lmat, g_sq * umat, small)
        else:
            gct = gc.T
            dmat = (jnp.concatenate([gc] * (n4 // 128), axis=1)
                    - jnp.concatenate([gct] * (n4 // 128), axis=0))
        gam = jnp.where(incl, jnp.exp(dmat), 0.0)
        eg = jnp.exp(gc)
        kb = kk * beta_st
        kq = _mm_nt(jnp.concatenate([kb, q], axis=0), kk, small)
        yield
        a_mat = jnp.where(strict, kq[:n4] * gam, 0.0)
        x = jnp.concatenate([v * beta_st, kb * eg], axis=1)
        p = -a_mat
        for i in range(n_pow):
            x = x + (_mm_hi if i < 2 else solve_mm)(p, x, small)
            if i < n_pow - 1:
                p = (_mm_hi if i < 1 else solve_mm)(p, p, small)
            yield
        gc_last = [gc[(hd + 1) * chunk - 1:(hd + 1) * chunk, :] for hd in range(DN_H)]
        gl_st = jnp.concatenate([jnp.broadcast_to(r, (chunk, 128)) for r in gc_last], axis=0)
        u_s[slot] = x[:, :128]
        w_s[slot] = x[:, 128:]
        qg_s[slot] = q * eg
        kdec_s[slot] = kk * jnp.exp(gl_st - gc)
        aqk_s[slot] = jnp.where(incl, kq[n4:] * gam, 0.0)
        gl_s[slot] = jnp.exp(jnp.concatenate(gc_last, axis=1))

    def level2(k, slot):
        b = k // n_ch
        r0 = (k % n_ch) * chunk
        if not isinstance(k, int):
            r0 = pl.multiple_of(r0, chunk)
        s_old = s_all[b]
        w = w_s[slot]
        qg = qg_s[slot]
        ws, qs = [], []
        for hd in range(DN_H):
            rs = slice(hd * chunk, (hd + 1) * chunk)
            r = _mm(jnp.concatenate([w[rs], qg[rs]], axis=0), s_old[:, hd * 128:(hd + 1) * 128], small)
            ws.append(r[:chunk])
            qs.append(r[chunk:])
        yield
        v_new = u_s[slot] - jnp.concatenate(ws, axis=0)
        o = jnp.concatenate(qs, axis=0) + _mm(aqk_s[slot], v_new, small)
        vbd = jnp.where(vmask, jnp.concatenate([v_new] * DN_H, axis=1), 0.0)
        s_all[b] = s_old * gl_s[slot] + _mm_tn(kdec_s[slot], vbd, small)
        yield
        on = _rms(o, normg)
        for hd in range(DN_H):
            sl = slice(hd * 128, (hd + 1) * 128)
            ob[b, pl.ds(r0, chunk), sl] = on[hd * chunk:(hd + 1) * chunk] * zb[b, pl.ds(r0, chunk), sl]

    assert total % DN_GROUP == 0

    def recurrence(m, base):
        for i in range(DN_GROUP):
            yield from level2(m * DN_GROUP + i, base + i)

    _run_interleaved([level1(i, i) for i in range(DN_GROUP)])

    n_groups = total // DN_GROUP

    def group_body(m, _):
        base = (m % 2) * DN_GROUP
        chains = [recurrence(m, base)]
        if not isinstance(m, int):
            chains += [level1(jnp.minimum((m + 1) * DN_GROUP + i, total - 1), DN_GROUP - base + i)
                       for i in range(DN_GROUP)]
        elif m + 1 < n_groups:
            chains += [level1((m + 1) * DN_GROUP + i, DN_GROUP - base + i) for i in range(DN_GROUP)]
        _run_interleaved(chains)
        return 0

    if n_groups <= DN_UNROLL_GROUPS:
        for m in range(n_groups):
            group_body(m, 0)
    else:
        lax.fori_loop(0, n_groups, group_body, 0)

    for hd in range(DN_H):
        snew_ref[:, hd] = s_all[:, :, hd * 128:(hd + 1) * 128]

    y = _dot(ob[...].reshape(rows, DN_QK_W).astype(BF16), wout_ref[...])
    y_ref[...] = y * gate_s[...]


def _deltanet(h, s0, cprev, wts, *, n_seq, t_pad, t_valid, bb_n, tt, chunk):
    n_t = t_pad // tt
    assert n_t == 1 or t_valid == t_pad
    tv_last = t_valid - (n_t - 1) * tt
    rows = bb_n * tt
    small = chunk < 16
    n4 = DN_H * chunk
    in_specs = [
        pl.BlockSpec((rows, D_MODEL), lambda b, t: (b * n_t + t, 0)),
        pl.BlockSpec((bb_n, DN_H, DN_DK, DN_DK), lambda b, t: (b, 0, 0, 0)),
        pl.BlockSpec((bb_n, DN_CONV - 1, DN_CONV_W), lambda b, t: (b, 0, 0)),
    ] + [_wspec(w) for w in wts]
    out_specs = [
        pl.BlockSpec((rows, D_MODEL), lambda b, t: (b * n_t + t, 0)),
        pl.BlockSpec((bb_n, DN_H, DN_DK, DN_DK), lambda b, t: (b, 0, 0, 0)),
        pl.BlockSpec((bb_n, DN_CONV - 1, DN_CONV_W), lambda b, t: (b, 0, 0)),
    ]
    out_shape = [
        jax.ShapeDtypeStruct((n_seq * t_pad, D_MODEL), F32),
        jax.ShapeDtypeStruct((n_seq, DN_H, DN_DK, DN_DK), F32),
        jax.ShapeDtypeStruct((n_seq, DN_CONV - 1, DN_CONV_W), F32),
    ]
    return pl.pallas_call(
        functools.partial(_dn_kernel, bb_n, tt, chunk, tv_last, n_t, small),
        grid=(n_seq // bb_n, n_t),
        in_specs=in_specs, out_specs=out_specs, out_shape=out_shape,
        scratch_shapes=[pltpu.VMEM((bb_n, tt + 8, DN_CONV_W), F32), pltpu.VMEM((bb_n, tt, DN_CONV_W), F32),
                        pltpu.VMEM((bb_n, tt, 128), F32), pltpu.VMEM((bb_n, tt, DN_QK_W), F32),
                        pltpu.VMEM((bb_n, tt, DN_QK_W), F32),
                        pltpu.VMEM((bb_n, DN_DK, DN_H * 128), F32)]
                       + [pltpu.VMEM((2 * DN_GROUP, n4, 128), F32)] * 4
                       + [pltpu.VMEM((2 * DN_GROUP, n4, n4), F32), pltpu.VMEM((2 * DN_GROUP, 1, DN_H * 128), F32),
                          pltpu.VMEM((rows, D_MODEL), F32)],
        compiler_params=_cparams(("parallel", "arbitrary")),
        name="deltanet",
    )(h, s0, cprev, *[w.arr for w in wts])


def _sc_kernel(bb_n, tt, tv_last, n_t,
               h_ref, prev_ref, win_ref, convw_ref, wout_ref, wgate_ref,
               y_ref, new_ref, ubuf):
    t = pl.program_id(1)
    rows = bb_n * tt

    @pl.when(t == 0)
    def _():
        ubuf[:, 6:8, :] = prev_ref[...]

    h = h_ref[...]
    p = _dot(h, win_ref[...])
    gate = jax.nn.sigmoid(_dot(h, wgate_ref[...]))
    bgate = p[:, :SC_W]
    u = p[:, SC_W:2 * SC_W] * p[:, 2 * SC_W:]
    ubuf[:, 8:8 + tt, :] = u.reshape(bb_n, tt, SC_W)
    y = 0.0
    for j in range(SC_CONV):
        y = y + ubuf[:, 6 + j:6 + j + tt, :] * convw_ref[j:j + 1, :].reshape(1, 1, SC_W)

    new_ref[...] = ubuf[:, 6 + tv_last:8 + tv_last, :]
    if n_t > 1:
        ubuf[:, 0:8, :] = ubuf[:, tt:tt + 8, :]

    z = (bgate * y.reshape(rows, SC_W)).astype(BF16)
    y_ref[...] = _dot(z, wout_ref[...]) * gate


def _shortconv(h, prev, wts, *, n_seq, t_pad, t_valid, bb_n, tt):
    n_t = t_pad // tt
    assert n_t == 1 or t_valid == t_pad
    tv_last = t_valid - (n_t - 1) * tt
    rows = bb_n * tt
    return pl.pallas_call(
        functools.partial(_sc_kernel, bb_n, tt, tv_last, n_t),
        grid=(n_seq // bb_n, n_t),
        in_specs=[pl.BlockSpec((rows, D_MODEL), lambda b, t: (b * n_t + t, 0)),
                  pl.BlockSpec((bb_n, SC_CONV - 1, SC_W), lambda b, t: (b, 0, 0))]
                 + [_wspec(w) for w in wts],
        out_specs=[pl.BlockSpec((rows, D_MODEL), lambda b, t: (b * n_t + t, 0)),
                   pl.BlockSpec((bb_n, SC_CONV - 1, SC_W), lambda b, t: (b, 0, 0))],
        out_shape=[jax.ShapeDtypeStruct((n_seq * t_pad, D_MODEL), F32),
                   jax.ShapeDtypeStruct((n_seq, SC_CONV - 1, SC_W), F32)],
        scratch_shapes=[pltpu.VMEM((bb_n, tt + 8, SC_W), F32)],
        compiler_params=_cparams(("parallel", "arbitrary")),
        name="shortconv",
    )(h, prev, *[w.arr for w in wts])


def _memkv_kernel(m_ref, g_ref, wkv_ref, kg_ref, k_ref, v_ref):
    n = _rms(m_ref[...], g_ref[...]).astype(BF16)
    kv = _dot(n, wkv_ref[...])
    kg = kg_ref[...]
    for hd in range(MEM_H):
        sl = slice(hd * MEM_HD, (hd + 1) * MEM_HD)
        k_ref[:, sl] = _rms(kv[:, sl], kg)
    v_ref[...] = kv[:, MEM_W:]


def _mem_kv(mem2d, wts, *, tm):
    m = mem2d.shape[0]
    return pl.pallas_call(
        _memkv_kernel,
        grid=(m // tm,),
        in_specs=[pl.BlockSpec((tm, D_MODEL), lambda i: (i, 0))] + [_wspec(w) for w in wts],
        out_specs=[pl.BlockSpec((tm, MEM_W), lambda i: (i, 0))] * 2,
        out_shape=[jax.ShapeDtypeStruct((m, MEM_W), F32)] * 2,
        compiler_params=_cparams(("parallel",)),
        name="mem_kv",
    )(mem2d, *[w.arr for w in wts])


def _memattn_kernel(bb_n, tt, small,
                    h_ref, mk_ref, mv_ref, wq_ref, qg_ref, wout_ref, wgate_ref,
                    y_ref, qs, ob):
    rows = bb_n * tt
    h = h_ref[...]
    q = _dot(h, wq_ref[...])
    gate = jax.nn.sigmoid(_dot(h, wgate_ref[...]))
    qg = qg_ref[...]
    for hd in range(MEM_H):
        sl = slice(hd * MEM_HD, (hd + 1) * MEM_HD)
        qs[:, :, sl] = _rms(q[:, sl], qg).reshape(bb_n, tt, MEM_HD)

    def seq_body(b, _):
        def scores(hd):
            sl = slice(hd * MEM_HD, (hd + 1) * MEM_HD)
            return _mm_nt(qs[b, :, sl], mk_ref[b, :, sl], small)

        s_next = scores(0)
        for hd in range(MEM_H):
            sl = slice(hd * MEM_HD, (hd + 1) * MEM_HD)
            s = s_next * (MEM_HD ** -0.5)
            if hd + 1 < MEM_H:
                s_next = scores(hd + 1)
            s = s - jnp.max(s, axis=-1, keepdims=True)
            e = jnp.exp(s)
            p = e / jnp.sum(e, axis=-1, keepdims=True)
            ob[b, :, sl] = _mm(p, mv_ref[b, :, sl], small)
        return 0

    if bb_n == 1:
        seq_body(0, 0)
    else:
        lax.fori_loop(0, bb_n, seq_body, 0)
    y = _dot(ob[...].reshape(rows, MEM_W).astype(BF16), wout_ref[...])
    y_ref[...] = y * gate


def _mem_attn(h, mk, mv, wts, *, n_seq, t_pad, bb_n, tt):
    n_t = t_pad // tt
    rows = bb_n * tt
    return pl.pallas_call(
        functools.partial(_memattn_kernel, bb_n, tt, tt < 16),
        grid=(n_seq // bb_n, n_t),
        in_specs=[pl.BlockSpec((rows, D_MODEL), lambda b, t: (b * n_t + t, 0)),
                  pl.BlockSpec((bb_n, N_MEM, MEM_W), lambda b, t: (b, 0, 0)),
                  pl.BlockSpec((bb_n, N_MEM, MEM_W), lambda b, t: (b, 0, 0))]
                 + [_wspec(w) for w in wts],
        out_specs=pl.BlockSpec((rows, D_MODEL), lambda b, t: (b * n_t + t, 0)),
        out_shape=jax.ShapeDtypeStruct((n_seq * t_pad, D_MODEL), F32),
        scratch_shapes=[pltpu.VMEM((bb_n, tt, MEM_W), F32), pltpu.VMEM((bb_n, tt, MEM_W), F32)],
        compiler_params=_cparams(("parallel", "arbitrary")),
        name="mem_attn",
    )(h, mk, mv, *[w.arr for w in wts])


def _mlaproj_kernel(h_ref, cos_ref, sin_ref, wq_ref, qna_ref, wqp_ref, wqs_ref, qg_ref,
                    wkv_ref, kvna_ref, wkr_ref, wkrs_ref, wuk_ref, wuv_ref, kg_ref,
                    q_ref, ckv_ref, kr_ref, k_ref, v_ref):
    h = h_ref[...]
    cos = cos_ref[...]
    sin = sin_ref[...]
    cq = _dot(h, wq_ref[...])
    ckv_raw = _dot(h, wkv_ref[...])
    kr_a = _dot(h, wkr_ref[...])
    kr_b = _dot(h, wkrs_ref[...])
    cqn = _rms(cq, qna_ref[...]).astype(BF16)
    q_raw = _dot(cqn, wqp_ref[...])
    q_swp = _dot(cqn, wqs_ref[...])
    ckv = _rms(ckv_raw, kvna_ref[...])
    cb = ckv.astype(BF16)
    k_raw = _dot(cb, wuk_ref[...])
    v_ref[...] = _dot(cb, wuv_ref[...]).astype(BF16)
    ckv_ref[...] = ckv
    krp = kr_a * cos + kr_b * sin
    kr_ref[...] = krp[:, :MLA_ROPE]

    qg = qg_ref[...]
    inv_n = 1.0 / MLA_QK
    for hd in range(MLA_H):
        sl = slice(hd * MLA_LANES, (hd + 1) * MLA_LANES)
        qh = q_raw[:, sl] * cos + q_swp[:, sl] * sin
        ms = jnp.sum(qh * qh, axis=-1, keepdims=True) * inv_n
        q_ref[:, sl] = (qh * lax.rsqrt(ms + EPS) * qg).astype(q_ref.dtype)

    kg = kg_ref[...]
    for hd in range(MLA_H):
        sl = slice(hd * MLA_LANES, (hd + 1) * MLA_LANES)
        kh = k_raw[:, sl] + krp
        ms = jnp.sum(kh * kh, axis=-1, keepdims=True) * inv_n
        k_ref[:, sl] = (kh * lax.rsqrt(ms + EPS) * kg).astype(k_ref.dtype)


def _mla_proj(h, cos, sin, wts, *, tm, n_tab, qk_dtype):
    m = h.shape[0]
    hw = MLA_H * MLA_LANES
    vw = MLA_H * MLA_V
    row = lambda width: pl.BlockSpec((tm, width), lambda i: (i, 0))
    in_specs = [row(D_MODEL),
                pl.BlockSpec((tm, MLA_LANES), lambda i: (i % n_tab, 0)),
                pl.BlockSpec((tm, MLA_LANES), lambda i: (i % n_tab, 0))] + [_wspec(w) for w in wts]
    return pl.pallas_call(
        _mlaproj_kernel,
        grid=(m // tm,),
        in_specs=in_specs,
        out_specs=[row(hw), row(MLA_RANK), row(MLA_ROPE), row(hw), row(vw)],
        out_shape=[jax.ShapeDtypeStruct((m, hw), qk_dtype),
                   jax.ShapeDtypeStruct((m, MLA_RANK), F32),
                   jax.ShapeDtypeStruct((m, MLA_ROPE), F32),
                   jax.ShapeDtypeStruct((m, hw), qk_dtype),
                   jax.ShapeDtypeStruct((m, vw), BF16)],
        compiler_params=_cparams(("parallel",)),
        name="mla_proj",
    )(h, cos, sin, *[w.arr for w in wts])


def _flash_kernel(tq, q_ref, k_ref, v_ref, o_ref, m_scr, l_scr, acc_scr):
    qi = pl.program_id(1)
    ki = pl.program_id(2)
    c2 = (MLA_QK ** -0.5) * LOG2E
    n_rep = tq // 128

    @pl.when(ki == 0)
    def _():
        m_scr[...] = jnp.full(m_scr.shape, -jnp.inf, F32)
        l_scr[...] = jnp.zeros_like(l_scr)
        acc_scr[...] = jnp.zeros_like(acc_scr)

    def compute(diag):
        if diag:
            row = lax.broadcasted_iota(jnp.int32, (tq, tq), 0)
            col = lax.broadcasted_iota(jnp.int32, (tq, tq), 1)
            keep = col <= row
        ones = jnp.ones((tq, 128), BF16)

        def qk(hd):
            sl = slice(hd * MLA_LANES, (hd + 1) * MLA_LANES)
            return _dot_nt(q_ref[:, sl], k_ref[:, sl])

        s_next = qk(0)
        for hd in range(MLA_H):
            s = s_next * c2
            if hd + 1 < MLA_H:
                s_next = qk(hd + 1)
            if diag:
                s = jnp.where(keep, s, -jnp.inf)
            m_old = m_scr[hd]
            m_new = jnp.maximum(m_old, jnp.max(s, axis=-1, keepdims=True))
            alpha = jnp.exp2(m_old - m_new)
            p = jnp.exp2(s - jnp.concatenate([m_new] * n_rep, axis=1)).astype(BF16)
            pair = hd // 2
            vext = jnp.concatenate([v_ref[:, pair * 128:(pair + 1) * 128], ones], axis=1)
            r = _dot(p, vext)
            acc_scr[hd] = alpha * acc_scr[hd] + r[:, :128]
            l_scr[hd] = alpha * l_scr[hd] + r[:, 128:]
            m_scr[hd] = m_new

    @pl.when(ki < qi)
    def _():
        compute(False)

    @pl.when(ki == qi)
    def _():
        compute(True)
        lane = lax.broadcasted_iota(jnp.int32, (tq, 128), 1)
        for pair in range(MLA_H // 2):
            even = acc_scr[2 * pair] / l_scr[2 * pair]
            odd = acc_scr[2 * pair + 1] / l_scr[2 * pair + 1]
            o_ref[:, pair * 128:(pair + 1) * 128] = jnp.where(lane < MLA_V, even, odd).astype(o_ref.dtype)


def _mla_prompt_attn(q, k, v, *, n_seq, seq, tq):
    nq = seq // tq
    hw = MLA_H * MLA_LANES
    vw = MLA_H * MLA_V
    return pl.pallas_call(
        functools.partial(_flash_kernel, tq),
        grid=(n_seq, nq, nq),
        in_specs=[pl.BlockSpec((tq, hw), lambda b, i, j: (b * nq + i, 0)),
                  pl.BlockSpec((tq, hw), lambda b, i, j: (b * nq + jnp.minimum(i, j), 0)),
                  pl.BlockSpec((tq, vw), lambda b, i, j: (b * nq + jnp.minimum(i, j), 0))],
        out_specs=pl.BlockSpec((tq, vw), lambda b, i, j: (b * nq + i, 0)),
        out_shape=jax.ShapeDtypeStruct((n_seq * seq, vw), BF16),
        scratch_shapes=[pltpu.VMEM((MLA_H, tq, 128), F32)] * 3,
        compiler_params=_cparams(("parallel", "parallel", "arbitrary")),
        name="mla_flash",
    )(q, k, v)


SAMPLE_ROWS = 8
SUB_KEYS = 1024


def _mla_sample_kernel(layer, n_seq, n_pg, n_steps, t_valid,
                       pt_ref, qall_ref, q_ref, knew_ref, cnew_ref, wukp_ref, wukt_ref, kg_ref, ckv_hbm, kr_hbm,
                       o_ref,
                       lhs, qabs_all, qabs, qrope, qblk, m_scr, l_scr, acc_scr, cbuf, rbuf, sem):
    b = pl.program_id(0)
    st = pl.program_id(1)
    c2 = (MLA_QK ** -0.5) * LOG2E
    inv_n = 1.0 / MLA_QK
    tk = n_pg * PAGE
    nq = SAMPLE_ROWS
    n_up = MLA_H * MLA_NOPE
    g = b * n_steps + st
    slot = g % 2

    def page_copies(bb, ss, sl):
        cps = []
        for i in range(n_pg):
            page = pt_ref[bb, ss * n_pg + i]
            cps.append((pltpu.make_async_copy(ckv_hbm.at[layer, page], cbuf.at[sl, pl.ds(i * PAGE, PAGE), :],
                                              sem.at[sl, 0]), i % 2))
            cps.append((pltpu.make_async_copy(kr_hbm.at[layer, page], rbuf.at[sl, i], sem.at[sl, 1]), (i + 1) % 2))
        return cps

    n_total = n_seq * n_steps
    is_last = g == n_total - 1

    @pl.when(g == 0)
    def _():
        for cp, prio in page_copies(0, 0, 0):
            cp.start(priority=prio)
        kg = kg_ref[...]
        for hd in range(MLA_H):
            sl = slice(hd * MLA_LANES, (hd + 1) * MLA_LANES)
            qh, ql = _split2(qall_ref[:, sl] * kg)
            w = wukp_ref[:, sl]
            qabs_all[hd] = _dot_nt(qh, w) + _dot_nt(ql, w)

    @pl.when(st == 0)
    def _():
        m_scr[...] = jnp.full(m_scr.shape, -jnp.inf, F32)
        l_scr[...] = jnp.zeros_like(l_scr)
        acc_scr[...] = jnp.zeros_like(acc_scr)
        kg = kg_ref[...]
        q = q_ref[...]
        lane_head = lax.broadcasted_iota(jnp.int32, q.shape, 1) // MLA_LANES
        r0 = pl.multiple_of(b * nq, nq)
        for hd in range(MLA_H):
            sl = slice(hd * MLA_LANES, (hd + 1) * MLA_LANES)
            qabs[hd * nq:(hd + 1) * nq, :] = qabs_all[hd, pl.ds(r0, nq), :]
            qrope[hd * nq:(hd + 1) * nq, :] = q[:, hd * MLA_LANES:hd * MLA_LANES + MLA_ROPE] * kg[:, :MLA_ROPE]
            qblk[hd * nq:(hd + 1) * nq, :] = jnp.where(lane_head == hd, q, 0.0)
        lhs[:n_up, :] = wukt_ref[...]
        lhs[n_up:, :] = qabs[...].astype(BF16)

    for cp, _ in page_copies(b, st, slot):
        cp.wait()

    seq_end = st == n_steps - 1
    nb = jnp.where(is_last, b, jnp.where(seq_end, b + 1, b))
    ns = jnp.where(is_last, st, jnp.where(seq_end, 0, st + 1))
    for cp, prio in page_copies(nb, ns, 1 - slot):
        cp.start(priority=prio)

    lhs_v = lhs[...]
    qr = qrope[...].astype(BF16)
    pg_sub = SUB_KEYS // PAGE

    def score_block(j):
        cb = cbuf[slot, j * SUB_KEYS:(j + 1) * SUB_KEYS, :].astype(BF16)
        krt = jnp.concatenate([rbuf[slot, i] for i in range(j * pg_sub, (j + 1) * pg_sub)],
                              axis=1)
        big = _dot_nt(lhs_v, cb)
        knt = big[:n_up]
        ssq = jnp.sum((knt * knt).reshape(MLA_H, MLA_NOPE, SUB_KEYS), axis=1)
        ssq_r = jnp.sum(krt * krt, axis=0, keepdims=True)
        rs = lax.rsqrt((ssq + ssq_r) * inv_n + EPS) * c2
        s = big[n_up:] + _dot(qr, krt.astype(BF16))
        s = jnp.concatenate([s[hd * nq:(hd + 1) * nq, :] * rs[hd:hd + 1, :] for hd in range(MLA_H)], axis=0)
        return s, cb

    m_run = m_scr[...]
    l_new = l_scr[...]
    acc = acc_scr[...]
    n_sub = tk // SUB_KEYS
    blk = score_block(0)
    for j in range(n_sub):
        s, cb = blk
        if j + 1 < n_sub:
            blk = score_block(j + 1)
        m_new = jnp.maximum(m_run, jnp.max(s, axis=-1, keepdims=True))
        alpha = jnp.exp2(m_run - m_new)
        p = jnp.exp2(s - m_new)
        l_new = alpha * l_new + jnp.sum(p, axis=-1, keepdims=True)
        acc = alpha * acc + _dot(p.astype(BF16), cb)
        m_run = m_new
    l_scr[...] = l_new
    acc_scr[...] = acc
    m_scr[...] = m_new

    @pl.when(st == n_steps - 1)
    def _():
        cn = cnew_ref[...]
        sn = _dot_nt(_rnd(qblk[...]), _rnd(knew_ref[...])) * c2
        row = lax.broadcasted_iota(jnp.int32, sn.shape, 0) % nq
        col = lax.broadcasted_iota(jnp.int32, sn.shape, 1)
        sn = jnp.where((col <= row) & (col < t_valid), sn, -jnp.inf)
        m_o = m_scr[...]
        m_n = jnp.maximum(m_o, jnp.max(sn, axis=-1, keepdims=True))
        al = jnp.exp2(m_o - m_n)
        pn = jnp.exp2(sn - m_n)
        l_f = al * l_scr[...] + jnp.sum(pn, axis=-1, keepdims=True)
        o_ref[...] = (al * acc_scr[...] + _dot(_rnd(pn), _rnd(cn))) / l_f

    @pl.when(is_last)
    def _():
        for cp, _ in page_copies(b, st, 1 - slot):
            cp.wait()


def _mla_sample_attn(page_table, q, k_new, c_new, wts, ckv_pool, kr_pool_t, layer, *, n_seq, t_valid, n_pg):
    n_pages = page_table.shape[1]
    n_steps = n_pages // n_pg
    hw = MLA_H * MLA_LANES
    tk = n_pg * PAGE
    nq = SAMPLE_ROWS
    nr = MLA_H * nq

    const = lambda *shape: pl.BlockSpec(shape, lambda b, s, pt: (0,) * len(shape))
    hbm = pl.BlockSpec(memory_space=pl.ANY)
    in_specs = [const(n_seq * nq, hw),
                pl.BlockSpec((nq, hw), lambda b, s, pt: (b, 0)),
                pl.BlockSpec((nq, hw), lambda b, s, pt: (b, 0)),
                pl.BlockSpec((nq, MLA_RANK), lambda b, s, pt: (b, 0))]
    in_specs += [_wspec(w) for w in wts] + [hbm, hbm]
    grid_spec = pltpu.PrefetchScalarGridSpec(
        num_scalar_prefetch=1,
        grid=(n_seq, n_steps),
        in_specs=in_specs,
        out_specs=pl.BlockSpec((nr, MLA_RANK), lambda b, s, pt: (b, 0)),
        scratch_shapes=[pltpu.VMEM((MLA_H * MLA_NOPE + nr, MLA_RANK), BF16),
                        pltpu.VMEM((MLA_H, n_seq * nq, MLA_RANK), F32),
                        pltpu.VMEM((nr, MLA_RANK), F32), pltpu.VMEM((nr, MLA_ROPE), F32),
                        pltpu.VMEM((nr, hw), F32),
                        pltpu.VMEM((nr, 1), F32), pltpu.VMEM((nr, 1), F32), pltpu.VMEM((nr, MLA_RANK), F32),
                        pltpu.VMEM((2, tk, MLA_RANK), F32), pltpu.VMEM((2, n_pg, MLA_ROPE, PAGE), F32),
                        pltpu.SemaphoreType.DMA((2, 2))],
    )
    return pl.pallas_call(
        functools.partial(_mla_sample_kernel, layer, n_seq, n_pg, n_steps, t_valid),
        grid_spec=grid_spec,
        out_shape=jax.ShapeDtypeStruct((n_seq * nr, MLA_RANK), F32),
        compiler_params=_cparams(("arbitrary", "arbitrary")),
        name="mla_paged",
    )(page_table, q, q, k_new, c_new, *[w.arr for w in wts], ckv_pool, kr_pool_t)


def _mla_up_kernel(n_seq, pc_ref, h_ref, wuvs_ref, wout_ref, wgate_ref, y_ref):
    nq = SAMPLE_ROWS
    o = jnp.zeros((n_seq * nq, MLA_H * MLA_V), F32)
    for hd in range(MLA_H):
        pch = pc_ref[:, hd * nq:(hd + 1) * nq, :].reshape(n_seq * nq, MLA_RANK)
        o = o + _dot(pch.astype(BF16), wuvs_ref[hd])
    y_ref[...] = _dot(o.astype(BF16), wout_ref[...]) * jax.nn.sigmoid(_dot(h_ref[...], wgate_ref[...]))


def _mla_up_proj_gate(pc, h, wuv_sel, w_out, w_gate, *, n_seq):
    rows = n_seq * SAMPLE_ROWS
    nr = MLA_H * SAMPLE_ROWS
    wts = (wuv_sel, w_out, w_gate)
    return pl.pallas_call(
        functools.partial(_mla_up_kernel, n_seq),
        grid=(1,),
        in_specs=[pl.BlockSpec((n_seq, nr, MLA_RANK), lambda i: (0, 0, 0)),
                  pl.BlockSpec((rows, D_MODEL), lambda i: (0, 0))] + [_wspec(w) for w in wts],
        out_specs=pl.BlockSpec((rows, D_MODEL), lambda i: (0, 0)),
        out_shape=jax.ShapeDtypeStruct((rows, D_MODEL), F32),
        compiler_params=_cparams(("arbitrary",)),
        name="mla_up_proj_gate",
    )(pc.reshape(n_seq, nr, MLA_RANK), h, *[w.arr for w in wts])


def _projgate_kernel(o_ref, h_ref, wout_ref, wgate_ref, y_ref):
    y_ref[...] = _dot(o_ref[...].astype(BF16), wout_ref[...]) * jax.nn.sigmoid(_dot(h_ref[...], wgate_ref[...]))


def _proj_gate(o, h, w_out, w_gate, *, tm):
    m, kdim = o.shape
    return pl.pallas_call(
        _projgate_kernel,
        grid=(m // tm,),
        in_specs=[pl.BlockSpec((tm, kdim), lambda i: (i, 0)),
                  pl.BlockSpec((tm, D_MODEL), lambda i: (i, 0)),
                  _wspec(w_out), _wspec(w_gate)],
        out_specs=pl.BlockSpec((tm, D_MODEL), lambda i: (i, 0)),
        out_shape=jax.ShapeDtypeStruct((m, D_MODEL), F32),
        compiler_params=_cparams(("parallel",)),
        name="proj_gate",
    )(o, h, w_out.arr, w_gate.arr)


def _pad_lanes(x, width):
    return jnp.pad(x, [(0, 0)] * (x.ndim - 1) + [(0, width - x.shape[-1])])


def _mla_head_layout(nope, r1, r2):
    z = jnp.zeros(nope.shape[:-1] + (MLA_LANES - MLA_QK,), nope.dtype)
    x = jnp.concatenate([r1, r2, nope, z], axis=-1)
    return x.reshape(x.shape[:-2] + (MLA_H * MLA_LANES,))


def _gain_layout(g):
    half = MLA_ROPE // 2
    return jnp.concatenate([g[MLA_NOPE:MLA_NOPE + half], g[MLA_NOPE + half:], g[:MLA_NOPE],
                            jnp.zeros((MLA_LANES - MLA_QK,), g.dtype)]).reshape(1, MLA_LANES)


def _rope_tables(pos):
    half = MLA_ROPE // 2
    inv = ROPE_THETA ** (-jnp.arange(half, dtype=F32) / half)
    ang = pos.astype(F32)[:, None] * inv
    cos, sin = jnp.cos(ang), jnp.sin(ang)
    n = pos.shape[0]
    cos_t = jnp.concatenate([cos, cos, jnp.ones((n, MLA_NOPE), F32), jnp.zeros((n, MLA_LANES - MLA_QK), F32)], -1)
    sin_t = jnp.concatenate([sin, sin, jnp.zeros((n, MLA_LANES - MLA_ROPE), F32)], -1)
    return cos_t, sin_t


def _layer_weights(p):
    w_in = p['w_in']
    sizes = (DN_CONV_W, DN_QK_W, DN_H, DN_H, SC_W, SC_W, SC_W, MLA_RANK, MLA_RANK, MLA_ROPE, MEM_W, 4 * D_MODEL)
    offs = np.concatenate([[0], np.cumsum(sizes)])
    seg = [w_in[:, offs[i]:offs[i + 1]] for i in range(len(sizes))]
    bf = lambda x: x.astype(BF16)
    row = lambda x: x.reshape(1, -1)
    gates = [bf(seg[11][:, i * D_MODEL:(i + 1) * D_MODEL]) for i in range(4)]
    half = MLA_ROPE // 2

    w = {}
    w['ffn1'] = (row(p['ffn1_norm']), bf(p['ffn1_w_gu']), bf(p['ffn1_w_down']))
    w['ffn2'] = (row(p['ffn2_norm']), bf(p['ffn2_w_gu']), bf(p['ffn2_w_down']))
    w['mix_norm'] = row(p['mix_norm'])
    w['w_o'] = bf(p['w_o'])
    w['dn'] = (bf(seg[0]), bf(seg[1]), bf(_pad_lanes(jnp.concatenate([seg[2], seg[3]], 1), 128)),
               p['dn_conv_w'], _pad_lanes(row(p['dn_A_log']), 128), _pad_lanes(row(p['dn_dt_bias']), 128),
               row(p['dn_norm']), bf(p['dn_w_out']), gates[0])
    w['sc'] = (bf(jnp.concatenate([seg[4], seg[5], seg[6]], 1)), p['sc_conv_w'], bf(p['sc_w_out']), gates[1])

    wq = p['mla_w_q_b'].reshape(MLA_RANK, MLA_H, MLA_QK)
    q_nope, q_r1, q_r2 = wq[..., :MLA_NOPE], wq[..., MLA_NOPE:MLA_NOPE + half], wq[..., MLA_NOPE + half:]
    wq_perm = _mla_head_layout(q_nope, q_r1, q_r2)
    wq_swap = _mla_head_layout(jnp.zeros_like(q_nope), -q_r2, q_r1)
    wkr = seg[9]
    wkr_pad = _pad_lanes(wkr, MLA_LANES)
    wkr_swap = _pad_lanes(jnp.concatenate([-wkr[:, half:], wkr[:, :half]], 1), MLA_LANES)
    wkv = p['mla_w_kv_b'].reshape(MLA_RANK, MLA_H, MLA_NOPE + MLA_V)
    w_uk, w_uv = wkv[..., :MLA_NOPE], wkv[..., MLA_NOPE:]
    zr = jnp.zeros((MLA_RANK, MLA_H, half), F32)
    wuk_perm = bf(_mla_head_layout(w_uk, zr, zr))
    k_gain = _gain_layout(p['mla_k_norm'])
    w['mla_proj'] = (bf(seg[7]), row(p['mla_q_norm_a']), bf(wq_perm), bf(wq_swap), _gain_layout(p['mla_q_norm']),
                     bf(seg[8]), row(p['mla_kv_norm_a']), bf(wkr_pad), bf(wkr_swap),
                     wuk_perm, bf(w_uv.reshape(MLA_RANK, MLA_H * MLA_V)), k_gain)
    eye = jnp.eye(MLA_H, dtype=F32)
    w['mla_wuv_sel'] = bf((w_uv[None] * eye[:, None, :, None]).reshape(MLA_H, MLA_RANK, MLA_H * MLA_V))
    w['mla_sample'] = (wuk_perm, bf(w_uk.reshape(MLA_RANK, MLA_H * MLA_NOPE).T), k_gain)
    w['mla_out'] = (bf(p['mla_w_out']), gates[2])
    w['mem_kv'] = (row(p['mem_norm']), bf(p['mem_w_kv']), row(p['mem_k_norm']))
    w['mem'] = (bf(seg[10]), row(p['mem_q_norm']), bf(p['mem_w_out']), gates[3])
    return w


def _group_layer(x, w, *, n_seq, t_pad, t_valid, tm, bb_n, tt, chunk, dn_state, sc_state, mem_kv, cos, sin,
                 n_tab, mla_attend, q_dtype):
    cfg = dict(n_seq=n_seq, t_pad=t_pad, bb_n=bb_n, tt=tt)
    x1, h = _ffn(x, *w['ffn1'], tm=tm, h_gain=w['mix_norm'])
    y_dn, dn_s, dn_c = _deltanet(h, dn_state[0], dn_state[1], w['dn'], t_valid=t_valid, chunk=chunk, **cfg)
    y_sc, sc_c = _shortconv(h, sc_state, w['sc'], t_valid=t_valid, **cfg)
    q, ckv, kr, k, v = _mla_proj(h, cos, sin, w['mla_proj'], tm=tm, n_tab=n_tab, qk_dtype=q_dtype)
    if mla_attend is None:
        o = _mla_prompt_attn(q, k, v, n_seq=n_seq, seq=t_pad, tq=tm)
        y_mla = _proj_gate(o, h, *w['mla_out'], tm=tm)
    else:
        y_mla = mla_attend(q, k, ckv, h)
    y_mem = _mem_attn(h, mem_kv[0], mem_kv[1], w['mem'], **cfg)
    x3 = _ffn(x1, *w['ffn2'], tm=min(tm, 256), merge=((y_dn, y_sc, y_mla, y_mem), w['w_o']))
    return x3, dn_s, dn_c, sc_c, ckv, kr


def kernel(x_prompt, x_sample, state_dn_S, state_dn_conv, state_sc_conv, cache_mla_ckv, cache_mla_krope, cache_mem_k, cache_mem_v, page_table, mem_prompt, ffn1_norm, ffn1_w_gu, ffn1_w_down, mix_norm, w_in, dn_conv_w, dn_A_log, dn_dt_bias, dn_norm, dn_w_out, sc_conv_w, sc_w_out, mla_q_norm_a, mla_w_q_b, mla_kv_norm_a, mla_w_kv_b, mla_q_norm, mla_k_norm, mla_w_out, mem_norm, mem_w_kv, mem_q_norm, mem_k_norm, mem_w_out, w_o, ffn2_norm, ffn2_w_gu, ffn2_w_down):
    params = dict(ffn1_norm=ffn1_norm, ffn1_w_gu=ffn1_w_gu, ffn1_w_down=ffn1_w_down, mix_norm=mix_norm, w_in=w_in,
                  dn_conv_w=dn_conv_w, dn_A_log=dn_A_log, dn_dt_bias=dn_dt_bias, dn_norm=dn_norm, dn_w_out=dn_w_out,
                  sc_conv_w=sc_conv_w, sc_w_out=sc_w_out, mla_q_norm_a=mla_q_norm_a, mla_w_q_b=mla_w_q_b,
                  mla_kv_norm_a=mla_kv_norm_a, mla_w_kv_b=mla_w_kv_b, mla_q_norm=mla_q_norm, mla_k_norm=mla_k_norm,
                  mla_w_out=mla_w_out, mem_norm=mem_norm, mem_w_kv=mem_w_kv, mem_q_norm=mem_q_norm,
                  mem_k_norm=mem_k_norm, mem_w_out=mem_w_out, w_o=w_o, ffn2_norm=ffn2_norm, ffn2_w_gu=ffn2_w_gu,
                  ffn2_w_down=ffn2_w_down)
    depth = w_in.shape[0]
    bp, seq, _ = x_prompt.shape
    bs, td, _ = x_sample.shape
    tds = SAMPLE_ROWS
    n_pages = page_table.shape[1]
    past = n_pages * PAGE
    krope_t = jnp.transpose(cache_mla_krope, (0, 1, 3, 2))

    cos_p, sin_p = _rope_tables(jnp.arange(seq))
    cos_s, sin_s = _rope_tables(past + jnp.arange(tds))
    cos_s, sin_s = jnp.tile(cos_s, (bs, 1)), jnp.tile(sin_s, (bs, 1))

    xp = x_prompt.reshape(bp * seq, D_MODEL)
    xs = jnp.pad(x_sample, ((0, 0), (0, tds - td), (0, 0))).reshape(bs * tds, D_MODEL)
    zero_s = jnp.zeros((bp, DN_H, DN_DK, DN_DK), F32)
    zero_dc = jnp.zeros((bp, DN_CONV - 1, DN_CONV_W), F32)
    zero_sc = jnp.zeros((bp, SC_CONV - 1, SC_W), F32)
    mem2d = mem_prompt.reshape(bp * N_MEM, D_MODEL)

    outs = {k: [] for k in ('pS', 'pdc', 'psc', 'pckv', 'pkr', 'pmk', 'pmv', 'sS', 'sdc', 'ssc', 'sckv', 'skr')}
    tm_p = 512
    w_all = jax.vmap(_layer_weights)(params)
    for l in range(depth):
        w = jax.tree.map(lambda a: _W(a, l), w_all)
        mk, mv = _mem_kv(mem2d, w['mem_kv'], tm=tm_p)
        mk3, mv3 = mk.reshape(bp, N_MEM, MEM_W), mv.reshape(bp, N_MEM, MEM_W)
        xp, s_p, dc_p, sc_p, ckv_p, kr_p = _group_layer(
            xp, w, n_seq=bp, t_pad=seq, t_valid=seq, tm=tm_p, bb_n=1, tt=tm_p, chunk=DN_CHUNK,
            dn_state=(zero_s, zero_dc), sc_state=zero_sc, mem_kv=(mk3, mv3), cos=cos_p, sin=sin_p,
            n_tab=seq // tm_p, mla_attend=None, q_dtype=BF16)
        outs['pS'].append(s_p); outs['pdc'].append(dc_p); outs['psc'].append(sc_p)
        outs['pckv'].append(ckv_p.reshape(bp, seq, MLA_RANK)); outs['pkr'].append(kr_p.reshape(bp, seq, MLA_ROPE))
        outs['pmk'].append(mk.reshape(bp, N_MEM, MEM_H, MEM_HD)); outs['pmv'].append(mv.reshape(bp, N_MEM, MEM_H, MEM_HD))

        def attend(q, k, ckv, h, l=l, w=w):
            pc = _mla_sample_attn(page_table, q, k, ckv, w['mla_sample'], cache_mla_ckv, krope_t, l,
                                  n_seq=bs, t_valid=td, n_pg=32)
            return _mla_up_proj_gate(pc, h, w['mla_wuv_sel'], *w['mla_out'], n_seq=bs)

        xs, s_s, dc_s, sc_s, ckv_s, kr_s = _group_layer(
            xs, w, n_seq=bs, t_pad=tds, t_valid=td, tm=bs * tds, bb_n=8, tt=tds, chunk=tds,
            dn_state=(state_dn_S[l], state_dn_conv[l]), sc_state=state_sc_conv[l],
            mem_kv=(cache_mem_k[l].reshape(bs, N_MEM, MEM_W), cache_mem_v[l].reshape(bs, N_MEM, MEM_W)),
            cos=cos_s, sin=sin_s, n_tab=1, mla_attend=attend, q_dtype=F32)
        outs['sS'].append(s_s); outs['sdc'].append(dc_s); outs['ssc'].append(sc_s)
        outs['sckv'].append(ckv_s.reshape(bs, tds, MLA_RANK)[:, :td])
        outs['skr'].append(kr_s.reshape(bs, tds, MLA_ROPE)[:, :td])

    st = lambda k: jnp.stack(outs[k])
    y_prompt = xp.reshape(bp, seq, D_MODEL)
    y_sample = xs.reshape(bs, tds, D_MODEL)[:, :td]
    return (y_prompt, y_sample, st('pS'), st('pdc'), st('psc'), st('pckv'), st('pkr'), st('pmk'), st('pmv'),
            st('sS'), st('sdc'), st('ssc'), st('sckv'), st('skr'))
```

```python
import functools

import numpy as np
import jax
import jax.numpy as jnp
from jax import lax
from jax.experimental import pallas as pl
from jax.experimental.pallas import tpu as pltpu

F32 = jnp.float32
BF16 = jnp.bfloat16

D_MODEL = 1024
D_FF = 2816
EPS = 1e-6
N_MEM = 256
PAGE = 128
DN_H = 4
DN_DK = 128
DN_QK_W = 512
DN_CONV_W = 1536
DN_CONV = 4
DN_CHUNK = 64
DN_UNROLL_GROUPS = 4
DN_GROUP = 2
SC_W = 512
SC_CONV = 3
MLA_H = 8
MLA_RANK = 256
MLA_NOPE = 64
MLA_ROPE = 32
MLA_V = 64
MLA_QK = 96
MLA_LANES = 128
ROPE_THETA = 10000.0
LOG2E = 1.4426950408889634
MEM_H = 4
MEM_HD = 128
MEM_W = 512

VMEM_LIMIT = 56 * 1024 * 1024


def _cparams(sem):
    return pltpu.CompilerParams(dimension_semantics=sem, vmem_limit_bytes=VMEM_LIMIT)


class _W:
    def __init__(self, arr, layer):
        self.arr, self.layer = arr, layer


def _wspec(w, resident=False):
    shape = w.arr.shape[1:]
    index = lambda *_: (w.layer,) + (0,) * len(shape)
    if resident:
        return pl.BlockSpec((None,) + shape, index, pipeline_mode=pl.Buffered(1))
    return pl.BlockSpec((None,) + shape, index)


def _rms(x, g):
    ms = jnp.mean(x * x, axis=-1, keepdims=True)
    return x * lax.rsqrt(ms + EPS) * g


def _silu(x):
    return x * jax.nn.sigmoid(x)


def _rnd(x):
    return x.astype(BF16).astype(F32)


def _dot(a, b):
    return jnp.dot(a, b, preferred_element_type=F32)


def _dot_nt(a, b):
    return lax.dot_general(a, b, (((1,), (1,)), ((), ())), preferred_element_type=F32)


def _mm(a, b, small):
    if small:
        return _dot(_rnd(a), _rnd(b))
    return _dot(a.astype(BF16), b.astype(BF16))


def _mm_nt(a, b, small):
    if small:
        return _dot_nt(_rnd(a), _rnd(b))
    return _dot_nt(a.astype(BF16), b.astype(BF16))


def _mm_tn(a, b, small):
    dn = (((0,), (0,)), ((), ()))
    if small:
        return lax.dot_general(_rnd(a), _rnd(b), dn, preferred_element_type=F32)
    return lax.dot_general(a.astype(BF16), b.astype(BF16), dn, preferred_element_type=F32)


def _run_interleaved(gens):
    gens = list(gens)
    while gens:
        for g in list(gens):
            try:
                next(g)
            except StopIteration:
                gens.remove(g)


def _split2(x):
    hi = x.astype(BF16)
    lo = (x - hi.astype(F32)).astype(BF16)
    return hi, lo


def _mm_hi(a, b, small):
    if small:
        return jnp.dot(a, b, preferred_element_type=F32, precision=lax.Precision.HIGHEST)
    ah, al = _split2(a)
    bh, bl = _split2(b)
    return _dot(ah, bh) + (_dot(ah, bl) + _dot(al, bh))


def _mm_exact_left(lmat, b, small):
    if small:
        return jnp.dot(lmat, b, preferred_element_type=F32, precision=lax.Precision.HIGHEST)
    lb = lmat.astype(BF16)
    b1 = b.astype(BF16)
    r1 = b - b1.astype(F32)
    b2 = r1.astype(BF16)
    b3 = (r1 - b2.astype(F32)).astype(BF16)
    return _dot(lb, b1) + (_dot(lb, b2) + _dot(lb, b3))


FFN_SPLIT = 2


def _ffn_kernel(merge, emit_h, *refs):
    it = iter(refs)
    x_ref = next(it)
    if merge:
        y_refs = [next(it) for _ in range(4)]
        wo_ref = next(it)
    g_ref, wgu_ref, wd_ref = next(it), next(it), next(it)
    if emit_h:
        g2_ref = next(it)
    o_ref = next(it)
    if emit_h:
        h_ref = next(it)

    x = x_ref[...]
    if merge:
        m = ((y_refs[0][...] + y_refs[1][...]) + y_refs[2][...]) + y_refs[3][...]
        x = x + _dot(m.astype(BF16), wo_ref[...])
    h = _rms(x, g_ref[...]).astype(BF16)
    tf = D_FF // FFN_SPLIT
    def gate_up(j):
        return (_dot(h, wgu_ref[:, j * tf:(j + 1) * tf]),
                _dot(h, wgu_ref[:, D_FF + j * tf:D_FF + (j + 1) * tf]))

    acc = None
    nxt = gate_up(0)
    for j in range(FFN_SPLIT):
        gate, up = nxt
        if j + 1 < FFN_SPLIT:
            nxt = gate_up(j + 1)
        a = (_silu(gate) * up).astype(BF16)
        d = _dot(a, wd_ref[j * tf:(j + 1) * tf, :])
        acc = d if acc is None else acc + d
    out = x + 0.5 * acc
    o_ref[...] = out
    if emit_h:
        h_ref[...] = _rms(out, g2_ref[...]).astype(BF16)


def _ffn(x, norm_g, w_gu, w_down, *, tm, merge=None, h_gain=None):
    m = x.shape[0]
    row = pl.BlockSpec((tm, D_MODEL), lambda i: (i, 0))
    in_specs, args = [row], [x]
    if merge is not None:
        ys, w_o = merge
        in_specs += [row] * 4 + [_wspec(w_o, resident=True)]
        args += list(ys) + [w_o.arr]
    in_specs += [_wspec(w, resident=True) for w in (norm_g, w_gu, w_down)]
    args += [norm_g.arr, w_gu.arr, w_down.arr]
    out_shape = [jax.ShapeDtypeStruct((m, D_MODEL), F32)]
    out_specs = [row]
    if h_gain is not None:
        in_specs.append(_wspec(h_gain, resident=True))
        args.append(h_gain.arr)
        out_shape.append(jax.ShapeDtypeStruct((m, D_MODEL), BF16))
        out_specs.append(row)
    res = pl.pallas_call(
        functools.partial(_ffn_kernel, merge is not None, h_gain is not None),
        grid=(m // tm,),
        in_specs=in_specs, out_specs=out_specs, out_shape=out_shape,
        compiler_params=_cparams(("parallel",)),
        name="ffn_merge" if merge is not None else "ffn",
    )(*args)
    return res if h_gain is not None else res[0]


def _dn_kernel(bb_n, tt, chunk, tv_last, n_t, small,
               h_ref, s0_ref, cprev_ref, wqkv_ref, wz_ref, wab_ref, convw_ref, alog_ref, dtb_ref,
               normg_ref, wout_ref, wgate_ref,
               y_ref, snew_ref, cnew_ref,
               xbuf, cs, gb, zb, ob, s_all, u_s, w_s, qg_s, kdec_s, aqk_s, gl_s, gate_s):
    t = pl.program_id(1)
    rows = bb_n * tt

    @pl.when(t == 0)
    def _():
        xbuf[:, 5:8, :] = cprev_ref[...]
        snew_ref[...] = s0_ref[...]

    h = h_ref[...]
    qkv = _dot(h, wqkv_ref[...])
    ab = _dot(h, wab_ref[...])
    zs = _dot(h, wz_ref[...])
    gate = _dot(h, wgate_ref[...])
    xbuf[:, 8:8 + tt, :] = qkv.reshape(bb_n, tt, DN_CONV_W)
    c = 0.0
    for j in range(DN_CONV):
        c = c + xbuf[:, 5 + j:5 + j + tt, :] * convw_ref[j:j + 1, :].reshape(1, 1, DN_CONV_W)
    cnew_ref[...] = xbuf[:, 5 + tv_last:8 + tv_last, :]
    if n_t > 1:
        xbuf[:, 0:8, :] = xbuf[:, tt:tt + 8, :]

    g = -jnp.exp(alog_ref[...]) * jax.nn.softplus(ab + dtb_ref[...])
    lane = lax.broadcasted_iota(jnp.int32, (rows, 128), 1)
    gbv = jnp.where(lane < DN_H, g, jax.nn.sigmoid(ab)).reshape(bb_n, tt, 128)
    if tv_last < tt:
        trow = lax.broadcasted_iota(jnp.int32, (bb_n, tt, 128), 1)
        gbv = jnp.where(trow < tv_last, gbv, 0.0)
    gb[...] = gbv
    c = _silu(c)
    for grp in range(8):
        sl = slice(grp * 128, (grp + 1) * 128)
        xg = c[:, :, sl]
        xn = xg * lax.rsqrt(jnp.sum(xg * xg, axis=-1, keepdims=True) + EPS)
        if grp < DN_H:
            xn = xn * (DN_DK ** -0.5)
        cs[:, :, sl] = xn
    cs[:, :, 2 * DN_QK_W:] = c[:, :, 2 * DN_QK_W:]
    zb[...] = _silu(zs).reshape(bb_n, tt, DN_QK_W)
    gate_s[...] = jax.nn.sigmoid(gate)

    n4 = DN_H * chunk
    ri = lax.broadcasted_iota(jnp.int32, (n4, n4), 0)
    ci = lax.broadcasted_iota(jnp.int32, (n4, n4), 1)
    same = (ri // chunk) == (ci // chunk)
    incl = same & (ci <= ri)
    strict = same & (ci < ri)
    lmat = incl.astype(F32)
    umat = strict.astype(F32)
    vmask = (lax.broadcasted_iota(jnp.int32, (n4, DN_H * 128), 0) // chunk
             == lax.broadcasted_iota(jnp.int32, (n4, DN_H * 128), 1) // 128)
    n_pow = int(np.log2(chunk))
    normg = normg_ref[...]
    n_ch = tt // chunk
    total = bb_n * n_ch
    solve_mm = _mm_hi if small else _mm

    for hd in range(DN_H):
        s_all[:, :, hd * 128:(hd + 1) * 128] = snew_ref[:, hd]

    def stack_rows(ref, b, r0, off):
        return jnp.concatenate([ref[b, pl.ds(r0, chunk), off + hd * 128:off + (hd + 1) * 128]
                                for hd in range(DN_H)], axis=0)

    def level1(k, slot):
        b = k // n_ch
        r0 = (k % n_ch) * chunk
        if not isinstance(k, int):
            r0 = pl.multiple_of(r0, chunk)
        q = stack_rows(cs, b, r0, 0)
        kk = stack_rows(cs, b, r0, DN_QK_W)
        v = stack_rows(cs, b, r0, 2 * DN_QK_W)
        gbc = gb[b, pl.ds(r0, chunk), :]
        g_st = jnp.concatenate([jnp.broadcast_to(gbc[:, hd:hd + 1], (chunk, 128)) for hd in range(DN_H)], axis=0)
        beta_st = jnp.concatenate([jnp.broadcast_to(gbc[:, DN_H + hd:DN_H + hd + 1], (chunk, 128))
                                   for hd in range(DN_H)], axis=0)
        g_sq = jnp.concatenate([g_st] * (n4 // 128), axis=1) if n4 >= 128 else g_st[:, :n4]
        gc = _mm_exact_left(lmat, g_st, small)
        yield
        if small:
            dmat = _mm_exact_left(lmat, g_sq * umat, small)
        else:
            gct = gc.T
            dmat = (jnp.concatenate([gc] * (n4 // 128), axis=1)
                    - jnp.concatenate([gct] * (n4 // 128), axis=0))
        gam = jnp.where(incl, jnp.exp(dmat), 0.0)
        eg = jnp.exp(gc)
        kb = kk * beta_st
        kq = _mm_nt(jnp.concatenate([kb, q], axis=0), kk, small)
        yield
        a_mat = jnp.where(strict, kq[:n4] * gam, 0.0)
        x = jnp.concatenate([v * beta_st, kb * eg], axis=1)
        p = -a_mat
        for i in range(n_pow):
            x = x + (_mm_hi if i < 2 else solve_mm)(p, x, small)
            if i < n_pow - 1:
                p = (_mm_hi if i < 1 else solve_mm)(p, p, small)
            yield
        gc_last = [gc[(hd + 1) * chunk - 1:(hd + 1) * chunk, :] for hd in range(DN_H)]
        gl_st = jnp.concatenate([jnp.broadcast_to(r, (chunk, 128)) for r in gc_last], axis=0)
        u_s[slot] = x[:, :128]
        w_s[slot] = x[:, 128:]
        qg_s[slot] = q * eg
        kdec_s[slot] = kk * jnp.exp(gl_st - gc)
        aqk_s[slot] = jnp.where(incl, kq[n4:] * gam, 0.0)
        gl_s[slot] = jnp.exp(jnp.concatenate(gc_last, axis=1))

    def level2(k, slot):
        b = k // n_ch
        r0 = (k % n_ch) * chunk
        if not isinstance(k, int):
            r0 = pl.multiple_of(r0, chunk)
        s_old = s_all[b]
        w = w_s[slot]
        qg = qg_s[slot]
        ws, qs = [], []
        for hd in range(DN_H):
            rs = slice(hd * chunk, (hd + 1) * chunk)
            r = _mm(jnp.concatenate([w[rs], qg[rs]], axis=0), s_old[:, hd * 128:(hd + 1) * 128], small)
            ws.append(r[:chunk])
            qs.append(r[chunk:])
        yield
        v_new = u_s[slot] - jnp.concatenate(ws, axis=0)
        o = jnp.concatenate(qs, axis=0) + _mm(aqk_s[slot], v_new, small)
        vbd = jnp.where(vmask, jnp.concatenate([v_new] * DN_H, axis=1), 0.0)
        s_all[b] = s_old * gl_s[slot] + _mm_tn(kdec_s[slot], vbd, small)
        yield
        on = _rms(o, normg)
        for hd in range(DN_H):
            sl = slice(hd * 128, (hd + 1) * 128)
            ob[b, pl.ds(r0, chunk), sl] = on[hd * chunk:(hd + 1) * chunk] * zb[b, pl.ds(r0, chunk), sl]

    assert total % DN_GROUP == 0

    def recurrence(m, base):
        for i in range(DN_GROUP):
            yield from level2(m * DN_GROUP + i, base + i)

    _run_interleaved([level1(i, i) for i in range(DN_GROUP)])

    n_groups = total // DN_GROUP

    def group_body(m, _):
        base = (m % 2) * DN_GROUP
        chains = [recurrence(m, base)]
        if not isinstance(m, int):
            chains += [level1(jnp.minimum((m + 1) * DN_GROUP + i, total - 1), DN_GROUP - base + i)
                       for i in range(DN_GROUP)]
        elif m + 1 < n_groups:
            chains += [level1((m + 1) * DN_GROUP + i, DN_GROUP - base + i) for i in range(DN_GROUP)]
        _run_interleaved(chains)
        return 0

    if n_groups <= DN_UNROLL_GROUPS:
        for m in range(n_groups):
            group_body(m, 0)
    else:
        lax.fori_loop(0, n_groups, group_body, 0)

    for hd in range(DN_H):
        snew_ref[:, hd] = s_all[:, :, hd * 128:(hd + 1) * 128]

    y = _dot(ob[...].reshape(rows, DN_QK_W).astype(BF16), wout_ref[...])
    y_ref[...] = y * gate_s[...]


def _deltanet(h, s0, cprev, wts, *, n_seq, t_pad, t_valid, bb_n, tt, chunk):
    n_t = t_pad // tt
    assert n_t == 1 or t_valid == t_pad
    tv_last = t_valid - (n_t - 1) * tt
    rows = bb_n * tt
    small = chunk < 16
    n4 = DN_H * chunk
    in_specs = [
        pl.BlockSpec((rows, D_MODEL), lambda b, t: (b * n_t + t, 0)),
        pl.BlockSpec((bb_n, DN_H, DN_DK, DN_DK), lambda b, t: (b, 0, 0, 0)),
        pl.BlockSpec((bb_n, DN_CONV - 1, DN_CONV_W), lambda b, t: (b, 0, 0)),
    ] + [_wspec(w) for w in wts]
    out_specs = [
        pl.BlockSpec((rows, D_MODEL), lambda b, t: (b * n_t + t, 0)),
        pl.BlockSpec((bb_n, DN_H, DN_DK, DN_DK), lambda b, t: (b, 0, 0, 0)),
        pl.BlockSpec((bb_n, DN_CONV - 1, DN_CONV_W), lambda b, t: (b, 0, 0)),
    ]
    out_shape = [
        jax.ShapeDtypeStruct((n_seq * t_pad, D_MODEL), F32),
        jax.ShapeDtypeStruct((n_seq, DN_H, DN_DK, DN_DK), F32),
        jax.ShapeDtypeStruct((n_seq, DN_CONV - 1, DN_CONV_W), F32),
    ]
    return pl.pallas_call(
        functools.partial(_dn_kernel, bb_n, tt, chunk, tv_last, n_t, small),
        grid=(n_seq // bb_n, n_t),
        in_specs=in_specs, out_specs=out_specs, out_shape=out_shape,
        scratch_shapes=[pltpu.VMEM((bb_n, tt + 8, DN_CONV_W), F32), pltpu.VMEM((bb_n, tt, DN_CONV_W), F32),
                        pltpu.VMEM((bb_n, tt, 128), F32), pltpu.VMEM((bb_n, tt, DN_QK_W), F32),
                        pltpu.VMEM((bb_n, tt, DN_QK_W), F32),
                        pltpu.VMEM((bb_n, DN_DK, DN_H * 128), F32)]
                       + [pltpu.VMEM((2 * DN_GROUP, n4, 128), F32)] * 4
                       + [pltpu.VMEM((2 * DN_GROUP, n4, n4), F32), pltpu.VMEM((2 * DN_GROUP, 1, DN_H * 128), F32),
                          pltpu.VMEM((rows, D_MODEL), F32)],
        compiler_params=_cparams(("parallel", "arbitrary")),
        name="deltanet",
    )(h, s0, cprev, *[w.arr for w in wts])


def _sc_kernel(bb_n, tt, tv_last, n_t,
               h_ref, prev_ref, win_ref, convw_ref, wout_ref, wgate_ref,
               y_ref, new_ref, ubuf):
    t = pl.program_id(1)
    rows = bb_n * tt

    @pl.when(t == 0)
    def _():
        ubuf[:, 6:8, :] = prev_ref[...]

    h = h_ref[...]
    p = _dot(h, win_ref[...])
    gate = jax.nn.sigmoid(_dot(h, wgate_ref[...]))
    bgate = p[:, :SC_W]
    u = p[:, SC_W:2 * SC_W] * p[:, 2 * SC_W:]
    ubuf[:, 8:8 + tt, :] = u.reshape(bb_n, tt, SC_W)
    y = 0.0
    for j in range(SC_CONV):
        y = y + ubuf[:, 6 + j:6 + j + tt, :] * convw_ref[j:j + 1, :].reshape(1, 1, SC_W)

    new_ref[...] = ubuf[:, 6 + tv_last:8 + tv_last, :]
    if n_t > 1:
        ubuf[:, 0:8, :] = ubuf[:, tt:tt + 8, :]

    z = (bgate * y.reshape(rows, SC_W)).astype(BF16)
    y_ref[...] = _dot(z, wout_ref[...]) * gate


def _shortconv(h, prev, wts, *, n_seq, t_pad, t_valid, bb_n, tt):
    n_t = t_pad // tt
    assert n_t == 1 or t_valid == t_pad
    tv_last = t_valid - (n_t - 1) * tt
    rows = bb_n * tt
    return pl.pallas_call(
        functools.partial(_sc_kernel, bb_n, tt, tv_last, n_t),
        grid=(n_seq // bb_n, n_t),
        in_specs=[pl.BlockSpec((rows, D_MODEL), lambda b, t: (b * n_t + t, 0)),
                  pl.BlockSpec((bb_n, SC_CONV - 1, SC_W), lambda b, t: (b, 0, 0))]
                 + [_wspec(w) for w in wts],
        out_specs=[pl.BlockSpec((rows, D_MODEL), lambda b, t: (b * n_t + t, 0)),
                   pl.BlockSpec((bb_n, SC_CONV - 1, SC_W), lambda b, t: (b, 0, 0))],
        out_shape=[jax.ShapeDtypeStruct((n_seq * t_pad, D_MODEL), F32),
                   jax.ShapeDtypeStruct((n_seq, SC_CONV - 1, SC_W), F32)],
        scratch_shapes=[pltpu.VMEM((bb_n, tt + 8, SC_W), F32)],
        compiler_params=_cparams(("parallel", "arbitrary")),
        name="shortconv",
    )(h, prev, *[w.arr for w in wts])


def _memkv_kernel(m_ref, g_ref, wkv_ref, kg_ref, k_ref, v_ref):
    n = _rms(m_ref[...], g_ref[...]).astype(BF16)
    kv = _dot(n, wkv_ref[...])
    kg = kg_ref[...]
    for hd in range(MEM_H):
        sl = slice(hd * MEM_HD, (hd + 1) * MEM_HD)
        k_ref[:, sl] = _rms(kv[:, sl], kg)
    v_ref[...] = kv[:, MEM_W:]


def _mem_kv(mem2d, wts, *, tm):
    m = mem2d.shape[0]
    return pl.pallas_call(
        _memkv_kernel,
        grid=(m // tm,),
        in_specs=[pl.BlockSpec((tm, D_MODEL), lambda i: (i, 0))] + [_wspec(w) for w in wts],
        out_specs=[pl.BlockSpec((tm, MEM_W), lambda i: (i, 0))] * 2,
        out_shape=[jax.ShapeDtypeStruct((m, MEM_W), F32)] * 2,
        compiler_params=_cparams(("parallel",)),
        name="mem_kv",
    )(mem2d, *[w.arr for w in wts])


def _memattn_kernel(bb_n, tt, small,
                    h_ref, mk_ref, mv_ref, wq_ref, qg_ref, wout_ref, wgate_ref,
                    y_ref, qs, ob):
    rows = bb_n * tt
    h = h_ref[...]
    q = _dot(h, wq_ref[...])
    gate = jax.nn.sigmoid(_dot(h, wgate_ref[...]))
    qg = qg_ref[...]
    for hd in range(MEM_H):
        sl = slice(hd * MEM_HD, (hd + 1) * MEM_HD)
        qs[:, :, sl] = _rms(q[:, sl], qg).reshape(bb_n, tt, MEM_HD)

    def seq_body(b, _):
        def scores(hd):
            sl = slice(hd * MEM_HD, (hd + 1) * MEM_HD)
            return _mm_nt(qs[b, :, sl], mk_ref[b, :, sl], small)

        s_next = scores(0)
        for hd in range(MEM_H):
            sl = slice(hd * MEM_HD, (hd + 1) * MEM_HD)
            s = s_next * (MEM_HD ** -0.5)
            if hd + 1 < MEM_H:
                s_next = scores(hd + 1)
            s = s - jnp.max(s, axis=-1, keepdims=True)
            e = jnp.exp(s)
            p = e / jnp.sum(e, axis=-1, keepdims=True)
            ob[b, :, sl] = _mm(p, mv_ref[b, :, sl], small)
        return 0

    if bb_n == 1:
        seq_body(0, 0)
    else:
        lax.fori_loop(0, bb_n, seq_body, 0)
    y = _dot(ob[...].reshape(rows, MEM_W).astype(BF16), wout_ref[...])
    y_ref[...] = y * gate


def _mem_attn(h, mk, mv, wts, *, n_seq, t_pad, bb_n, tt):
    n_t = t_pad // tt
    rows = bb_n * tt
    return pl.pallas_call(
        functools.partial(_memattn_kernel, bb_n, tt, tt < 16),
        grid=(n_seq // bb_n, n_t),
        in_specs=[pl.BlockSpec((rows, D_MODEL), lambda b, t: (b * n_t + t, 0)),
                  pl.BlockSpec((bb_n, N_MEM, MEM_W), lambda b, t: (b, 0, 0)),
                  pl.BlockSpec((bb_n, N_MEM, MEM_W), lambda b, t: (b, 0, 0))]
                 + [_wspec(w) for w in wts],
        out_specs=pl.BlockSpec((rows, D_MODEL), lambda b, t: (b * n_t + t, 0)),
        out_shape=jax.ShapeDtypeStruct((n_seq * t_pad, D_MODEL), F32),
        scratch_shapes=[pltpu.VMEM((bb_n, tt, MEM_W), F32), pltpu.VMEM((bb_n, tt, MEM_W), F32)],
        compiler_params=_cparams(("parallel", "arbitrary")),
        name="mem_attn",
    )(h, mk, mv, *[w.arr for w in wts])


def _mlaproj_kernel(h_ref, cos_ref, sin_ref, wq_ref, qna_ref, wqp_ref, wqs_ref, qg_ref,
                    wkv_ref, kvna_ref, wkr_ref, wkrs_ref, wuk_ref, wuv_ref, kg_ref,
                    q_ref, ckv_ref, kr_ref, k_ref, v_ref):
    h = h_ref[...]
    cos = cos_ref[...]
    sin = sin_ref[...]
    cq = _dot(h, wq_ref[...])
    ckv_raw = _dot(h, wkv_ref[...])
    kr_a = _dot(h, wkr_ref[...])
    kr_b = _dot(h, wkrs_ref[...])
    cqn = _rms(cq, qna_ref[...]).astype(BF16)
    q_raw = _dot(cqn, wqp_ref[...])
    q_swp = _dot(cqn, wqs_ref[...])
    ckv = _rms(ckv_raw, kvna_ref[...])
    cb = ckv.astype(BF16)
    k_raw = _dot(cb, wuk_ref[...])
    v_ref[...] = _dot(cb, wuv_ref[...]).astype(BF16)
    ckv_ref[...] = ckv
    krp = kr_a * cos + kr_b * sin
    kr_ref[...] = krp[:, :MLA_ROPE]

    qg = qg_ref[...]
    inv_n = 1.0 / MLA_QK
    for hd in range(MLA_H):
        sl = slice(hd * MLA_LANES, (hd + 1) * MLA_LANES)
        qh = q_raw[:, sl] * cos + q_swp[:, sl] * sin
        ms = jnp.sum(qh * qh, axis=-1, keepdims=True) * inv_n
        q_ref[:, sl] = (qh * lax.rsqrt(ms + EPS) * qg).astype(q_ref.dtype)

    kg = kg_ref[...]
    for hd in range(MLA_H):
        sl = slice(hd * MLA_LANES, (hd + 1) * MLA_LANES)
        kh = k_raw[:, sl] + krp
        ms = jnp.sum(kh * kh, axis=-1, keepdims=True) * inv_n
        k_ref[:, sl] = (kh * lax.rsqrt(ms + EPS) * kg).astype(k_ref.dtype)


def _mla_proj(h, cos, sin, wts, *, tm, n_tab, qk_dtype):
    m = h.shape[0]
    hw = MLA_H * MLA_LANES
    vw = MLA_H * MLA_V
    row = lambda width: pl.BlockSpec((tm, width), lambda i: (i, 0))
    in_specs = [row(D_MODEL),
                pl.BlockSpec((tm, MLA_LANES), lambda i: (i % n_tab, 0)),
                pl.BlockSpec((tm, MLA_LANES), lambda i: (i % n_tab, 0))] + [_wspec(w) for w in wts]
    return pl.pallas_call(
        _mlaproj_kernel,
        grid=(m // tm,),
        in_specs=in_specs,
        out_specs=[row(hw), row(MLA_RANK), row(MLA_ROPE), row(hw), row(vw)],
        out_shape=[jax.ShapeDtypeStruct((m, hw), qk_dtype),
                   jax.ShapeDtypeStruct((m, MLA_RANK), F32),
                   jax.ShapeDtypeStruct((m, MLA_ROPE), F32),
                   jax.ShapeDtypeStruct((m, hw), qk_dtype),
                   jax.ShapeDtypeStruct((m, vw), BF16)],
        compiler_params=_cparams(("parallel",)),
        name="mla_proj",
    )(h, cos, sin, *[w.arr for w in wts])


def _flash_kernel(tq, q_ref, k_ref, v_ref, o_ref, m_scr, l_scr, acc_scr):
    qi = pl.program_id(1)
    ki = pl.program_id(2)
    c2 = (MLA_QK ** -0.5) * LOG2E

    @pl.when(ki == 0)
    def _():
        m_scr[...] = jnp.full(m_scr.shape, -jnp.inf, F32)
        l_scr[...] = jnp.zeros_like(l_scr)
        acc_scr[...] = jnp.zeros_like(acc_scr)

    def compute(diag):
        half = tq // 2
        blocks = [(0, half, half), (half, tq, tq)] if diag else [(0, tq, tq)]
        keep = {}
        if diag:
            for r0, r1, nk in blocks:
                row = lax.broadcasted_iota(jnp.int32, (r1 - r0, nk), 0) + r0
                col = lax.broadcasted_iota(jnp.int32, (r1 - r0, nk), 1)
                keep[r0] = col <= row
        ones = jnp.ones((tq, 128), BF16)
        work = [(hd, blk) for hd in range(MLA_H) for blk in blocks]

        def qk(hd, blk):
            r0, r1, nk = blk
            sl = slice(hd * MLA_LANES, (hd + 1) * MLA_LANES)
            return _dot_nt(q_ref[r0:r1, sl], k_ref[0:nk, sl])

        s_next = qk(*work[0])
        for idx, (hd, (r0, r1, nk)) in enumerate(work):
            s = s_next * c2
            if idx + 1 < len(work):
                s_next = qk(*work[idx + 1])
            if diag:
                s = jnp.where(keep[r0], s, -jnp.inf)
            m_old = m_scr[hd, r0:r1]
            m_new = jnp.maximum(m_old, jnp.max(s, axis=-1, keepdims=True))
            alpha = jnp.exp2(m_old - m_new)
            p = jnp.exp2(s - jnp.concatenate([m_new] * (nk // 128), axis=1)).astype(BF16)
            pair = hd // 2
            vext = jnp.concatenate([v_ref[0:nk, pair * 128:(pair + 1) * 128], ones[0:nk]], axis=1)
            r = _dot(p, vext)
            acc_scr[hd, r0:r1] = alpha * acc_scr[hd, r0:r1] + r[:, :128]
            l_scr[hd, r0:r1] = alpha * l_scr[hd, r0:r1] + r[:, 128:]
            m_scr[hd, r0:r1] = m_new

    @pl.when(ki < qi)
    def _():
        compute(False)

    @pl.when(ki == qi)
    def _():
        compute(True)
        lane = lax.broadcasted_iota(jnp.int32, (tq, 128), 1)
        for pair in range(MLA_H // 2):
            even = acc_scr[2 * pair] / l_scr[2 * pair]
            odd = acc_scr[2 * pair + 1] / l_scr[2 * pair + 1]
            o_ref[:, pair * 128:(pair + 1) * 128] = jnp.where(lane < MLA_V, even, odd).astype(o_ref.dtype)


def _mla_prompt_attn(q, k, v, *, n_seq, seq, tq):
    nq = seq // tq
    hw = MLA_H * MLA_LANES
    vw = MLA_H * MLA_V
    return pl.pallas_call(
        functools.partial(_flash_kernel, tq),
        grid=(n_seq, nq, nq),
        in_specs=[pl.BlockSpec((tq, hw), lambda b, i, j: (b * nq + i, 0)),
                  pl.BlockSpec((tq, hw), lambda b, i, j: (b * nq + jnp.minimum(i, j), 0)),
                  pl.BlockSpec((tq, vw), lambda b, i, j: (b * nq + jnp.minimum(i, j), 0))],
        out_specs=pl.BlockSpec((tq, vw), lambda b, i, j: (b * nq + i, 0)),
        out_shape=jax.ShapeDtypeStruct((n_seq * seq, vw), BF16),
        scratch_shapes=[pltpu.VMEM((MLA_H, tq, 128), F32)] * 3,
        compiler_params=_cparams(("parallel", "parallel", "arbitrary")),
        name="mla_flash",
    )(q, k, v)


SAMPLE_ROWS = 8
SUB_KEYS = 1024


def _mla_sample_kernel(layer, n_seq, n_pg, n_steps, t_valid,
                       pt_ref, qall_ref, q_ref, knew_ref, cnew_ref, wukp_ref, wukt_ref, kg_ref, ckv_hbm, kr_hbm,
                       o_ref,
                       lhs, qabs_all, qabs, qrope, qblk, m_scr, l_scr, acc_scr, cbuf, rbuf, sem):
    b = pl.program_id(0)
    st = pl.program_id(1)
    c2 = (MLA_QK ** -0.5) * LOG2E
    inv_n = 1.0 / MLA_QK
    tk = n_pg * PAGE
    nq = SAMPLE_ROWS
    n_up = MLA_H * MLA_NOPE
    g = b * n_steps + st
    slot = g % 2

    def page_copies(bb, ss, sl):
        cps = []
        for i in range(n_pg):
            page = pt_ref[bb, ss * n_pg + i]
            cps.append((pltpu.make_async_copy(ckv_hbm.at[layer, page], cbuf.at[sl, pl.ds(i * PAGE, PAGE), :],
                                              sem.at[sl, 0]), i % 2))
            cps.append((pltpu.make_async_copy(kr_hbm.at[layer, page], rbuf.at[sl, i], sem.at[sl, 1]), (i + 1) % 2))
        return cps

    n_total = n_seq * n_steps
    is_last = g == n_total - 1

    @pl.when(g == 0)
    def _():
        for cp, prio in page_copies(0, 0, 0):
            cp.start(priority=prio)
        kg = kg_ref[...]
        for hd in range(MLA_H):
            sl = slice(hd * MLA_LANES, (hd + 1) * MLA_LANES)
            qh, ql = _split2(qall_ref[:, sl] * kg)
            w = wukp_ref[:, sl]
            qabs_all[hd] = _dot_nt(qh, w) + _dot_nt(ql, w)

    @pl.when(st == 0)
    def _():
        m_scr[...] = jnp.full(m_scr.shape, -jnp.inf, F32)
        l_scr[...] = jnp.zeros_like(l_scr)
        acc_scr[...] = jnp.zeros_like(acc_scr)
        kg = kg_ref[...]
        q = q_ref[...]
        lane_head = lax.broadcasted_iota(jnp.int32, q.shape, 1) // MLA_LANES
        r0 = pl.multiple_of(b * nq, nq)
        for hd in range(MLA_H):
            sl = slice(hd * MLA_LANES, (hd + 1) * MLA_LANES)
            qabs[hd * nq:(hd + 1) * nq, :] = qabs_all[hd, pl.ds(r0, nq), :]
            qrope[hd * nq:(hd + 1) * nq, :] = q[:, hd * MLA_LANES:hd * MLA_LANES + MLA_ROPE] * kg[:, :MLA_ROPE]
            qblk[hd * nq:(hd + 1) * nq, :] = jnp.where(lane_head == hd, q, 0.0)
        lhs[:n_up, :] = wukt_ref[...]
        lhs[n_up:, :] = qabs[...].astype(BF16)

    for cp, _ in page_copies(b, st, slot):
        cp.wait()

    lhs_v = lhs[...]
    qr = qrope[...].astype(BF16)
    pg_sub = SUB_KEYS // PAGE

    def score_block(j):
        cb = cbuf[slot, j * SUB_KEYS:(j + 1) * SUB_KEYS, :].astype(BF16)
        krt = jnp.concatenate([rbuf[slot, i] for i in range(j * pg_sub, (j + 1) * pg_sub)],
                              axis=1)
        big = _dot_nt(lhs_v, cb)
        knt = big[:n_up]
        ssq = jnp.sum((knt * knt).reshape(MLA_H, MLA_NOPE, SUB_KEYS), axis=1)
        ssq_r = jnp.sum(krt * krt, axis=0, keepdims=True)
        rs = lax.rsqrt((ssq + ssq_r) * inv_n + EPS) * c2
        s = big[n_up:] + _dot(qr, krt.astype(BF16))
        s = jnp.concatenate([s[hd * nq:(hd + 1) * nq, :] * rs[hd:hd + 1, :] for hd in range(MLA_H)], axis=0)
        return s, cb

    m_run = m_scr[...]
    l_new = l_scr[...]
    acc = acc_scr[...]
    n_sub = tk // SUB_KEYS
    blk = score_block(0)

    seq_end = st == n_steps - 1
    nb = jnp.where(is_last, b, jnp.where(seq_end, b + 1, b))
    ns = jnp.where(is_last, st, jnp.where(seq_end, 0, st + 1))
    for cp, prio in page_copies(nb, ns, 1 - slot):
        cp.start(priority=prio)

    for j in range(n_sub):
        s, cb = blk
        if j + 1 < n_sub:
            blk = score_block(j + 1)
        m_new = jnp.maximum(m_run, jnp.max(s, axis=-1, keepdims=True))
        alpha = jnp.exp2(m_run - m_new)
        p = jnp.exp2(s - m_new)
        l_new = alpha * l_new + jnp.sum(p, axis=-1, keepdims=True)
        acc = alpha * acc + _dot(p.astype(BF16), cb)
        m_run = m_new
    l_scr[...] = l_new
    acc_scr[...] = acc
    m_scr[...] = m_new

    @pl.when(st == n_steps - 1)
    def _():
        cn = cnew_ref[...]
        sn = _dot_nt(_rnd(qblk[...]), _rnd(knew_ref[...])) * c2
        row = lax.broadcasted_iota(jnp.int32, sn.shape, 0) % nq
        col = lax.broadcasted_iota(jnp.int32, sn.shape, 1)
        sn = jnp.where((col <= row) & (col < t_valid), sn, -jnp.inf)
        m_o = m_scr[...]
        m_n = jnp.maximum(m_o, jnp.max(sn, axis=-1, keepdims=True))
        al = jnp.exp2(m_o - m_n)
        pn = jnp.exp2(sn - m_n)
        l_f = al * l_scr[...] + jnp.sum(pn, axis=-1, keepdims=True)
        o_ref[...] = (al * acc_scr[...] + _dot(_rnd(pn), _rnd(cn))) / l_f

    @pl.when(is_last)
    def _():
        for cp, _ in page_copies(b, st, 1 - slot):
            cp.wait()


def _mla_sample_attn(page_table, q, k_new, c_new, wts, ckv_pool, kr_pool_t, layer, *, n_seq, t_valid, n_pg):
    n_pages = page_table.shape[1]
    n_steps = n_pages // n_pg
    hw = MLA_H * MLA_LANES
    tk = n_pg * PAGE
    nq = SAMPLE_ROWS
    nr = MLA_H * nq

    const = lambda *shape: pl.BlockSpec(shape, lambda b, s, pt: (0,) * len(shape))
    hbm = pl.BlockSpec(memory_space=pl.ANY)
    in_specs = [const(n_seq * nq, hw),
                pl.BlockSpec((nq, hw), lambda b, s, pt: (b, 0)),
                pl.BlockSpec((nq, hw), lambda b, s, pt: (b, 0)),
                pl.BlockSpec((nq, MLA_RANK), lambda b, s, pt: (b, 0))]
    in_specs += [_wspec(w) for w in wts] + [hbm, hbm]
    grid_spec = pltpu.PrefetchScalarGridSpec(
        num_scalar_prefetch=1,
        grid=(n_seq, n_steps),
        in_specs=in_specs,
        out_specs=pl.BlockSpec((nr, MLA_RANK), lambda b, s, pt: (b, 0)),
        scratch_shapes=[pltpu.VMEM((MLA_H * MLA_NOPE + nr, MLA_RANK), BF16),
                        pltpu.VMEM((MLA_H, n_seq * nq, MLA_RANK), F32),
                        pltpu.VMEM((nr, MLA_RANK), F32), pltpu.VMEM((nr, MLA_ROPE), F32),
                        pltpu.VMEM((nr, hw), F32),
                        pltpu.VMEM((nr, 1), F32), pltpu.VMEM((nr, 1), F32), pltpu.VMEM((nr, MLA_RANK), F32),
                        pltpu.VMEM((2, tk, MLA_RANK), F32), pltpu.VMEM((2, n_pg, MLA_ROPE, PAGE), F32),
                        pltpu.SemaphoreType.DMA((2, 2))],
    )
    return pl.pallas_call(
        functools.partial(_mla_sample_kernel, layer, n_seq, n_pg, n_steps, t_valid),
        grid_spec=grid_spec,
        out_shape=jax.ShapeDtypeStruct((n_seq * nr, MLA_RANK), F32),
        compiler_params=_cparams(("arbitrary", "arbitrary")),
        name="mla_paged",
    )(page_table, q, q, k_new, c_new, *[w.arr for w in wts], ckv_pool, kr_pool_t)


def _mla_up_kernel(n_seq, pc_ref, h_ref, wuvs_ref, wout_ref, wgate_ref, y_ref):
    nq = SAMPLE_ROWS
    o = jnp.zeros((n_seq * nq, MLA_H * MLA_V), F32)
    for hd in range(MLA_H):
        pch = pc_ref[:, hd * nq:(hd + 1) * nq, :].reshape(n_seq * nq, MLA_RANK)
        o = o + _dot(pch.astype(BF16), wuvs_ref[hd])
    y_ref[...] = _dot(o.astype(BF16), wout_ref[...]) * jax.nn.sigmoid(_dot(h_ref[...], wgate_ref[...]))


def _mla_up_proj_gate(pc, h, wuv_sel, w_out, w_gate, *, n_seq):
    rows = n_seq * SAMPLE_ROWS
    nr = MLA_H * SAMPLE_ROWS
    wts = (wuv_sel, w_out, w_gate)
    return pl.pallas_call(
        functools.partial(_mla_up_kernel, n_seq),
        grid=(1,),
        in_specs=[pl.BlockSpec((n_seq, nr, MLA_RANK), lambda i: (0, 0, 0)),
                  pl.BlockSpec((rows, D_MODEL), lambda i: (0, 0))] + [_wspec(w) for w in wts],
        out_specs=pl.BlockSpec((rows, D_MODEL), lambda i: (0, 0)),
        out_shape=jax.ShapeDtypeStruct((rows, D_MODEL), F32),
        compiler_params=_cparams(("arbitrary",)),
        name="mla_up_proj_gate",
    )(pc.reshape(n_seq, nr, MLA_RANK), h, *[w.arr for w in wts])


def _projgate_kernel(o_ref, h_ref, wout_ref, wgate_ref, y_ref):
    y_ref[...] = _dot(o_ref[...].astype(BF16), wout_ref[...]) * jax.nn.sigmoid(_dot(h_ref[...], wgate_ref[...]))


def _proj_gate(o, h, w_out, w_gate, *, tm):
    m, kdim = o.shape
    return pl.pallas_call(
        _projgate_kernel,
        grid=(m // tm,),
        in_specs=[pl.BlockSpec((tm, kdim), lambda i: (i, 0)),
                  pl.BlockSpec((tm, D_MODEL), lambda i: (i, 0)),
                  _wspec(w_out), _wspec(w_gate)],
        out_specs=pl.BlockSpec((tm, D_MODEL), lambda i: (i, 0)),
        out_shape=jax.ShapeDtypeStruct((m, D_MODEL), F32),
        compiler_params=_cparams(("parallel",)),
        name="proj_gate",
    )(o, h, w_out.arr, w_gate.arr)


def _pad_lanes(x, width):
    return jnp.pad(x, [(0, 0)] * (x.ndim - 1) + [(0, width - x.shape[-1])])


def _mla_head_layout(nope, r1, r2):
    z = jnp.zeros(nope.shape[:-1] + (MLA_LANES - MLA_QK,), nope.dtype)
    x = jnp.concatenate([r1, r2, nope, z], axis=-1)
    return x.reshape(x.shape[:-2] + (MLA_H * MLA_LANES,))


def _gain_layout(g):
    half = MLA_ROPE // 2
    return jnp.concatenate([g[MLA_NOPE:MLA_NOPE + half], g[MLA_NOPE + half:], g[:MLA_NOPE],
                            jnp.zeros((MLA_LANES - MLA_QK,), g.dtype)]).reshape(1, MLA_LANES)


def _rope_tables(pos):
    half = MLA_ROPE // 2
    inv = ROPE_THETA ** (-jnp.arange(half, dtype=F32) / half)
    ang = pos.astype(F32)[:, None] * inv
    cos, sin = jnp.cos(ang), jnp.sin(ang)
    n = pos.shape[0]
    cos_t = jnp.concatenate([cos, cos, jnp.ones((n, MLA_NOPE), F32), jnp.zeros((n, MLA_LANES - MLA_QK), F32)], -1)
    sin_t = jnp.concatenate([sin, sin, jnp.zeros((n, MLA_LANES - MLA_ROPE), F32)], -1)
    return cos_t, sin_t


def _layer_weights(p):
    w_in = p['w_in']
    sizes = (DN_CONV_W, DN_QK_W, DN_H, DN_H, SC_W, SC_W, SC_W, MLA_RANK, MLA_RANK, MLA_ROPE, MEM_W, 4 * D_MODEL)
    offs = np.concatenate([[0], np.cumsum(sizes)])
    seg = [w_in[:, offs[i]:offs[i + 1]] for i in range(len(sizes))]
    bf = lambda x: x.astype(BF16)
    row = lambda x: x.reshape(1, -1)
    gates = [bf(seg[11][:, i * D_MODEL:(i + 1) * D_MODEL]) for i in range(4)]
    half = MLA_ROPE // 2

    w = {}
    w['ffn1'] = (row(p['ffn1_norm']), bf(p['ffn1_w_gu']), bf(p['ffn1_w_down']))
    w['ffn2'] = (row(p['ffn2_norm']), bf(p['ffn2_w_gu']), bf(p['ffn2_w_down']))
    w['mix_norm'] = row(p['mix_norm'])
    w['w_o'] = bf(p['w_o'])
    w['dn'] = (bf(seg[0]), bf(seg[1]), bf(_pad_lanes(jnp.concatenate([seg[2], seg[3]], 1), 128)),
               p['dn_conv_w'], _pad_lanes(row(p['dn_A_log']), 128), _pad_lanes(row(p['dn_dt_bias']), 128),
               row(p['dn_norm']), bf(p['dn_w_out']), gates[0])
    w['sc'] = (bf(jnp.concatenate([seg[4], seg[5], seg[6]], 1)), p['sc_conv_w'], bf(p['sc_w_out']), gates[1])

    wq = p['mla_w_q_b'].reshape(MLA_RANK, MLA_H, MLA_QK)
    q_nope, q_r1, q_r2 = wq[..., :MLA_NOPE], wq[..., MLA_NOPE:MLA_NOPE + half], wq[..., MLA_NOPE + half:]
    wq_perm = _mla_head_layout(q_nope, q_r1, q_r2)
    wq_swap = _mla_head_layout(jnp.zeros_like(q_nope), -q_r2, q_r1)
    wkr = seg[9]
    wkr_pad = _pad_lanes(wkr, MLA_LANES)
    wkr_swap = _pad_lanes(jnp.concatenate([-wkr[:, half:], wkr[:, :half]], 1), MLA_LANES)
    wkv = p['mla_w_kv_b'].reshape(MLA_RANK, MLA_H, MLA_NOPE + MLA_V)
    w_uk, w_uv = wkv[..., :MLA_NOPE], wkv[..., MLA_NOPE:]
    zr = jnp.zeros((MLA_RANK, MLA_H, half), F32)
    wuk_perm = bf(_mla_head_layout(w_uk, zr, zr))
    k_gain = _gain_layout(p['mla_k_norm'])
    w['mla_proj'] = (bf(seg[7]), row(p['mla_q_norm_a']), bf(wq_perm), bf(wq_swap), _gain_layout(p['mla_q_norm']),
                     bf(seg[8]), row(p['mla_kv_norm_a']), bf(wkr_pad), bf(wkr_swap),
                     wuk_perm, bf(w_uv.reshape(MLA_RANK, MLA_H * MLA_V)), k_gain)
    eye = jnp.eye(MLA_H, dtype=F32)
    w['mla_wuv_sel'] = bf((w_uv[None] * eye[:, None, :, None]).reshape(MLA_H, MLA_RANK, MLA_H * MLA_V))
    w['mla_sample'] = (wuk_perm, bf(w_uk.reshape(MLA_RANK, MLA_H * MLA_NOPE).T), k_gain)
    w['mla_out'] = (bf(p['mla_w_out']), gates[2])
    w['mem_kv'] = (row(p['mem_norm']), bf(p['mem_w_kv']), row(p['mem_k_norm']))
    w['mem'] = (bf(seg[10]), row(p['mem_q_norm']), bf(p['mem_w_out']), gates[3])
    return w


def _group_layer(x, w, *, n_seq, t_pad, t_valid, tm, bb_n, tt, chunk, dn_state, sc_state, mem_kv, cos, sin,
                 n_tab, mla_attend, q_dtype):
    cfg = dict(n_seq=n_seq, t_pad=t_pad, bb_n=bb_n, tt=tt)
    x1, h = _ffn(x, *w['ffn1'], tm=tm, h_gain=w['mix_norm'])
    y_dn, dn_s, dn_c = _deltanet(h, dn_state[0], dn_state[1], w['dn'], t_valid=t_valid, chunk=chunk, **cfg)
    y_sc, sc_c = _shortconv(h, sc_state, w['sc'], t_valid=t_valid, **cfg)
    q, ckv, kr, k, v = _mla_proj(h, cos, sin, w['mla_proj'], tm=tm, n_tab=n_tab, qk_dtype=q_dtype)
    if mla_attend is None:
        o = _mla_prompt_attn(q, k, v, n_seq=n_seq, seq=t_pad, tq=tm)
        y_mla = _proj_gate(o, h, *w['mla_out'], tm=tm)
    else:
        y_mla = mla_attend(q, k, ckv, h)
    y_mem = _mem_attn(h, mem_kv[0], mem_kv[1], w['mem'], **cfg)
    x3 = _ffn(x1, *w['ffn2'], tm=min(tm, 256), merge=((y_dn, y_sc, y_mla, y_mem), w['w_o']))
    return x3, dn_s, dn_c, sc_c, ckv, kr


def kernel(x_prompt, x_sample, state_dn_S, state_dn_conv, state_sc_conv, cache_mla_ckv, cache_mla_krope, cache_mem_k, cache_mem_v, page_table, mem_prompt, ffn1_norm, ffn1_w_gu, ffn1_w_down, mix_norm, w_in, dn_conv_w, dn_A_log, dn_dt_bias, dn_norm, dn_w_out, sc_conv_w, sc_w_out, mla_q_norm_a, mla_w_q_b, mla_kv_norm_a, mla_w_kv_b, mla_q_norm, mla_k_norm, mla_w_out, mem_norm, mem_w_kv, mem_q_norm, mem_k_norm, mem_w_out, w_o, ffn2_norm, ffn2_w_gu, ffn2_w_down):
    params = dict(ffn1_norm=ffn1_norm, ffn1_w_gu=ffn1_w_gu, ffn1_w_down=ffn1_w_down, mix_norm=mix_norm, w_in=w_in,
                  dn_conv_w=dn_conv_w, dn_A_log=dn_A_log, dn_dt_bias=dn_dt_bias, dn_norm=dn_norm, dn_w_out=dn_w_out,
                  sc_conv_w=sc_conv_w, sc_w_out=sc_w_out, mla_q_norm_a=mla_q_norm_a, mla_w_q_b=mla_w_q_b,
                  mla_kv_norm_a=mla_kv_norm_a, mla_w_kv_b=mla_w_kv_b, mla_q_norm=mla_q_norm, mla_k_norm=mla_k_norm,
                  mla_w_out=mla_w_out, mem_norm=mem_norm, mem_w_kv=mem_w_kv, mem_q_norm=mem_q_norm,
                  mem_k_norm=mem_k_norm, mem_w_out=mem_w_out, w_o=w_o, ffn2_norm=ffn2_norm, ffn2_w_gu=ffn2_w_gu,
                  ffn2_w_down=ffn2_w_down)
    depth = w_in.shape[0]
    bp, seq, _ = x_prompt.shape
    bs, td, _ = x_sample.shape
    tds = SAMPLE_ROWS
    n_pages = page_table.shape[1]
    past = n_pages * PAGE
    krope_t = jnp.transpose(cache_mla_krope, (0, 1, 3, 2))

    cos_p, sin_p = _rope_tables(jnp.arange(seq))
    cos_s, sin_s = _rope_tables(past + jnp.arange(tds))
    cos_s, sin_s = jnp.tile(cos_s, (bs, 1)), jnp.tile(sin_s, (bs, 1))

    xp = x_prompt.reshape(bp * seq, D_MODEL)
    xs = jnp.pad(x_sample, ((0, 0), (0, tds - td), (0, 0))).reshape(bs * tds, D_MODEL)
    zero_s = jnp.zeros((bp, DN_H, DN_DK, DN_DK), F32)
    zero_dc = jnp.zeros((bp, DN_CONV - 1, DN_CONV_W), F32)
    zero_sc = jnp.zeros((bp, SC_CONV - 1, SC_W), F32)
    mem2d = mem_prompt.reshape(bp * N_MEM, D_MODEL)

    outs = {k: [] for k in ('pS', 'pdc', 'psc', 'pckv', 'pkr', 'pmk', 'pmv', 'sS', 'sdc', 'ssc', 'sckv', 'skr')}
    tm_p = 512
    w_all = jax.vmap(_layer_weights)(params)
    for l in range(depth):
        w = jax.tree.map(lambda a: _W(a, l), w_all)
        mk, mv = _mem_kv(mem2d, w['mem_kv'], tm=tm_p)
        mk3, mv3 = mk.reshape(bp, N_MEM, MEM_W), mv.reshape(bp, N_MEM, MEM_W)
        xp, s_p, dc_p, sc_p, ckv_p, kr_p = _group_layer(
            xp, w, n_seq=bp, t_pad=seq, t_valid=seq, tm=tm_p, bb_n=1, tt=tm_p, chunk=DN_CHUNK,
            dn_state=(zero_s, zero_dc), sc_state=zero_sc, mem_kv=(mk3, mv3), cos=cos_p, sin=sin_p,
            n_tab=seq // tm_p, mla_attend=None, q_dtype=BF16)
        outs['pS'].append(s_p); outs['pdc'].append(dc_p); outs['psc'].append(sc_p)
        outs['pckv'].append(ckv_p.reshape(bp, seq, MLA_RANK)); outs['pkr'].append(kr_p.reshape(bp, seq, MLA_ROPE))
        outs['pmk'].append(mk.reshape(bp, N_MEM, MEM_H, MEM_HD)); outs['pmv'].append(mv.reshape(bp, N_MEM, MEM_H, MEM_HD))

        def attend(q, k, ckv, h, l=l, w=w):
            pc = _mla_sample_attn(page_table, q, k, ckv, w['mla_sample'], cache_mla_ckv, krope_t, l,
                                  n_seq=bs, t_valid=td, n_pg=32)
            return _mla_up_proj_gate(pc, h, w['mla_wuv_sel'], *w['mla_out'], n_seq=bs)

        xs, s_s, dc_s, sc_s, ckv_s, kr_s = _group_layer(
            xs, w, n_seq=bs, t_pad=tds, t_valid=td, tm=bs * tds, bb_n=8, tt=tds, chunk=tds,
            dn_state=(state_dn_S[l], state_dn_conv[l]), sc_state=state_sc_conv[l],
            mem_kv=(cache_mem_k[l].reshape(bs, N_MEM, MEM_W), cache_mem_v[l].reshape(bs, N_MEM, MEM_W)),
            cos=cos_s, sin=sin_s, n_tab=1, mla_attend=attend, q_dtype=F32)
        outs['sS'].append(s_s); outs['sdc'].append(dc_s); outs['ssc'].append(sc_s)
        outs['sckv'].append(ckv_s.reshape(bs, tds, MLA_RANK)[:, :td])
        outs['skr'].append(kr_s.reshape(bs, tds, MLA_ROPE)[:, :td])

    st = lambda k: jnp.stack(outs[k])
    y_prompt = xp.reshape(bp, seq, D_MODEL)
    y_sample = xs.reshape(bs, tds, D_MODEL)[:, :td]
    return (y_prompt, y_sample, st('pS'), st('pdc'), st('psc'), st('pckv'), st('pkr'), st('pmk'), st('pmv'),
            st('sS'), st('sdc'), st('ssc'), st('sckv'), st('skr'))
```

```python
import functools

import numpy as np
import jax
import jax.numpy as jnp
from jax import lax
from jax.experimental import pallas as pl
from jax.experimental.pallas import tpu as pltpu

F32 = jnp.float32
BF16 = jnp.bfloat16

D_MODEL = 1024
D_FF = 2816
EPS = 1e-6
N_MEM = 256
PAGE = 128
DN_H = 4
DN_DK = 128
DN_QK_W = 512
DN_CONV_W = 1536
DN_CONV = 4
DN_CHUNK = 64
DN_UNROLL_GROUPS = 4
DN_GROUP = 4
SC_W = 512
SC_CONV = 3
MLA_H = 8
MLA_RANK = 256
MLA_NOPE = 64
MLA_ROPE = 32
MLA_V = 64
MLA_QK = 96
MLA_LANES = 128
ROPE_THETA = 10000.0
LOG2E = 1.4426950408889634
MEM_H = 4
MEM_HD = 128
MEM_W = 512

VMEM_LIMIT = 56 * 1024 * 1024


def _cparams(sem):
    return pltpu.CompilerParams(dimension_semantics=sem, vmem_limit_bytes=VMEM_LIMIT)


class _W:
    def __init__(self, arr, layer):
        self.arr, self.layer = arr, layer


def _wspec(w, resident=False):
    shape = w.arr.shape[1:]
    index = lambda *_: (w.layer,) + (0,) * len(shape)
    if resident:
        return pl.BlockSpec((None,) + shape, index, pipeline_mode=pl.Buffered(1))
    return pl.BlockSpec((None,) + shape, index)


def _rms(x, g):
    ms = jnp.mean(x * x, axis=-1, keepdims=True)
    return x * lax.rsqrt(ms + EPS) * g


def _silu(x):
    return x * jax.nn.sigmoid(x)


def _rnd(x):
    return x.astype(BF16).astype(F32)


def _dot(a, b):
    return jnp.dot(a, b, preferred_element_type=F32)


def _dot_nt(a, b):
    return lax.dot_general(a, b, (((1,), (1,)), ((), ())), preferred_element_type=F32)


def _mm(a, b, small):
    if small:
        return _dot(_rnd(a), _rnd(b))
    return _dot(a.astype(BF16), b.astype(BF16))


def _mm_nt(a, b, small):
    if small:
        return _dot_nt(_rnd(a), _rnd(b))
    return _dot_nt(a.astype(BF16), b.astype(BF16))


def _mm_tn(a, b, small):
    dn = (((0,), (0,)), ((), ()))
    if small:
        return lax.dot_general(_rnd(a), _rnd(b), dn, preferred_element_type=F32)
    return lax.dot_general(a.astype(BF16), b.astype(BF16), dn, preferred_element_type=F32)


def _run_interleaved(gens):
    gens = list(gens)
    while gens:
        for g in list(gens):
            try:
                next(g)
            except StopIteration:
                gens.remove(g)


def _split2(x):
    hi = x.astype(BF16)
    lo = (x - hi.astype(F32)).astype(BF16)
    return hi, lo


def _mm_hi(a, b, small):
    if small:
        return jnp.dot(a, b, preferred_element_type=F32, precision=lax.Precision.HIGHEST)
    ah, al = _split2(a)
    bh, bl = _split2(b)
    return _dot(ah, bh) + (_dot(ah, bl) + _dot(al, bh))


def _mm_exact_left(lmat, b, small):
    if small:
        return jnp.dot(lmat, b, preferred_element_type=F32, precision=lax.Precision.HIGHEST)
    lb = lmat.astype(BF16)
    b1 = b.astype(BF16)
    r1 = b - b1.astype(F32)
    b2 = r1.astype(BF16)
    b3 = (r1 - b2.astype(F32)).astype(BF16)
    return _dot(lb, b1) + (_dot(lb, b2) + _dot(lb, b3))


FFN_SPLIT = 2


def _ffn_kernel(merge, emit_h, *refs):
    it = iter(refs)
    x_ref = next(it)
    if merge:
        y_refs = [next(it) for _ in range(4)]
        wo_ref = next(it)
    g_ref, wgu_ref, wd_ref = next(it), next(it), next(it)
    if emit_h:
        g2_ref = next(it)
    o_ref = next(it)
    if emit_h:
        h_ref = next(it)

    x = x_ref[...]
    if merge:
        m = ((y_refs[0][...] + y_refs[1][...]) + y_refs[2][...]) + y_refs[3][...]
        x = x + _dot(m.astype(BF16), wo_ref[...])
    h = _rms(x, g_ref[...]).astype(BF16)
    tf = D_FF // FFN_SPLIT
    def gate_up(j):
        return (_dot(h, wgu_ref[:, j * tf:(j + 1) * tf]),
                _dot(h, wgu_ref[:, D_FF + j * tf:D_FF + (j + 1) * tf]))

    acc = None
    nxt = gate_up(0)
    for j in range(FFN_SPLIT):
        gate, up = nxt
        if j + 1 < FFN_SPLIT:
            nxt = gate_up(j + 1)
        a = (_silu(gate) * up).astype(BF16)
        d = _dot(a, wd_ref[j * tf:(j + 1) * tf, :])
        acc = d if acc is None else acc + d
    out = x + 0.5 * acc
    o_ref[...] = out
    if emit_h:
        h_ref[...] = _rms(out, g2_ref[...]).astype(BF16)


def _ffn(x, norm_g, w_gu, w_down, *, tm, merge=None, h_gain=None):
    m = x.shape[0]
    row = pl.BlockSpec((tm, D_MODEL), lambda i: (i, 0))
    in_specs, args = [row], [x]
    if merge is not None:
        ys, w_o = merge
        in_specs += [row] * 4 + [_wspec(w_o, resident=True)]
        args += list(ys) + [w_o.arr]
    in_specs += [_wspec(w, resident=True) for w in (norm_g, w_gu, w_down)]
    args += [norm_g.arr, w_gu.arr, w_down.arr]
    out_shape = [jax.ShapeDtypeStruct((m, D_MODEL), F32)]
    out_specs = [row]
    if h_gain is not None:
        in_specs.append(_wspec(h_gain, resident=True))
        args.append(h_gain.arr)
        out_shape.append(jax.ShapeDtypeStruct((m, D_MODEL), BF16))
        out_specs.append(row)
    res = pl.pallas_call(
        functools.partial(_ffn_kernel, merge is not None, h_gain is not None),
        grid=(m // tm,),
        in_specs=in_specs, out_specs=out_specs, out_shape=out_shape,
        compiler_params=_cparams(("parallel",)),
        name="ffn_merge" if merge is not None else "ffn",
    )(*args)
    return res if h_gain is not None else res[0]


def _dn_kernel(bb_n, tt, chunk, tv_last, n_t, small,
               h_ref, s0_ref, cprev_ref, wqkv_ref, wz_ref, wab_ref, convw_ref, alog_ref, dtb_ref,
               normg_ref, wout_ref, wgate_ref,
               y_ref, snew_ref, cnew_ref,
               xbuf, cs, gb, zb, ob, s_all, u_s, w_s, qg_s, kdec_s, aqk_s, gl_s, gate_s):
    t = pl.program_id(1)
    rows = bb_n * tt

    @pl.when(t == 0)
    def _():
        xbuf[:, 5:8, :] = cprev_ref[...]
        snew_ref[...] = s0_ref[...]

    h = h_ref[...]
    qkv = _dot(h, wqkv_ref[...])
    ab = _dot(h, wab_ref[...])
    zs = _dot(h, wz_ref[...])
    gate = _dot(h, wgate_ref[...])
    xbuf[:, 8:8 + tt, :] = qkv.reshape(bb_n, tt, DN_CONV_W)
    c = 0.0
    for j in range(DN_CONV):
        c = c + xbuf[:, 5 + j:5 + j + tt, :] * convw_ref[j:j + 1, :].reshape(1, 1, DN_CONV_W)
    cnew_ref[...] = xbuf[:, 5 + tv_last:8 + tv_last, :]
    if n_t > 1:
        xbuf[:, 0:8, :] = xbuf[:, tt:tt + 8, :]

    g = -jnp.exp(alog_ref[...]) * jax.nn.softplus(ab + dtb_ref[...])
    lane = lax.broadcasted_iota(jnp.int32, (rows, 128), 1)
    gbv = jnp.where(lane < DN_H, g, jax.nn.sigmoid(ab)).reshape(bb_n, tt, 128)
    if tv_last < tt:
        trow = lax.broadcasted_iota(jnp.int32, (bb_n, tt, 128), 1)
        gbv = jnp.where(trow < tv_last, gbv, 0.0)
    gb[...] = gbv
    c = _silu(c)
    for grp in range(8):
        sl = slice(grp * 128, (grp + 1) * 128)
        xg = c[:, :, sl]
        xn = xg * lax.rsqrt(jnp.sum(xg * xg, axis=-1, keepdims=True) + EPS)
        if grp < DN_H:
            xn = xn * (DN_DK ** -0.5)
        cs[:, :, sl] = xn
    cs[:, :, 2 * DN_QK_W:] = c[:, :, 2 * DN_QK_W:]
    zb[...] = _silu(zs).reshape(bb_n, tt, DN_QK_W)
    gate_s[...] = jax.nn.sigmoid(gate)

    n4 = DN_H * chunk
    ri = lax.broadcasted_iota(jnp.int32, (n4, n4), 0)
    ci = lax.broadcasted_iota(jnp.int32, (n4, n4), 1)
    same = (ri // chunk) == (ci // chunk)
    incl = same & (ci <= ri)
    strict = same & (ci < ri)
    lmat = incl.astype(F32)
    umat = strict.astype(F32)
    vmask = (lax.broadcasted_iota(jnp.int32, (n4, DN_H * 128), 0) // chunk
             == lax.broadcasted_iota(jnp.int32, (n4, DN_H * 128), 1) // 128)
    n_pow = int(np.log2(chunk))
    normg = normg_ref[...]
    n_ch = tt // chunk
    total = bb_n * n_ch
    solve_mm = _mm_hi if small else _mm

    for hd in range(DN_H):
        s_all[:, :, hd * 128:(hd + 1) * 128] = snew_ref[:, hd]

    def stack_rows(ref, b, r0, off):
        return jnp.concatenate([ref[b, pl.ds(r0, chunk), off + hd * 128:off + (hd + 1) * 128]
                                for hd in range(DN_H)], axis=0)

    def level1(k, slot):
        b = k // n_ch
        r0 = (k % n_ch) * chunk
        if not isinstance(k, int):
            r0 = pl.multiple_of(r0, chunk)
        q = stack_rows(cs, b, r0, 0)
        kk = stack_rows(cs, b, r0, DN_QK_W)
        v = stack_rows(cs, b, r0, 2 * DN_QK_W)
        gbc = gb[b, pl.ds(r0, chunk), :]
        g_st = jnp.concatenate([jnp.broadcast_to(gbc[:, hd:hd + 1], (chunk, 128)) for hd in range(DN_H)], axis=0)
        beta_st = jnp.concatenate([jnp.broadcast_to(gbc[:, DN_H + hd:DN_H + hd + 1], (chunk, 128))
                                   for hd in range(DN_H)], axis=0)
        g_sq = jnp.concatenate([g_st] * (n4 // 128), axis=1) if n4 >= 128 else g_st[:, :n4]
        gc = _mm_exact_left(lmat, g_st, small)
        yield
        if small:
            dmat = _mm_exact_left(lmat, g_sq * umat, small)
        else:
            gct = gc.T
            dmat = (jnp.concatenate([gc] * (n4 // 128), axis=1)
                    - jnp.concatenate([gct] * (n4 // 128), axis=0))
        gam = jnp.where(incl, jnp.exp(dmat), 0.0)
        eg = jnp.exp(gc)
        kb = kk * beta_st
        kq = _mm_nt(jnp.concatenate([kb, q], axis=0), kk, small)
        yield
        a_mat = jnp.where(strict, kq[:n4] * gam, 0.0)
        x = jnp.concatenate([v * beta_st, kb * eg], axis=1)
        p = -a_mat
        for i in range(n_pow):
            x = x + (_mm_hi if i < 2 else solve_mm)(p, x, small)
            if i < n_pow - 1:
                p = (_mm_hi if i < 1 else solve_mm)(p, p, small)
            yield
        gc_last = [gc[(hd + 1) * chunk - 1:(hd + 1) * chunk, :] for hd in range(DN_H)]
        gl_st = jnp.concatenate([jnp.broadcast_to(r, (chunk, 128)) for r in gc_last], axis=0)
        u_s[slot] = x[:, :128]
        w_s[slot] = x[:, 128:]
        qg_s[slot] = q * eg
        kdec_s[slot] = kk * jnp.exp(gl_st - gc)
        aqk_s[slot] = jnp.where(incl, kq[n4:] * gam, 0.0)
        gl_s[slot] = jnp.exp(jnp.concatenate(gc_last, axis=1))

    def level2(k, slot):
        b = k // n_ch
        r0 = (k % n_ch) * chunk
        if not isinstance(k, int):
            r0 = pl.multiple_of(r0, chunk)
        s_old = s_all[b]
        w = w_s[slot]
        qg = qg_s[slot]
        ws, qs = [], []
        for hd in range(DN_H):
            rs = slice(hd * chunk, (hd + 1) * chunk)
            r = _mm(jnp.concatenate([w[rs], qg[rs]], axis=0), s_old[:, hd * 128:(hd + 1) * 128], small)
            ws.append(r[:chunk])
            qs.append(r[chunk:])
        yield
        v_new = u_s[slot] - jnp.concatenate(ws, axis=0)
        o = jnp.concatenate(qs, axis=0) + _mm(aqk_s[slot], v_new, small)
        vbd = jnp.where(vmask, jnp.concatenate([v_new] * DN_H, axis=1), 0.0)
        s_all[b] = s_old * gl_s[slot] + _mm_tn(kdec_s[slot], vbd, small)
        yield
        on = _rms(o, normg)
        for hd in range(DN_H):
            sl = slice(hd * 128, (hd + 1) * 128)
            ob[b, pl.ds(r0, chunk), sl] = on[hd * chunk:(hd + 1) * chunk] * zb[b, pl.ds(r0, chunk), sl]

    assert total % DN_GROUP == 0

    def recurrence(m, base):
        for i in range(DN_GROUP):
            yield from level2(m * DN_GROUP + i, base + i)

    _run_interleaved([level1(i, i) for i in range(DN_GROUP)])

    n_groups = total // DN_GROUP

    def group_body(m, _):
        base = (m % 2) * DN_GROUP
        chains = [recurrence(m, base)]
        if not isinstance(m, int):
            chains += [level1(jnp.minimum((m + 1) * DN_GROUP + i, total - 1), DN_GROUP - base + i)
                       for i in range(DN_GROUP)]
        elif m + 1 < n_groups:
            chains += [level1((m + 1) * DN_GROUP + i, DN_GROUP - base + i) for i in range(DN_GROUP)]
        _run_interleaved(chains)
        return 0

    if n_groups <= DN_UNROLL_GROUPS:
        for m in range(n_groups):
            group_body(m, 0)
    else:
        lax.fori_loop(0, n_groups, group_body, 0)

    for hd in range(DN_H):
        snew_ref[:, hd] = s_all[:, :, hd * 128:(hd + 1) * 128]

    y = _dot(ob[...].reshape(rows, DN_QK_W).astype(BF16), wout_ref[...])
    y_ref[...] = y * gate_s[...]


def _deltanet(h, s0, cprev, wts, *, n_seq, t_pad, t_valid, bb_n, tt, chunk):
    n_t = t_pad // tt
    assert n_t == 1 or t_valid == t_pad
    tv_last = t_valid - (n_t - 1) * tt
    rows = bb_n * tt
    small = chunk < 16
    n4 = DN_H * chunk
    in_specs = [
        pl.BlockSpec((rows, D_MODEL), lambda b, t: (b * n_t + t, 0)),
        pl.BlockSpec((bb_n, DN_H, DN_DK, DN_DK), lambda b, t: (b, 0, 0, 0)),
        pl.BlockSpec((bb_n, DN_CONV - 1, DN_CONV_W), lambda b, t: (b, 0, 0)),
    ] + [_wspec(w) for w in wts]
    out_specs = [
        pl.BlockSpec((rows, D_MODEL), lambda b, t: (b * n_t + t, 0)),
        pl.BlockSpec((bb_n, DN_H, DN_DK, DN_DK), lambda b, t: (b, 0, 0, 0)),
        pl.BlockSpec((bb_n, DN_CONV - 1, DN_CONV_W), lambda b, t: (b, 0, 0)),
    ]
    out_shape = [
        jax.ShapeDtypeStruct((n_seq * t_pad, D_MODEL), F32),
        jax.ShapeDtypeStruct((n_seq, DN_H, DN_DK, DN_DK), F32),
        jax.ShapeDtypeStruct((n_seq, DN_CONV - 1, DN_CONV_W), F32),
    ]
    return pl.pallas_call(
        functools.partial(_dn_kernel, bb_n, tt, chunk, tv_last, n_t, small),
        grid=(n_seq // bb_n, n_t),
        in_specs=in_specs, out_specs=out_specs, out_shape=out_shape,
        scratch_shapes=[pltpu.VMEM((bb_n, tt + 8, DN_CONV_W), F32), pltpu.VMEM((bb_n, tt, DN_CONV_W), F32),
                        pltpu.VMEM((bb_n, tt, 128), F32), pltpu.VMEM((bb_n, tt, DN_QK_W), F32),
                        pltpu.VMEM((bb_n, tt, DN_QK_W), F32),
                        pltpu.VMEM((bb_n, DN_DK, DN_H * 128), F32)]
                       + [pltpu.VMEM((2 * DN_GROUP, n4, 128), F32)] * 4
                       + [pltpu.VMEM((2 * DN_GROUP, n4, n4), F32), pltpu.VMEM((2 * DN_GROUP, 1, DN_H * 128), F32),
                          pltpu.VMEM((rows, D_MODEL), F32)],
        compiler_params=_cparams(("parallel", "arbitrary")),
        name="deltanet",
    )(h, s0, cprev, *[w.arr for w in wts])


def _sc_kernel(bb_n, tt, tv_last, n_t,
               h_ref, prev_ref, win_ref, convw_ref, wout_ref, wgate_ref,
               y_ref, new_ref, ubuf):
    t = pl.program_id(1)
    rows = bb_n * tt

    @pl.when(t == 0)
    def _():
        ubuf[:, 6:8, :] = prev_ref[...]

    h = h_ref[...]
    p = _dot(h, win_ref[...])
    gate = jax.nn.sigmoid(_dot(h, wgate_ref[...]))
    bgate = p[:, :SC_W]
    u = p[:, SC_W:2 * SC_W] * p[:, 2 * SC_W:]
    ubuf[:, 8:8 + tt, :] = u.reshape(bb_n, tt, SC_W)
    y = 0.0
    for j in range(SC_CONV):
        y = y + ubuf[:, 6 + j:6 + j + tt, :] * convw_ref[j:j + 1, :].reshape(1, 1, SC_W)

    new_ref[...] = ubuf[:, 6 + tv_last:8 + tv_last, :]
    if n_t > 1:
        ubuf[:, 0:8, :] = ubuf[:, tt:tt + 8, :]

    z = (bgate * y.reshape(rows, SC_W)).astype(BF16)
    y_ref[...] = _dot(z, wout_ref[...]) * gate


def _memkv_kernel(m_ref, g_ref, wkv_ref, kg_ref, k_ref, v_ref):
    n = _rms(m_ref[...], g_ref[...]).astype(BF16)
    kv = _dot(n, wkv_ref[...])
    kg = kg_ref[...]
    for hd in range(MEM_H):
        sl = slice(hd * MEM_HD, (hd + 1) * MEM_HD)
        k_ref[:, sl] = _rms(kv[:, sl], kg)
    v_ref[...] = kv[:, MEM_W:]


def _mem_kv(mem2d, wts, *, tm):
    m = mem2d.shape[0]
    return pl.pallas_call(
        _memkv_kernel,
        grid=(m // tm,),
        in_specs=[pl.BlockSpec((tm, D_MODEL), lambda i: (i, 0))] + [_wspec(w) for w in wts],
        out_specs=[pl.BlockSpec((tm, MEM_W), lambda i: (i, 0))] * 2,
        out_shape=[jax.ShapeDtypeStruct((m, MEM_W), F32)] * 2,
        compiler_params=_cparams(("parallel",)),
        name="mem_kv",
    )(mem2d, *[w.arr for w in wts])


def _memattn_kernel(bb_n, tt, small,
                    h_ref, mk_ref, mv_ref, wq_ref, qg_ref, wout_ref, wgate_ref,
                    y_ref, qs, ob):
    rows = bb_n * tt
    h = h_ref[...]
    q = _dot(h, wq_ref[...])
    gate = jax.nn.sigmoid(_dot(h, wgate_ref[...]))
    qg = qg_ref[...]
    for hd in range(MEM_H):
        sl = slice(hd * MEM_HD, (hd + 1) * MEM_HD)
        qs[:, :, sl] = _rms(q[:, sl], qg).reshape(bb_n, tt, MEM_HD)

    def seq_body(b, _):
        def scores(hd):
            sl = slice(hd * MEM_HD, (hd + 1) * MEM_HD)
            return _mm_nt(qs[b, :, sl], mk_ref[b, :, sl], small)

        s_next = scores(0)
        for hd in range(MEM_H):
            sl = slice(hd * MEM_HD, (hd + 1) * MEM_HD)
            s = s_next * (MEM_HD ** -0.5)
            if hd + 1 < MEM_H:
                s_next = scores(hd + 1)
            s = s - jnp.max(s, axis=-1, keepdims=True)
            e = jnp.exp(s)
            p = e / jnp.sum(e, axis=-1, keepdims=True)
            ob[b, :, sl] = _mm(p, mv_ref[b, :, sl], small)
        return 0

    if bb_n == 1:
        seq_body(0, 0)
    else:
        lax.fori_loop(0, bb_n, seq_body, 0)
    y = _dot(ob[...].reshape(rows, MEM_W).astype(BF16), wout_ref[...])
    y_ref[...] = y * gate


def _mlaproj_kernel(h_ref, cos_ref, sin_ref, wq_ref, qna_ref, wqp_ref, wqs_ref, qg_ref,
                    wkv_ref, kvna_ref, wkr_ref, wkrs_ref, wuk_ref, wuv_ref, kg_ref,
                    q_ref, ckv_ref, kr_ref, k_ref, v_ref):
    h = h_ref[...]
    cos = cos_ref[...]
    sin = sin_ref[...]
    cq = _dot(h, wq_ref[...])
    ckv_raw = _dot(h, wkv_ref[...])
    kr_a = _dot(h, wkr_ref[...])
    kr_b = _dot(h, wkrs_ref[...])
    cqn = _rms(cq, qna_ref[...]).astype(BF16)
    q_raw = _dot(cqn, wqp_ref[...])
    q_swp = _dot(cqn, wqs_ref[...])
    ckv = _rms(ckv_raw, kvna_ref[...])
    cb = ckv.astype(BF16)
    k_raw = _dot(cb, wuk_ref[...])
    v_ref[...] = _dot(cb, wuv_ref[...]).astype(BF16)
    ckv_ref[...] = ckv
    krp = kr_a * cos + kr_b * sin
    kr_ref[...] = krp[:, :MLA_ROPE]

    qg = qg_ref[...]
    inv_n = 1.0 / MLA_QK
    for hd in range(MLA_H):
        sl = slice(hd * MLA_LANES, (hd + 1) * MLA_LANES)
        qh = q_raw[:, sl] * cos + q_swp[:, sl] * sin
        ms = jnp.sum(qh * qh, axis=-1, keepdims=True) * inv_n
        q_ref[:, sl] = (qh * lax.rsqrt(ms + EPS) * qg).astype(q_ref.dtype)

    kg = kg_ref[...]
    for hd in range(MLA_H):
        sl = slice(hd * MLA_LANES, (hd + 1) * MLA_LANES)
        kh = k_raw[:, sl] + krp
        ms = jnp.sum(kh * kh, axis=-1, keepdims=True) * inv_n
        k_ref[:, sl] = (kh * lax.rsqrt(ms + EPS) * kg).astype(k_ref.dtype)


def _branches_kernel(bb_n, tt, tv_last, n_t, n_sc, n_mem, *refs):
    it = iter(refs)
    h_ref, prev_ref, mk_ref, mv_ref, cos_ref, sin_ref = (next(it) for _ in range(6))
    sc_w = [next(it) for _ in range(n_sc)]
    mem_w = [next(it) for _ in range(n_mem)]
    mla_w = [next(it) for _ in range(12)]
    y_sc, sc_new, y_mem, q_ref, ckv_ref, kr_ref, k_ref, v_ref = (next(it) for _ in range(8))
    ubuf, qs, ob = next(it), next(it), next(it)
    _sc_kernel(bb_n, tt, tv_last, n_t, h_ref, prev_ref, *sc_w, y_sc, sc_new, ubuf)
    _memattn_kernel(bb_n, tt, tt < 16, h_ref, mk_ref, mv_ref, *mem_w, y_mem, qs, ob)
    _mlaproj_kernel(h_ref, cos_ref, sin_ref, *mla_w, q_ref, ckv_ref, kr_ref, k_ref, v_ref)


def _branches(h, sc_prev, mk, mv, cos, sin, sc_wts, mem_wts, mla_wts, *, n_seq, t_pad, t_valid, bb_n, tt, qk_dtype):
    n_t = t_pad // tt
    assert n_t == 1 or t_valid == t_pad
    tv_last = t_valid - (n_t - 1) * tt
    rows = bb_n * tt
    m = n_seq * t_pad
    n_tab = cos.shape[0] // rows
    hw = MLA_H * MLA_LANES
    vw = MLA_H * MLA_V
    tile = lambda width: pl.BlockSpec((rows, width), lambda b, t: (b * n_t + t, 0))
    table = pl.BlockSpec((rows, MLA_LANES), lambda b, t: ((b * n_t + t) % n_tab, 0))
    per_seq = lambda *shape: pl.BlockSpec((bb_n,) + shape, lambda b, t: (b,) + (0,) * len(shape))
    wts = list(sc_wts) + list(mem_wts) + list(mla_wts)
    return pl.pallas_call(
        functools.partial(_branches_kernel, bb_n, tt, tv_last, n_t, len(sc_wts), len(mem_wts)),
        grid=(n_seq // bb_n, n_t),
        in_specs=[tile(D_MODEL), per_seq(SC_CONV - 1, SC_W), per_seq(N_MEM, MEM_W), per_seq(N_MEM, MEM_W),
                  table, table] + [_wspec(w, resident=True) for w in wts],
        out_specs=[tile(D_MODEL), per_seq(SC_CONV - 1, SC_W), tile(D_MODEL),
                   tile(hw), tile(MLA_RANK), tile(MLA_ROPE), tile(hw), tile(vw)],
        out_shape=[jax.ShapeDtypeStruct((m, D_MODEL), F32),
                   jax.ShapeDtypeStruct((n_seq, SC_CONV - 1, SC_W), F32),
                   jax.ShapeDtypeStruct((m, D_MODEL), F32),
                   jax.ShapeDtypeStruct((m, hw), qk_dtype),
                   jax.ShapeDtypeStruct((m, MLA_RANK), F32),
                   jax.ShapeDtypeStruct((m, MLA_ROPE), F32),
                   jax.ShapeDtypeStruct((m, hw), qk_dtype),
                   jax.ShapeDtypeStruct((m, vw), BF16)],
        scratch_shapes=[pltpu.VMEM((bb_n, tt + 8, SC_W), F32),
                        pltpu.VMEM((bb_n, tt, MEM_W), F32), pltpu.VMEM((bb_n, tt, MEM_W), F32)],
        compiler_params=_cparams(("parallel", "arbitrary")),
        name="branches",
    )(h, sc_prev, mk, mv, cos, sin, *[w.arr for w in wts])


def _flash_kernel(tq, q_ref, k_ref, v_ref, o_ref, m_scr, l_scr, acc_scr):
    qi = pl.program_id(1)
    ki = pl.program_id(2)
    c2 = (MLA_QK ** -0.5) * LOG2E

    @pl.when(ki == 0)
    def _():
        m_scr[...] = jnp.full(m_scr.shape, -jnp.inf, F32)
        l_scr[...] = jnp.zeros_like(l_scr)
        acc_scr[...] = jnp.zeros_like(acc_scr)

    def compute(diag):
        half = tq // 2
        blocks = [(0, half, half), (half, tq, tq)] if diag else [(0, tq, tq)]
        keep = {}
        if diag:
            for r0, r1, nk in blocks:
                row = lax.broadcasted_iota(jnp.int32, (r1 - r0, nk), 0) + r0
                col = lax.broadcasted_iota(jnp.int32, (r1 - r0, nk), 1)
                keep[r0] = col <= row
        ones = jnp.ones((tq, 128), BF16)
        work = [(hd, blk) for hd in range(MLA_H) for blk in blocks]

        def qk(hd, blk):
            r0, r1, nk = blk
            sl = slice(hd * MLA_LANES, (hd + 1) * MLA_LANES)
            return _dot_nt(q_ref[r0:r1, sl], k_ref[0:nk, sl])

        s_next = qk(*work[0])
        for idx, (hd, (r0, r1, nk)) in enumerate(work):
            s = s_next * c2
            if idx + 1 < len(work):
                s_next = qk(*work[idx + 1])
            if diag:
                s = jnp.where(keep[r0], s, -jnp.inf)
            m_old = m_scr[hd, r0:r1]
            m_new = jnp.maximum(m_old, jnp.max(s, axis=-1, keepdims=True))
            alpha = jnp.exp2(m_old - m_new)
            p = jnp.exp2(s - jnp.concatenate([m_new] * (nk // 128), axis=1)).astype(BF16)
            pair = hd // 2
            vext = jnp.concatenate([v_ref[0:nk, pair * 128:(pair + 1) * 128], ones[0:nk]], axis=1)
            r = _dot(p, vext)
            acc_scr[hd, r0:r1] = alpha * acc_scr[hd, r0:r1] + r[:, :128]
            l_scr[hd, r0:r1] = alpha * l_scr[hd, r0:r1] + r[:, 128:]
            m_scr[hd, r0:r1] = m_new

    @pl.when(ki < qi)
    def _():
        compute(False)

    @pl.when(ki == qi)
    def _():
        compute(True)
        lane = lax.broadcasted_iota(jnp.int32, (tq, 128), 1)
        for pair in range(MLA_H // 2):
            even = acc_scr[2 * pair] / l_scr[2 * pair]
            odd = acc_scr[2 * pair + 1] / l_scr[2 * pair + 1]
            o_ref[:, pair * 128:(pair + 1) * 128] = jnp.where(lane < MLA_V, even, odd).astype(o_ref.dtype)


def _mla_prompt_attn(q, k, v, *, n_seq, seq, tq):
    nq = seq // tq
    hw = MLA_H * MLA_LANES
    vw = MLA_H * MLA_V
    return pl.pallas_call(
        functools.partial(_flash_kernel, tq),
        grid=(n_seq, nq, nq),
        in_specs=[pl.BlockSpec((tq, hw), lambda b, i, j: (b * nq + i, 0)),
                  pl.BlockSpec((tq, hw), lambda b, i, j: (b * nq + jnp.minimum(i, j), 0)),
                  pl.BlockSpec((tq, vw), lambda b, i, j: (b * nq + jnp.minimum(i, j), 0))],
        out_specs=pl.BlockSpec((tq, vw), lambda b, i, j: (b * nq + i, 0)),
        out_shape=jax.ShapeDtypeStruct((n_seq * seq, vw), BF16),
        scratch_shapes=[pltpu.VMEM((MLA_H, tq, 128), F32)] * 3,
        compiler_params=_cparams(("parallel", "parallel", "arbitrary")),
        name="mla_flash",
    )(q, k, v)


SAMPLE_ROWS = 8
SUB_KEYS = 1024


def _mla_sample_kernel(layer, n_seq, n_pg, n_steps, t_valid,
                       pt_ref, qall_ref, q_ref, knew_ref, cnew_ref, wukp_ref, wukt_ref, kg_ref, ckv_hbm, kr_hbm,
                       o_ref,
                       lhs, qabs_all, qabs, qrope, qblk, m_scr, l_scr, acc_scr, cbuf, rbuf, sem):
    b = pl.program_id(0)
    st = pl.program_id(1)
    c2 = (MLA_QK ** -0.5) * LOG2E
    inv_n = 1.0 / MLA_QK
    tk = n_pg * PAGE
    nq = SAMPLE_ROWS
    n_up = MLA_H * MLA_NOPE
    g = b * n_steps + st
    slot = g % 2

    def page_copies(bb, ss, sl):
        cps = []
        for i in range(n_pg):
            page = pt_ref[bb, ss * n_pg + i]
            cps.append((pltpu.make_async_copy(ckv_hbm.at[layer, page], cbuf.at[sl, pl.ds(i * PAGE, PAGE), :],
                                              sem.at[sl, 0]), i % 2))
            cps.append((pltpu.make_async_copy(kr_hbm.at[layer, page], rbuf.at[sl, i], sem.at[sl, 1]), (i + 1) % 2))
        return cps

    n_total = n_seq * n_steps
    is_last = g == n_total - 1

    @pl.when(g == 0)
    def _():
        for cp, prio in page_copies(0, 0, 0):
            cp.start(priority=prio)
        kg = kg_ref[...]
        for hd in range(MLA_H):
            sl = slice(hd * MLA_LANES, (hd + 1) * MLA_LANES)
            qh, ql = _split2(qall_ref[:, sl] * kg)
            w = wukp_ref[:, sl]
            qabs_all[hd] = _dot_nt(qh, w) + _dot_nt(ql, w)

    @pl.when(st == 0)
    def _():
        m_scr[...] = jnp.full(m_scr.shape, -jnp.inf, F32)
        l_scr[...] = jnp.zeros_like(l_scr)
        acc_scr[...] = jnp.zeros_like(acc_scr)
        kg = kg_ref[...]
        q = q_ref[...]
        lane_head = lax.broadcasted_iota(jnp.int32, q.shape, 1) // MLA_LANES
        r0 = pl.multiple_of(b * nq, nq)
        for hd in range(MLA_H):
            sl = slice(hd * MLA_LANES, (hd + 1) * MLA_LANES)
            qabs[hd * nq:(hd + 1) * nq, :] = qabs_all[hd, pl.ds(r0, nq), :]
            qrope[hd * nq:(hd + 1) * nq, :] = q[:, hd * MLA_LANES:hd * MLA_LANES + MLA_ROPE] * kg[:, :MLA_ROPE]
            qblk[hd * nq:(hd + 1) * nq, :] = jnp.where(lane_head == hd, q, 0.0)
        lhs[:n_up, :] = wukt_ref[...]
        lhs[n_up:, :] = qabs[...].astype(BF16)

    for cp, _ in page_copies(b, st, slot):
        cp.wait()

    lhs_v = lhs[...]
    qr = qrope[...].astype(BF16)
    pg_sub = SUB_KEYS // PAGE

    def score_block(j):
        cb = cbuf[slot, j * SUB_KEYS:(j + 1) * SUB_KEYS, :].astype(BF16)
        krt = jnp.concatenate([rbuf[slot, i] for i in range(j * pg_sub, (j + 1) * pg_sub)],
                              axis=1)
        big = _dot_nt(lhs_v, cb)
        knt = big[:n_up]
        ssq = jnp.sum((knt * knt).reshape(MLA_H, MLA_NOPE, SUB_KEYS), axis=1)
        ssq_r = jnp.sum(krt * krt, axis=0, keepdims=True)
        rs = lax.rsqrt((ssq + ssq_r) * inv_n + EPS) * c2
        s = big[n_up:] + _dot(qr, krt.astype(BF16))
        s = jnp.concatenate([s[hd * nq:(hd + 1) * nq, :] * rs[hd:hd + 1, :] for hd in range(MLA_H)], axis=0)
        return s, cb

    m_run = m_scr[...]
    l_new = l_scr[...]
    acc = acc_scr[...]
    n_sub = tk // SUB_KEYS
    blk = score_block(0)

    seq_end = st == n_steps - 1
    nb = jnp.where(is_last, b, jnp.where(seq_end, b + 1, b))
    ns = jnp.where(is_last, st, jnp.where(seq_end, 0, st + 1))
    for cp, prio in page_copies(nb, ns, 1 - slot):
        cp.start(priority=prio)

    for j in range(n_sub):
        s, cb = blk
        if j + 1 < n_sub:
            blk = score_block(j + 1)
        m_new = jnp.maximum(m_run, jnp.max(s, axis=-1, keepdims=True))
        alpha = jnp.exp2(m_run - m_new)
        p = jnp.exp2(s - m_new)
        l_new = alpha * l_new + jnp.sum(p, axis=-1, keepdims=True)
        acc = alpha * acc + _dot(p.astype(BF16), cb)
        m_run = m_new
    l_scr[...] = l_new
    acc_scr[...] = acc
    m_scr[...] = m_new

    @pl.when(st == n_steps - 1)
    def _():
        cn = cnew_ref[...]
        sn = _dot_nt(_rnd(qblk[...]), _rnd(knew_ref[...])) * c2
        row = lax.broadcasted_iota(jnp.int32, sn.shape, 0) % nq
        col = lax.broadcasted_iota(jnp.int32, sn.shape, 1)
        sn = jnp.where((col <= row) & (col < t_valid), sn, -jnp.inf)
        m_o = m_scr[...]
        m_n = jnp.maximum(m_o, jnp.max(sn, axis=-1, keepdims=True))
        al = jnp.exp2(m_o - m_n)
        pn = jnp.exp2(sn - m_n)
        l_f = al * l_scr[...] + jnp.sum(pn, axis=-1, keepdims=True)
        o_ref[...] = (al * acc_scr[...] + _dot(_rnd(pn), _rnd(cn))) / l_f

    @pl.when(is_last)
    def _():
        for cp, _ in page_copies(b, st, 1 - slot):
            cp.wait()


def _mla_sample_attn(page_table, q, k_new, c_new, wts, ckv_pool, kr_pool_t, layer, *, n_seq, t_valid, n_pg):
    n_pages = page_table.shape[1]
    n_steps = n_pages // n_pg
    hw = MLA_H * MLA_LANES
    tk = n_pg * PAGE
    nq = SAMPLE_ROWS
    nr = MLA_H * nq

    const = lambda *shape: pl.BlockSpec(shape, lambda b, s, pt: (0,) * len(shape))
    hbm = pl.BlockSpec(memory_space=pl.ANY)
    in_specs = [const(n_seq * nq, hw),
                pl.BlockSpec((nq, hw), lambda b, s, pt: (b, 0)),
                pl.BlockSpec((nq, hw), lambda b, s, pt: (b, 0)),
                pl.BlockSpec((nq, MLA_RANK), lambda b, s, pt: (b, 0))]
    in_specs += [_wspec(w) for w in wts] + [hbm, hbm]
    grid_spec = pltpu.PrefetchScalarGridSpec(
        num_scalar_prefetch=1,
        grid=(n_seq, n_steps),
        in_specs=in_specs,
        out_specs=pl.BlockSpec((nr, MLA_RANK), lambda b, s, pt: (b, 0)),
        scratch_shapes=[pltpu.VMEM((MLA_H * MLA_NOPE + nr, MLA_RANK), BF16),
                        pltpu.VMEM((MLA_H, n_seq * nq, MLA_RANK), F32),
                        pltpu.VMEM((nr, MLA_RANK), F32), pltpu.VMEM((nr, MLA_ROPE), F32),
                        pltpu.VMEM((nr, hw), F32),
                        pltpu.VMEM((nr, 1), F32), pltpu.VMEM((nr, 1), F32), pltpu.VMEM((nr, MLA_RANK), F32),
                        pltpu.VMEM((2, tk, MLA_RANK), F32), pltpu.VMEM((2, n_pg, MLA_ROPE, PAGE), F32),
                        pltpu.SemaphoreType.DMA((2, 2))],
    )
    return pl.pallas_call(
        functools.partial(_mla_sample_kernel, layer, n_seq, n_pg, n_steps, t_valid),
        grid_spec=grid_spec,
        out_shape=jax.ShapeDtypeStruct((n_seq * nr, MLA_RANK), F32),
        compiler_params=_cparams(("arbitrary", "arbitrary")),
        name="mla_paged",
    )(page_table, q, q, k_new, c_new, *[w.arr for w in wts], ckv_pool, kr_pool_t)


def _mla_up_kernel(n_seq, pc_ref, h_ref, wuvs_ref, wout_ref, wgate_ref, y_ref):
    nq = SAMPLE_ROWS
    o = jnp.zeros((n_seq * nq, MLA_H * MLA_V), F32)
    for hd in range(MLA_H):
        pch = pc_ref[:, hd * nq:(hd + 1) * nq, :].reshape(n_seq * nq, MLA_RANK)
        o = o + _dot(pch.astype(BF16), wuvs_ref[hd])
    y_ref[...] = _dot(o.astype(BF16), wout_ref[...]) * jax.nn.sigmoid(_dot(h_ref[...], wgate_ref[...]))


def _mla_up_proj_gate(pc, h, wuv_sel, w_out, w_gate, *, n_seq):
    rows = n_seq * SAMPLE_ROWS
    nr = MLA_H * SAMPLE_ROWS
    wts = (wuv_sel, w_out, w_gate)
    return pl.pallas_call(
        functools.partial(_mla_up_kernel, n_seq),
        grid=(1,),
        in_specs=[pl.BlockSpec((n_seq, nr, MLA_RANK), lambda i: (0, 0, 0)),
                  pl.BlockSpec((rows, D_MODEL), lambda i: (0, 0))] + [_wspec(w) for w in wts],
        out_specs=pl.BlockSpec((rows, D_MODEL), lambda i: (0, 0)),
        out_shape=jax.ShapeDtypeStruct((rows, D_MODEL), F32),
        compiler_params=_cparams(("arbitrary",)),
        name="mla_up_proj_gate",
    )(pc.reshape(n_seq, nr, MLA_RANK), h, *[w.arr for w in wts])


def _projgate_kernel(o_ref, h_ref, wout_ref, wgate_ref, y_ref):
    y_ref[...] = _dot(o_ref[...].astype(BF16), wout_ref[...]) * jax.nn.sigmoid(_dot(h_ref[...], wgate_ref[...]))


def _proj_gate(o, h, w_out, w_gate, *, tm):
    m, kdim = o.shape
    return pl.pallas_call(
        _projgate_kernel,
        grid=(m // tm,),
        in_specs=[pl.BlockSpec((tm, kdim), lambda i: (i, 0)),
                  pl.BlockSpec((tm, D_MODEL), lambda i: (i, 0)),
                  _wspec(w_out), _wspec(w_gate)],
        out_specs=pl.BlockSpec((tm, D_MODEL), lambda i: (i, 0)),
        out_shape=jax.ShapeDtypeStruct((m, D_MODEL), F32),
        compiler_params=_cparams(("parallel",)),
        name="proj_gate",
    )(o, h, w_out.arr, w_gate.arr)


def _pad_lanes(x, width):
    return jnp.pad(x, [(0, 0)] * (x.ndim - 1) + [(0, width - x.shape[-1])])


def _mla_head_layout(nope, r1, r2):
    z = jnp.zeros(nope.shape[:-1] + (MLA_LANES - MLA_QK,), nope.dtype)
    x = jnp.concatenate([r1, r2, nope, z], axis=-1)
    return x.reshape(x.shape[:-2] + (MLA_H * MLA_LANES,))


def _gain_layout(g):
    half = MLA_ROPE // 2
    return jnp.concatenate([g[MLA_NOPE:MLA_NOPE + half], g[MLA_NOPE + half:], g[:MLA_NOPE],
                            jnp.zeros((MLA_LANES - MLA_QK,), g.dtype)]).reshape(1, MLA_LANES)


def _rope_tables(pos):
    half = MLA_ROPE // 2
    inv = ROPE_THETA ** (-jnp.arange(half, dtype=F32) / half)
    ang = pos.astype(F32)[:, None] * inv
    cos, sin = jnp.cos(ang), jnp.sin(ang)
    n = pos.shape[0]
    cos_t = jnp.concatenate([cos, cos, jnp.ones((n, MLA_NOPE), F32), jnp.zeros((n, MLA_LANES - MLA_QK), F32)], -1)
    sin_t = jnp.concatenate([sin, sin, jnp.zeros((n, MLA_LANES - MLA_ROPE), F32)], -1)
    return cos_t, sin_t


def _layer_weights(p):
    w_in = p['w_in']
    sizes = (DN_CONV_W, DN_QK_W, DN_H, DN_H, SC_W, SC_W, SC_W, MLA_RANK, MLA_RANK, MLA_ROPE, MEM_W, 4 * D_MODEL)
    offs = np.concatenate([[0], np.cumsum(sizes)])
    seg = [w_in[:, offs[i]:offs[i + 1]] for i in range(len(sizes))]
    bf = lambda x: x.astype(BF16)
    row = lambda x: x.reshape(1, -1)
    gates = [bf(seg[11][:, i * D_MODEL:(i + 1) * D_MODEL]) for i in range(4)]
    half = MLA_ROPE // 2

    w = {}
    w['ffn1'] = (row(p['ffn1_norm']), bf(p['ffn1_w_gu']), bf(p['ffn1_w_down']))
    w['ffn2'] = (row(p['ffn2_norm']), bf(p['ffn2_w_gu']), bf(p['ffn2_w_down']))
    w['mix_norm'] = row(p['mix_norm'])
    w['w_o'] = bf(p['w_o'])
    w['dn'] = (bf(seg[0]), bf(seg[1]), bf(_pad_lanes(jnp.concatenate([seg[2], seg[3]], 1), 128)),
               p['dn_conv_w'], _pad_lanes(row(p['dn_A_log']), 128), _pad_lanes(row(p['dn_dt_bias']), 128),
               row(p['dn_norm']), bf(p['dn_w_out']), gates[0])
    w['sc'] = (bf(jnp.concatenate([seg[4], seg[5], seg[6]], 1)), p['sc_conv_w'], bf(p['sc_w_out']), gates[1])

    wq = p['mla_w_q_b'].reshape(MLA_RANK, MLA_H, MLA_QK)
    q_nope, q_r1, q_r2 = wq[..., :MLA_NOPE], wq[..., MLA_NOPE:MLA_NOPE + half], wq[..., MLA_NOPE + half:]
    wq_perm = _mla_head_layout(q_nope, q_r1, q_r2)
    wq_swap = _mla_head_layout(jnp.zeros_like(q_nope), -q_r2, q_r1)
    wkr = seg[9]
    wkr_pad = _pad_lanes(wkr, MLA_LANES)
    wkr_swap = _pad_lanes(jnp.concatenate([-wkr[:, half:], wkr[:, :half]], 1), MLA_LANES)
    wkv = p['mla_w_kv_b'].reshape(MLA_RANK, MLA_H, MLA_NOPE + MLA_V)
    w_uk, w_uv = wkv[..., :MLA_NOPE], wkv[..., MLA_NOPE:]
    zr = jnp.zeros((MLA_RANK, MLA_H, half), F32)
    wuk_perm = bf(_mla_head_layout(w_uk, zr, zr))
    k_gain = _gain_layout(p['mla_k_norm'])
    w['mla_proj'] = (bf(seg[7]), row(p['mla_q_norm_a']), bf(wq_perm), bf(wq_swap), _gain_layout(p['mla_q_norm']),
                     bf(seg[8]), row(p['mla_kv_norm_a']), bf(wkr_pad), bf(wkr_swap),
                     wuk_perm, bf(w_uv.reshape(MLA_RANK, MLA_H * MLA_V)), k_gain)
    eye = jnp.eye(MLA_H, dtype=F32)
    w['mla_wuv_sel'] = bf((w_uv[None] * eye[:, None, :, None]).reshape(MLA_H, MLA_RANK, MLA_H * MLA_V))
    w['mla_sample'] = (wuk_perm, bf(w_uk.reshape(MLA_RANK, MLA_H * MLA_NOPE).T), k_gain)
    w['mla_out'] = (bf(p['mla_w_out']), gates[2])
    w['mem_kv'] = (row(p['mem_norm']), bf(p['mem_w_kv']), row(p['mem_k_norm']))
    w['mem'] = (bf(seg[10]), row(p['mem_q_norm']), bf(p['mem_w_out']), gates[3])
    return w


def _group_layer(x, w, *, n_seq, t_pad, t_valid, tm, bb_n, tt, chunk, dn_state, sc_state, mem_kv, cos, sin,
                 n_tab, mla_attend, q_dtype):
    cfg = dict(n_seq=n_seq, t_pad=t_pad, bb_n=bb_n, tt=tt)
    x1, h = _ffn(x, *w['ffn1'], tm=tm, h_gain=w['mix_norm'])
    y_dn, dn_s, dn_c = _deltanet(h, dn_state[0], dn_state[1], w['dn'], t_valid=t_valid, chunk=chunk, **cfg)
    y_sc, sc_c, y_mem, q, ckv, kr, k, v = _branches(
        h, sc_state, mem_kv[0], mem_kv[1], cos, sin, w['sc'], w['mem'], w['mla_proj'],
        t_valid=t_valid, qk_dtype=q_dtype, **cfg)
    if mla_attend is None:
        o = _mla_prompt_attn(q, k, v, n_seq=n_seq, seq=t_pad, tq=tm)
        y_mla = _proj_gate(o, h, *w['mla_out'], tm=tm)
    else:
        y_mla = mla_attend(q, k, ckv, h)
    x3 = _ffn(x1, *w['ffn2'], tm=min(tm, 256), merge=((y_dn, y_sc, y_mla, y_mem), w['w_o']))
    return x3, dn_s, dn_c, sc_c, ckv, kr


def kernel(x_prompt, x_sample, state_dn_S, state_dn_conv, state_sc_conv, cache_mla_ckv, cache_mla_krope, cache_mem_k, cache_mem_v, page_table, mem_prompt, ffn1_norm, ffn1_w_gu, ffn1_w_down, mix_norm, w_in, dn_conv_w, dn_A_log, dn_dt_bias, dn_norm, dn_w_out, sc_conv_w, sc_w_out, mla_q_norm_a, mla_w_q_b, mla_kv_norm_a, mla_w_kv_b, mla_q_norm, mla_k_norm, mla_w_out, mem_norm, mem_w_kv, mem_q_norm, mem_k_norm, mem_w_out, w_o, ffn2_norm, ffn2_w_gu, ffn2_w_down):
    params = dict(ffn1_norm=ffn1_norm, ffn1_w_gu=ffn1_w_gu, ffn1_w_down=ffn1_w_down, mix_norm=mix_norm, w_in=w_in,
                  dn_conv_w=dn_conv_w, dn_A_log=dn_A_log, dn_dt_bias=dn_dt_bias, dn_norm=dn_norm, dn_w_out=dn_w_out,
                  sc_conv_w=sc_conv_w, sc_w_out=sc_w_out, mla_q_norm_a=mla_q_norm_a, mla_w_q_b=mla_w_q_b,
                  mla_kv_norm_a=mla_kv_norm_a, mla_w_kv_b=mla_w_kv_b, mla_q_norm=mla_q_norm, mla_k_norm=mla_k_norm,
                  mla_w_out=mla_w_out, mem_norm=mem_norm, mem_w_kv=mem_w_kv, mem_q_norm=mem_q_norm,
                  mem_k_norm=mem_k_norm, mem_w_out=mem_w_out, w_o=w_o, ffn2_norm=ffn2_norm, ffn2_w_gu=ffn2_w_gu,
                  ffn2_w_down=ffn2_w_down)
    depth = w_in.shape[0]
    bp, seq, _ = x_prompt.shape
    bs, td, _ = x_sample.shape
    tds = SAMPLE_ROWS
    n_pages = page_table.shape[1]
    past = n_pages * PAGE
    krope_t = jnp.transpose(cache_mla_krope, (0, 1, 3, 2))

    cos_p, sin_p = _rope_tables(jnp.arange(seq))
    cos_s, sin_s = _rope_tables(past + jnp.arange(tds))
    cos_s, sin_s = jnp.tile(cos_s, (bs, 1)), jnp.tile(sin_s, (bs, 1))

    xp = x_prompt.reshape(bp * seq, D_MODEL)
    xs = jnp.pad(x_sample, ((0, 0), (0, tds - td), (0, 0))).reshape(bs * tds, D_MODEL)
    zero_s = jnp.zeros((bp, DN_H, DN_DK, DN_DK), F32)
    zero_dc = jnp.zeros((bp, DN_CONV - 1, DN_CONV_W), F32)
    zero_sc = jnp.zeros((bp, SC_CONV - 1, SC_W), F32)
    mem2d = mem_prompt.reshape(bp * N_MEM, D_MODEL)

    outs = {k: [] for k in ('pS', 'pdc', 'psc', 'pckv', 'pkr', 'pmk', 'pmv', 'sS', 'sdc', 'ssc', 'sckv', 'skr')}
    tm_p = 512
    w_all = jax.vmap(_layer_weights)(params)
    for l in range(depth):
        w = jax.tree.map(lambda a: _W(a, l), w_all)
        mk, mv = _mem_kv(mem2d, w['mem_kv'], tm=tm_p)
        mk3, mv3 = mk.reshape(bp, N_MEM, MEM_W), mv.reshape(bp, N_MEM, MEM_W)
        xp, s_p, dc_p, sc_p, ckv_p, kr_p = _group_layer(
            xp, w, n_seq=bp, t_pad=seq, t_valid=seq, tm=tm_p, bb_n=1, tt=tm_p, chunk=DN_CHUNK,
            dn_state=(zero_s, zero_dc), sc_state=zero_sc, mem_kv=(mk3, mv3), cos=cos_p, sin=sin_p,
            n_tab=seq // tm_p, mla_attend=None, q_dtype=BF16)
        outs['pS'].append(s_p); outs['pdc'].append(dc_p); outs['psc'].append(sc_p)
        outs['pckv'].append(ckv_p.reshape(bp, seq, MLA_RANK)); outs['pkr'].append(kr_p.reshape(bp, seq, MLA_ROPE))
        outs['pmk'].append(mk.reshape(bp, N_MEM, MEM_H, MEM_HD)); outs['pmv'].append(mv.reshape(bp, N_MEM, MEM_H, MEM_HD))

        def attend(q, k, ckv, h, l=l, w=w):
            pc = _mla_sample_attn(page_table, q, k, ckv, w['mla_sample'], cache_mla_ckv, krope_t, l,
                                  n_seq=bs, t_valid=td, n_pg=32)
            return _mla_up_proj_gate(pc, h, w['mla_wuv_sel'], *w['mla_out'], n_seq=bs)

        xs, s_s, dc_s, sc_s, ckv_s, kr_s = _group_layer(
            xs, w, n_seq=bs, t_pad=tds, t_valid=td, tm=bs * tds, bb_n=8, tt=tds, chunk=tds,
            dn_state=(state_dn_S[l], state_dn_conv[l]), sc_state=state_sc_conv[l],
            mem_kv=(cache_mem_k[l].reshape(bs, N_MEM, MEM_W), cache_mem_v[l].reshape(bs, N_MEM, MEM_W)),
            cos=cos_s, sin=sin_s, n_tab=1, mla_attend=attend, q_dtype=F32)
        outs['sS'].append(s_s); outs['sdc'].append(dc_s); outs['ssc'].append(sc_s)
        outs['sckv'].append(ckv_s.reshape(bs, tds, MLA_RANK)[:, :td])
        outs['skr'].append(kr_s.reshape(bs, tds, MLA_ROPE)[:, :td])

    st = lambda k: jnp.stack(outs[k])
    y_prompt = xp.reshape(bp, seq, D_MODEL)
    y_sample = xs.reshape(bs, tds, D_MODEL)[:, :td]
    return (y_prompt, y_sample, st('pS'), st('pdc'), st('psc'), st('pckv'), st('pkr'), st('pmk'), st('pmv'),
            st('sS'), st('sdc'), st('ssc'), st('sckv'), st('skr'))
```

```python
import functools

import numpy as np
import jax
import jax.numpy as jnp
from jax import lax
from jax.experimental import pallas as pl
from jax.experimental.pallas import tpu as pltpu

F32 = jnp.float32
BF16 = jnp.bfloat16

D_MODEL = 1024
D_FF = 2816
EPS = 1e-6
N_MEM = 256
PAGE = 128
DN_H = 4
DN_DK = 128
DN_QK_W = 512
DN_CONV_W = 1536
DN_CONV = 4
DN_CHUNK = 64
DN_UNROLL_GROUPS = 4
DN_GROUP = 4
SC_W = 512
SC_CONV = 3
MLA_H = 8
MLA_RANK = 256
MLA_NOPE = 64
MLA_ROPE = 32
MLA_V = 64
MLA_QK = 96
MLA_LANES = 128
ROPE_THETA = 10000.0
LOG2E = 1.4426950408889634
MEM_H = 4
MEM_HD = 128
MEM_W = 512

VMEM_LIMIT = 56 * 1024 * 1024
BRANCH_DTYPE = jnp.bfloat16


def _cparams(sem):
    return pltpu.CompilerParams(dimension_semantics=sem, vmem_limit_bytes=VMEM_LIMIT)


class _W:
    def __init__(self, arr, layer):
        self.arr, self.layer = arr, layer


def _wspec(w, resident=False):
    shape = w.arr.shape[1:]
    index = lambda *_: (w.layer,) + (0,) * len(shape)
    if resident:
        return pl.BlockSpec((None,) + shape, index, pipeline_mode=pl.Buffered(1))
    return pl.BlockSpec((None,) + shape, index)


def _rms(x, g):
    ms = jnp.mean(x * x, axis=-1, keepdims=True)
    return x * lax.rsqrt(ms + EPS) * g


def _silu(x):
    return x * jax.nn.sigmoid(x)


def _rnd(x):
    return x.astype(BF16).astype(F32)


def _dot(a, b):
    return jnp.dot(a, b, preferred_element_type=F32)


def _dot_nt(a, b):
    return lax.dot_general(a, b, (((1,), (1,)), ((), ())), preferred_element_type=F32)


def _mm(a, b, small):
    if small:
        return _dot(_rnd(a), _rnd(b))
    return _dot(a.astype(BF16), b.astype(BF16))


def _mm_nt(a, b, small):
    if small:
        return _dot_nt(_rnd(a), _rnd(b))
    return _dot_nt(a.astype(BF16), b.astype(BF16))


def _mm_tn(a, b, small):
    dn = (((0,), (0,)), ((), ()))
    if small:
        return lax.dot_general(_rnd(a), _rnd(b), dn, preferred_element_type=F32)
    return lax.dot_general(a.astype(BF16), b.astype(BF16), dn, preferred_element_type=F32)


def _run_interleaved(gens):
    gens = list(gens)
    while gens:
        for g in list(gens):
            try:
                next(g)
            except StopIteration:
                gens.remove(g)


def _split2(x):
    hi = x.astype(BF16)
    lo = (x - hi.astype(F32)).astype(BF16)
    return hi, lo


def _mm_hi(a, b, small):
    if small:
        return jnp.dot(a, b, preferred_element_type=F32, precision=lax.Precision.HIGHEST)
    ah, al = _split2(a)
    bh, bl = _split2(b)
    return _dot(ah, bh) + (_dot(ah, bl) + _dot(al, bh))


def _mm_exact_left(lmat, b, small):
    if small:
        return jnp.dot(lmat, b, preferred_element_type=F32, precision=lax.Precision.HIGHEST)
    lb = lmat.astype(BF16)
    b1 = b.astype(BF16)
    r1 = b - b1.astype(F32)
    b2 = r1.astype(BF16)
    b3 = (r1 - b2.astype(F32)).astype(BF16)
    return _dot(lb, b1) + (_dot(lb, b2) + _dot(lb, b3))


FFN_SPLIT = 2


def _ffn_kernel(merge, emit_h, *refs):
    it = iter(refs)
    x_ref = next(it)
    if merge:
        y_refs = [next(it) for _ in range(4)]
        wo_ref = next(it)
    g_ref, wgu_ref, wd_ref = next(it), next(it), next(it)
    if emit_h:
        g2_ref = next(it)
    o_ref = next(it)
    if emit_h:
        h_ref = next(it)

    x = x_ref[...]
    if merge:
        ys = [r[...].astype(F32) for r in y_refs]
        m = ((ys[0] + ys[1]) + ys[2]) + ys[3]
        x = x + _dot(m.astype(BF16), wo_ref[...])
    h = _rms(x, g_ref[...]).astype(BF16)
    tf = D_FF // FFN_SPLIT
    def gate_up(j):
        return (_dot(h, wgu_ref[:, j * tf:(j + 1) * tf]),
                _dot(h, wgu_ref[:, D_FF + j * tf:D_FF + (j + 1) * tf]))

    acc = None
    nxt = gate_up(0)
    for j in range(FFN_SPLIT):
        gate, up = nxt
        if j + 1 < FFN_SPLIT:
            nxt = gate_up(j + 1)
        a = (_silu(gate) * up).astype(BF16)
        d = _dot(a, wd_ref[j * tf:(j + 1) * tf, :])
        acc = d if acc is None else acc + d
    out = x + 0.5 * acc
    o_ref[...] = out
    if emit_h:
        h_ref[...] = _rms(out, g2_ref[...]).astype(BF16)


def _ffn(x, norm_g, w_gu, w_down, *, tm, merge=None, h_gain=None):
    m = x.shape[0]
    row = pl.BlockSpec((tm, D_MODEL), lambda i: (i, 0))
    in_specs, args = [row], [x]
    if merge is not None:
        ys, w_o = merge
        in_specs += [row] * 4 + [_wspec(w_o, resident=True)]
        args += list(ys) + [w_o.arr]
    in_specs += [_wspec(w, resident=True) for w in (norm_g, w_gu, w_down)]
    args += [norm_g.arr, w_gu.arr, w_down.arr]
    out_shape = [jax.ShapeDtypeStruct((m, D_MODEL), F32)]
    out_specs = [row]
    if h_gain is not None:
        in_specs.append(_wspec(h_gain, resident=True))
        args.append(h_gain.arr)
        out_shape.append(jax.ShapeDtypeStruct((m, D_MODEL), BF16))
        out_specs.append(row)
    res = pl.pallas_call(
        functools.partial(_ffn_kernel, merge is not None, h_gain is not None),
        grid=(m // tm,),
        in_specs=in_specs, out_specs=out_specs, out_shape=out_shape,
        compiler_params=_cparams(("parallel",)),
        name="ffn_merge" if merge is not None else "ffn",
    )(*args)
    return res if h_gain is not None else res[0]


def _dn_kernel(bb_n, tt, chunk, tv_last, n_t, small,
               h_ref, s0_ref, cprev_ref, wqkv_ref, wz_ref, wab_ref, convw_ref, alog_ref, dtb_ref,
               normg_ref, wout_ref, wgate_ref,
               y_ref, snew_ref, cnew_ref,
               xbuf, cs, gb, zb, ob, s_all, u_s, w_s, qg_s, kdec_s, aqk_s, gl_s, gate_s):
    t = pl.program_id(1)
    rows = bb_n * tt

    @pl.when(t == 0)
    def _():
        xbuf[:, 5:8, :] = cprev_ref[...]
        snew_ref[...] = s0_ref[...]

    h = h_ref[...]
    qkv = _dot(h, wqkv_ref[...])
    ab = _dot(h, wab_ref[...])
    zs = _dot(h, wz_ref[...])
    gate = _dot(h, wgate_ref[...])
    xbuf[:, 8:8 + tt, :] = qkv.reshape(bb_n, tt, DN_CONV_W)
    c = 0.0
    for j in range(DN_CONV):
        c = c + xbuf[:, 5 + j:5 + j + tt, :] * convw_ref[j:j + 1, :].reshape(1, 1, DN_CONV_W)
    cnew_ref[...] = xbuf[:, 5 + tv_last:8 + tv_last, :]
    if n_t > 1:
        xbuf[:, 0:8, :] = xbuf[:, tt:tt + 8, :]

    g = -jnp.exp(alog_ref[...]) * jax.nn.softplus(ab + dtb_ref[...])
    lane = lax.broadcasted_iota(jnp.int32, (rows, 128), 1)
    gbv = jnp.where(lane < DN_H, g, jax.nn.sigmoid(ab)).reshape(bb_n, tt, 128)
    if tv_last < tt:
        trow = lax.broadcasted_iota(jnp.int32, (bb_n, tt, 128), 1)
        gbv = jnp.where(trow < tv_last, gbv, 0.0)
    gb[...] = gbv
    c = _silu(c)
    for grp in range(8):
        sl = slice(grp * 128, (grp + 1) * 128)
        xg = c[:, :, sl]
        xn = xg * lax.rsqrt(jnp.sum(xg * xg, axis=-1, keepdims=True) + EPS)
        if grp < DN_H:
            xn = xn * (DN_DK ** -0.5)
        cs[:, :, sl] = xn
    cs[:, :, 2 * DN_QK_W:] = c[:, :, 2 * DN_QK_W:]
    zb[...] = _silu(zs).reshape(bb_n, tt, DN_QK_W)
    gate_s[...] = jax.nn.sigmoid(gate)

    n4 = DN_H * chunk
    ri = lax.broadcasted_iota(jnp.int32, (n4, n4), 0)
    ci = lax.broadcasted_iota(jnp.int32, (n4, n4), 1)
    same = (ri // chunk) == (ci // chunk)
    incl = same & (ci <= ri)
    strict = same & (ci < ri)
    lmat = incl.astype(F32)
    umat = strict.astype(F32)
    vmask = (lax.broadcasted_iota(jnp.int32, (n4, DN_H * 128), 0) // chunk
             == lax.broadcasted_iota(jnp.int32, (n4, DN_H * 128), 1) // 128)
    n_pow = int(np.log2(chunk))
    normg = normg_ref[...]
    n_ch = tt // chunk
    total = bb_n * n_ch
    solve_mm = _mm_hi if small else _mm

    for hd in range(DN_H):
        s_all[:, :, hd * 128:(hd + 1) * 128] = snew_ref[:, hd]

    def stack_rows(ref, b, r0, off):
        return jnp.concatenate([ref[b, pl.ds(r0, chunk), off + hd * 128:off + (hd + 1) * 128]
                                for hd in range(DN_H)], axis=0)

    def level1(k, slot):
        b = k // n_ch
        r0 = (k % n_ch) * chunk
        if not isinstance(k, int):
            r0 = pl.multiple_of(r0, chunk)
        q = stack_rows(cs, b, r0, 0)
        kk = stack_rows(cs, b, r0, DN_QK_W)
        v = stack_rows(cs, b, r0, 2 * DN_QK_W)
        gbc = gb[b, pl.ds(r0, chunk), :]
        g_st = jnp.concatenate([jnp.broadcast_to(gbc[:, hd:hd + 1], (chunk, 128)) for hd in range(DN_H)], axis=0)
        beta_st = jnp.concatenate([jnp.broadcast_to(gbc[:, DN_H + hd:DN_H + hd + 1], (chunk, 128))
                                   for hd in range(DN_H)], axis=0)
        g_sq = jnp.concatenate([g_st] * (n4 // 128), axis=1) if n4 >= 128 else g_st[:, :n4]
        gc = _mm_exact_left(lmat, g_st, small)
        yield
        if small:
            dmat = _mm_exact_left(lmat, g_sq * umat, small)
        else:
            gct = gc.T
            dmat = (jnp.concatenate([gc] * (n4 // 128), axis=1)
                    - jnp.concatenate([gct] * (n4 // 128), axis=0))
        gam = jnp.where(incl, jnp.exp(dmat), 0.0)
        eg = jnp.exp(gc)
        kb = kk * beta_st
        kq = _mm_nt(jnp.concatenate([kb, q], axis=0), kk, small)
        yield
        a_mat = jnp.where(strict, kq[:n4] * gam, 0.0)
        x = jnp.concatenate([v * beta_st, kb * eg], axis=1)
        p = -a_mat
        for i in range(n_pow):
            x = x + (_mm_hi if i < 2 else solve_mm)(p, x, small)
            if i < n_pow - 1:
                p = (_mm_hi if i < 1 else solve_mm)(p, p, small)
            yield
        gc_last = [gc[(hd + 1) * chunk - 1:(hd + 1) * chunk, :] for hd in range(DN_H)]
        gl_st = jnp.concatenate([jnp.broadcast_to(r, (chunk, 128)) for r in gc_last], axis=0)
        u_s[slot] = x[:, :128]
        w_s[slot] = x[:, 128:]
        qg_s[slot] = q * eg
        kdec_s[slot] = kk * jnp.exp(gl_st - gc)
        aqk_s[slot] = jnp.where(incl, kq[n4:] * gam, 0.0)
        gl_s[slot] = jnp.exp(jnp.concatenate(gc_last, axis=1))

    def level2(k, slot):
        b = k // n_ch
        r0 = (k % n_ch) * chunk
        if not isinstance(k, int):
            r0 = pl.multiple_of(r0, chunk)
        s_old = s_all[b]
        w = w_s[slot]
        qg = qg_s[slot]
        ws, qs = [], []
        for hd in range(DN_H):
            rs = slice(hd * chunk, (hd + 1) * chunk)
            r = _mm(jnp.concatenate([w[rs], qg[rs]], axis=0), s_old[:, hd * 128:(hd + 1) * 128], small)
            ws.append(r[:chunk])
            qs.append(r[chunk:])
        yield
        v_new = u_s[slot] - jnp.concatenate(ws, axis=0)
        o = jnp.concatenate(qs, axis=0) + _mm(aqk_s[slot], v_new, small)
        vbd = jnp.where(vmask, jnp.concatenate([v_new] * DN_H, axis=1), 0.0)
        s_all[b] = s_old * gl_s[slot] + _mm_tn(kdec_s[slot], vbd, small)
        yield
        on = _rms(o, normg)
        for hd in range(DN_H):
            sl = slice(hd * 128, (hd + 1) * 128)
            ob[b, pl.ds(r0, chunk), sl] = on[hd * chunk:(hd + 1) * chunk] * zb[b, pl.ds(r0, chunk), sl]

    assert total % DN_GROUP == 0

    def recurrence(m, base):
        for i in range(DN_GROUP):
            yield from level2(m * DN_GROUP + i, base + i)

    _run_interleaved([level1(i, i) for i in range(DN_GROUP)])

    n_groups = total // DN_GROUP

    def group_body(m, _):
        base = (m % 2) * DN_GROUP
        chains = [recurrence(m, base)]
        if not isinstance(m, int):
            chains += [level1(jnp.minimum((m + 1) * DN_GROUP + i, total - 1), DN_GROUP - base + i)
                       for i in range(DN_GROUP)]
        elif m + 1 < n_groups:
            chains += [level1((m + 1) * DN_GROUP + i, DN_GROUP - base + i) for i in range(DN_GROUP)]
        _run_interleaved(chains)
        return 0

    if n_groups <= DN_UNROLL_GROUPS:
        for m in range(n_groups):
            group_body(m, 0)
    else:
        lax.fori_loop(0, n_groups, group_body, 0)

    for hd in range(DN_H):
        snew_ref[:, hd] = s_all[:, :, hd * 128:(hd + 1) * 128]

    y = _dot(ob[...].reshape(rows, DN_QK_W).astype(BF16), wout_ref[...])
    y_ref[...] = (y * gate_s[...]).astype(y_ref.dtype)


def _deltanet(h, s0, cprev, wts, *, n_seq, t_pad, t_valid, bb_n, tt, chunk):
    n_t = t_pad // tt
    assert n_t == 1 or t_valid == t_pad
    tv_last = t_valid - (n_t - 1) * tt
    rows = bb_n * tt
    small = chunk < 16
    n4 = DN_H * chunk
    in_specs = [
        pl.BlockSpec((rows, D_MODEL), lambda b, t: (b * n_t + t, 0)),
        pl.BlockSpec((bb_n, DN_H, DN_DK, DN_DK), lambda b, t: (b, 0, 0, 0)),
        pl.BlockSpec((bb_n, DN_CONV - 1, DN_CONV_W), lambda b, t: (b, 0, 0)),
    ] + [_wspec(w) for w in wts]
    out_specs = [
        pl.BlockSpec((rows, D_MODEL), lambda b, t: (b * n_t + t, 0)),
        pl.BlockSpec((bb_n, DN_H, DN_DK, DN_DK), lambda b, t: (b, 0, 0, 0)),
        pl.BlockSpec((bb_n, DN_CONV - 1, DN_CONV_W), lambda b, t: (b, 0, 0)),
    ]
    out_shape = [
        jax.ShapeDtypeStruct((n_seq * t_pad, D_MODEL), BRANCH_DTYPE),
        jax.ShapeDtypeStruct((n_seq, DN_H, DN_DK, DN_DK), F32),
        jax.ShapeDtypeStruct((n_seq, DN_CONV - 1, DN_CONV_W), F32),
    ]
    return pl.pallas_call(
        functools.partial(_dn_kernel, bb_n, tt, chunk, tv_last, n_t, small),
        grid=(n_seq // bb_n, n_t),
        in_specs=in_specs, out_specs=out_specs, out_shape=out_shape,
        scratch_shapes=[pltpu.VMEM((bb_n, tt + 8, DN_CONV_W), F32), pltpu.VMEM((bb_n, tt, DN_CONV_W), F32),
                        pltpu.VMEM((bb_n, tt, 128), F32), pltpu.VMEM((bb_n, tt, DN_QK_W), F32),
                        pltpu.VMEM((bb_n, tt, DN_QK_W), F32),
                        pltpu.VMEM((bb_n, DN_DK, DN_H * 128), F32)]
                       + [pltpu.VMEM((2 * DN_GROUP, n4, 128), F32)] * 4
                       + [pltpu.VMEM((2 * DN_GROUP, n4, n4), F32), pltpu.VMEM((2 * DN_GROUP, 1, DN_H * 128), F32),
                          pltpu.VMEM((rows, D_MODEL), F32)],
        compiler_params=_cparams(("parallel", "arbitrary")),
        name="deltanet",
    )(h, s0, cprev, *[w.arr for w in wts])


def _sc_kernel(bb_n, tt, tv_last, n_t,
               h_ref, prev_ref, win_ref, convw_ref, wout_ref, wgate_ref,
               y_ref, new_ref, ubuf):
    t = pl.program_id(1)
    rows = bb_n * tt

    @pl.when(t == 0)
    def _():
        ubuf[:, 6:8, :] = prev_ref[...]

    h = h_ref[...]
    p = _dot(h, win_ref[...])
    gate = jax.nn.sigmoid(_dot(h, wgate_ref[...]))
    bgate = p[:, :SC_W]
    u = p[:, SC_W:2 * SC_W] * p[:, 2 * SC_W:]
    ubuf[:, 8:8 + tt, :] = u.reshape(bb_n, tt, SC_W)
    y = 0.0
    for j in range(SC_CONV):
        y = y + ubuf[:, 6 + j:6 + j + tt, :] * convw_ref[j:j + 1, :].reshape(1, 1, SC_W)

    new_ref[...] = ubuf[:, 6 + tv_last:8 + tv_last, :]
    if n_t > 1:
        ubuf[:, 0:8, :] = ubuf[:, tt:tt + 8, :]

    z = (bgate * y.reshape(rows, SC_W)).astype(BF16)
    y_ref[...] = (_dot(z, wout_ref[...]) * gate).astype(y_ref.dtype)


def _memkv_kernel(m_ref, g_ref, wkv_ref, kg_ref, k_ref, v_ref):
    n = _rms(m_ref[...], g_ref[...]).astype(BF16)
    kv = _dot(n, wkv_ref[...])
    kg = kg_ref[...]
    for hd in range(MEM_H):
        sl = slice(hd * MEM_HD, (hd + 1) * MEM_HD)
        k_ref[:, sl] = _rms(kv[:, sl], kg)
    v_ref[...] = kv[:, MEM_W:]


def _mem_kv(mem2d, wts, *, tm):
    m = mem2d.shape[0]
    return pl.pallas_call(
        _memkv_kernel,
        grid=(m // tm,),
        in_specs=[pl.BlockSpec((tm, D_MODEL), lambda i: (i, 0))] + [_wspec(w) for w in wts],
        out_specs=[pl.BlockSpec((tm, MEM_W), lambda i: (i, 0))] * 2,
        out_shape=[jax.ShapeDtypeStruct((m, MEM_W), F32)] * 2,
        compiler_params=_cparams(("parallel",)),
        name="mem_kv",
    )(mem2d, *[w.arr for w in wts])


def _memattn_kernel(bb_n, tt, small,
                    h_ref, mk_ref, mv_ref, wq_ref, qg_ref, wout_ref, wgate_ref,
                    y_ref, qs, ob):
    rows = bb_n * tt
    h = h_ref[...]
    q = _dot(h, wq_ref[...])
    gate = jax.nn.sigmoid(_dot(h, wgate_ref[...]))
    qg = qg_ref[...]
    for hd in range(MEM_H):
        sl = slice(hd * MEM_HD, (hd + 1) * MEM_HD)
        qs[:, :, sl] = _rms(q[:, sl], qg).reshape(bb_n, tt, MEM_HD)

    def seq_body(b, _):
        def scores(hd):
            sl = slice(hd * MEM_HD, (hd + 1) * MEM_HD)
            return _mm_nt(qs[b, :, sl], mk_ref[b, :, sl], small)

        s_next = scores(0)
        for hd in range(MEM_H):
            sl = slice(hd * MEM_HD, (hd + 1) * MEM_HD)
            s = s_next * (MEM_HD ** -0.5)
            if hd + 1 < MEM_H:
                s_next = scores(hd + 1)
            s = s - jnp.max(s, axis=-1, keepdims=True)
            e = jnp.exp(s)
            p = e / jnp.sum(e, axis=-1, keepdims=True)
            ob[b, :, sl] = _mm(p, mv_ref[b, :, sl], small)
        return 0

    if bb_n == 1:
        seq_body(0, 0)
    else:
        lax.fori_loop(0, bb_n, seq_body, 0)
    y = _dot(ob[...].reshape(rows, MEM_W).astype(BF16), wout_ref[...])
    y_ref[...] = (y * gate).astype(y_ref.dtype)


def _mlaproj_kernel(h_ref, cos_ref, sin_ref, wq_ref, qna_ref, wqp_ref, wqs_ref, qg_ref,
                    wkv_ref, kvna_ref, wkr_ref, wkrs_ref, wuk_ref, wuv_ref, kg_ref,
                    q_ref, ckv_ref, kr_ref, k_ref, v_ref):
    h = h_ref[...]
    cos = cos_ref[...]
    sin = sin_ref[...]
    cq = _dot(h, wq_ref[...])
    ckv_raw = _dot(h, wkv_ref[...])
    kr_a = _dot(h, wkr_ref[...])
    kr_b = _dot(h, wkrs_ref[...])
    cqn = _rms(cq, qna_ref[...]).astype(BF16)
    q_raw = _dot(cqn, wqp_ref[...])
    q_swp = _dot(cqn, wqs_ref[...])
    ckv = _rms(ckv_raw, kvna_ref[...])
    cb = ckv.astype(BF16)
    k_raw = _dot(cb, wuk_ref[...])
    v_ref[...] = _dot(cb, wuv_ref[...]).astype(BF16)
    ckv_ref[...] = ckv
    krp = kr_a * cos + kr_b * sin
    kr_ref[...] = krp[:, :MLA_ROPE]

    qg = qg_ref[...]
    inv_n = 1.0 / MLA_QK
    for hd in range(MLA_H):
        sl = slice(hd * MLA_LANES, (hd + 1) * MLA_LANES)
        qh = q_raw[:, sl] * cos + q_swp[:, sl] * sin
        ms = jnp.sum(qh * qh, axis=-1, keepdims=True) * inv_n
        q_ref[:, sl] = (qh * lax.rsqrt(ms + EPS) * qg).astype(q_ref.dtype)

    kg = kg_ref[...]
    for hd in range(MLA_H):
        sl = slice(hd * MLA_LANES, (hd + 1) * MLA_LANES)
        kh = k_raw[:, sl] + krp
        ms = jnp.sum(kh * kh, axis=-1, keepdims=True) * inv_n
        k_ref[:, sl] = (kh * lax.rsqrt(ms + EPS) * kg).astype(k_ref.dtype)


def _branches_kernel(bb_n, tt, tv_last, n_t, n_sc, n_mem, *refs):
    it = iter(refs)
    h_ref, prev_ref, mk_ref, mv_ref, cos_ref, sin_ref = (next(it) for _ in range(6))
    sc_w = [next(it) for _ in range(n_sc)]
    mem_w = [next(it) for _ in range(n_mem)]
    mla_w = [next(it) for _ in range(12)]
    y_sc, sc_new, y_mem, q_ref, ckv_ref, kr_ref, k_ref, v_ref = (next(it) for _ in range(8))
    ubuf, qs, ob = next(it), next(it), next(it)
    _sc_kernel(bb_n, tt, tv_last, n_t, h_ref, prev_ref, *sc_w, y_sc, sc_new, ubuf)
    _memattn_kernel(bb_n, tt, tt < 16, h_ref, mk_ref, mv_ref, *mem_w, y_mem, qs, ob)
    _mlaproj_kernel(h_ref, cos_ref, sin_ref, *mla_w, q_ref, ckv_ref, kr_ref, k_ref, v_ref)


def _branches(h, sc_prev, mk, mv, cos, sin, sc_wts, mem_wts, mla_wts, *, n_seq, t_pad, t_valid, bb_n, tt, qk_dtype):
    n_t = t_pad // tt
    assert n_t == 1 or t_valid == t_pad
    tv_last = t_valid - (n_t - 1) * tt
    rows = bb_n * tt
    m = n_seq * t_pad
    n_tab = cos.shape[0] // rows
    hw = MLA_H * MLA_LANES
    vw = MLA_H * MLA_V
    tile = lambda width: pl.BlockSpec((rows, width), lambda b, t: (b * n_t + t, 0))
    table = pl.BlockSpec((rows, MLA_LANES), lambda b, t: ((b * n_t + t) % n_tab, 0))
    per_seq = lambda *shape: pl.BlockSpec((bb_n,) + shape, lambda b, t: (b,) + (0,) * len(shape))
    wts = list(sc_wts) + list(mem_wts) + list(mla_wts)
    return pl.pallas_call(
        functools.partial(_branches_kernel, bb_n, tt, tv_last, n_t, len(sc_wts), len(mem_wts)),
        grid=(n_seq // bb_n, n_t),
        in_specs=[tile(D_MODEL), per_seq(SC_CONV - 1, SC_W), per_seq(N_MEM, MEM_W), per_seq(N_MEM, MEM_W),
                  table, table] + [_wspec(w, resident=True) for w in wts],
        out_specs=[tile(D_MODEL), per_seq(SC_CONV - 1, SC_W), tile(D_MODEL),
                   tile(hw), tile(MLA_RANK), tile(MLA_ROPE), tile(hw), tile(vw)],
        out_shape=[jax.ShapeDtypeStruct((m, D_MODEL), BRANCH_DTYPE),
                   jax.ShapeDtypeStruct((n_seq, SC_CONV - 1, SC_W), F32),
                   jax.ShapeDtypeStruct((m, D_MODEL), BRANCH_DTYPE),
                   jax.ShapeDtypeStruct((m, hw), qk_dtype),
                   jax.ShapeDtypeStruct((m, MLA_RANK), F32),
                   jax.ShapeDtypeStruct((m, MLA_ROPE), F32),
                   jax.ShapeDtypeStruct((m, hw), qk_dtype),
                   jax.ShapeDtypeStruct((m, vw), BF16)],
        scratch_shapes=[pltpu.VMEM((bb_n, tt + 8, SC_W), F32),
                        pltpu.VMEM((bb_n, tt, MEM_W), F32), pltpu.VMEM((bb_n, tt, MEM_W), F32)],
        compiler_params=_cparams(("parallel", "arbitrary")),
        name="branches",
    )(h, sc_prev, mk, mv, cos, sin, *[w.arr for w in wts])


def _flash_kernel(tq, q_ref, k_ref, v_ref, o_ref, m_scr, l_scr, acc_scr):
    qi = pl.program_id(1)
    ki = pl.program_id(2)
    c2 = (MLA_QK ** -0.5) * LOG2E

    @pl.when(ki == 0)
    def _():
        m_scr[...] = jnp.full(m_scr.shape, -jnp.inf, F32)
        l_scr[...] = jnp.zeros_like(l_scr)
        acc_scr[...] = jnp.zeros_like(acc_scr)

    def compute(diag):
        half = tq // 2
        blocks = [(0, half, half), (half, tq, tq)] if diag else [(0, tq, tq)]
        keep = {}
        if diag:
            for r0, r1, nk in blocks:
                row = lax.broadcasted_iota(jnp.int32, (r1 - r0, nk), 0) + r0
                col = lax.broadcasted_iota(jnp.int32, (r1 - r0, nk), 1)
                keep[r0] = col <= row
        ones = jnp.ones((tq, 128), BF16)
        work = [(hd, blk) for hd in range(MLA_H) for blk in blocks]

        def qk(hd, blk):
            r0, r1, nk = blk
            sl = slice(hd * MLA_LANES, (hd + 1) * MLA_LANES)
            return _dot_nt(q_ref[r0:r1, sl], k_ref[0:nk, sl])

        s_next = qk(*work[0])
        for idx, (hd, (r0, r1, nk)) in enumerate(work):
            s = s_next * c2
            if idx + 1 < len(work):
                s_next = qk(*work[idx + 1])
            if diag:
                s = jnp.where(keep[r0], s, -jnp.inf)
            m_old = m_scr[hd, r0:r1]
            m_new = jnp.maximum(m_old, jnp.max(s, axis=-1, keepdims=True))
            alpha = jnp.exp2(m_old - m_new)
            p = jnp.exp2(s - jnp.concatenate([m_new] * (nk // 128), axis=1)).astype(BF16)
            pair = hd // 2
            vext = jnp.concatenate([v_ref[0:nk, pair * 128:(pair + 1) * 128], ones[0:nk]], axis=1)
            r = _dot(p, vext)
            acc_scr[hd, r0:r1] = alpha * acc_scr[hd, r0:r1] + r[:, :128]
            l_scr[hd, r0:r1] = alpha * l_scr[hd, r0:r1] + r[:, 128:]
            m_scr[hd, r0:r1] = m_new

    @pl.when(ki < qi)
    def _():
        compute(False)

    @pl.when(ki == qi)
    def _():
        compute(True)
        lane = lax.broadcasted_iota(jnp.int32, (tq, 128), 1)
        for pair in range(MLA_H // 2):
            even = acc_scr[2 * pair] / l_scr[2 * pair]
            odd = acc_scr[2 * pair + 1] / l_scr[2 * pair + 1]
            o_ref[:, pair * 128:(pair + 1) * 128] = jnp.where(lane < MLA_V, even, odd).astype(o_ref.dtype)


def _mla_prompt_attn(q, k, v, *, n_seq, seq, tq):
    nq = seq // tq
    hw = MLA_H * MLA_LANES
    vw = MLA_H * MLA_V
    return pl.pallas_call(
        functools.partial(_flash_kernel, tq),
        grid=(n_seq, nq, nq),
        in_specs=[pl.BlockSpec((tq, hw), lambda b, i, j: (b * nq + i, 0)),
                  pl.BlockSpec((tq, hw), lambda b, i, j: (b * nq + jnp.minimum(i, j), 0)),
                  pl.BlockSpec((tq, vw), lambda b, i, j: (b * nq + jnp.minimum(i, j), 0))],
        out_specs=pl.BlockSpec((tq, vw), lambda b, i, j: (b * nq + i, 0)),
        out_shape=jax.ShapeDtypeStruct((n_seq * seq, vw), BF16),
        scratch_shapes=[pltpu.VMEM((MLA_H, tq, 128), F32)] * 3,
        compiler_params=_cparams(("parallel", "parallel", "arbitrary")),
        name="mla_flash",
    )(q, k, v)


SAMPLE_ROWS = 8
SUB_KEYS = 1024


def _mla_sample_kernel(layer, n_seq, n_pg, n_steps, t_valid,
                       pt_ref, qall_ref, q_ref, knew_ref, cnew_ref, wukp_ref, wukt_ref, kg_ref, ckv_hbm, kr_hbm,
                       o_ref,
                       lhs, qabs_all, qabs, qrope, qblk, m_scr, l_scr, acc_scr, cbuf, rbuf, sem):
    b = pl.program_id(0)
    st = pl.program_id(1)
    c2 = (MLA_QK ** -0.5) * LOG2E
    inv_n = 1.0 / MLA_QK
    tk = n_pg * PAGE
    nq = SAMPLE_ROWS
    n_up = MLA_H * MLA_NOPE
    g = b * n_steps + st
    slot = g % 2

    def page_copies(bb, ss, sl):
        cps = []
        for i in range(n_pg):
            page = pt_ref[bb, ss * n_pg + i]
            cps.append((pltpu.make_async_copy(ckv_hbm.at[layer, page], cbuf.at[sl, pl.ds(i * PAGE, PAGE), :],
                                              sem.at[sl, 0]), i % 2))
            cps.append((pltpu.make_async_copy(kr_hbm.at[layer, page], rbuf.at[sl, i], sem.at[sl, 1]), (i + 1) % 2))
        return cps

    n_total = n_seq * n_steps
    is_last = g == n_total - 1

    @pl.when(g == 0)
    def _():
        for cp, prio in page_copies(0, 0, 0):
            cp.start(priority=prio)
        kg = kg_ref[...]
        for hd in range(MLA_H):
            sl = slice(hd * MLA_LANES, (hd + 1) * MLA_LANES)
            qh, ql = _split2(qall_ref[:, sl] * kg)
            w = wukp_ref[:, sl]
            qabs_all[hd] = _dot_nt(qh, w) + _dot_nt(ql, w)

    @pl.when(st == 0)
    def _():
        m_scr[...] = jnp.full(m_scr.shape, -jnp.inf, F32)
        l_scr[...] = jnp.zeros_like(l_scr)
        acc_scr[...] = jnp.zeros_like(acc_scr)
        kg = kg_ref[...]
        q = q_ref[...]
        lane_head = lax.broadcasted_iota(jnp.int32, q.shape, 1) // MLA_LANES
        r0 = pl.multiple_of(b * nq, nq)
        for hd in range(MLA_H):
            sl = slice(hd * MLA_LANES, (hd + 1) * MLA_LANES)
            qabs[hd * nq:(hd + 1) * nq, :] = qabs_all[hd, pl.ds(r0, nq), :]
            qrope[hd * nq:(hd + 1) * nq, :] = q[:, hd * MLA_LANES:hd * MLA_LANES + MLA_ROPE] * kg[:, :MLA_ROPE]
            qblk[hd * nq:(hd + 1) * nq, :] = jnp.where(lane_head == hd, q, 0.0)
        lhs[:n_up, :] = wukt_ref[...]
        lhs[n_up:, :] = qabs[...].astype(BF16)

    for cp, _ in page_copies(b, st, slot):
        cp.wait()

    lhs_v = lhs[...]
    qr = qrope[...].astype(BF16)
    pg_sub = SUB_KEYS // PAGE

    def score_block(j):
        cb = cbuf[slot, j * SUB_KEYS:(j + 1) * SUB_KEYS, :].astype(BF16)
        krt = jnp.concatenate([rbuf[slot, i] for i in range(j * pg_sub, (j + 1) * pg_sub)],
                              axis=1)
        big = _dot_nt(lhs_v, cb)
        knt = big[:n_up]
        ssq = jnp.sum((knt * knt).reshape(MLA_H, MLA_NOPE, SUB_KEYS), axis=1)
        ssq_r = jnp.sum(krt * krt, axis=0, keepdims=True)
        rs = lax.rsqrt((ssq + ssq_r) * inv_n + EPS) * c2
        s = big[n_up:] + _dot(qr, krt.astype(BF16))
        s = jnp.concatenate([s[hd * nq:(hd + 1) * nq, :] * rs[hd:hd + 1, :] for hd in range(MLA_H)], axis=0)
        return s, cb

    m_run = m_scr[...]
    l_new = l_scr[...]
    acc = acc_scr[...]
    n_sub = tk // SUB_KEYS
    blk = score_block(0)

    seq_end = st == n_steps - 1
    nb = jnp.where(is_last, b, jnp.where(seq_end, b + 1, b))
    ns = jnp.where(is_last, st, jnp.where(seq_end, 0, st + 1))
    for cp, prio in page_copies(nb, ns, 1 - slot):
        cp.start(priority=prio)

    for j in range(n_sub):
        s, cb = blk
        if j + 1 < n_sub:
            blk = score_block(j + 1)
        m_new = jnp.maximum(m_run, jnp.max(s, axis=-1, keepdims=True))
        alpha = jnp.exp2(m_run - m_new)
        p = jnp.exp2(s - m_new)
        l_new = alpha * l_new + jnp.sum(p, axis=-1, keepdims=True)
        acc = alpha * acc + _dot(p.astype(BF16), cb)
        m_run = m_new
    l_scr[...] = l_new
    acc_scr[...] = acc
    m_scr[...] = m_new

    @pl.when(st == n_steps - 1)
    def _():
        cn = cnew_ref[...]
        sn = _dot_nt(_rnd(qblk[...]), _rnd(knew_ref[...])) * c2
        row = lax.broadcasted_iota(jnp.int32, sn.shape, 0) % nq
        col = lax.broadcasted_iota(jnp.int32, sn.shape, 1)
        sn = jnp.where((col <= row) & (col < t_valid), sn, -jnp.inf)
        m_o = m_scr[...]
        m_n = jnp.maximum(m_o, jnp.max(sn, axis=-1, keepdims=True))
        al = jnp.exp2(m_o - m_n)
        pn = jnp.exp2(sn - m_n)
        l_f = al * l_scr[...] + jnp.sum(pn, axis=-1, keepdims=True)
        o_ref[...] = (al * acc_scr[...] + _dot(_rnd(pn), _rnd(cn))) / l_f

    @pl.when(is_last)
    def _():
        for cp, _ in page_copies(b, st, 1 - slot):
            cp.wait()


def _mla_sample_attn(page_table, q, k_new, c_new, wts, ckv_pool, kr_pool_t, layer, *, n_seq, t_valid, n_pg):
    n_pages = page_table.shape[1]
    n_steps = n_pages // n_pg
    hw = MLA_H * MLA_LANES
    tk = n_pg * PAGE
    nq = SAMPLE_ROWS
    nr = MLA_H * nq

    const = lambda *shape: pl.BlockSpec(shape, lambda b, s, pt: (0,) * len(shape))
    hbm = pl.BlockSpec(memory_space=pl.ANY)
    in_specs = [const(n_seq * nq, hw),
                pl.BlockSpec((nq, hw), lambda b, s, pt: (b, 0)),
                pl.BlockSpec((nq, hw), lambda b, s, pt: (b, 0)),
                pl.BlockSpec((nq, MLA_RANK), lambda b, s, pt: (b, 0))]
    in_specs += [_wspec(w) for w in wts] + [hbm, hbm]
    grid_spec = pltpu.PrefetchScalarGridSpec(
        num_scalar_prefetch=1,
        grid=(n_seq, n_steps),
        in_specs=in_specs,
        out_specs=pl.BlockSpec((nr, MLA_RANK), lambda b, s, pt: (b, 0)),
        scratch_shapes=[pltpu.VMEM((MLA_H * MLA_NOPE + nr, MLA_RANK), BF16),
                        pltpu.VMEM((MLA_H, n_seq * nq, MLA_RANK), F32),
                        pltpu.VMEM((nr, MLA_RANK), F32), pltpu.VMEM((nr, MLA_ROPE), F32),
                        pltpu.VMEM((nr, hw), F32),
                        pltpu.VMEM((nr, 1), F32), pltpu.VMEM((nr, 1), F32), pltpu.VMEM((nr, MLA_RANK), F32),
                        pltpu.VMEM((2, tk, MLA_RANK), F32), pltpu.VMEM((2, n_pg, MLA_ROPE, PAGE), F32),
                        pltpu.SemaphoreType.DMA((2, 2))],
    )
    return pl.pallas_call(
        functools.partial(_mla_sample_kernel, layer, n_seq, n_pg, n_steps, t_valid),
        grid_spec=grid_spec,
        out_shape=jax.ShapeDtypeStruct((n_seq * nr, MLA_RANK), F32),
        compiler_params=_cparams(("arbitrary", "arbitrary")),
        name="mla_paged",
    )(page_table, q, q, k_new, c_new, *[w.arr for w in wts], ckv_pool, kr_pool_t)


def _mla_up_kernel(n_seq, pc_ref, h_ref, wuvs_ref, wout_ref, wgate_ref, y_ref):
    nq = SAMPLE_ROWS
    o = jnp.zeros((n_seq * nq, MLA_H * MLA_V), F32)
    for hd in range(MLA_H):
        pch = pc_ref[:, hd * nq:(hd + 1) * nq, :].reshape(n_seq * nq, MLA_RANK)
        o = o + _dot(pch.astype(BF16), wuvs_ref[hd])
    y = _dot(o.astype(BF16), wout_ref[...]) * jax.nn.sigmoid(_dot(h_ref[...], wgate_ref[...]))
    y_ref[...] = y.astype(y_ref.dtype)


def _mla_up_proj_gate(pc, h, wuv_sel, w_out, w_gate, *, n_seq):
    rows = n_seq * SAMPLE_ROWS
    nr = MLA_H * SAMPLE_ROWS
    wts = (wuv_sel, w_out, w_gate)
    return pl.pallas_call(
        functools.partial(_mla_up_kernel, n_seq),
        grid=(1,),
        in_specs=[pl.BlockSpec((n_seq, nr, MLA_RANK), lambda i: (0, 0, 0)),
                  pl.BlockSpec((rows, D_MODEL), lambda i: (0, 0))] + [_wspec(w) for w in wts],
        out_specs=pl.BlockSpec((rows, D_MODEL), lambda i: (0, 0)),
        out_shape=jax.ShapeDtypeStruct((rows, D_MODEL), BRANCH_DTYPE),
        compiler_params=_cparams(("arbitrary",)),
        name="mla_up_proj_gate",
    )(pc.reshape(n_seq, nr, MLA_RANK), h, *[w.arr for w in wts])


def _projgate_kernel(o_ref, h_ref, wout_ref, wgate_ref, y_ref):
    y = _dot(o_ref[...].astype(BF16), wout_ref[...]) * jax.nn.sigmoid(_dot(h_ref[...], wgate_ref[...]))
    y_ref[...] = y.astype(y_ref.dtype)


def _proj_gate(o, h, w_out, w_gate, *, tm):
    m, kdim = o.shape
    return pl.pallas_call(
        _projgate_kernel,
        grid=(m // tm,),
        in_specs=[pl.BlockSpec((tm, kdim), lambda i: (i, 0)),
                  pl.BlockSpec((tm, D_MODEL), lambda i: (i, 0)),
                  _wspec(w_out), _wspec(w_gate)],
        out_specs=pl.BlockSpec((tm, D_MODEL), lambda i: (i, 0)),
        out_shape=jax.ShapeDtypeStruct((m, D_MODEL), BRANCH_DTYPE),
        compiler_params=_cparams(("parallel",)),
        name="proj_gate",
    )(o, h, w_out.arr, w_gate.arr)


def _pad_lanes(x, width):
    return jnp.pad(x, [(0, 0)] * (x.ndim - 1) + [(0, width - x.shape[-1])])


def _mla_head_layout(nope, r1, r2):
    z = jnp.zeros(nope.shape[:-1] + (MLA_LANES - MLA_QK,), nope.dtype)
    x = jnp.concatenate([r1, r2, nope, z], axis=-1)
    return x.reshape(x.shape[:-2] + (MLA_H * MLA_LANES,))


def _gain_layout(g):
    half = MLA_ROPE // 2
    return jnp.concatenate([g[MLA_NOPE:MLA_NOPE + half], g[MLA_NOPE + half:], g[:MLA_NOPE],
                            jnp.zeros((MLA_LANES - MLA_QK,), g.dtype)]).reshape(1, MLA_LANES)


def _rope_tables(pos):
    half = MLA_ROPE // 2
    inv = ROPE_THETA ** (-jnp.arange(half, dtype=F32) / half)
    ang = pos.astype(F32)[:, None] * inv
    cos, sin = jnp.cos(ang), jnp.sin(ang)
    n = pos.shape[0]
    cos_t = jnp.concatenate([cos, cos, jnp.ones((n, MLA_NOPE), F32), jnp.zeros((n, MLA_LANES - MLA_QK), F32)], -1)
    sin_t = jnp.concatenate([sin, sin, jnp.zeros((n, MLA_LANES - MLA_ROPE), F32)], -1)
    return cos_t, sin_t


def _layer_weights(p):
    w_in = p['w_in']
    sizes = (DN_CONV_W, DN_QK_W, DN_H, DN_H, SC_W, SC_W, SC_W, MLA_RANK, MLA_RANK, MLA_ROPE, MEM_W, 4 * D_MODEL)
    offs = np.concatenate([[0], np.cumsum(sizes)])
    seg = [w_in[:, offs[i]:offs[i + 1]] for i in range(len(sizes))]
    bf = lambda x: x.astype(BF16)
    row = lambda x: x.reshape(1, -1)
    gates = [bf(seg[11][:, i * D_MODEL:(i + 1) * D_MODEL]) for i in range(4)]
    half = MLA_ROPE // 2

    w = {}
    w['ffn1'] = (row(p['ffn1_norm']), bf(p['ffn1_w_gu']), bf(p['ffn1_w_down']))
    w['ffn2'] = (row(p['ffn2_norm']), bf(p['ffn2_w_gu']), bf(p['ffn2_w_down']))
    w['mix_norm'] = row(p['mix_norm'])
    w['w_o'] = bf(p['w_o'])
    w['dn'] = (bf(seg[0]), bf(seg[1]), bf(_pad_lanes(jnp.concatenate([seg[2], seg[3]], 1), 128)),
               p['dn_conv_w'], _pad_lanes(row(p['dn_A_log']), 128), _pad_lanes(row(p['dn_dt_bias']), 128),
               row(p['dn_norm']), bf(p['dn_w_out']), gates[0])
    w['sc'] = (bf(jnp.concatenate([seg[4], seg[5], seg[6]], 1)), p['sc_conv_w'], bf(p['sc_w_out']), gates[1])

    wq = p['mla_w_q_b'].reshape(MLA_RANK, MLA_H, MLA_QK)
    q_nope, q_r1, q_r2 = wq[..., :MLA_NOPE], wq[..., MLA_NOPE:MLA_NOPE + half], wq[..., MLA_NOPE + half:]
    wq_perm = _mla_head_layout(q_nope, q_r1, q_r2)
    wq_swap = _mla_head_layout(jnp.zeros_like(q_nope), -q_r2, q_r1)
    wkr = seg[9]
    wkr_pad = _pad_lanes(wkr, MLA_LANES)
    wkr_swap = _pad_lanes(jnp.concatenate([-wkr[:, half:], wkr[:, :half]], 1), MLA_LANES)
    wkv = p['mla_w_kv_b'].reshape(MLA_RANK, MLA_H, MLA_NOPE + MLA_V)
    w_uk, w_uv = wkv[..., :MLA_NOPE], wkv[..., MLA_NOPE:]
    zr = jnp.zeros((MLA_RANK, MLA_H, half), F32)
    wuk_perm = bf(_mla_head_layout(w_uk, zr, zr))
    k_gain = _gain_layout(p['mla_k_norm'])
    w['mla_proj'] = (bf(seg[7]), row(p['mla_q_norm_a']), bf(wq_perm), bf(wq_swap), _gain_layout(p['mla_q_norm']),
                     bf(seg[8]), row(p['mla_kv_norm_a']), bf(wkr_pad), bf(wkr_swap),
                     wuk_perm, bf(w_uv.reshape(MLA_RANK, MLA_H * MLA_V)), k_gain)
    eye = jnp.eye(MLA_H, dtype=F32)
    w['mla_wuv_sel'] = bf((w_uv[None] * eye[:, None, :, None]).reshape(MLA_H, MLA_RANK, MLA_H * MLA_V))
    w['mla_sample'] = (wuk_perm, bf(w_uk.reshape(MLA_RANK, MLA_H * MLA_NOPE).T), k_gain)
    w['mla_out'] = (bf(p['mla_w_out']), gates[2])
    w['mem_kv'] = (row(p['mem_norm']), bf(p['mem_w_kv']), row(p['mem_k_norm']))
    w['mem'] = (bf(seg[10]), row(p['mem_q_norm']), bf(p['mem_w_out']), gates[3])
    return w


def _group_layer(x, w, *, n_seq, t_pad, t_valid, tm, bb_n, tt, chunk, dn_state, sc_state, mem_kv, cos, sin,
                 n_tab, mla_attend, q_dtype):
    cfg = dict(n_seq=n_seq, t_pad=t_pad, bb_n=bb_n, tt=tt)
    x1, h = _ffn(x, *w['ffn1'], tm=tm, h_gain=w['mix_norm'])
    y_dn, dn_s, dn_c = _deltanet(h, dn_state[0], dn_state[1], w['dn'], t_valid=t_valid, chunk=chunk, **cfg)
    y_sc, sc_c, y_mem, q, ckv, kr, k, v = _branches(
        h, sc_state, mem_kv[0], mem_kv[1], cos, sin, w['sc'], w['mem'], w['mla_proj'],
        t_valid=t_valid, qk_dtype=q_dtype, **cfg)
    if mla_attend is None:
        o = _mla_prompt_attn(q, k, v, n_seq=n_seq, seq=t_pad, tq=tm)
        y_mla = _proj_gate(o, h, *w['mla_out'], tm=tm)
    else:
        y_mla = mla_attend(q, k, ckv, h)
    x3 = _ffn(x1, *w['ffn2'], tm=tm, merge=((y_dn, y_sc, y_mla, y_mem), w['w_o']))
    return x3, dn_s, dn_c, sc_c, ckv, kr


def kernel(x_prompt, x_sample, state_dn_S, state_dn_conv, state_sc_conv, cache_mla_ckv, cache_mla_krope, cache_mem_k, cache_mem_v, page_table, mem_prompt, ffn1_norm, ffn1_w_gu, ffn1_w_down, mix_norm, w_in, dn_conv_w, dn_A_log, dn_dt_bias, dn_norm, dn_w_out, sc_conv_w, sc_w_out, mla_q_norm_a, mla_w_q_b, mla_kv_norm_a, mla_w_kv_b, mla_q_norm, mla_k_norm, mla_w_out, mem_norm, mem_w_kv, mem_q_norm, mem_k_norm, mem_w_out, w_o, ffn2_norm, ffn2_w_gu, ffn2_w_down):
    params = dict(ffn1_norm=ffn1_norm, ffn1_w_gu=ffn1_w_gu, ffn1_w_down=ffn1_w_down, mix_norm=mix_norm, w_in=w_in,
                  dn_conv_w=dn_conv_w, dn_A_log=dn_A_log, dn_dt_bias=dn_dt_bias, dn_norm=dn_norm, dn_w_out=dn_w_out,
                  sc_conv_w=sc_conv_w, sc_w_out=sc_w_out, mla_q_norm_a=mla_q_norm_a, mla_w_q_b=mla_w_q_b,
                  mla_kv_norm_a=mla_kv_norm_a, mla_w_kv_b=mla_w_kv_b, mla_q_norm=mla_q_norm, mla_k_norm=mla_k_norm,
                  mla_w_out=mla_w_out, mem_norm=mem_norm, mem_w_kv=mem_w_kv, mem_q_norm=mem_q_norm,
                  mem_k_norm=mem_k_norm, mem_w_out=mem_w_out, w_o=w_o, ffn2_norm=ffn2_norm, ffn2_w_gu=ffn2_w_gu,
                  ffn2_w_down=ffn2_w_down)
    depth = w_in.shape[0]
    bp, seq, _ = x_prompt.shape
    bs, td, _ = x_sample.shape
    tds = SAMPLE_ROWS
    n_pages = page_table.shape[1]
    past = n_pages * PAGE
    krope_t = jnp.transpose(cache_mla_krope, (0, 1, 3, 2))

    cos_p, sin_p = _rope_tables(jnp.arange(seq))
    cos_s, sin_s = _rope_tables(past + jnp.arange(tds))
    cos_s, sin_s = jnp.tile(cos_s, (bs, 1)), jnp.tile(sin_s, (bs, 1))

    xp = x_prompt.reshape(bp * seq, D_MODEL)
    xs = jnp.pad(x_sample, ((0, 0), (0, tds - td), (0, 0))).reshape(bs * tds, D_MODEL)
    zero_s = jnp.zeros((bp, DN_H, DN_DK, DN_DK), F32)
    zero_dc = jnp.zeros((bp, DN_CONV - 1, DN_CONV_W), F32)
    zero_sc = jnp.zeros((bp, SC_CONV - 1, SC_W), F32)
    mem2d = mem_prompt.reshape(bp * N_MEM, D_MODEL)

    outs = {k: [] for k in ('pS', 'pdc', 'psc', 'pckv', 'pkr', 'pmk', 'pmv', 'sS', 'sdc', 'ssc', 'sckv', 'skr')}
    tm_p = 512
    w_all = jax.vmap(_layer_weights)(params)
    for l in range(depth):
        w = jax.tree.map(lambda a: _W(a, l), w_all)
        mk, mv = _mem_kv(mem2d, w['mem_kv'], tm=tm_p)
        mk3, mv3 = mk.reshape(bp, N_MEM, MEM_W), mv.reshape(bp, N_MEM, MEM_W)
        xp, s_p, dc_p, sc_p, ckv_p, kr_p = _group_layer(
            xp, w, n_seq=bp, t_pad=seq, t_valid=seq, tm=tm_p, bb_n=1, tt=tm_p, chunk=DN_CHUNK,
            dn_state=(zero_s, zero_dc), sc_state=zero_sc, mem_kv=(mk3, mv3), cos=cos_p, sin=sin_p,
            n_tab=seq // tm_p, mla_attend=None, q_dtype=BF16)
        outs['pS'].append(s_p); outs['pdc'].append(dc_p); outs['psc'].append(sc_p)
        outs['pckv'].append(ckv_p.reshape(bp, seq, MLA_RANK)); outs['pkr'].append(kr_p.reshape(bp, seq, MLA_ROPE))
        outs['pmk'].append(mk.reshape(bp, N_MEM, MEM_H, MEM_HD)); outs['pmv'].append(mv.reshape(bp, N_MEM, MEM_H, MEM_HD))

        def attend(q, k, ckv, h, l=l, w=w):
            pc = _mla_sample_attn(page_table, q, k, ckv, w['mla_sample'], cache_mla_ckv, krope_t, l,
                                  n_seq=bs, t_valid=td, n_pg=32)
            return _mla_up_proj_gate(pc, h, w['mla_wuv_sel'], *w['mla_out'], n_seq=bs)

        xs, s_s, dc_s, sc_s, ckv_s, kr_s = _group_layer(
            xs, w, n_seq=bs, t_pad=tds, t_valid=td, tm=bs * tds, bb_n=8, tt=tds, chunk=tds,
            dn_state=(state_dn_S[l], state_dn_conv[l]), sc_state=state_sc_conv[l],
            mem_kv=(cache_mem_k[l].reshape(bs, N_MEM, MEM_W), cache_mem_v[l].reshape(bs, N_MEM, MEM_W)),
            cos=cos_s, sin=sin_s, n_tab=1, mla_attend=attend, q_dtype=F32)
        outs['sS'].append(s_s); outs['sdc'].append(dc_s); outs['ssc'].append(sc_s)
        outs['sckv'].append(ckv_s.reshape(bs, tds, MLA_RANK)[:, :td])
        outs['skr'].append(kr_s.reshape(bs, tds, MLA_ROPE)[:, :td])

    st = lambda k: jnp.stack(outs[k])
    y_prompt = xp.reshape(bp, seq, D_MODEL)
    y_sample = xs.reshape(bs, tds, D_MODEL)[:, :td]
    return (y_prompt, y_sample, st('pS'), st('pdc'), st('psc'), st('pckv'), st('pkr'), st('pmk'), st('pmv'),
            st('sS'), st('sdc'), st('ssc'), st('sckv'), st('skr'))
```

```python
import functools

import numpy as np
import jax
import jax.numpy as jnp
from jax import lax
from jax.experimental import pallas as pl
from jax.experimental.pallas import tpu as pltpu

F32 = jnp.float32
BF16 = jnp.bfloat16

D_MODEL = 1024
D_FF = 2816
EPS = 1e-6
N_MEM = 256
PAGE = 128
DN_H = 4
DN_DK = 128
DN_QK_W = 512
DN_CONV_W = 1536
DN_CONV = 4
DN_CHUNK = 64
DN_UNROLL_GROUPS = 4
DN_GROUP = 4
SC_W = 512
SC_CONV = 3
MLA_H = 8
MLA_RANK = 256
MLA_NOPE = 64
MLA_ROPE = 32
MLA_V = 64
MLA_QK = 96
MLA_LANES = 128
ROPE_THETA = 10000.0
LOG2E = 1.4426950408889634
MEM_H = 4
MEM_HD = 128
MEM_W = 512

VMEM_LIMIT = 56 * 1024 * 1024
BRANCH_DTYPE = jnp.bfloat16


def _cparams(sem):
    return pltpu.CompilerParams(dimension_semantics=sem, vmem_limit_bytes=VMEM_LIMIT)


class _W:
    def __init__(self, arr, layer):
        self.arr, self.layer = arr, layer


def _wspec(w, resident=False):
    shape = w.arr.shape[1:]
    index = lambda *_: (w.layer,) + (0,) * len(shape)
    if resident:
        return pl.BlockSpec((None,) + shape, index, pipeline_mode=pl.Buffered(1))
    return pl.BlockSpec((None,) + shape, index)


def _rms(x, g):
    ms = jnp.mean(x * x, axis=-1, keepdims=True)
    return x * lax.rsqrt(ms + EPS) * g


def _silu(x):
    return x * jax.nn.sigmoid(x)


def _rnd(x):
    return x.astype(BF16).astype(F32)


def _dot(a, b):
    return jnp.dot(a, b, preferred_element_type=F32)


def _dot_nt(a, b):
    return lax.dot_general(a, b, (((1,), (1,)), ((), ())), preferred_element_type=F32)


def _mm(a, b, small):
    if small:
        return _dot(_rnd(a), _rnd(b))
    return _dot(a.astype(BF16), b.astype(BF16))


def _mm_nt(a, b, small):
    if small:
        return _dot_nt(_rnd(a), _rnd(b))
    return _dot_nt(a.astype(BF16), b.astype(BF16))


def _mm_tn(a, b, small):
    dn = (((0,), (0,)), ((), ()))
    if small:
        return lax.dot_general(_rnd(a), _rnd(b), dn, preferred_element_type=F32)
    return lax.dot_general(a.astype(BF16), b.astype(BF16), dn, preferred_element_type=F32)


def _run_interleaved(gens):
    gens = list(gens)
    while gens:
        for g in list(gens):
            try:
                next(g)
            except StopIteration:
                gens.remove(g)


def _split2(x):
    hi = x.astype(BF16)
    lo = (x - hi.astype(F32)).astype(BF16)
    return hi, lo


def _mm_hi(a, b, small):
    if small:
        return jnp.dot(a, b, preferred_element_type=F32, precision=lax.Precision.HIGHEST)
    ah, al = _split2(a)
    bh, bl = _split2(b)
    return _dot(ah, bh) + (_dot(ah, bl) + _dot(al, bh))


def _mm_exact_left(lmat, b, small):
    if small:
        return jnp.dot(lmat, b, preferred_element_type=F32, precision=lax.Precision.HIGHEST)
    lb = lmat.astype(BF16)
    b1 = b.astype(BF16)
    r1 = b - b1.astype(F32)
    b2 = r1.astype(BF16)
    b3 = (r1 - b2.astype(F32)).astype(BF16)
    return _dot(lb, b1) + (_dot(lb, b2) + _dot(lb, b3))


FFN_SPLIT = 2


def _ffn_kernel(merge, emit_h, *refs):
    it = iter(refs)
    x_ref = next(it)
    if merge:
        y_refs = [next(it) for _ in range(4)]
        wo_ref = next(it)
    g_ref, wgu_ref, wd_ref = next(it), next(it), next(it)
    if emit_h:
        g2_ref = next(it)
    o_ref = next(it)
    if emit_h:
        h_ref = next(it)

    x = x_ref[...]
    if merge:
        ys = [r[...].astype(F32) for r in y_refs]
        m = ((ys[0] + ys[1]) + ys[2]) + ys[3]
        x = x + _dot(m.astype(BF16), wo_ref[...])
    h = _rms(x, g_ref[...]).astype(BF16)
    tf = D_FF // FFN_SPLIT
    def gate_up(j):
        return (_dot(h, wgu_ref[:, j * tf:(j + 1) * tf]),
                _dot(h, wgu_ref[:, D_FF + j * tf:D_FF + (j + 1) * tf]))

    acc = None
    nxt = gate_up(0)
    for j in range(FFN_SPLIT):
        gate, up = nxt
        if j + 1 < FFN_SPLIT:
            nxt = gate_up(j + 1)
        a = (_silu(gate) * up).astype(BF16)
        d = _dot(a, wd_ref[j * tf:(j + 1) * tf, :])
        acc = d if acc is None else acc + d
    out = x + 0.5 * acc
    o_ref[...] = out
    if emit_h:
        h_ref[...] = _rms(out, g2_ref[...]).astype(BF16)


def _ffn(x, norm_g, w_gu, w_down, *, tm, merge=None, h_gain=None):
    m = x.shape[0]
    row = pl.BlockSpec((tm, D_MODEL), lambda i: (i, 0))
    in_specs, args = [row], [x]
    if merge is not None:
        ys, w_o = merge
        in_specs += [row] * 4 + [_wspec(w_o, resident=True)]
        args += list(ys) + [w_o.arr]
    in_specs += [_wspec(w, resident=True) for w in (norm_g, w_gu, w_down)]
    args += [norm_g.arr, w_gu.arr, w_down.arr]
    out_shape = [jax.ShapeDtypeStruct((m, D_MODEL), F32)]
    out_specs = [row]
    if h_gain is not None:
        in_specs.append(_wspec(h_gain, resident=True))
        args.append(h_gain.arr)
        out_shape.append(jax.ShapeDtypeStruct((m, D_MODEL), BF16))
        out_specs.append(row)
    res = pl.pallas_call(
        functools.partial(_ffn_kernel, merge is not None, h_gain is not None),
        grid=(m // tm,),
        in_specs=in_specs, out_specs=out_specs, out_shape=out_shape,
        compiler_params=_cparams(("parallel",)),
        name="ffn_merge" if merge is not None else "ffn",
    )(*args)
    return res if h_gain is not None else res[0]


def _dn_kernel(bb_n, tt, chunk, tv_last, n_t, small,
               h_ref, s0_ref, cprev_ref, wqkv_ref, wz_ref, wab_ref, convw_ref, alog_ref, dtb_ref,
               normg_ref, wout_ref, wgate_ref,
               y_ref, snew_ref, cnew_ref,
               xbuf, cs, gb, zb, ob, s_all, u_s, w_s, qg_s, kdec_s, aqk_s, gl_s, gate_s):
    t = pl.program_id(1)
    rows = bb_n * tt

    @pl.when(t == 0)
    def _():
        xbuf[:, 5:8, :] = cprev_ref[...]
        snew_ref[...] = s0_ref[...]

    h = h_ref[...]
    qkv = _dot(h, wqkv_ref[...])
    ab = _dot(h, wab_ref[...])
    zs = _dot(h, wz_ref[...])
    gate = _dot(h, wgate_ref[...])
    xbuf[:, 8:8 + tt, :] = qkv.reshape(bb_n, tt, DN_CONV_W)
    c = 0.0
    for j in range(DN_CONV):
        c = c + xbuf[:, 5 + j:5 + j + tt, :] * convw_ref[j:j + 1, :].reshape(1, 1, DN_CONV_W)
    cnew_ref[...] = xbuf[:, 5 + tv_last:8 + tv_last, :]
    if n_t > 1:
        xbuf[:, 0:8, :] = xbuf[:, tt:tt + 8, :]

    g = -jnp.exp(alog_ref[...]) * jax.nn.softplus(ab + dtb_ref[...])
    lane = lax.broadcasted_iota(jnp.int32, (rows, 128), 1)
    gbv = jnp.where(lane < DN_H, g, jax.nn.sigmoid(ab)).reshape(bb_n, tt, 128)
    if tv_last < tt:
        trow = lax.broadcasted_iota(jnp.int32, (bb_n, tt, 128), 1)
        gbv = jnp.where(trow < tv_last, gbv, 0.0)
    gb[...] = gbv
    c = _silu(c)
    for grp in range(8):
        sl = slice(grp * 128, (grp + 1) * 128)
        xg = c[:, :, sl]
        xn = xg * lax.rsqrt(jnp.sum(xg * xg, axis=-1, keepdims=True) + EPS)
        if grp < DN_H:
            xn = xn * (DN_DK ** -0.5)
        cs[:, :, sl] = xn
    cs[:, :, 2 * DN_QK_W:] = c[:, :, 2 * DN_QK_W:]
    zb[...] = _silu(zs).reshape(bb_n, tt, DN_QK_W)
    gate_s[...] = jax.nn.sigmoid(gate)

    n4 = DN_H * chunk
    ri = lax.broadcasted_iota(jnp.int32, (n4, n4), 0)
    ci = lax.broadcasted_iota(jnp.int32, (n4, n4), 1)
    same = (ri // chunk) == (ci // chunk)
    incl = same & (ci <= ri)
    strict = same & (ci < ri)
    lmat = incl.astype(F32)
    umat = strict.astype(F32)
    vmask = (lax.broadcasted_iota(jnp.int32, (n4, DN_H * 128), 0) // chunk
             == lax.broadcasted_iota(jnp.int32, (n4, DN_H * 128), 1) // 128)
    n_pow = int(np.log2(chunk))
    normg = normg_ref[...]
    n_ch = tt // chunk
    total = bb_n * n_ch
    solve_mm = _mm_hi if small else _mm

    for hd in range(DN_H):
        s_all[:, :, hd * 128:(hd + 1) * 128] = snew_ref[:, hd]

    def stack_rows(ref, b, r0, off):
        return jnp.concatenate([ref[b, pl.ds(r0, chunk), off + hd * 128:off + (hd + 1) * 128]
                                for hd in range(DN_H)], axis=0)

    def level1(k, slot):
        b = k // n_ch
        r0 = (k % n_ch) * chunk
        if not isinstance(k, int):
            r0 = pl.multiple_of(r0, chunk)
        q = stack_rows(cs, b, r0, 0)
        kk = stack_rows(cs, b, r0, DN_QK_W)
        v = stack_rows(cs, b, r0, 2 * DN_QK_W)
        gbc = gb[b, pl.ds(r0, chunk), :]
        g_st = jnp.concatenate([jnp.broadcast_to(gbc[:, hd:hd + 1], (chunk, 128)) for hd in range(DN_H)], axis=0)
        beta_st = jnp.concatenate([jnp.broadcast_to(gbc[:, DN_H + hd:DN_H + hd + 1], (chunk, 128))
                                   for hd in range(DN_H)], axis=0)
        g_sq = jnp.concatenate([g_st] * (n4 // 128), axis=1) if n4 >= 128 else g_st[:, :n4]
        gc = _mm_exact_left(lmat, g_st, small)
        yield
        if small:
            dmat = _mm_exact_left(lmat, g_sq * umat, small)
        else:
            gct = gc.T
            dmat = (jnp.concatenate([gc] * (n4 // 128), axis=1)
                    - jnp.concatenate([gct] * (n4 // 128), axis=0))
        gam = jnp.where(incl, jnp.exp(dmat), 0.0)
        eg = jnp.exp(gc)
        kb = kk * beta_st
        kq = _mm_nt(jnp.concatenate([kb, q], axis=0), kk, small)
        yield
        a_mat = jnp.where(strict, kq[:n4] * gam, 0.0)
        x = jnp.concatenate([v * beta_st, kb * eg], axis=1)
        p = -a_mat
        for i in range(n_pow):
            x = x + (_mm_hi if i < 2 else solve_mm)(p, x, small)
            if i < n_pow - 1:
                p = (_mm_hi if i < 1 else solve_mm)(p, p, small)
            yield
        gc_last = [gc[(hd + 1) * chunk - 1:(hd + 1) * chunk, :] for hd in range(DN_H)]
        gl_st = jnp.concatenate([jnp.broadcast_to(r, (chunk, 128)) for r in gc_last], axis=0)
        u_s[slot] = x[:, :128]
        w_s[slot] = x[:, 128:]
        qg_s[slot] = q * eg
        kdec_s[slot] = kk * jnp.exp(gl_st - gc)
        aqk_s[slot] = jnp.where(incl, kq[n4:] * gam, 0.0)
        gl_s[slot] = jnp.exp(jnp.concatenate(gc_last, axis=1))

    def level2(k, slot):
        b = k // n_ch
        r0 = (k % n_ch) * chunk
        if not isinstance(k, int):
            r0 = pl.multiple_of(r0, chunk)
        s_old = s_all[b]
        w = w_s[slot]
        qg = qg_s[slot]
        ws, qs = [], []
        for hd in range(DN_H):
            rs = slice(hd * chunk, (hd + 1) * chunk)
            r = _mm(jnp.concatenate([w[rs], qg[rs]], axis=0), s_old[:, hd * 128:(hd + 1) * 128], small)
            ws.append(r[:chunk])
            qs.append(r[chunk:])
        yield
        v_new = u_s[slot] - jnp.concatenate(ws, axis=0)
        o = jnp.concatenate(qs, axis=0) + _mm(aqk_s[slot], v_new, small)
        vbd = jnp.where(vmask, jnp.concatenate([v_new] * DN_H, axis=1), 0.0)
        s_all[b] = s_old * gl_s[slot] + _mm_tn(kdec_s[slot], vbd, small)
        yield
        on = _rms(o, normg)
        for hd in range(DN_H):
            sl = slice(hd * 128, (hd + 1) * 128)
            ob[b, pl.ds(r0, chunk), sl] = on[hd * chunk:(hd + 1) * chunk] * zb[b, pl.ds(r0, chunk), sl]

    assert total % DN_GROUP == 0

    def recurrence(m, base):
        for i in range(DN_GROUP):
            yield from level2(m * DN_GROUP + i, base + i)

    _run_interleaved([level1(i, i) for i in range(DN_GROUP)])

    n_groups = total // DN_GROUP

    def group_body(m, _):
        base = (m % 2) * DN_GROUP
        chains = [recurrence(m, base)]
        if not isinstance(m, int):
            chains += [level1(jnp.minimum((m + 1) * DN_GROUP + i, total - 1), DN_GROUP - base + i)
                       for i in range(DN_GROUP)]
        elif m + 1 < n_groups:
            chains += [level1((m + 1) * DN_GROUP + i, DN_GROUP - base + i) for i in range(DN_GROUP)]
        _run_interleaved(chains)
        return 0

    if n_groups <= DN_UNROLL_GROUPS:
        for m in range(n_groups):
            group_body(m, 0)
    else:
        lax.fori_loop(0, n_groups, group_body, 0)

    for hd in range(DN_H):
        snew_ref[:, hd] = s_all[:, :, hd * 128:(hd + 1) * 128]

    y = _dot(ob[...].reshape(rows, DN_QK_W).astype(BF16), wout_ref[...])
    y_ref[...] = (y * gate_s[...]).astype(y_ref.dtype)


def _deltanet(h, s0, cprev, wts, *, n_seq, t_pad, t_valid, bb_n, tt, chunk):
    n_t = t_pad // tt
    assert n_t == 1 or t_valid == t_pad
    tv_last = t_valid - (n_t - 1) * tt
    rows = bb_n * tt
    small = chunk < 16
    n4 = DN_H * chunk
    in_specs = [
        pl.BlockSpec((rows, D_MODEL), lambda b, t: (b * n_t + t, 0)),
        pl.BlockSpec((bb_n, DN_H, DN_DK, DN_DK), lambda b, t: (b, 0, 0, 0)),
        pl.BlockSpec((bb_n, DN_CONV - 1, DN_CONV_W), lambda b, t: (b, 0, 0)),
    ] + [_wspec(w) for w in wts]
    out_specs = [
        pl.BlockSpec((rows, D_MODEL), lambda b, t: (b * n_t + t, 0)),
        pl.BlockSpec((bb_n, DN_H, DN_DK, DN_DK), lambda b, t: (b, 0, 0, 0)),
        pl.BlockSpec((bb_n, DN_CONV - 1, DN_CONV_W), lambda b, t: (b, 0, 0)),
    ]
    out_shape = [
        jax.ShapeDtypeStruct((n_seq * t_pad, D_MODEL), BRANCH_DTYPE),
        jax.ShapeDtypeStruct((n_seq, DN_H, DN_DK, DN_DK), F32),
        jax.ShapeDtypeStruct((n_seq, DN_CONV - 1, DN_CONV_W), F32),
    ]
    return pl.pallas_call(
        functools.partial(_dn_kernel, bb_n, tt, chunk, tv_last, n_t, small),
        grid=(n_seq // bb_n, n_t),
        in_specs=in_specs, out_specs=out_specs, out_shape=out_shape,
        scratch_shapes=[pltpu.VMEM((bb_n, tt + 8, DN_CONV_W), F32), pltpu.VMEM((bb_n, tt, DN_CONV_W), F32),
                        pltpu.VMEM((bb_n, tt, 128), F32), pltpu.VMEM((bb_n, tt, DN_QK_W), F32),
                        pltpu.VMEM((bb_n, tt, DN_QK_W), F32),
                        pltpu.VMEM((bb_n, DN_DK, DN_H * 128), F32)]
                       + [pltpu.VMEM((2 * DN_GROUP, n4, 128), F32)] * 4
                       + [pltpu.VMEM((2 * DN_GROUP, n4, n4), F32), pltpu.VMEM((2 * DN_GROUP, 1, DN_H * 128), F32),
                          pltpu.VMEM((rows, D_MODEL), F32)],
        compiler_params=_cparams(("parallel", "arbitrary")),
        name="deltanet",
    )(h, s0, cprev, *[w.arr for w in wts])


def _sc_kernel(bb_n, tt, tv_last, n_t,
               h_ref, prev_ref, win_ref, convw_ref, wout_ref, wgate_ref,
               y_ref, new_ref, ubuf):
    t = pl.program_id(1)
    rows = bb_n * tt

    @pl.when(t == 0)
    def _():
        ubuf[:, 6:8, :] = prev_ref[...]

    h = h_ref[...]
    p = _dot(h, win_ref[...])
    gate = jax.nn.sigmoid(_dot(h, wgate_ref[...]))
    bgate = p[:, :SC_W]
    u = p[:, SC_W:2 * SC_W] * p[:, 2 * SC_W:]
    ubuf[:, 8:8 + tt, :] = u.reshape(bb_n, tt, SC_W)
    y = 0.0
    for j in range(SC_CONV):
        y = y + ubuf[:, 6 + j:6 + j + tt, :] * convw_ref[j:j + 1, :].reshape(1, 1, SC_W)

    new_ref[...] = ubuf[:, 6 + tv_last:8 + tv_last, :]
    if n_t > 1:
        ubuf[:, 0:8, :] = ubuf[:, tt:tt + 8, :]

    z = (bgate * y.reshape(rows, SC_W)).astype(BF16)
    y_ref[...] = (_dot(z, wout_ref[...]) * gate).astype(y_ref.dtype)


def _memkv_kernel(m_ref, g_ref, wkv_ref, kg_ref, k_ref, v_ref):
    n = _rms(m_ref[...], g_ref[...]).astype(BF16)
    kv = _dot(n, wkv_ref[...])
    kg = kg_ref[...]
    for hd in range(MEM_H):
        sl = slice(hd * MEM_HD, (hd + 1) * MEM_HD)
        k_ref[:, sl] = _rms(kv[:, sl], kg)
    v_ref[...] = kv[:, MEM_W:]


def _mem_kv(mem2d, wts, *, tm):
    m = mem2d.shape[0]
    return pl.pallas_call(
        _memkv_kernel,
        grid=(m // tm,),
        in_specs=[pl.BlockSpec((tm, D_MODEL), lambda i: (i, 0))] + [_wspec(w) for w in wts],
        out_specs=[pl.BlockSpec((tm, MEM_W), lambda i: (i, 0))] * 2,
        out_shape=[jax.ShapeDtypeStruct((m, MEM_W), F32)] * 2,
        compiler_params=_cparams(("parallel",)),
        name="mem_kv",
    )(mem2d, *[w.arr for w in wts])


def _memattn_kernel(bb_n, tt, small,
                    h_ref, mk_ref, mv_ref, wq_ref, qg_ref, wout_ref, wgate_ref,
                    y_ref, qs, ob):
    rows = bb_n * tt
    h = h_ref[...]
    q = _dot(h, wq_ref[...])
    gate = jax.nn.sigmoid(_dot(h, wgate_ref[...]))
    qg = qg_ref[...]
    for hd in range(MEM_H):
        sl = slice(hd * MEM_HD, (hd + 1) * MEM_HD)
        qs[:, :, sl] = _rms(q[:, sl], qg).reshape(bb_n, tt, MEM_HD)

    def seq_body(b, _):
        def scores(hd):
            sl = slice(hd * MEM_HD, (hd + 1) * MEM_HD)
            return _mm_nt(qs[b, :, sl], mk_ref[b, :, sl], small)

        s_next = scores(0)
        for hd in range(MEM_H):
            sl = slice(hd * MEM_HD, (hd + 1) * MEM_HD)
            s = s_next * (MEM_HD ** -0.5)
            if hd + 1 < MEM_H:
                s_next = scores(hd + 1)
            s = s - jnp.max(s, axis=-1, keepdims=True)
            e = jnp.exp(s)
            p = e / jnp.sum(e, axis=-1, keepdims=True)
            ob[b, :, sl] = _mm(p, mv_ref[b, :, sl], small)
        return 0

    if bb_n == 1:
        seq_body(0, 0)
    else:
        lax.fori_loop(0, bb_n, seq_body, 0)
    y = _dot(ob[...].reshape(rows, MEM_W).astype(BF16), wout_ref[...])
    y_ref[...] = (y * gate).astype(y_ref.dtype)


def _mlaproj_kernel(h_ref, cos_ref, sin_ref, wq_ref, qna_ref, wqp_ref, wqs_ref, qg_ref,
                    wkv_ref, kvna_ref, wkr_ref, wkrs_ref, wuk_ref, wuv_ref, kg_ref,
                    q_ref, ckv_ref, kr_ref, k_ref, v_ref):
    h = h_ref[...]
    cos = cos_ref[...]
    sin = sin_ref[...]
    cq = _dot(h, wq_ref[...])
    ckv_raw = _dot(h, wkv_ref[...])
    kr_a = _dot(h, wkr_ref[...])
    kr_b = _dot(h, wkrs_ref[...])
    cqn = _rms(cq, qna_ref[...]).astype(BF16)
    q_raw = _dot(cqn, wqp_ref[...])
    q_swp = _dot(cqn, wqs_ref[...])
    ckv = _rms(ckv_raw, kvna_ref[...])
    cb = ckv.astype(BF16)
    k_raw = _dot(cb, wuk_ref[...])
    v_ref[...] = _dot(cb, wuv_ref[...]).astype(BF16)
    ckv_ref[...] = ckv
    krp = kr_a * cos + kr_b * sin
    kr_ref[...] = krp[:, :MLA_ROPE]

    qg = qg_ref[...]
    inv_n = 1.0 / MLA_QK
    for hd in range(MLA_H):
        sl = slice(hd * MLA_LANES, (hd + 1) * MLA_LANES)
        qh = q_raw[:, sl] * cos + q_swp[:, sl] * sin
        ms = jnp.sum(qh * qh, axis=-1, keepdims=True) * inv_n
        q_ref[:, sl] = (qh * lax.rsqrt(ms + EPS) * qg).astype(q_ref.dtype)

    kg = kg_ref[...]
    for hd in range(MLA_H):
        sl = slice(hd * MLA_LANES, (hd + 1) * MLA_LANES)
        kh = k_raw[:, sl] + krp
        ms = jnp.sum(kh * kh, axis=-1, keepdims=True) * inv_n
        k_ref[:, sl] = (kh * lax.rsqrt(ms + EPS) * kg).astype(k_ref.dtype)


def _branches_kernel(bb_n, tt, tv_last, n_t, n_sc, n_mem, *refs):
    it = iter(refs)
    h_ref, prev_ref, mk_ref, mv_ref, cos_ref, sin_ref = (next(it) for _ in range(6))
    sc_w = [next(it) for _ in range(n_sc)]
    mem_w = [next(it) for _ in range(n_mem)]
    mla_w = [next(it) for _ in range(12)]
    y_sc, sc_new, y_mem, q_ref, ckv_ref, kr_ref, k_ref, v_ref = (next(it) for _ in range(8))
    ubuf, qs, ob = next(it), next(it), next(it)
    _sc_kernel(bb_n, tt, tv_last, n_t, h_ref, prev_ref, *sc_w, y_sc, sc_new, ubuf)
    _memattn_kernel(bb_n, tt, tt < 16, h_ref, mk_ref, mv_ref, *mem_w, y_mem, qs, ob)
    _mlaproj_kernel(h_ref, cos_ref, sin_ref, *mla_w, q_ref, ckv_ref, kr_ref, k_ref, v_ref)


def _branches(h, sc_prev, mk, mv, cos, sin, sc_wts, mem_wts, mla_wts, *, n_seq, t_pad, t_valid, bb_n, tt, qk_dtype):
    n_t = t_pad // tt
    assert n_t == 1 or t_valid == t_pad
    tv_last = t_valid - (n_t - 1) * tt
    rows = bb_n * tt
    m = n_seq * t_pad
    n_tab = cos.shape[0] // rows
    hw = MLA_H * MLA_LANES
    vw = MLA_H * MLA_V
    tile = lambda width: pl.BlockSpec((rows, width), lambda b, t: (b * n_t + t, 0))
    table = pl.BlockSpec((rows, MLA_LANES), lambda b, t: ((b * n_t + t) % n_tab, 0))
    per_seq = lambda *shape: pl.BlockSpec((bb_n,) + shape, lambda b, t: (b,) + (0,) * len(shape))
    wts = list(sc_wts) + list(mem_wts) + list(mla_wts)
    return pl.pallas_call(
        functools.partial(_branches_kernel, bb_n, tt, tv_last, n_t, len(sc_wts), len(mem_wts)),
        grid=(n_seq // bb_n, n_t),
        in_specs=[tile(D_MODEL), per_seq(SC_CONV - 1, SC_W), per_seq(N_MEM, MEM_W), per_seq(N_MEM, MEM_W),
                  table, table] + [_wspec(w, resident=True) for w in wts],
        out_specs=[tile(D_MODEL), per_seq(SC_CONV - 1, SC_W), tile(D_MODEL),
                   tile(hw), tile(MLA_RANK), tile(MLA_ROPE), tile(hw), tile(vw)],
        out_shape=[jax.ShapeDtypeStruct((m, D_MODEL), BRANCH_DTYPE),
                   jax.ShapeDtypeStruct((n_seq, SC_CONV - 1, SC_W), F32),
                   jax.ShapeDtypeStruct((m, D_MODEL), BRANCH_DTYPE),
                   jax.ShapeDtypeStruct((m, hw), qk_dtype),
                   jax.ShapeDtypeStruct((m, MLA_RANK), F32),
                   jax.ShapeDtypeStruct((m, MLA_ROPE), F32),
                   jax.ShapeDtypeStruct((m, hw), qk_dtype),
                   jax.ShapeDtypeStruct((m, vw), BF16)],
        scratch_shapes=[pltpu.VMEM((bb_n, tt + 8, SC_W), F32),
                        pltpu.VMEM((bb_n, tt, MEM_W), F32), pltpu.VMEM((bb_n, tt, MEM_W), F32)],
        compiler_params=_cparams(("parallel", "arbitrary")),
        name="branches",
    )(h, sc_prev, mk, mv, cos, sin, *[w.arr for w in wts])


def _flash_kernel(tq, q_ref, k_ref, v_ref, o_ref, m_scr, l_scr, acc_scr):
    qi = pl.program_id(1)
    ki = pl.program_id(2)
    c2 = (MLA_QK ** -0.5) * LOG2E

    @pl.when(ki == 0)
    def _():
        m_scr[...] = jnp.full(m_scr.shape, -jnp.inf, F32)
        l_scr[...] = jnp.zeros_like(l_scr)
        acc_scr[...] = jnp.zeros_like(acc_scr)

    def compute(diag):
        half = tq // 2
        blocks = [(0, half, half), (half, tq, tq)] if diag else [(0, tq, tq)]
        keep = {}
        if diag:
            for r0, r1, nk in blocks:
                row = lax.broadcasted_iota(jnp.int32, (r1 - r0, nk), 0) + r0
                col = lax.broadcasted_iota(jnp.int32, (r1 - r0, nk), 1)
                keep[r0] = col <= row
        ones = jnp.ones((tq, 128), BF16)
        work = [(hd, blk) for hd in range(MLA_H) for blk in blocks]

        def qk(hd, blk):
            r0, r1, nk = blk
            sl = slice(hd * MLA_LANES, (hd + 1) * MLA_LANES)
            return _dot_nt(q_ref[r0:r1, sl], k_ref[0:nk, sl])

        s_next = qk(*work[0])
        for idx, (hd, (r0, r1, nk)) in enumerate(work):
            s = s_next * c2
            if idx + 1 < len(work):
                s_next = qk(*work[idx + 1])
            if diag:
                s = jnp.where(keep[r0], s, -jnp.inf)
            m_old = m_scr[hd, r0:r1]
            m_new = jnp.maximum(m_old, jnp.max(s, axis=-1, keepdims=True))
            alpha = jnp.exp2(m_old - m_new)
            p = jnp.exp2(s - jnp.concatenate([m_new] * (nk // 128), axis=1)).astype(BF16)
            pair = hd // 2
            vext = jnp.concatenate([v_ref[0:nk, pair * 128:(pair + 1) * 128], ones[0:nk]], axis=1)
            r = _dot(p, vext)
            acc_scr[hd, r0:r1] = alpha * acc_scr[hd, r0:r1] + r[:, :128]
            l_scr[hd, r0:r1] = alpha * l_scr[hd, r0:r1] + r[:, 128:]
            m_scr[hd, r0:r1] = m_new

    @pl.when(ki < qi)
    def _():
        compute(False)

    @pl.when(ki == qi)
    def _():
        compute(True)
        lane = lax.broadcasted_iota(jnp.int32, (tq, 128), 1)
        for pair in range(MLA_H // 2):
            even = acc_scr[2 * pair] / l_scr[2 * pair]
            odd = acc_scr[2 * pair + 1] / l_scr[2 * pair + 1]
            o_ref[:, pair * 128:(pair + 1) * 128] = jnp.where(lane < MLA_V, even, odd).astype(o_ref.dtype)


def _mla_prompt_attn(q, k, v, *, n_seq, seq, tq):
    nq = seq // tq
    hw = MLA_H * MLA_LANES
    vw = MLA_H * MLA_V
    return pl.pallas_call(
        functools.partial(_flash_kernel, tq),
        grid=(n_seq, nq, nq),
        in_specs=[pl.BlockSpec((tq, hw), lambda b, i, j: (b * nq + i, 0)),
                  pl.BlockSpec((tq, hw), lambda b, i, j: (b * nq + jnp.minimum(i, j), 0)),
                  pl.BlockSpec((tq, vw), lambda b, i, j: (b * nq + jnp.minimum(i, j), 0))],
        out_specs=pl.BlockSpec((tq, vw), lambda b, i, j: (b * nq + i, 0)),
        out_shape=jax.ShapeDtypeStruct((n_seq * seq, vw), BF16),
        scratch_shapes=[pltpu.VMEM((MLA_H, tq, 128), F32)] * 3,
        compiler_params=_cparams(("parallel", "parallel", "arbitrary")),
        name="mla_flash",
    )(q, k, v)


SAMPLE_ROWS = 8
SUB_KEYS = 1024


def _mla_sample_kernel(layer, n_seq, n_pg, n_steps, t_valid,
                       pt_ref, qall_ref, q_ref, knew_ref, cnew_ref, wukp_ref, wukt_ref, kg_ref, ckv_hbm, kr_hbm,
                       o_ref,
                       lhs, qabs_all, qabs, qrope, qblk, m_scr, l_scr, acc_scr, cbuf, rbuf, sem):
    b = pl.program_id(0)
    st = pl.program_id(1)
    c2 = (MLA_QK ** -0.5) * LOG2E
    inv_n = 1.0 / MLA_QK
    tk = n_pg * PAGE
    nq = SAMPLE_ROWS
    n_up = MLA_H * MLA_NOPE
    g = b * n_steps + st
    slot = g % 2

    def page_copies(bb, ss, sl):
        cps = []
        for i in range(n_pg):
            page = pt_ref[bb, ss * n_pg + i]
            cps.append((pltpu.make_async_copy(ckv_hbm.at[layer, page], cbuf.at[sl, pl.ds(i * PAGE, PAGE), :],
                                              sem.at[sl, 0]), i % 2))
            cps.append((pltpu.make_async_copy(kr_hbm.at[layer, page], rbuf.at[sl, i], sem.at[sl, 1]), (i + 1) % 2))
        return cps

    n_total = n_seq * n_steps
    is_last = g == n_total - 1

    @pl.when(g == 0)
    def _():
        for cp, prio in page_copies(0, 0, 0):
            cp.start(priority=prio)
        kg = kg_ref[...]
        for hd in range(MLA_H):
            sl = slice(hd * MLA_LANES, (hd + 1) * MLA_LANES)
            qh, ql = _split2(qall_ref[:, sl] * kg)
            w = wukp_ref[:, sl]
            qabs_all[hd] = _dot_nt(qh, w) + _dot_nt(ql, w)

    @pl.when(st == 0)
    def _():
        m_scr[...] = jnp.full(m_scr.shape, -jnp.inf, F32)
        l_scr[...] = jnp.zeros_like(l_scr)
        acc_scr[...] = jnp.zeros_like(acc_scr)
        kg = kg_ref[...]
        q = q_ref[...]
        lane_head = lax.broadcasted_iota(jnp.int32, q.shape, 1) // MLA_LANES
        r0 = pl.multiple_of(b * nq, nq)
        for hd in range(MLA_H):
            sl = slice(hd * MLA_LANES, (hd + 1) * MLA_LANES)
            qabs[hd * nq:(hd + 1) * nq, :] = qabs_all[hd, pl.ds(r0, nq), :]
            qrope[hd * nq:(hd + 1) * nq, :] = q[:, hd * MLA_LANES:hd * MLA_LANES + MLA_ROPE] * kg[:, :MLA_ROPE]
            qblk[hd * nq:(hd + 1) * nq, :] = jnp.where(lane_head == hd, q, 0.0)
        lhs[:n_up, :] = wukt_ref[...]
        lhs[n_up:, :] = qabs[...].astype(BF16)

    for cp, _ in page_copies(b, st, slot):
        cp.wait()

    lhs_v = lhs[...]
    qr = qrope[...].astype(BF16)
    pg_sub = SUB_KEYS // PAGE

    def score_block(j):
        cb = cbuf[slot, j * SUB_KEYS:(j + 1) * SUB_KEYS, :].astype(BF16)
        krt = jnp.concatenate([rbuf[slot, i] for i in range(j * pg_sub, (j + 1) * pg_sub)],
                              axis=1)
        big = _dot_nt(lhs_v, cb)
        knt = big[:n_up]
        ssq = jnp.sum((knt * knt).reshape(MLA_H, MLA_NOPE, SUB_KEYS), axis=1)
        ssq_r = jnp.sum(krt * krt, axis=0, keepdims=True)
        rs = lax.rsqrt((ssq + ssq_r) * inv_n + EPS) * c2
        s = big[n_up:] + _dot(qr, krt.astype(BF16))
        s = jnp.concatenate([s[hd * nq:(hd + 1) * nq, :] * rs[hd:hd + 1, :] for hd in range(MLA_H)], axis=0)
        return s, cb

    m_run = m_scr[...]
    l_new = l_scr[...]
    acc = acc_scr[...]
    n_sub = tk // SUB_KEYS
    blk = score_block(0)

    seq_end = st == n_steps - 1
    nb = jnp.where(is_last, b, jnp.where(seq_end, b + 1, b))
    ns = jnp.where(is_last, st, jnp.where(seq_end, 0, st + 1))
    for cp, prio in page_copies(nb, ns, 1 - slot):
        cp.start(priority=prio)

    for j in range(n_sub):
        s, cb = blk
        if j + 1 < n_sub:
            blk = score_block(j + 1)
        m_new = jnp.maximum(m_run, jnp.max(s, axis=-1, keepdims=True))
        alpha = jnp.exp2(m_run - m_new)
        p = jnp.exp2(s - m_new)
        l_new = alpha * l_new + jnp.sum(p, axis=-1, keepdims=True)
        acc = alpha * acc + _dot(p.astype(BF16), cb)
        m_run = m_new
    l_scr[...] = l_new
    acc_scr[...] = acc
    m_scr[...] = m_new

    @pl.when(st == n_steps - 1)
    def _():
        cn = cnew_ref[...]
        sn = _dot_nt(_rnd(qblk[...]), _rnd(knew_ref[...])) * c2
        row = lax.broadcasted_iota(jnp.int32, sn.shape, 0) % nq
        col = lax.broadcasted_iota(jnp.int32, sn.shape, 1)
        sn = jnp.where((col <= row) & (col < t_valid), sn, -jnp.inf)
        m_o = m_scr[...]
        m_n = jnp.maximum(m_o, jnp.max(sn, axis=-1, keepdims=True))
        al = jnp.exp2(m_o - m_n)
        pn = jnp.exp2(sn - m_n)
        l_f = al * l_scr[...] + jnp.sum(pn, axis=-1, keepdims=True)
        o_ref[...] = (al * acc_scr[...] + _dot(_rnd(pn), _rnd(cn))) / l_f

    @pl.when(is_last)
    def _():
        for cp, _ in page_copies(b, st, 1 - slot):
            cp.wait()


def _mla_sample_attn(page_table, q, k_new, c_new, wts, ckv_pool, kr_pool_t, layer, *, n_seq, t_valid, n_pg):
    n_pages = page_table.shape[1]
    n_steps = n_pages // n_pg
    hw = MLA_H * MLA_LANES
    tk = n_pg * PAGE
    nq = SAMPLE_ROWS
    nr = MLA_H * nq

    const = lambda *shape: pl.BlockSpec(shape, lambda b, s, pt: (0,) * len(shape))
    hbm = pl.BlockSpec(memory_space=pl.ANY)
    in_specs = [const(n_seq * nq, hw),
                pl.BlockSpec((nq, hw), lambda b, s, pt: (b, 0)),
                pl.BlockSpec((nq, hw), lambda b, s, pt: (b, 0)),
                pl.BlockSpec((nq, MLA_RANK), lambda b, s, pt: (b, 0))]
    in_specs += [_wspec(w) for w in wts] + [hbm, hbm]
    grid_spec = pltpu.PrefetchScalarGridSpec(
        num_scalar_prefetch=1,
        grid=(n_seq, n_steps),
        in_specs=in_specs,
        out_specs=pl.BlockSpec((nr, MLA_RANK), lambda b, s, pt: (b, 0)),
        scratch_shapes=[pltpu.VMEM((MLA_H * MLA_NOPE + nr, MLA_RANK), BF16),
                        pltpu.VMEM((MLA_H, n_seq * nq, MLA_RANK), F32),
                        pltpu.VMEM((nr, MLA_RANK), F32), pltpu.VMEM((nr, MLA_ROPE), F32),
                        pltpu.VMEM((nr, hw), F32),
                        pltpu.VMEM((nr, 1), F32), pltpu.VMEM((nr, 1), F32), pltpu.VMEM((nr, MLA_RANK), F32),
                        pltpu.VMEM((2, tk, MLA_RANK), F32), pltpu.VMEM((2, n_pg, MLA_ROPE, PAGE), F32),
                        pltpu.SemaphoreType.DMA((2, 2))],
    )
    return pl.pallas_call(
        functools.partial(_mla_sample_kernel, layer, n_seq, n_pg, n_steps, t_valid),
        grid_spec=grid_spec,
        out_shape=jax.ShapeDtypeStruct((n_seq * nr, MLA_RANK), F32),
        compiler_params=_cparams(("arbitrary", "arbitrary")),
        name="mla_paged",
    )(page_table, q, q, k_new, c_new, *[w.arr for w in wts], ckv_pool, kr_pool_t)


def _mla_up_kernel(n_seq, pc_ref, h_ref, wuvs_ref, wout_ref, wgate_ref, y_ref):
    nq = SAMPLE_ROWS
    o = jnp.zeros((n_seq * nq, MLA_H * MLA_V), F32)
    for hd in range(MLA_H):
        pch = pc_ref[:, hd * nq:(hd + 1) * nq, :].reshape(n_seq * nq, MLA_RANK)
        o = o + _dot(pch.astype(BF16), wuvs_ref[hd])
    y = _dot(o.astype(BF16), wout_ref[...]) * jax.nn.sigmoid(_dot(h_ref[...], wgate_ref[...]))
    y_ref[...] = y.astype(y_ref.dtype)


def _mla_up_proj_gate(pc, h, wuv_sel, w_out, w_gate, *, n_seq):
    rows = n_seq * SAMPLE_ROWS
    nr = MLA_H * SAMPLE_ROWS
    wts = (wuv_sel, w_out, w_gate)
    return pl.pallas_call(
        functools.partial(_mla_up_kernel, n_seq),
        grid=(1,),
        in_specs=[pl.BlockSpec((n_seq, nr, MLA_RANK), lambda i: (0, 0, 0)),
                  pl.BlockSpec((rows, D_MODEL), lambda i: (0, 0))] + [_wspec(w) for w in wts],
        out_specs=pl.BlockSpec((rows, D_MODEL), lambda i: (0, 0)),
        out_shape=jax.ShapeDtypeStruct((rows, D_MODEL), BRANCH_DTYPE),
        compiler_params=_cparams(("arbitrary",)),
        name="mla_up_proj_gate",
    )(pc.reshape(n_seq, nr, MLA_RANK), h, *[w.arr for w in wts])


def _projgate_kernel(o_ref, h_ref, wout_ref, wgate_ref, y_ref):
    y = _dot(o_ref[...].astype(BF16), wout_ref[...]) * jax.nn.sigmoid(_dot(h_ref[...], wgate_ref[...]))
    y_ref[...] = y.astype(y_ref.dtype)


def _proj_gate(o, h, w_out, w_gate, *, tm):
    m, kdim = o.shape
    return pl.pallas_call(
        _projgate_kernel,
        grid=(m // tm,),
        in_specs=[pl.BlockSpec((tm, kdim), lambda i: (i, 0)),
                  pl.BlockSpec((tm, D_MODEL), lambda i: (i, 0)),
                  _wspec(w_out), _wspec(w_gate)],
        out_specs=pl.BlockSpec((tm, D_MODEL), lambda i: (i, 0)),
        out_shape=jax.ShapeDtypeStruct((m, D_MODEL), BRANCH_DTYPE),
        compiler_params=_cparams(("parallel",)),
        name="proj_gate",
    )(o, h, w_out.arr, w_gate.arr)


def _pad_lanes(x, width):
    return jnp.pad(x, [(0, 0)] * (x.ndim - 1) + [(0, width - x.shape[-1])])


def _mla_head_layout(nope, r1, r2):
    z = jnp.zeros(nope.shape[:-1] + (MLA_LANES - MLA_QK,), nope.dtype)
    x = jnp.concatenate([r1, r2, nope, z], axis=-1)
    return x.reshape(x.shape[:-2] + (MLA_H * MLA_LANES,))


def _gain_layout(g):
    half = MLA_ROPE // 2
    return jnp.concatenate([g[MLA_NOPE:MLA_NOPE + half], g[MLA_NOPE + half:], g[:MLA_NOPE],
                            jnp.zeros((MLA_LANES - MLA_QK,), g.dtype)]).reshape(1, MLA_LANES)


def _rope_tables(pos):
    half = MLA_ROPE // 2
    inv = ROPE_THETA ** (-jnp.arange(half, dtype=F32) / half)
    ang = pos.astype(F32)[:, None] * inv
    cos, sin = jnp.cos(ang), jnp.sin(ang)
    n = pos.shape[0]
    cos_t = jnp.concatenate([cos, cos, jnp.ones((n, MLA_NOPE), F32), jnp.zeros((n, MLA_LANES - MLA_QK), F32)], -1)
    sin_t = jnp.concatenate([sin, sin, jnp.zeros((n, MLA_LANES - MLA_ROPE), F32)], -1)
    return cos_t, sin_t


def _layer_weights(p):
    w_in = p['w_in']
    sizes = (DN_CONV_W, DN_QK_W, DN_H, DN_H, SC_W, SC_W, SC_W, MLA_RANK, MLA_RANK, MLA_ROPE, MEM_W, 4 * D_MODEL)
    offs = np.concatenate([[0], np.cumsum(sizes)])
    seg = [w_in[:, offs[i]:offs[i + 1]] for i in range(len(sizes))]
    bf = lambda x: x.astype(BF16)
    row = lambda x: x.reshape(1, -1)
    gates = [bf(seg[11][:, i * D_MODEL:(i + 1) * D_MODEL]) for i in range(4)]
    half = MLA_ROPE // 2

    w = {}
    w['ffn1'] = (row(p['ffn1_norm']), bf(p['ffn1_w_gu']), bf(p['ffn1_w_down']))
    w['ffn2'] = (row(p['ffn2_norm']), bf(p['ffn2_w_gu']), bf(p['ffn2_w_down']))
    w['mix_norm'] = row(p['mix_norm'])
    w['w_o'] = bf(p['w_o'])
    w['dn'] = (bf(seg[0]), bf(seg[1]), bf(_pad_lanes(jnp.concatenate([seg[2], seg[3]], 1), 128)),
               p['dn_conv_w'], _pad_lanes(row(p['dn_A_log']), 128), _pad_lanes(row(p['dn_dt_bias']), 128),
               row(p['dn_norm']), bf(p['dn_w_out']), gates[0])
    w['sc'] = (bf(jnp.concatenate([seg[4], seg[5], seg[6]], 1)), p['sc_conv_w'], bf(p['sc_w_out']), gates[1])

    wq = p['mla_w_q_b'].reshape(MLA_RANK, MLA_H, MLA_QK)
    q_nope, q_r1, q_r2 = wq[..., :MLA_NOPE], wq[..., MLA_NOPE:MLA_NOPE + half], wq[..., MLA_NOPE + half:]
    wq_perm = _mla_head_layout(q_nope, q_r1, q_r2)
    wq_swap = _mla_head_layout(jnp.zeros_like(q_nope), -q_r2, q_r1)
    wkr = seg[9]
    wkr_pad = _pad_lanes(wkr, MLA_LANES)
    wkr_swap = _pad_lanes(jnp.concatenate([-wkr[:, half:], wkr[:, :half]], 1), MLA_LANES)
    wkv = p['mla_w_kv_b'].reshape(MLA_RANK, MLA_H, MLA_NOPE + MLA_V)
    w_uk, w_uv = wkv[..., :MLA_NOPE], wkv[..., MLA_NOPE:]
    zr = jnp.zeros((MLA_RANK, MLA_H, half), F32)
    wuk_perm = bf(_mla_head_layout(w_uk, zr, zr))
    k_gain = _gain_layout(p['mla_k_norm'])
    w['mla_proj'] = (bf(seg[7]), row(p['mla_q_norm_a']), bf(wq_perm), bf(wq_swap), _gain_layout(p['mla_q_norm']),
                     bf(seg[8]), row(p['mla_kv_norm_a']), bf(wkr_pad), bf(wkr_swap),
                     wuk_perm, bf(w_uv.reshape(MLA_RANK, MLA_H * MLA_V)), k_gain)
    eye = jnp.eye(MLA_H, dtype=F32)
    w['mla_wuv_sel'] = bf((w_uv[None] * eye[:, None, :, None]).reshape(MLA_H, MLA_RANK, MLA_H * MLA_V))
    w['mla_sample'] = (wuk_perm, bf(w_uk.reshape(MLA_RANK, MLA_H * MLA_NOPE).T), k_gain)
    w['mla_out'] = (bf(p['mla_w_out']), gates[2])
    w['mem_kv'] = (row(p['mem_norm']), bf(p['mem_w_kv']), row(p['mem_k_norm']))
    w['mem'] = (bf(seg[10]), row(p['mem_q_norm']), bf(p['mem_w_out']), gates[3])
    return w


def _group_layer(x, w, *, n_seq, t_pad, t_valid, tm, bb_n, tt, chunk, dn_state, sc_state, mem_kv, cos, sin,
                 mla_attend, q_dtype):
    cfg = dict(n_seq=n_seq, t_pad=t_pad, bb_n=bb_n, tt=tt)
    x1, h = _ffn(x, *w['ffn1'], tm=tm, h_gain=w['mix_norm'])
    y_dn, dn_s, dn_c = _deltanet(h, dn_state[0], dn_state[1], w['dn'], t_valid=t_valid, chunk=chunk, **cfg)
    y_sc, sc_c, y_mem, q, ckv, kr, k, v = _branches(
        h, sc_state, mem_kv[0], mem_kv[1], cos, sin, w['sc'], w['mem'], w['mla_proj'],
        t_valid=t_valid, qk_dtype=q_dtype, **cfg)
    if mla_attend is None:
        o = _mla_prompt_attn(q, k, v, n_seq=n_seq, seq=t_pad, tq=tm)
        y_mla = _proj_gate(o, h, *w['mla_out'], tm=2 * tm)
    else:
        y_mla = mla_attend(q, k, ckv, h)
    x3 = _ffn(x1, *w['ffn2'], tm=tm, merge=((y_dn, y_sc, y_mla, y_mem), w['w_o']))
    return x3, dn_s, dn_c, sc_c, ckv, kr


def kernel(x_prompt, x_sample, state_dn_S, state_dn_conv, state_sc_conv, cache_mla_ckv, cache_mla_krope, cache_mem_k, cache_mem_v, page_table, mem_prompt, ffn1_norm, ffn1_w_gu, ffn1_w_down, mix_norm, w_in, dn_conv_w, dn_A_log, dn_dt_bias, dn_norm, dn_w_out, sc_conv_w, sc_w_out, mla_q_norm_a, mla_w_q_b, mla_kv_norm_a, mla_w_kv_b, mla_q_norm, mla_k_norm, mla_w_out, mem_norm, mem_w_kv, mem_q_norm, mem_k_norm, mem_w_out, w_o, ffn2_norm, ffn2_w_gu, ffn2_w_down):
    params = dict(ffn1_norm=ffn1_norm, ffn1_w_gu=ffn1_w_gu, ffn1_w_down=ffn1_w_down, mix_norm=mix_norm, w_in=w_in,
                  dn_conv_w=dn_conv_w, dn_A_log=dn_A_log, dn_dt_bias=dn_dt_bias, dn_norm=dn_norm, dn_w_out=dn_w_out,
                  sc_conv_w=sc_conv_w, sc_w_out=sc_w_out, mla_q_norm_a=mla_q_norm_a, mla_w_q_b=mla_w_q_b,
                  mla_kv_norm_a=mla_kv_norm_a, mla_w_kv_b=mla_w_kv_b, mla_q_norm=mla_q_norm, mla_k_norm=mla_k_norm,
                  mla_w_out=mla_w_out, mem_norm=mem_norm, mem_w_kv=mem_w_kv, mem_q_norm=mem_q_norm,
                  mem_k_norm=mem_k_norm, mem_w_out=mem_w_out, w_o=w_o, ffn2_norm=ffn2_norm, ffn2_w_gu=ffn2_w_gu,
                  ffn2_w_down=ffn2_w_down)
    depth = w_in.shape[0]
    bp, seq, _ = x_prompt.shape
    bs, td, _ = x_sample.shape
    tds = SAMPLE_ROWS
    n_pages = page_table.shape[1]
    past = n_pages * PAGE
    krope_t = jnp.transpose(cache_mla_krope, (0, 1, 3, 2))

    cos_p, sin_p = _rope_tables(jnp.arange(seq))
    cos_s, sin_s = _rope_tables(past + jnp.arange(tds))
    cos_s, sin_s = jnp.tile(cos_s, (bs, 1)), jnp.tile(sin_s, (bs, 1))

    xp = x_prompt.reshape(bp * seq, D_MODEL)
    xs = jnp.pad(x_sample, ((0, 0), (0, tds - td), (0, 0))).reshape(bs * tds, D_MODEL)
    zero_s = jnp.zeros((bp, DN_H, DN_DK, DN_DK), F32)
    zero_dc = jnp.zeros((bp, DN_CONV - 1, DN_CONV_W), F32)
    zero_sc = jnp.zeros((bp, SC_CONV - 1, SC_W), F32)
    mem2d = mem_prompt.reshape(bp * N_MEM, D_MODEL)

    outs = {k: [] for k in ('pS', 'pdc', 'psc', 'pckv', 'pkr', 'pmk', 'pmv', 'sS', 'sdc', 'ssc', 'sckv', 'skr')}
    tm_p = 512
    w_all = jax.vmap(_layer_weights)(params)
    for l in range(depth):
        w = jax.tree.map(lambda a: _W(a, l), w_all)
        mk, mv = _mem_kv(mem2d, w['mem_kv'], tm=tm_p)
        mk3, mv3 = mk.reshape(bp, N_MEM, MEM_W), mv.reshape(bp, N_MEM, MEM_W)
        xp, s_p, dc_p, sc_p, ckv_p, kr_p = _group_layer(
            xp, w, n_seq=bp, t_pad=seq, t_valid=seq, tm=tm_p, bb_n=1, tt=tm_p, chunk=DN_CHUNK,
            dn_state=(zero_s, zero_dc), sc_state=zero_sc, mem_kv=(mk3, mv3), cos=cos_p, sin=sin_p,
            mla_attend=None, q_dtype=BF16)
        outs['pS'].append(s_p); outs['pdc'].append(dc_p); outs['psc'].append(sc_p)
        outs['pckv'].append(ckv_p.reshape(bp, seq, MLA_RANK)); outs['pkr'].append(kr_p.reshape(bp, seq, MLA_ROPE))
        outs['pmk'].append(mk.reshape(bp, N_MEM, MEM_H, MEM_HD)); outs['pmv'].append(mv.reshape(bp, N_MEM, MEM_H, MEM_HD))

        def attend(q, k, ckv, h, l=l, w=w):
            pc = _mla_sample_attn(page_table, q, k, ckv, w['mla_sample'], cache_mla_ckv, krope_t, l,
                                  n_seq=bs, t_valid=td, n_pg=32)
            return _mla_up_proj_gate(pc, h, w['mla_wuv_sel'], *w['mla_out'], n_seq=bs)

        xs, s_s, dc_s, sc_s, ckv_s, kr_s = _group_layer(
            xs, w, n_seq=bs, t_pad=tds, t_valid=td, tm=bs * tds, bb_n=8, tt=tds, chunk=tds,
            dn_state=(state_dn_S[l], state_dn_conv[l]), sc_state=state_sc_conv[l],
            mem_kv=(cache_mem_k[l].reshape(bs, N_MEM, MEM_W), cache_mem_v[l].reshape(bs, N_MEM, MEM_W)),
            cos=cos_s, sin=sin_s, mla_attend=attend, q_dtype=F32)
        outs['sS'].append(s_s); outs['sdc'].append(dc_s); outs['ssc'].append(sc_s)
        outs['sckv'].append(ckv_s.reshape(bs, tds, MLA_RANK)[:, :td])
        outs['skr'].append(kr_s.reshape(bs, tds, MLA_ROPE)[:, :td])

    st = lambda k: jnp.stack(outs[k])
    y_prompt = xp.reshape(bp, seq, D_MODEL)
    y_sample = xs.reshape(bs, tds, D_MODEL)[:, :td]
    return (y_prompt, y_sample, st('pS'), st('pdc'), st('psc'), st('pckv'), st('pkr'), st('pmk'), st('pmv'),
            st('sS'), st('sdc'), st('ssc'), st('sckv'), st('skr'))
```
